```python
import math
import jax
import jax.numpy as jnp
from jax import lax
import numpy as np

D_MODEL = 2048
BATCH = 32
SEQ = 256
DEPTH = 4
DEC_BATCH = 8
DEC_SEQ = 2048
PAST_LEN = 512

N_MIXERS = 3
N_WIN = (DEPTH + 2) // 3
N_HY = (DEPTH + 1) // 3
N_AX = DEPTH // 3
N_HEADS = 16
N_KV_HEADS = 4
HEAD_DIM = 128
QKV_DIM = (N_HEADS + 2 * N_KV_HEADS) * HEAD_DIM
WINDOW = 128
BLOCK = 128
GRID_W = 64
ROPE_THETA = 10000.0
D_FF = -(-8 * D_MODEL // (3 * 256)) * 256
HY_ORDER = 2
HY_EMB = 33
HY_FILTER_HIDDEN = 64
HY_TARGET = 1e-2
HY_FAST_DECAY = 0.3
HY_SLOW_DECAY = 1.5
HY_MIN_DECAY = math.log(HY_TARGET) / HY_SLOW_DECAY
HY_MAX_DECAY = math.log(HY_TARGET) / HY_FAST_DECAY
EPS = 1e-6
NEG = -1e30

kernel_name = 'hybrid_diffusion_step'


def rmsnorm(x, g):
    xf = x.astype(jnp.float32)
    y = xf * lax.rsqrt(jnp.mean(xf * xf, axis=-1, keepdims=True) + EPS)
    return (y * g.astype(jnp.float32)).astype(x.dtype)


def adaln(cvec, w, b):
    m = jax.nn.silu(cvec) @ w + b
    return jnp.split(m[:, None, :], 6, axis=-1)


def modulate(h, shift, scale):
    return h * (1 + scale) + shift


def swiglu(h, w_gu, w_down):
    g, u = jnp.split(h @ w_gu, 2, axis=-1)
    return (jax.nn.silu(g) * u) @ w_down


def project_qkv(h, wqkv, q_g, k_g):
    B, L, _ = h.shape
    y = h @ wqkv
    nq, nk = N_HEADS * HEAD_DIM, N_KV_HEADS * HEAD_DIM
    q = y[..., :nq].reshape(B, L, N_HEADS, HEAD_DIM)
    k = y[..., nq:nq + nk].reshape(B, L, N_KV_HEADS, HEAD_DIM)
    v = y[..., nq + nk:].reshape(B, L, N_KV_HEADS, HEAD_DIM)
    if q_g is not None:
        q = rmsnorm(q, q_g)
        k = rmsnorm(k, k_g)
    return q, k, v


def axial_rope_tables(L):
    n_rows = L // GRID_W
    row = jnp.repeat(jnp.arange(n_rows, dtype=jnp.float32), GRID_W)
    col = jnp.tile(jnp.arange(GRID_W, dtype=jnp.float32), n_rows)
    half = HEAD_DIM // 2
    inv = ROPE_THETA ** (-jnp.arange(0, half, 2, dtype=jnp.float32) / half)
    ang = jnp.concatenate([row[:, None] * inv, col[:, None] * inv], axis=-1)
    return jnp.cos(ang), jnp.sin(ang)


def apply_axial_rope(x, cos, sin):
    half, quarter = HEAD_DIM // 2, HEAD_DIM // 4
    xf = x.astype(jnp.float32)
    parts = []
    for a in range(2):
        xa = xf[..., a * half:(a + 1) * half]
        x1, x2 = xa[..., :quarter], xa[..., quarter:]
        cs = cos[None, :, None, a * quarter:(a + 1) * quarter]
        sn = sin[None, :, None, a * quarter:(a + 1) * quarter]
        parts += [x1 * cs - x2 * sn, x2 * cs + x1 * sn]
    return jnp.concatenate(parts, axis=-1).astype(x.dtype)


def blocked_attention(q, k_ctx, v_ctx, sink=None, k_lat=None, v_lat=None, window=None):
    B, Lq, H, hd = q.shape
    kvh = k_ctx.shape[2]
    grp = H // kvh
    nb = Lq // BLOCK
    scale = hd ** -0.5
    qb = jnp.moveaxis(q.reshape(B, nb, BLOCK, kvh, grp, hd), 1, 0)
    if k_lat is None:
        k_all, v_all = k_ctx, v_ctx
    elif window is None:
        k_all = jnp.concatenate([k_lat, k_ctx], axis=1)
        v_all = jnp.concatenate([v_lat, v_ctx], axis=1)
    else:
        pad = ((0, 0), (BLOCK, BLOCK), (0, 0), (0, 0))
        k_src, v_src = jnp.pad(k_lat, pad), jnp.pad(v_lat, pad)
        ctx_mask = jnp.ones((BLOCK, k_ctx.shape[1]), dtype=bool)

    def one_block(args):
        b, qblk = args
        if k_lat is not None and window is not None:
            kw = lax.dynamic_slice_in_dim(k_src, b * BLOCK, 3 * BLOCK, axis=1)
            vw = lax.dynamic_slice_in_dim(v_src, b * BLOCK, 3 * BLOCK, axis=1)
            qpos = b * BLOCK + jnp.arange(BLOCK)
            kpos = (b - 1) * BLOCK + jnp.arange(3 * BLOCK)
            m_lat = ((jnp.abs(qpos[:, None] - kpos[None, :]) <= window)
                     & (kpos >= 0)[None, :] & (kpos < Lq)[None, :])
            keys = jnp.concatenate([kw, k_ctx], axis=1)
            vals = jnp.concatenate([vw, v_ctx], axis=1)
            mask = jnp.concatenate([m_lat, ctx_mask], axis=1)
        else:
            keys, vals, mask = k_all, v_all, None
        s = jnp.einsum('bqkgd,bskd->bkgqs', qblk, keys).astype(jnp.float32) * scale
        if mask is not None:
            s = jnp.where(mask, s, NEG)
        if sink is not None:
            sk = sink.astype(jnp.float32).reshape(kvh, grp)[None, :, :, None, None]
            s = jnp.concatenate([s, jnp.broadcast_to(sk, s.shape[:-1] + (1,))], axis=-1)
        p = jax.nn.softmax(s, axis=-1)
        if sink is not None:
            p = p[..., :-1]
        return jnp.einsum('bkgqs,bskd->bqkgd', p.astype(vals.dtype), vals)

    out = lax.map(one_block, (jnp.arange(nb), qb))
    return jnp.moveaxis(out, 0, 1).reshape(B, Lq, H * hd)


def attn_context(h, wqkv, wo, sink, q_g, k_g):
    q, k, v = project_qkv(h, wqkv, q_g, k_g)
    o = blocked_attention(q, k, v, sink=sink)
    return o @ wo, k, v


def attn_latent(h, wqkv, wo, sink, q_g, k_g, k_ctx, v_ctx, window):
    q, k, v = project_qkv(h, wqkv, q_g, k_g)
    cos, sin = axial_rope_tables(h.shape[1])
    q = apply_axial_rope(q, cos, sin)
    k = apply_axial_rope(k, cos, sin)
    o = blocked_attention(q, k_ctx, v_ctx, sink=sink, k_lat=k, v_lat=v, window=window)
    return o @ wo


def short_conv(u, w, b):
    up = jnp.pad(u, ((0, 0), (1, 1), (0, 0)))
    return up[:, :-2] * w[0] + up[:, 1:-1] * w[1] + up[:, 2:] * w[2] + b


def hyena_filters(L, f_w1, f_b1, f_w2, f_b2, f_w3, freq):
    D = f_w3.shape[-1] // (2 * HY_ORDER)
    f32 = jnp.float32
    t = jnp.arange(L, dtype=f32)
    tn = t / max(L - 1, 1)
    bands = (HY_EMB - 1) // 2
    fb = jnp.linspace(1e-4, bands - 1, bands, dtype=f32)
    w = 2.0 * math.pi * t / L
    feats = jnp.concatenate([tn[:, None], jnp.cos(w[:, None] * fb), -jnp.sin(w[:, None] * fb)], axis=-1)
    a = jnp.sin(freq[0].astype(f32) * (feats @ f_w1.astype(f32) + f_b1.astype(f32)))
    a = jnp.sin(freq[1].astype(f32) * (a @ f_w2.astype(f32) + f_b2.astype(f32)))
    hf = (a @ f_w3.astype(f32)).reshape(L, 2, HY_ORDER, D)
    deltas = jnp.abs(jnp.linspace(HY_MIN_DECAY, HY_MAX_DECAY, D, dtype=f32))
    hf = hf * jnp.exp(-tn[:, None] * deltas[None, :])[:, None, None, :]
    fwd, bwd = hf[:, 0], hf[:, 1]
    two = jnp.concatenate([fwd, jnp.zeros((1, HY_ORDER, D), f32), bwd[:0:-1]], axis=0)
    two = two / (jnp.sum(jnp.abs(two), axis=0, keepdims=True) + EPS)
    return jnp.fft.rfft(two, n=2 * L, axis=0)


def long_conv(u, hf, bias):
    L = u.shape[1]
    uf32 = u.astype(jnp.float32)
    uf = jnp.fft.rfft(uf32, n=2 * L, axis=1)
    y = jnp.fft.irfft(uf * hf[None], n=2 * L, axis=1)[:, :L]
    return (y + uf32 * bias.astype(jnp.float32)).astype(u.dtype)


def hyena(h, w_in, conv_w, conv_b, f_w1, f_b1, f_w2, f_b2, f_w3, freq, skip, w_out):
    L = h.shape[1]
    u = short_conv(h @ w_in, conv_w, conv_b)
    v, x1, x2 = jnp.split(u, 3, axis=-1)
    filt = hyena_filters(L, f_w1, f_b1, f_w2, f_b2, f_w3, freq)
    z = x1 * long_conv(v, filt[:, 0], skip[0])
    z = x2 * long_conv(z, filt[:, 1], skip[1])
    return z @ w_out


def setup_inputs(seed: int = 0) -> dict:
    key = jax.random.key(seed)
    ks = iter(jax.random.split(key, 40))
    D = D_MODEL

    def nrm(shape, scale):
        return jax.random.normal(next(ks), shape, jnp.float32) * scale

    kv_win = (DEC_BATCH, N_WIN, PAST_LEN, N_KV_HEADS, HEAD_DIM)
    kv_ax = (DEC_BATCH, N_AX, PAST_LEN, N_KV_HEADS, HEAD_DIM)
    return {
        'x_prompt': nrm((BATCH, SEQ, D), 1.0),
        'x_sample': nrm((DEC_BATCH, DEC_SEQ, D), 1.0),
        'cache_win_k': nrm(kv_win, 1.0),
        'cache_win_v': nrm(kv_win, 1.0),
        'cache_ax_k': nrm(kv_ax, 1.0),
        'cache_ax_v': nrm(kv_ax, 1.0),
        'c': nrm((DEC_BATCH, D), 1.0),
        'c_ctx': nrm((D,), 1.0),
        'norm_mix_g': 1.0 + nrm((DEPTH, D), 0.02),
        'norm_ffn_g': 1.0 + nrm((DEPTH, D), 0.02),
        'mod_w': nrm((DEPTH, D, 6 * D), D ** -0.5),
        'mod_b': nrm((DEPTH, 6 * D), 0.02),
        'win_wqkv': nrm((N_WIN, D, QKV_DIM), D ** -0.5),
        'win_wo': nrm((N_WIN, N_HEADS * HEAD_DIM, D), (N_HEADS * HEAD_DIM) ** -0.5),
        'win_sink': nrm((N_WIN, N_HEADS), 0.5),
        'hy_w_in': nrm((N_HY, D, 3 * D), D ** -0.5),
        'hy_conv_w': nrm((N_HY, 3, 3 * D), 3 ** -0.5),
        'hy_conv_b': nrm((N_HY, 3 * D), 0.02),
        'hy_f_w1': nrm((N_HY, HY_EMB, HY_FILTER_HIDDEN), 1.0),
        'hy_f_b1': nrm((N_HY, HY_FILTER_HIDDEN), 0.1),
        'hy_f_w2': nrm((N_HY, HY_FILTER_HIDDEN, HY_FILTER_HIDDEN), HY_FILTER_HIDDEN ** -0.5),
        'hy_f_b2': nrm((N_HY, HY_FILTER_HIDDEN), 0.1),
        'hy_f_w3': nrm((N_HY, HY_FILTER_HIDDEN, 2 * HY_ORDER * D), HY_FILTER_HIDDEN ** -0.5),
        'hy_freq': 1.0 + nrm((N_HY, 2, HY_FILTER_HIDDEN), 0.02),
        'hy_skip': nrm((N_HY, HY_ORDER, D), 0.1),
        'hy_wo': nrm((N_HY, D, D), D ** -0.5),
        'ax_wqkv': nrm((N_AX, D, QKV_DIM), D ** -0.5),
        'ax_q_g': 1.0 + nrm((N_AX, HEAD_DIM), 0.02),
        'ax_k_g': 1.0 + nrm((N_AX, HEAD_DIM), 0.02),
        'ax_wo': nrm((N_AX, N_HEADS * HEAD_DIM, D), (N_HEADS * HEAD_DIM) ** -0.5),
        'ffn_w_gu': nrm((DEPTH, D, 2 * D_FF), D ** -0.5),
        'ffn_w_down': nrm((DEPTH, D_FF, D), D_FF ** -0.5),
        'final_g': 1.0 + nrm((D,), 0.02),
    }


def reference(x_prompt, x_sample, cache_win_k, cache_win_v, cache_ax_k, cache_ax_v, c, c_ctx,
              norm_mix_g, norm_ffn_g, mod_w, mod_b, win_wqkv, win_wo, win_sink,
              hy_w_in, hy_conv_w, hy_conv_b, hy_f_w1, hy_f_b1, hy_f_w2, hy_f_b2, hy_f_w3, hy_freq, hy_skip, hy_wo,
              ax_wqkv, ax_q_g, ax_k_g, ax_wo, ffn_w_gu, ffn_w_down, final_g):
    ctx, lat = x_prompt, x_sample
    win_k, win_v, ax_k, ax_v = [], [], [], []
    for i in range(DEPTH):
        sh1c, sc1c, g1c, sh2c, sc2c, g2c = adaln(c_ctx[None, :], mod_w[i], mod_b[i])
        sh1l, sc1l, g1l, sh2l, sc2l, g2l = adaln(c, mod_w[i], mod_b[i])
        h_c = modulate(rmsnorm(ctx, norm_mix_g[i]), sh1c, sc1c)
        h_l = modulate(rmsnorm(lat, norm_mix_g[i]), sh1l, sc1l)
        kind, j = i % N_MIXERS, i // N_MIXERS
        if kind == 0:
            o_c, k_c, v_c = attn_context(h_c, win_wqkv[j], win_wo[j], win_sink[j], None, None)
            o_l = attn_latent(h_l, win_wqkv[j], win_wo[j], win_sink[j], None, None,
                              cache_win_k[:, j], cache_win_v[:, j], WINDOW)
            win_k.append(k_c)
            win_v.append(v_c)
        elif kind == 1:
            hp = (hy_w_in[j], hy_conv_w[j], hy_conv_b[j], hy_f_w1[j], hy_f_b1[j], hy_f_w2[j], hy_f_b2[j],
                  hy_f_w3[j], hy_freq[j], hy_skip[j], hy_wo[j])
            o_c = hyena(h_c, *hp)
            o_l = hyena(h_l, *hp)
        else:
            o_c, k_c, v_c = attn_context(h_c, ax_wqkv[j], ax_wo[j], None, ax_q_g[j], ax_k_g[j])
            o_l = attn_latent(h_l, ax_wqkv[j], ax_wo[j], None, ax_q_g[j], ax_k_g[j],
                              cache_ax_k[:, j], cache_ax_v[:, j], None)
            ax_k.append(k_c)
            ax_v.append(v_c)
        ctx = ctx + g1c * o_c
        lat = lat + g1l * o_l
        h_c = modulate(rmsnorm(ctx, norm_ffn_g[i]), sh2c, sc2c)
        h_l = modulate(rmsnorm(lat, norm_ffn_g[i]), sh2l, sc2l)
        ctx = ctx + g2c * swiglu(h_c, ffn_w_gu[i], ffn_w_down[i])
        lat = lat + g2l * swiglu(h_l, ffn_w_gu[i], ffn_w_down[i])
    y_prompt = rmsnorm(ctx, final_g)
    y_sample = rmsnorm(lat, final_g)
    state_win_k = jnp.stack(win_k, axis=1)
    state_win_v = jnp.stack(win_v, axis=1)
    state_ax_k = jnp.stack(ax_k, axis=1)
    state_ax_v = jnp.stack(ax_v, axis=1)
    return (y_prompt, y_sample, state_win_k, state_win_v, state_ax_k, state_ax_v)
```

```python
import functools
import math

import jax
import jax.numpy as jnp
from jax import lax
from jax.experimental import pallas as pl
from jax.experimental.pallas import tpu as pltpu

F32 = jnp.float32
BF16 = jnp.bfloat16

N_HEADS = 16
N_KV_HEADS = 4
GROUP = N_HEADS // N_KV_HEADS
HEAD_DIM = 128
WINDOW = 128
BLOCK = 128
GRID_W = 64
ROPE_THETA = 10000.0
HY_ORDER = 2
HY_EMB = 33
HY_TARGET = 1e-2
HY_FAST_DECAY = 0.3
HY_SLOW_DECAY = 1.5
HY_MIN_DECAY = math.log(HY_TARGET) / HY_SLOW_DECAY
HY_MAX_DECAY = math.log(HY_TARGET) / HY_FAST_DECAY
EPS = 1e-6
NEG = -1e30
SCALE = HEAD_DIM ** -0.5

LANES = 128
MOD_ROWS = 16
VMEM_LIMIT = 56 * 1024 * 1024

NT_DIMS = (((1,), (1,)), ((), ()))


def _params(*sem):
    return pltpu.CompilerParams(dimension_semantics=sem, vmem_limit_bytes=VMEM_LIMIT)


def _silu(x):
    return x * (1.0 / (1.0 + jnp.exp(-x)))


def _mod_row(i, tm, n_ctx_rows, lat_len):
    return jnp.maximum(i * tm - n_ctx_rows + lat_len, 0) // lat_len


def _mod_kernel(c_ref, w_ref, b_ref, o_ref):
    s = _silu(c_ref[...]).astype(BF16)
    o_ref[0] = jnp.dot(s, w_ref[0].astype(BF16), preferred_element_type=F32) + b_ref[0]


def _adaln_all(cvec, mod_w, mod_b):
    depth, d, n = mod_w.shape
    tn = 1024
    return pl.pallas_call(
        _mod_kernel,
        grid=(depth, n // tn),
        in_specs=[pl.BlockSpec((MOD_ROWS, d), lambda l, j: (0, 0)),
                  pl.BlockSpec((1, d, tn), lambda l, j: (l, 0, j)),
                  pl.BlockSpec((1, 1, tn), lambda l, j: (l, 0, j))],
        out_specs=pl.BlockSpec((1, MOD_ROWS, tn), lambda l, j: (l, 0, j)),
        out_shape=jax.ShapeDtypeStruct((depth, MOD_ROWS, n), F32),
        compiler_params=_params("parallel", "parallel"),
        name="adaln_mod",
    )(cvec, mod_w, mod_b.reshape(depth, 1, n))


def _norm_mod(x, g, shift, scale):
    y = x * lax.rsqrt(jnp.mean(x * x, axis=-1, keepdims=True) + EPS) * g
    return y * (1.0 + scale) + shift


def _rope_rotate(a, cos, sin_signed):
    lane = lax.broadcasted_iota(jnp.int32, a.shape, 1)
    partner = jnp.where((lane & 32) == 0, pltpu.roll(a, 96, 1), pltpu.roll(a, 32, 1))
    return a * cos + partner * sin_signed


def _nmm_kernel(*refs, mode, heads_per_tile):
    if mode == "plain":
        x_ref, g_ref, sh_ref, sc_ref, w_ref, o_ref, h_scr = refs
    else:
        x_ref, g_ref, sh_ref, sc_ref, w_ref, cos_ref, sin_ref, hg_ref, o_ref, h_scr = refs
    j = pl.program_id(1)

    @pl.when(j == 0)
    def _():
        h_scr[...] = _norm_mod(x_ref[...], g_ref[...], sh_ref[0], sc_ref[0]).astype(BF16)

    acc = jnp.dot(h_scr[...], w_ref[...], preferred_element_type=F32)
    if mode == "plain":
        o_ref[...] = acc
        return
    cos = cos_ref[...]
    sin = sin_ref[...]
    for hh in range(heads_per_tile):
        head = j * heads_per_tile + hh
        a = acc[:, hh * HEAD_DIM:(hh + 1) * HEAD_DIM]
        r = a
        if mode == "normrope":
            gain = jnp.where(head < N_HEADS, hg_ref[0:1, :], hg_ref[1:2, :])
            r = r * lax.rsqrt(jnp.mean(r * r, axis=-1, keepdims=True) + EPS) * gain
        r = _rope_rotate(r, cos, sin)
        o_ref[:, hh * HEAD_DIM:(hh + 1) * HEAD_DIM] = jnp.where(head < N_HEADS + N_KV_HEADS, r, a)


def _norm_mod_matmul(x, g, mod, which, w, *, n_ctx_rows, lat_len, tm, tn, rope=None):
    t, d = x.shape
    n = w.shape[1]
    row = functools.partial(_mod_row, tm=tm, n_ctx_rows=n_ctx_rows, lat_len=lat_len)
    in_specs = [pl.BlockSpec((tm, d), lambda i, j: (i, 0)),
                pl.BlockSpec((1, d), lambda i, j: (0, 0)),
                pl.BlockSpec((1, 1, d), lambda i, j: (row(i) * 6 + which, 0, 0)),
                pl.BlockSpec((1, 1, d), lambda i, j: (row(i) * 6 + which + 1, 0, 0)),
                pl.BlockSpec((d, tn), lambda i, j: (0, j))]
    args = [x, g.reshape(1, d), mod, mod, w]
    mode = "plain"
    if rope is not None:
        cos, sin, hg = rope
        mode = "rope" if hg is None else "normrope"
        if hg is None:
            hg = jnp.ones((2, HEAD_DIM), F32)
        blocks_per_seq = lat_len // tm

        def tab(i, j):
            lat_blk = jnp.maximum(i * tm - n_ctx_rows, 0) // tm
            return (jnp.where(i * tm < n_ctx_rows, 0, 1 + lat_blk % blocks_per_seq), 0)

        in_specs += [pl.BlockSpec((tm, HEAD_DIM), tab), pl.BlockSpec((tm, HEAD_DIM), tab),
                     pl.BlockSpec((2, HEAD_DIM), lambda i, j: (0, 0))]
        args += [cos, sin, hg]
    return pl.pallas_call(
        functools.partial(_nmm_kernel, mode=mode, heads_per_tile=tn // HEAD_DIM),
        grid=(t // tm, n // tn),
        in_specs=in_specs,
        out_specs=pl.BlockSpec((tm, tn), lambda i, j: (i, j)),
        out_shape=jax.ShapeDtypeStruct((t, n), F32),
        scratch_shapes=[pltpu.VMEM((tm, d), BF16)],
        compiler_params=_params("parallel", "arbitrary"),
        name="norm_mod_matmul_" + mode,
    )(*args)


def _proj_res_kernel(ac_ref, al_ref, w_ref, x_ref, gate_ref, o_ref, *, n_ctx_blocks):
    i = pl.program_id(0)

    def emit(a_ref):
        acc = jnp.dot(a_ref[...], w_ref[...], preferred_element_type=F32)
        o_ref[...] = x_ref[...] + gate_ref[0] * acc

    pl.when(i < n_ctx_blocks)(lambda: emit(ac_ref))
    pl.when(i >= n_ctx_blocks)(lambda: emit(al_ref))


def _proj_residual(a_ctx, a_lat, w, x, mod, which, *, n_ctx_rows, lat_len, tm, tn):
    t, d = x.shape
    k = w.shape[0]
    ncb = n_ctx_rows // tm
    row = functools.partial(_mod_row, tm=tm, n_ctx_rows=n_ctx_rows, lat_len=lat_len)
    return pl.pallas_call(
        functools.partial(_proj_res_kernel, n_ctx_blocks=ncb),
        grid=(t // tm, d // tn),
        in_specs=[pl.BlockSpec((tm, k), lambda i, j: (jnp.minimum(i, ncb - 1), 0)),
                  pl.BlockSpec((tm, k), lambda i, j: (jnp.maximum(i - ncb, 0), 0)),
                  pl.BlockSpec((k, tn), lambda i, j: (0, j)),
                  pl.BlockSpec((tm, tn), lambda i, j: (i, j)),
                  pl.BlockSpec((1, 1, tn), lambda i, j: (row(i) * 6 + which, 0, j))],
        out_specs=pl.BlockSpec((tm, tn), lambda i, j: (i, j)),
        out_shape=jax.ShapeDtypeStruct((t, d), F32),
        compiler_params=_params("parallel", "arbitrary"),
        name="proj_residual",
    )(a_ctx, a_lat, w, x, mod)


def _ffn_kernel(x_ref, g_ref, sh_ref, sc_ref, gate_ref, wg_ref, wu_ref, wd_ref, o_ref, h_scr, acc_scr):
    f = pl.program_id(1)

    @pl.when(f == 0)
    def _():
        h_scr[...] = _norm_mod(x_ref[...], g_ref[...], sh_ref[0], sc_ref[0]).astype(BF16)
        acc_scr[...] = jnp.zeros_like(acc_scr)

    h = h_scr[...]
    gv = jnp.dot(h, wg_ref[...], preferred_element_type=F32)
    uv = jnp.dot(h, wu_ref[...], preferred_element_type=F32)
    act = (_silu(gv) * uv).astype(BF16)
    acc_scr[...] += jnp.dot(act, wd_ref[...], preferred_element_type=F32)

    @pl.when(f == pl.num_programs(1) - 1)
    def _():
        o_ref[...] = x_ref[...] + gate_ref[0] * acc_scr[...]


def _ffn(x, g, mod, w_gu, w_down, *, n_ctx_rows, lat_len, tm, tf):
    t, d = x.shape
    ff = w_down.shape[0]
    nf = ff // tf
    row = functools.partial(_mod_row, tm=tm, n_ctx_rows=n_ctx_rows, lat_len=lat_len)
    return pl.pallas_call(
        _ffn_kernel,
        grid=(t // tm, nf),
        in_specs=[pl.BlockSpec((tm, d), lambda i, f: (i, 0)),
                  pl.BlockSpec((1, d), lambda i, f: (0, 0)),
                  pl.BlockSpec((1, 1, d), lambda i, f: (row(i) * 6 + 3, 0, 0)),
                  pl.BlockSpec((1, 1, d), lambda i, f: (row(i) * 6 + 4, 0, 0)),
                  pl.BlockSpec((1, 1, d), lambda i, f: (row(i) * 6 + 5, 0, 0)),
                  pl.BlockSpec((d, tf), lambda i, f: (0, f)),
                  pl.BlockSpec((d, tf), lambda i, f: (0, nf + f)),
                  pl.BlockSpec((tf, d), lambda i, f: (f, 0))],
        out_specs=pl.BlockSpec((tm, d), lambda i, f: (i, 0)),
        out_shape=jax.ShapeDtypeStruct((t, d), F32),
        scratch_shapes=[pltpu.VMEM((tm, d), BF16), pltpu.VMEM((tm, d), F32)],
        compiler_params=_params("parallel", "arbitrary"),
        name="ffn_swiglu",
    )(x, g.reshape(1, d), mod, mod, mod, w_gu, w_gu, w_down)


def _stack_heads(q):
    return jnp.concatenate([q[:, h * HEAD_DIM:(h + 1) * HEAD_DIM] for h in range(GROUP)], axis=0)


def _unstack_heads(o, rows):
    return jnp.concatenate([o[h * rows:(h + 1) * rows] for h in range(GROUP)], axis=1)


def _sink_column(sink_ref, kh, rows):
    head = lax.broadcasted_iota(jnp.int32, (GROUP * rows, 1), 0) // rows
    col = jnp.zeros((GROUP * rows, 1), F32)
    for h in range(GROUP):
        col = jnp.where(head == h, sink_ref[kh * GROUP + h], col)
    return col


def _softmax_av(s, v, sink_col):
    m = jnp.max(s, axis=-1, keepdims=True)
    if sink_col is not None:
        m = jnp.maximum(m, sink_col)
    p = jnp.exp(s - m)
    l = jnp.sum(p, axis=-1, keepdims=True)
    if sink_col is not None:
        l = l + jnp.exp(sink_col - m)
    return jnp.dot(p.astype(BF16), v, preferred_element_type=F32) / l


def _attn_ctx_kernel(sink_ref, q_ref, k_ref, v_ref, o_ref, *, use_sink):
    rows = q_ref.shape[0]
    qs = _stack_heads(q_ref[...] * SCALE).astype(BF16)
    s = lax.dot_general(qs, k_ref[...].astype(BF16), NT_DIMS, preferred_element_type=F32)
    sink_col = _sink_column(sink_ref, pl.program_id(1), rows) if use_sink else None
    o = _softmax_av(s, v_ref[...].astype(BF16), sink_col)
    o_ref[...] = _unstack_heads(o, rows).astype(BF16)


def _attn_ctx(y, sink, *, n_ctx_seqs, ctx_len, use_sink):
    t = n_ctx_seqs * ctx_len
    qw = GROUP * HEAD_DIM
    return pl.pallas_call(
        functools.partial(_attn_ctx_kernel, use_sink=use_sink),
        grid=(n_ctx_seqs, N_KV_HEADS),
        in_specs=[pl.BlockSpec(memory_space=pltpu.SMEM),
                  pl.BlockSpec((ctx_len, qw), lambda b, kh: (b, kh)),
                  pl.BlockSpec((ctx_len, HEAD_DIM), lambda b, kh: (b, N_HEADS + kh)),
                  pl.BlockSpec((ctx_len, HEAD_DIM), lambda b, kh: (b, N_HEADS + N_KV_HEADS + kh))],
        out_specs=pl.BlockSpec((ctx_len, qw), lambda b, kh: (b, kh)),
        out_shape=jax.ShapeDtypeStruct((t, N_HEADS * HEAD_DIM), BF16),
        compiler_params=_params("parallel", "parallel"),
        name="attn_context",
    )(sink, y, y, y)


def _attn_lat_win_kernel(sink_ref, q_ref, kp_ref, kc_ref, kn_ref, vp_ref, vc_ref, vn_ref, ck_ref, cv_ref,
                         o_ref):
    kh = pl.program_id(1)
    qb = pl.program_id(2)
    nb = pl.num_programs(2)
    qs = _stack_heads(q_ref[...] * SCALE).astype(BF16)
    k = jnp.concatenate([kp_ref[...], kc_ref[...], kn_ref[...], ck_ref[...]], axis=0).astype(BF16)
    v = jnp.concatenate([vp_ref[...], vc_ref[...], vn_ref[...], cv_ref[...]], axis=0).astype(BF16)
    s = lax.dot_general(qs, k, NT_DIMS, preferred_element_type=F32)
    r = lax.broadcasted_iota(jnp.int32, s.shape, 0) % BLOCK
    c = lax.broadcasted_iota(jnp.int32, s.shape, 1)
    dist = r + BLOCK - c
    ok = (c >= 3 * BLOCK) | ((jnp.abs(dist) <= WINDOW)
                             & ((c >= BLOCK) | (qb > 0)) & ((c < 2 * BLOCK) | (qb < nb - 1)))
    s = jnp.where(ok, s, NEG)
    o = _softmax_av(s, v, _sink_column(sink_ref, kh, BLOCK))
    o_ref[...] = _unstack_heads(o, BLOCK).astype(BF16)


def _attn_lat_win(y, sink, cache_k, cache_v, *, n_ctx_rows, lat_batch, lat_len):
    qw = GROUP * HEAD_DIM
    nb = lat_len // BLOCK
    base = n_ctx_rows // BLOCK
    past = cache_k.shape[1]

    def qmap(b, kh, qb):
        return (base + b * nb + qb, kh)

    def kvmap(col0, off):
        def f(b, kh, qb):
            return (base + b * nb + jnp.clip(qb + off, 0, nb - 1), col0 + kh)
        return f

    kcol, vcol = N_HEADS, N_HEADS + N_KV_HEADS
    blk = lambda m: pl.BlockSpec((BLOCK, HEAD_DIM), m)
    cache_spec = pl.BlockSpec((None, past, HEAD_DIM), lambda b, kh, qb: (b, 0, kh))
    return pl.pallas_call(
        _attn_lat_win_kernel,
        grid=(lat_batch, N_KV_HEADS, nb),
        in_specs=[pl.BlockSpec(memory_space=pltpu.SMEM),
                  pl.BlockSpec((BLOCK, qw), qmap),
                  blk(kvmap(kcol, -1)), blk(kvmap(kcol, 0)), blk(kvmap(kcol, 1)),
                  blk(kvmap(vcol, -1)), blk(kvmap(vcol, 0)), blk(kvmap(vcol, 1)),
                  cache_spec, cache_spec],
        out_specs=pl.BlockSpec((BLOCK, qw), lambda b, kh, qb: (b * nb + qb, kh)),
        out_shape=jax.ShapeDtypeStruct((lat_batch * lat_len, N_HEADS * HEAD_DIM), BF16),
        compiler_params=_params("parallel", "parallel", "arbitrary"),
        name="attn_latent_window",
    )(sink, y, y, y, y, y, y, y, cache_k, cache_v)


def _attn_lat_full_kernel(q_ref, k_ref, v_ref, ck_ref, cv_ref, o_ref, k_scr, v_scr):
    lat_len = k_ref.shape[0]

    @pl.when(pl.program_id(2) == 0)
    def _():
        k_scr[0:lat_len, :] = k_ref[...].astype(BF16)
        k_scr[lat_len:, :] = ck_ref[...].astype(BF16)
        v_scr[0:lat_len, :] = v_ref[...].astype(BF16)
        v_scr[lat_len:, :] = cv_ref[...].astype(BF16)

    qs = _stack_heads(q_ref[...] * SCALE).astype(BF16)
    s = lax.dot_general(qs, k_scr[...], NT_DIMS, preferred_element_type=F32)
    o = _softmax_av(s, v_scr[...], None)
    o_ref[...] = _unstack_heads(o, BLOCK).astype(BF16)


def _attn_lat_full(y, cache_k, cache_v, *, n_ctx_rows, lat_batch, lat_len):
    qw = GROUP * HEAD_DIM
    nb = lat_len // BLOCK
    base = n_ctx_rows // BLOCK
    seq_base = n_ctx_rows // lat_len
    past = cache_k.shape[1]

    def qmap(b, kh, qb):
        return (base + b * nb + qb, kh)

    cache_spec = pl.BlockSpec((None, past, HEAD_DIM), lambda b, kh, qb: (b, 0, kh))
    return pl.pallas_call(
        _attn_lat_full_kernel,
        grid=(lat_batch, N_KV_HEADS, nb),
        in_specs=[pl.BlockSpec((BLOCK, qw), qmap),
                  pl.BlockSpec((lat_len, HEAD_DIM), lambda b, kh, qb: (seq_base + b, N_HEADS + kh)),
                  pl.BlockSpec((lat_len, HEAD_DIM), lambda b, kh, qb: (seq_base + b, N_HEADS + N_KV_HEADS + kh)),
                  cache_spec, cache_spec],
        out_specs=pl.BlockSpec((BLOCK, qw), lambda b, kh, qb: (b * nb + qb, kh)),
        out_shape=jax.ShapeDtypeStruct((lat_batch * lat_len, N_HEADS * HEAD_DIM), BF16),
        scratch_shapes=[pltpu.VMEM((lat_len + past, HEAD_DIM), BF16),
                        pltpu.VMEM((lat_len + past, HEAD_DIM), BF16)],
        compiler_params=_params("parallel", "parallel", "arbitrary"),
        name="attn_latent_full",
    )(y, y, y, cache_k, cache_v)


def _rope_tables(lat_len, tm):
    half, quarter = HEAD_DIM // 2, HEAD_DIM // 4
    n_rows = lat_len // GRID_W
    row = jnp.repeat(jnp.arange(n_rows, dtype=F32), GRID_W)
    col = jnp.tile(jnp.arange(GRID_W, dtype=F32), n_rows)
    inv = ROPE_THETA ** (-jnp.arange(0, half, 2, dtype=F32) / half)
    ang_r, ang_c = row[:, None] * inv, col[:, None] * inv
    cos = jnp.concatenate([jnp.cos(ang_r)] * 2 + [jnp.cos(ang_c)] * 2, axis=-1)
    sin = jnp.concatenate([-jnp.sin(ang_r), jnp.sin(ang_r), -jnp.sin(ang_c), jnp.sin(ang_c)], axis=-1)
    del quarter
    cos = jnp.concatenate([jnp.ones((tm, HEAD_DIM), F32), cos], axis=0)
    sin = jnp.concatenate([jnp.zeros((tm, HEAD_DIM), F32), sin], axis=0)
    return cos, sin


def _dft_mats(seq_len):
    n = 2 * seq_len
    k = jnp.arange(seq_len, dtype=jnp.int32)
    ang = ((k[:, None] * k[None, :]) % n).astype(F32) * (2.0 * math.pi / n)
    fc = jnp.cos(ang)
    fs = -jnp.sin(ang)
    sign = jnp.where(k % 2 == 0, 1.0, -1.0).astype(F32)
    fs = jnp.where(k[:, None] == 0, sign[None, :], fs)
    return fc.astype(BF16), fs.astype(BF16), fs.T.astype(BF16)


def _hy_filter_kernel(feat_ref, w1_ref, b1_ref, w2_ref, b2_ref, fq_ref, w3f_ref, w3b_ref, tn_ref, dl_ref,
                      fc_ref, fs_ref, hr_ref, hi_ref, p_scr, q_scr, inv_scr, nyq_scr):
    kf = pl.program_id(1)
    hp = lax.Precision.HIGHEST

    @pl.when(kf == 0)
    def _():
        a = jnp.dot(feat_ref[...], w1_ref[...], precision=hp, preferred_element_type=F32) + b1_ref[...]
        a = jnp.sin(fq_ref[0:1, :] * a)
        a = jnp.dot(a, w2_ref[...], precision=hp, preferred_element_type=F32) + b2_ref[...]
        a = jnp.sin(fq_ref[1:2, :] * a)
        decay = jnp.exp(-tn_ref[...] * dl_ref[...])
        fwd = jnp.dot(a, w3f_ref[...], precision=hp, preferred_element_type=F32) * decay
        bwd = jnp.dot(a, w3b_ref[...], precision=hp, preferred_element_type=F32) * decay
        rowi = lax.broadcasted_iota(jnp.int32, fwd.shape, 0)
        bwd = jnp.where(rowi == 0, 0.0, bwd)
        norm = jnp.sum(jnp.abs(fwd), axis=0, keepdims=True) + jnp.sum(jnp.abs(bwd), axis=0, keepdims=True) + EPS
        inv_scr[...] = 1.0 / norm
        p = fwd + bwd
        nyq_scr[...] = jnp.sum(jnp.where((rowi & 1) == 0, p, -p), axis=0, keepdims=True)
        p_scr[...] = p.astype(BF16)
        q_scr[...] = (fwd - bwd).astype(BF16)

    inv = inv_scr[...]
    hr_ref[...] = jnp.dot(fc_ref[...], p_scr[...], preferred_element_type=F32) * inv
    hi = jnp.dot(fs_ref[...], q_scr[...], preferred_element_type=F32)
    rowk = lax.broadcasted_iota(jnp.int32, hi.shape, 0)
    hi = jnp.where((rowk == 0) & (kf == 0), nyq_scr[...], hi)
    hi_ref[...] = hi * inv


def _hy_filters(seq_len, f_w1, f_b1, f_w2, f_b2, f_w3, freq, fc, fs, *, tc, tk):
    hid = f_w1.shape[1]
    nd = f_w3.shape[1] // 2
    d = nd // HY_ORDER
    t = jnp.arange(seq_len, dtype=F32)
    tnorm = t / max(seq_len - 1, 1)
    bands = (HY_EMB - 1) // 2
    fb = jnp.linspace(1e-4, bands - 1, bands, dtype=F32)
    w = 2.0 * math.pi * t / seq_len
    feats = jnp.concatenate([tnorm[:, None], jnp.cos(w[:, None] * fb), -jnp.sin(w[:, None] * fb)], axis=-1)
    emb = 64
    feats = jnp.pad(feats, ((0, 0), (0, emb - HY_EMB)))
    w1 = jnp.pad(f_w1, ((0, emb - HY_EMB), (0, 0)))
    deltas = jnp.abs(jnp.linspace(HY_MIN_DECAY, HY_MAX_DECAY, d, dtype=F32))
    deltas = jnp.tile(deltas, HY_ORDER).reshape(1, nd)
    nct = nd // tc
    const = lambda shape: pl.BlockSpec(shape, lambda c, kf: (0, 0))
    out = jax.ShapeDtypeStruct((seq_len, nd), F32)
    return pl.pallas_call(
        _hy_filter_kernel,
        grid=(nct, seq_len // tk),
        in_specs=[const((seq_len, emb)), const((emb, hid)), const((1, hid)), const((hid, hid)), const((1, hid)),
                  const((2, hid)),
                  pl.BlockSpec((hid, tc), lambda c, kf: (0, c)),
                  pl.BlockSpec((hid, tc), lambda c, kf: (0, nct + c)),
                  const((seq_len, 1)),
                  pl.BlockSpec((1, tc), lambda c, kf: (0, c)),
                  pl.BlockSpec((tk, seq_len), lambda c, kf: (kf, 0)),
                  pl.BlockSpec((tk, seq_len), lambda c, kf: (kf, 0))],
        out_specs=[pl.BlockSpec((tk, tc), lambda c, kf: (kf, c)),
                   pl.BlockSpec((tk, tc), lambda c, kf: (kf, c))],
        out_shape=[out, out],
        scratch_shapes=[pltpu.VMEM((seq_len, tc), BF16), pltpu.VMEM((seq_len, tc), BF16),
                        pltpu.VMEM((1, tc), F32), pltpu.VMEM((1, tc), F32)],
        compiler_params=_params("parallel", "arbitrary"),
        name="hyena_filter",
    )(feats, w1, f_b1.reshape(1, hid), f_w2, f_b2.reshape(1, hid), freq, f_w3, f_w3,
      tnorm.reshape(seq_len, 1), deltas, fc, fs)


def _short_conv(y, w, b):
    n = y.shape[0]
    row = lax.broadcasted_iota(jnp.int32, y.shape, 0)
    prev = jnp.where(row == 0, 0.0, pltpu.roll(y, 1, 0))
    nxt = jnp.where(row == n - 1, 0.0, pltpu.roll(y, n - 1, 0))
    return prev * w[0:1, :] + y * w[1:2, :] + nxt * w[2:3, :] + b


def _hy_conv_kernel(yv_ref, y1_ref, y2_ref, wv_ref, w1_ref, w2_ref, bv_ref, b1_ref, b2_ref, skip_ref,
                    fcr_ref, fsr_ref, fcc_ref, fst_ref, hr_ref, hi_ref, o_ref, u_scr, z_scr, acc_scr, *, kf_count):
    s = pl.program_id(2)
    kf = s % kf_count
    seq_len = yv_ref.shape[0]

    @pl.when(s == 0)
    def _():
        u_scr[...] = _short_conv(yv_ref[...], wv_ref[...], bv_ref[...]).astype(BF16)
        acc_scr[...] = jnp.zeros_like(acc_scr)

    u = u_scr[...]
    ur = jnp.dot(fcr_ref[...], u, preferred_element_type=F32)
    ui = jnp.dot(fsr_ref[...], u, preferred_element_type=F32)
    hr = hr_ref[...]
    hi = hi_ref[...]
    yr = ur * hr - ui * hi
    yi = ur * hi + ui * hr
    first = (lax.broadcasted_iota(jnp.int32, ur.shape, 0) == 0) & (kf == 0)
    yr = jnp.where(first, 0.5 * (ur * hr), yr)
    yi = jnp.where(first, 0.5 * (ui * hi), yi)
    acc_scr[...] += (jnp.dot(fcc_ref[...], yr.astype(BF16), preferred_element_type=F32)
                     + jnp.dot(fst_ref[...], yi.astype(BF16), preferred_element_type=F32))

    @pl.when(s == kf_count - 1)
    def _():
        v = _short_conv(yv_ref[...], wv_ref[...], bv_ref[...])
        x1 = _short_conv(y1_ref[...], w1_ref[...], b1_ref[...])
        z = x1 * (acc_scr[...] * (1.0 / seq_len) + v * skip_ref[0:1, :])
        z_scr[...] = z
        u_scr[...] = z.astype(BF16)
        acc_scr[...] = jnp.zeros_like(acc_scr)

    @pl.when(s == 2 * kf_count - 1)
    def _():
        x2 = _short_conv(y2_ref[...], w2_ref[...], b2_ref[...])
        z = z_scr[...]
        o_ref[...] = (x2 * (acc_scr[...] * (1.0 / seq_len) + z * skip_ref[1:2, :])).astype(BF16)


def _hy_conv(y_in, conv_w, conv_b, skip, mats, hr, hi, *, seq_base, n_seqs, seq_len, tc, tk):
    t, d3 = y_in.shape
    d = d3 // 3
    nct = d // tc
    kfc = seq_len // tk
    fc, fs, fst = mats

    def ymap(part):
        return lambda b, c, s: (seq_base + b, part * nct + c)

    def wmap(part):
        return lambda b, c, s: (0, part * nct + c)

    in_specs = ([pl.BlockSpec((seq_len, tc), ymap(p)) for p in range(3)]
                + [pl.BlockSpec((3, tc), wmap(p)) for p in range(3)]
                + [pl.BlockSpec((1, tc), wmap(p)) for p in range(3)]
                + [pl.BlockSpec((HY_ORDER, tc), lambda b, c, s: (0, c)),
                   pl.BlockSpec((tk, seq_len), lambda b, c, s: (s % kfc, 0)),
                   pl.BlockSpec((tk, seq_len), lambda b, c, s: (s % kfc, 0)),
                   pl.BlockSpec((seq_len, tk), lambda b, c, s: (0, s % kfc)),
                   pl.BlockSpec((seq_len, tk), lambda b, c, s: (0, s % kfc)),
                   pl.BlockSpec((tk, tc), lambda b, c, s: (s % kfc, (s // kfc) * nct + c)),
                   pl.BlockSpec((tk, tc), lambda b, c, s: (s % kfc, (s // kfc) * nct + c))])
    args = [y_in] * 3 + [conv_w] * 3 + [conv_b.reshape(1, d3)] * 3 + [skip, fc, fs, fc, fst, hr, hi]
    del t
    return pl.pallas_call(
        functools.partial(_hy_conv_kernel, kf_count=kfc),
        grid=(n_seqs, nct, 2 * kfc),
        in_specs=in_specs,
        out_specs=pl.BlockSpec((seq_len, tc), lambda b, c, s: (b, c)),
        out_shape=jax.ShapeDtypeStruct((n_seqs * seq_len, d), BF16),
        scratch_shapes=[pltpu.VMEM((seq_len, tc), BF16), pltpu.VMEM((seq_len, tc), F32),
                        pltpu.VMEM((seq_len, tc), F32)],
        compiler_params=_params("parallel", "parallel", "arbitrary"),
        name="hyena_conv",
    )(*args)


def _final_norm_kernel(x_ref, g_ref, o_ref):
    x = x_ref[...]
    o_ref[...] = x * lax.rsqrt(jnp.mean(x * x, axis=-1, keepdims=True) + EPS) * g_ref[...]


def _final_norm(x, g, *, row0, rows, tm):
    d = x.shape[1]
    base = row0 // tm
    return pl.pallas_call(
        _final_norm_kernel,
        grid=(rows // tm,),
        in_specs=[pl.BlockSpec((tm, d), lambda i: (base + i, 0)),
                  pl.BlockSpec((1, d), lambda i: (0, 0))],
        out_specs=pl.BlockSpec((tm, d), lambda i: (i, 0)),
        out_shape=jax.ShapeDtypeStruct((rows, d), F32),
        compiler_params=_params("parallel"),
        name="final_norm",
    )(x, g.reshape(1, d))


def kernel(x_prompt, x_sample, cache_win_k, cache_win_v, cache_ax_k, cache_ax_v, c, c_ctx, norm_mix_g, norm_ffn_g, mod_w, mod_b, win_wqkv, win_wo, win_sink, hy_w_in, hy_conv_w, hy_conv_b, hy_f_w1, hy_f_b1, hy_f_w2, hy_f_b2, hy_f_w3, hy_freq, hy_skip, hy_wo, ax_wqkv, ax_q_g, ax_k_g, ax_wo, ffn_w_gu, ffn_w_down, final_g):
    n_ctx_seqs, ctx_len, d = x_prompt.shape
    lat_batch, lat_len, _ = x_sample.shape
    depth = mod_w.shape[0]
    past = cache_win_k.shape[2]
    n_ctx_rows = n_ctx_seqs * ctx_len
    n_lat_rows = lat_batch * lat_len
    tm = 512
    assert n_ctx_rows % tm == 0 and lat_len % tm == 0 and n_ctx_rows % lat_len == 0
    assert lat_batch + 1 <= MOD_ROWS
    geom = dict(n_ctx_rows=n_ctx_rows, lat_len=lat_len, tm=tm)

    x = jnp.concatenate([x_prompt.reshape(n_ctx_rows, d), x_sample.reshape(n_lat_rows, d)], axis=0)
    cvec = jnp.concatenate([c_ctx[None, :], c, jnp.zeros((MOD_ROWS - 1 - lat_batch, d), F32)], axis=0)
    mod_all = _adaln_all(cvec, mod_w, mod_b)
    cos, sin = _rope_tables(lat_len, tm)

    win_k, win_v, ax_k, ax_v = [], [], [], []
    kv_w = N_KV_HEADS * HEAD_DIM
    q_w = N_HEADS * HEAD_DIM
    for i in range(depth):
        mod = mod_all[i].reshape(MOD_ROWS * 6, 1, d)
        kind, j = i % 3, i // 3
        if kind == 1:
            y = _norm_mod_matmul(x, norm_mix_g[i], mod, 0, hy_w_in[j].astype(BF16), tn=1024, **geom)
            zs = []
            for (sb, ns, sl) in ((0, n_ctx_seqs, ctx_len), (n_ctx_rows // lat_len, lat_batch, lat_len)):
                tk = min(256, sl)
                mats = _dft_mats(sl)
                hr, hi = _hy_filters(sl, hy_f_w1[j], hy_f_b1[j], hy_f_w2[j], hy_f_b2[j], hy_f_w3[j], hy_freq[j],
                                     mats[0], mats[1], tc=256, tk=tk)
                zs.append(_hy_conv(y, hy_conv_w[j], hy_conv_b[j], hy_skip[j], mats, hr, hi,
                                   seq_base=sb, n_seqs=ns, seq_len=sl, tc=256, tk=tk))
            x = _proj_residual(zs[0], zs[1], hy_wo[j].astype(BF16), x, mod, 2, tn=1024, **geom)
        else:
            if kind == 0:
                wqkv, wo, sink, hg = win_wqkv[j], win_wo[j], win_sink[j], None
                ck, cv = cache_win_k[:, j], cache_win_v[:, j]
            else:
                wqkv, wo, sink = ax_wqkv[j], ax_wo[j], jnp.zeros((N_HEADS,), F32)
                hg = jnp.stack([ax_q_g[j], ax_k_g[j]], axis=0)
                ck, cv = cache_ax_k[:, j], cache_ax_v[:, j]
            ck = ck.reshape(lat_batch, past, kv_w)
            cv = cv.reshape(lat_batch, past, kv_w)
            y = _norm_mod_matmul(x, norm_mix_g[i], mod, 0, wqkv.astype(BF16), tn=1024, rope=(cos, sin, hg), **geom)
            o_c = _attn_ctx(y, sink, n_ctx_seqs=n_ctx_seqs, ctx_len=ctx_len, use_sink=kind == 0)
            if kind == 0:
                o_l = _attn_lat_win(y, sink, ck, cv, n_ctx_rows=n_ctx_rows, lat_batch=lat_batch, lat_len=lat_len)
            else:
                o_l = _attn_lat_full(y, ck, cv, n_ctx_rows=n_ctx_rows, lat_batch=lat_batch, lat_len=lat_len)
            k_c = y[:n_ctx_rows, q_w:q_w + kv_w].reshape(n_ctx_seqs, ctx_len, N_KV_HEADS, HEAD_DIM)
            v_c = y[:n_ctx_rows, q_w + kv_w:].reshape(n_ctx_seqs, ctx_len, N_KV_HEADS, HEAD_DIM)
            (win_k if kind == 0 else ax_k).append(k_c)
            (win_v if kind == 0 else ax_v).append(v_c)
            x = _proj_residual(o_c, o_l, wo.astype(BF16), x, mod, 2, tn=1024, **geom)
        x = _ffn(x, norm_ffn_g[i], mod, ffn_w_gu[i].astype(BF16), ffn_w_down[i].astype(BF16), tf=512, **geom)

    y_prompt = _final_norm(x, final_g, row0=0, rows=n_ctx_rows, tm=tm).reshape(n_ctx_seqs, ctx_len, d)
    y_sample = _final_norm(x, final_g, row0=n_ctx_rows, rows=n_lat_rows, tm=tm).reshape(lat_batch, lat_len, d)
    return (y_prompt, y_sample, jnp.stack(win_k, axis=1), jnp.stack(win_v, axis=1),
            jnp.stack(ax_k, axis=1), jnp.stack(ax_v, axis=1))
```

```python
import functools
import math

import jax
import jax.numpy as jnp
from jax import lax
from jax.experimental import pallas as pl
from jax.experimental.pallas import tpu as pltpu

F32 = jnp.float32
BF16 = jnp.bfloat16

N_HEADS = 16
N_KV_HEADS = 4
GROUP = N_HEADS // N_KV_HEADS
HEAD_DIM = 128
WINDOW = 128
BLOCK = 128
GRID_W = 64
ROPE_THETA = 10000.0
HY_ORDER = 2
HY_EMB = 33
HY_TARGET = 1e-2
HY_FAST_DECAY = 0.3
HY_SLOW_DECAY = 1.5
HY_MIN_DECAY = math.log(HY_TARGET) / HY_SLOW_DECAY
HY_MAX_DECAY = math.log(HY_TARGET) / HY_FAST_DECAY
EPS = 1e-6
NEG = -1e30
SCALE = HEAD_DIM ** -0.5
LOG2E = math.log2(math.e)
HALF_Q = GROUP * BLOCK // 2

LANES = 128
MOD_ROWS = 16
VMEM_LIMIT = 56 * 1024 * 1024

NT_DIMS = (((1,), (1,)), ((), ()))


def _params(*sem):
    return pltpu.CompilerParams(dimension_semantics=sem, vmem_limit_bytes=VMEM_LIMIT)


def _silu(x):
    return x * (1.0 / (1.0 + jnp.exp(-x)))


def _mod_row(i, tm, n_ctx_rows, lat_len):
    return jnp.maximum(i * tm - n_ctx_rows + lat_len, 0) // lat_len


def _mod_kernel(c_ref, w_ref, b_ref, o_ref):
    s = _silu(c_ref[...]).astype(BF16)
    o_ref[0] = jnp.dot(s, w_ref[0].astype(BF16), preferred_element_type=F32) + b_ref[0]


def _adaln_all(cvec, mod_w, mod_b):
    depth, d, n = mod_w.shape
    tn = 1024
    return pl.pallas_call(
        _mod_kernel,
        grid=(depth, n // tn),
        in_specs=[pl.BlockSpec((MOD_ROWS, d), lambda l, j: (0, 0)),
                  pl.BlockSpec((1, d, tn), lambda l, j: (l, 0, j)),
                  pl.BlockSpec((1, 1, tn), lambda l, j: (l, 0, j))],
        out_specs=pl.BlockSpec((1, MOD_ROWS, tn), lambda l, j: (l, 0, j)),
        out_shape=jax.ShapeDtypeStruct((depth, MOD_ROWS, n), F32),
        compiler_params=_params("parallel", "parallel"),
        name="adaln_mod",
    )(cvec, mod_w, mod_b.reshape(depth, 1, n))


def _norm_mod(x, g, shift, scale):
    y = x * lax.rsqrt(jnp.mean(x * x, axis=-1, keepdims=True) + EPS) * g
    return y * (1.0 + scale) + shift


def _rope_rotate(a, cos, sin_signed):
    lane = lax.broadcasted_iota(jnp.int32, a.shape, 1)
    partner = jnp.where((lane & 32) == 0, pltpu.roll(a, 96, 1), pltpu.roll(a, 32, 1))
    return a * cos + partner * sin_signed


def _nmm_kernel(*refs, mode, heads_per_tile):
    if mode == "plain":
        x_ref, g_ref, sh_ref, sc_ref, w_ref, o_ref, h_scr = refs
    else:
        x_ref, g_ref, sh_ref, sc_ref, w_ref, cos_ref, sin_ref, hg_ref, o_ref, h_scr = refs
    j = pl.program_id(1)

    @pl.when(j == 0)
    def _():
        h_scr[...] = _norm_mod(x_ref[...], g_ref[...], sh_ref[0], sc_ref[0]).astype(BF16)

    acc = jnp.dot(h_scr[...], w_ref[...], preferred_element_type=F32)
    if mode == "plain":
        o_ref[...] = acc
        return
    cos = cos_ref[...]
    sin = sin_ref[...]
    for hh in range(heads_per_tile):
        head = j * heads_per_tile + hh
        a = acc[:, hh * HEAD_DIM:(hh + 1) * HEAD_DIM]
        r = a
        if mode == "normrope":
            gain = jnp.where(head < N_HEADS, hg_ref[0:1, :], hg_ref[1:2, :])
            r = r * lax.rsqrt(jnp.mean(r * r, axis=-1, keepdims=True) + EPS) * gain
        r = _rope_rotate(r, cos, sin)
        o_ref[:, hh * HEAD_DIM:(hh + 1) * HEAD_DIM] = jnp.where(head < N_HEADS + N_KV_HEADS, r, a)


def _norm_mod_matmul(x, g, mod, which, w, *, n_ctx_rows, lat_len, tm, tn, rope=None):
    t, d = x.shape
    n = w.shape[1]
    row = functools.partial(_mod_row, tm=tm, n_ctx_rows=n_ctx_rows, lat_len=lat_len)
    in_specs = [pl.BlockSpec((tm, d), lambda i, j: (i, 0)),
                pl.BlockSpec((1, d), lambda i, j: (0, 0)),
                pl.BlockSpec((1, 1, d), lambda i, j: (row(i) * 6 + which, 0, 0)),
                pl.BlockSpec((1, 1, d), lambda i, j: (row(i) * 6 + which + 1, 0, 0)),
                pl.BlockSpec((d, tn), lambda i, j: (0, j))]
    args = [x, g.reshape(1, d), mod, mod, w]
    mode = "plain"
    if rope is not None:
        cos, sin, hg = rope
        mode = "rope" if hg is None else "normrope"
        if hg is None:
            hg = jnp.ones((2, HEAD_DIM), F32)
        blocks_per_seq = lat_len // tm

        def tab(i, j):
            lat_blk = jnp.maximum(i * tm - n_ctx_rows, 0) // tm
            return (jnp.where(i * tm < n_ctx_rows, 0, 1 + lat_blk % blocks_per_seq), 0)

        in_specs += [pl.BlockSpec((tm, HEAD_DIM), tab), pl.BlockSpec((tm, HEAD_DIM), tab),
                     pl.BlockSpec((2, HEAD_DIM), lambda i, j: (0, 0))]
        args += [cos, sin, hg]
    return pl.pallas_call(
        functools.partial(_nmm_kernel, mode=mode, heads_per_tile=tn // HEAD_DIM),
        grid=(t // tm, n // tn),
        in_specs=in_specs,
        out_specs=pl.BlockSpec((tm, tn), lambda i, j: (i, j)),
        out_shape=jax.ShapeDtypeStruct((t, n), F32),
        scratch_shapes=[pltpu.VMEM((tm, d), BF16)],
        compiler_params=_params("parallel", "arbitrary"),
        name="norm_mod_matmul_" + mode,
    )(*args)


def _proj_res_kernel(ac_ref, al_ref, w_ref, x_ref, gate_ref, o_ref, *, n_ctx_blocks):
    i = pl.program_id(0)

    def emit(a_ref):
        acc = jnp.dot(a_ref[...], w_ref[...], preferred_element_type=F32)
        o_ref[...] = x_ref[...] + gate_ref[0] * acc

    pl.when(i < n_ctx_blocks)(lambda: emit(ac_ref))
    pl.when(i >= n_ctx_blocks)(lambda: emit(al_ref))


def _proj_residual(a_ctx, a_lat, w, x, mod, which, *, n_ctx_rows, lat_len, tm, tn):
    t, d = x.shape
    k = w.shape[0]
    ncb = n_ctx_rows // tm
    row = functools.partial(_mod_row, tm=tm, n_ctx_rows=n_ctx_rows, lat_len=lat_len)
    return pl.pallas_call(
        functools.partial(_proj_res_kernel, n_ctx_blocks=ncb),
        grid=(t // tm, d // tn),
        in_specs=[pl.BlockSpec((tm, k), lambda i, j: (jnp.minimum(i, ncb - 1), 0)),
                  pl.BlockSpec((tm, k), lambda i, j: (jnp.maximum(i - ncb, 0), 0)),
                  pl.BlockSpec((k, tn), lambda i, j: (0, j)),
                  pl.BlockSpec((tm, tn), lambda i, j: (i, j)),
                  pl.BlockSpec((1, 1, tn), lambda i, j: (row(i) * 6 + which, 0, j))],
        out_specs=pl.BlockSpec((tm, tn), lambda i, j: (i, j)),
        out_shape=jax.ShapeDtypeStruct((t, d), F32),
        compiler_params=_params("parallel", "arbitrary"),
        name="proj_residual",
    )(a_ctx, a_lat, w, x, mod)


def _ffn_kernel(x_ref, g_ref, sh_ref, sc_ref, gate_ref, wg_ref, wu_ref, wd_ref, o_ref, h_scr, acc_scr):
    f = pl.program_id(1)

    @pl.when(f == 0)
    def _():
        h_scr[...] = _norm_mod(x_ref[...], g_ref[...], sh_ref[0], sc_ref[0]).astype(BF16)
        acc_scr[...] = jnp.zeros_like(acc_scr)

    h = h_scr[...]
    gv = jnp.dot(h, wg_ref[...], preferred_element_type=F32)
    uv = jnp.dot(h, wu_ref[...], preferred_element_type=F32)
    act = (_silu(gv) * uv).astype(BF16)
    acc_scr[...] += jnp.dot(act, wd_ref[...], preferred_element_type=F32)

    @pl.when(f == pl.num_programs(1) - 1)
    def _():
        o_ref[...] = x_ref[...] + gate_ref[0] * acc_scr[...]


def _ffn(x, g, mod, w_gu, w_down, *, n_ctx_rows, lat_len, tm, tf):
    t, d = x.shape
    ff = w_down.shape[0]
    nf = ff // tf
    row = functools.partial(_mod_row, tm=tm, n_ctx_rows=n_ctx_rows, lat_len=lat_len)
    return pl.pallas_call(
        _ffn_kernel,
        grid=(t // tm, nf),
        in_specs=[pl.BlockSpec((tm, d), lambda i, f: (i, 0)),
                  pl.BlockSpec((1, d), lambda i, f: (0, 0)),
                  pl.BlockSpec((1, 1, d), lambda i, f: (row(i) * 6 + 3, 0, 0)),
                  pl.BlockSpec((1, 1, d), lambda i, f: (row(i) * 6 + 4, 0, 0)),
                  pl.BlockSpec((1, 1, d), lambda i, f: (row(i) * 6 + 5, 0, 0)),
                  pl.BlockSpec((d, tf), lambda i, f: (0, f)),
                  pl.BlockSpec((d, tf), lambda i, f: (0, nf + f)),
                  pl.BlockSpec((tf, d), lambda i, f: (f, 0))],
        out_specs=pl.BlockSpec((tm, d), lambda i, f: (i, 0)),
        out_shape=jax.ShapeDtypeStruct((t, d), F32),
        scratch_shapes=[pltpu.VMEM((tm, d), BF16), pltpu.VMEM((tm, d), F32)],
        compiler_params=_params("parallel", "arbitrary"),
        name="ffn_swiglu",
    )(x, g.reshape(1, d), mod, mod, mod, w_gu, w_gu, w_down)


def _stack_heads(q):
    return jnp.concatenate([q[:, h * HEAD_DIM:(h + 1) * HEAD_DIM] for h in range(GROUP)], axis=0)


def _unstack_heads(o, rows):
    return jnp.concatenate([o[h * rows:(h + 1) * rows] for h in range(GROUP)], axis=1)


def _sink_column(sink_ref, kh, rows):
    head = lax.broadcasted_iota(jnp.int32, (GROUP * rows, 1), 0) // rows
    col = jnp.zeros((GROUP * rows, 1), F32)
    for h in range(GROUP):
        col = jnp.where(head == h, sink_ref[kh * GROUP + h], col)
    return col


def _softmax_av(s, v, sink_col):
    m = jnp.max(s, axis=-1, keepdims=True)
    if sink_col is not None:
        m = jnp.maximum(m, sink_col)
    p = jnp.exp(s - m)
    l = jnp.sum(p, axis=-1, keepdims=True)
    if sink_col is not None:
        l = l + jnp.exp(sink_col - m)
    return jnp.dot(p.astype(BF16), v, preferred_element_type=F32) / l


def _attn_ctx_kernel(sink_ref, q_ref, k_ref, v_ref, o_ref, *, use_sink):
    rows = q_ref.shape[0]
    qs = _stack_heads(q_ref[...] * SCALE).astype(BF16)
    s = lax.dot_general(qs, k_ref[...].astype(BF16), NT_DIMS, preferred_element_type=F32)
    sink_col = _sink_column(sink_ref, pl.program_id(1), rows) if use_sink else None
    o = _softmax_av(s, v_ref[...].astype(BF16), sink_col)
    o_ref[...] = _unstack_heads(o, rows).astype(BF16)


def _attn_ctx(y, sink, *, n_ctx_seqs, ctx_len, use_sink):
    t = n_ctx_seqs * ctx_len
    qw = GROUP * HEAD_DIM
    return pl.pallas_call(
        functools.partial(_attn_ctx_kernel, use_sink=use_sink),
        grid=(n_ctx_seqs, N_KV_HEADS),
        in_specs=[pl.BlockSpec(memory_space=pltpu.SMEM),
                  pl.BlockSpec((ctx_len, qw), lambda b, kh: (b, kh)),
                  pl.BlockSpec((ctx_len, HEAD_DIM), lambda b, kh: (b, N_HEADS + kh)),
                  pl.BlockSpec((ctx_len, HEAD_DIM), lambda b, kh: (b, N_HEADS + N_KV_HEADS + kh))],
        out_specs=pl.BlockSpec((ctx_len, qw), lambda b, kh: (b, kh)),
        out_shape=jax.ShapeDtypeStruct((t, N_HEADS * HEAD_DIM), BF16),
        compiler_params=_params("parallel", "parallel"),
        name="attn_context",
    )(sink, y, y, y)


def _keys_by_queries_attention(qs, k_scr, vt_scr, s_scr, chunks, bias_ref, sink2, o_ref):
    hw = HALF_Q

    def logits(hf, ci):
        st, sz = chunks[ci]
        st_ = lax.dot_general(k_scr[st:st + sz, :], qs[hf * hw:(hf + 1) * hw, :], NT_DIMS,
                              preferred_element_type=F32)
        if bias_ref is not None and ci == 0:
            st_ = st_ + bias_ref[...]
        s_scr[st:st + sz, hf * hw:(hf + 1) * hw] = st_
        return jnp.max(st_, axis=0, keepdims=True)

    def weigh(hf, ci, m):
        st, sz = chunks[ci]
        p = jnp.exp2(s_scr[st:st + sz, hf * hw:(hf + 1) * hw] - m)
        pv = jnp.dot(vt_scr[:, st:st + sz], p.astype(BF16), preferred_element_type=F32)
        return jnp.sum(p, axis=0, keepdims=True), pv

    def row_max(parts, hf):
        m = functools.reduce(jnp.maximum, parts)
        return m if sink2 is None else jnp.maximum(m, sink2[:, hf * hw:(hf + 1) * hw])

    def finish(hf, m, l, acc):
        if sink2 is not None:
            l = l + jnp.exp2(sink2[:, hf * hw:(hf + 1) * hw] - m)
        ot = acc * (1.0 / l)
        for hh in range(hw // BLOCK):
            col = (hf * (hw // BLOCK) + hh) * HEAD_DIM
            o_ref[:, col:col + HEAD_DIM] = ot[:, hh * BLOCK:(hh + 1) * BLOCK].T.astype(BF16)

    n = len(chunks)
    m0 = row_max([logits(0, ci) for ci in range(n)], 0)
    parts1, l0, acc0 = [], None, None
    for ci in range(n):
        parts1.append(logits(1, ci))
        l, pv = weigh(0, ci, m0)
        l0, acc0 = (l, pv) if l0 is None else (l0 + l, acc0 + pv)
    finish(0, m0, l0, acc0)
    m1 = row_max(parts1, 1)
    l1, acc1 = None, None
    for ci in range(n):
        l, pv = weigh(1, ci, m1)
        l1, acc1 = (l, pv) if l1 is None else (l1 + l, acc1 + pv)
    finish(1, m1, l1, acc1)


def _sink_row(sink_ref, kh):
    head = lax.broadcasted_iota(jnp.int32, (1, GROUP * BLOCK), 1) // BLOCK
    row = jnp.zeros((1, GROUP * BLOCK), F32)
    for h in range(GROUP):
        row = jnp.where(head == h, sink_ref[kh * GROUP + h] * LOG2E, row)
    return row


def _attn_lat_win_kernel(sink_ref, q_ref, kp_ref, kc_ref, kn_ref, vp_ref, vc_ref, vn_ref, ck_ref, cv_ref, bias_ref,
                         o_ref, k_scr, vt_scr, s_scr):
    past = ck_ref.shape[0]
    win = 3 * BLOCK

    @pl.when(pl.program_id(2) == 0)
    def _():
        k_scr[win:, :] = ck_ref[...].astype(BF16)
        vt_scr[:, win:] = cv_ref[...].T.astype(BF16)

    for n, (k_ref, v_ref) in enumerate(((kp_ref, vp_ref), (kc_ref, vc_ref), (kn_ref, vn_ref))):
        k_scr[n * BLOCK:(n + 1) * BLOCK, :] = k_ref[...].astype(BF16)
        vt_scr[:, n * BLOCK:(n + 1) * BLOCK] = v_ref[...].T.astype(BF16)
    qs = _stack_heads(q_ref[...] * (SCALE * LOG2E)).astype(BF16)
    _keys_by_queries_attention(qs, k_scr, vt_scr, s_scr, ((0, win), (win, past)), bias_ref,
                               _sink_row(sink_ref, pl.program_id(1)), o_ref)


def _window_bias(nb):
    c = jnp.arange(3 * BLOCK, dtype=jnp.int32)[:, None]
    r = jnp.arange(HALF_Q, dtype=jnp.int32)[None, :] % BLOCK
    band = jnp.abs(r + BLOCK - c) <= WINDOW
    variants = [band & (c >= BLOCK), band, band & (c < 2 * BLOCK)]
    if nb == 1:
        variants = [variants[0] & (c < 2 * BLOCK)] * 3
    return jnp.where(jnp.stack(variants), 0.0, NEG).astype(F32)


def _attn_lat_win(y, sink, cache_k, cache_v, *, n_ctx_rows, lat_batch, lat_len):
    qw = GROUP * HEAD_DIM
    nb = lat_len // BLOCK
    base = n_ctx_rows // BLOCK
    past = cache_k.shape[1]
    n_keys = 3 * BLOCK + past

    def qmap(b, kh, qb):
        return (base + b * nb + qb, kh)

    def kvmap(col0, off):
        def f(b, kh, qb):
            return (base + b * nb + jnp.clip(qb + off, 0, nb - 1), col0 + kh)
        return f

    kcol, vcol = N_HEADS, N_HEADS + N_KV_HEADS
    blk = lambda m: pl.BlockSpec((BLOCK, HEAD_DIM), m)
    cache_spec = pl.BlockSpec((None, past, HEAD_DIM), lambda b, kh, qb: (b, 0, kh))
    return pl.pallas_call(
        _attn_lat_win_kernel,
        grid=(lat_batch, N_KV_HEADS, nb),
        in_specs=[pl.BlockSpec(memory_space=pltpu.SMEM),
                  pl.BlockSpec((BLOCK, qw), qmap),
                  blk(kvmap(kcol, -1)), blk(kvmap(kcol, 0)), blk(kvmap(kcol, 1)),
                  blk(kvmap(vcol, -1)), blk(kvmap(vcol, 0)), blk(kvmap(vcol, 1)),
                  cache_spec, cache_spec,
                  pl.BlockSpec((None, 3 * BLOCK, HALF_Q),
                               lambda b, kh, qb: (jnp.where(qb == 0, 0, jnp.where(qb == nb - 1, 2, 1)), 0, 0))],
        out_specs=pl.BlockSpec((BLOCK, qw), lambda b, kh, qb: (b * nb + qb, kh)),
        out_shape=jax.ShapeDtypeStruct((lat_batch * lat_len, N_HEADS * HEAD_DIM), BF16),
        scratch_shapes=[pltpu.VMEM((n_keys, HEAD_DIM), BF16), pltpu.VMEM((HEAD_DIM, n_keys), BF16),
                        pltpu.VMEM((n_keys, GROUP * BLOCK), F32)],
        compiler_params=_params("parallel", "parallel", "arbitrary"),
        name="attn_latent_window",
    )(sink, y, y, y, y, y, y, y, cache_k, cache_v, _window_bias(nb))


def _attn_lat_full_kernel(q_ref, k_ref, v_ref, ck_ref, cv_ref, o_ref, k_scr, vt_scr, s_scr, *, chunk):
    lat_len = k_ref.shape[0]
    n_keys = k_scr.shape[0]

    @pl.when(pl.program_id(2) == 0)
    def _():
        k_scr[0:lat_len, :] = k_ref[...].astype(BF16)
        k_scr[lat_len:, :] = ck_ref[...].astype(BF16)
        vt_scr[:, 0:lat_len] = v_ref[...].T.astype(BF16)
        vt_scr[:, lat_len:] = cv_ref[...].T.astype(BF16)

    qs = _stack_heads(q_ref[...] * (SCALE * LOG2E)).astype(BF16)
    chunks = tuple((st, chunk) for st in range(0, n_keys, chunk))
    _keys_by_queries_attention(qs, k_scr, vt_scr, s_scr, chunks, None, None, o_ref)


def _attn_lat_full(y, cache_k, cache_v, *, n_ctx_rows, lat_batch, lat_len):
    qw = GROUP * HEAD_DIM
    nb = lat_len // BLOCK
    base = n_ctx_rows // BLOCK
    seq_base = n_ctx_rows // lat_len
    past = cache_k.shape[1]

    def qmap(b, kh, qb):
        return (base + b * nb + qb, kh)

    cache_spec = pl.BlockSpec((None, past, HEAD_DIM), lambda b, kh, qb: (b, 0, kh))
    n_keys = lat_len + past
    chunk = 512
    assert n_keys % chunk == 0
    return pl.pallas_call(
        functools.partial(_attn_lat_full_kernel, chunk=chunk),
        grid=(lat_batch, N_KV_HEADS, nb),
        in_specs=[pl.BlockSpec((BLOCK, qw), qmap),
                  pl.BlockSpec((lat_len, HEAD_DIM), lambda b, kh, qb: (seq_base + b, N_HEADS + kh)),
                  pl.BlockSpec((lat_len, HEAD_DIM), lambda b, kh, qb: (seq_base + b, N_HEADS + N_KV_HEADS + kh)),
                  cache_spec, cache_spec],
        out_specs=pl.BlockSpec((BLOCK, qw), lambda b, kh, qb: (b * nb + qb, kh)),
        out_shape=jax.ShapeDtypeStruct((lat_batch * lat_len, N_HEADS * HEAD_DIM), BF16),
        scratch_shapes=[pltpu.VMEM((n_keys, HEAD_DIM), BF16), pltpu.VMEM((HEAD_DIM, n_keys), BF16),
                        pltpu.VMEM((n_keys, GROUP * BLOCK), F32)],
        compiler_params=_params("parallel", "parallel", "arbitrary"),
        name="attn_latent_full",
    )(y, y, y, cache_k, cache_v)


def _rope_tables(lat_len, tm):
    half, quarter = HEAD_DIM // 2, HEAD_DIM // 4
    n_rows = lat_len // GRID_W
    row = jnp.repeat(jnp.arange(n_rows, dtype=F32), GRID_W)
    col = jnp.tile(jnp.arange(GRID_W, dtype=F32), n_rows)
    inv = ROPE_THETA ** (-jnp.arange(0, half, 2, dtype=F32) / half)
    ang_r, ang_c = row[:, None] * inv, col[:, None] * inv
    cos = jnp.concatenate([jnp.cos(ang_r)] * 2 + [jnp.cos(ang_c)] * 2, axis=-1)
    sin = jnp.concatenate([-jnp.sin(ang_r), jnp.sin(ang_r), -jnp.sin(ang_c), jnp.sin(ang_c)], axis=-1)
    del quarter
    cos = jnp.concatenate([jnp.ones((tm, HEAD_DIM), F32), cos], axis=0)
    sin = jnp.concatenate([jnp.zeros((tm, HEAD_DIM), F32), sin], axis=0)
    return cos, sin


def _dft_mats(seq_len, tk):
    n = 2 * seq_len
    k = jnp.arange(seq_len, dtype=jnp.int32)
    ang = ((k[:, None] * k[None, :]) % n).astype(F32) * (2.0 * math.pi / n)
    fc = jnp.cos(ang)
    fs = -jnp.sin(ang)
    sign = jnp.where(k % 2 == 0, 1.0, -1.0).astype(F32)
    fs = jnp.where(k[:, None] == 0, sign[None, :], fs)
    kfc = seq_len // tk
    ana = jnp.concatenate([fc.reshape(kfc, tk, seq_len), fs.reshape(kfc, tk, seq_len)], axis=1)
    syn = jnp.concatenate([fc.reshape(seq_len, kfc, tk), fs.T.reshape(seq_len, kfc, tk)], axis=-1)
    syn = syn.transpose(1, 0, 2)
    return ana.astype(BF16), syn.astype(BF16)


def _hy_filter_kernel(feat_ref, w1_ref, b1_ref, w2_ref, b2_ref, fq_ref, w3f_ref, w3b_ref, tn_ref, dl_ref,
                      ana_ref, hr_ref, hi_ref, p_scr, q_scr, inv_scr, nyq_scr):
    kf = pl.program_id(1)
    hp = lax.Precision.HIGHEST
    tk = hr_ref.shape[0]

    @pl.when(kf == 0)
    def _():
        a = jnp.dot(feat_ref[...], w1_ref[...], precision=hp, preferred_element_type=F32) + b1_ref[...]
        a = jnp.sin(fq_ref[0:1, :] * a)
        a = jnp.dot(a, w2_ref[...], precision=hp, preferred_element_type=F32) + b2_ref[...]
        a = jnp.sin(fq_ref[1:2, :] * a)
        decay = jnp.exp(-tn_ref[...] * dl_ref[...])
        fwd = jnp.dot(a, w3f_ref[...], precision=hp, preferred_element_type=F32) * decay
        bwd = jnp.dot(a, w3b_ref[...], precision=hp, preferred_element_type=F32) * decay
        rowi = lax.broadcasted_iota(jnp.int32, fwd.shape, 0)
        bwd = jnp.where(rowi == 0, 0.0, bwd)
        norm = jnp.sum(jnp.abs(fwd), axis=0, keepdims=True) + jnp.sum(jnp.abs(bwd), axis=0, keepdims=True) + EPS
        inv_scr[...] = 1.0 / norm
        p = fwd + bwd
        nyq_scr[...] = jnp.sum(jnp.where((rowi & 1) == 0, p, -p), axis=0, keepdims=True)
        p_scr[...] = p.astype(BF16)
        q_scr[...] = (fwd - bwd).astype(BF16)

    inv = inv_scr[...]
    hr_ref[...] = jnp.dot(ana_ref[0:tk, :], p_scr[...], preferred_element_type=F32) * inv
    hi = jnp.dot(ana_ref[tk:, :], q_scr[...], preferred_element_type=F32)
    rowk = lax.broadcasted_iota(jnp.int32, hi.shape, 0)
    hi = jnp.where((rowk == 0) & (kf == 0), nyq_scr[...], hi)
    hi_ref[...] = hi * inv


def _hy_filters(seq_len, f_w1, f_b1, f_w2, f_b2, f_w3, freq, ana_mats, *, tc):
    tk = ana_mats.shape[1] // 2
    hid = f_w1.shape[1]
    nd = f_w3.shape[1] // 2
    d = nd // HY_ORDER
    t = jnp.arange(seq_len, dtype=F32)
    tnorm = t / max(seq_len - 1, 1)
    bands = (HY_EMB - 1) // 2
    fb = jnp.linspace(1e-4, bands - 1, bands, dtype=F32)
    w = 2.0 * math.pi * t / seq_len
    feats = jnp.concatenate([tnorm[:, None], jnp.cos(w[:, None] * fb), -jnp.sin(w[:, None] * fb)], axis=-1)
    emb = 64
    feats = jnp.pad(feats, ((0, 0), (0, emb - HY_EMB)))
    w1 = jnp.pad(f_w1, ((0, emb - HY_EMB), (0, 0)))
    deltas = jnp.abs(jnp.linspace(HY_MIN_DECAY, HY_MAX_DECAY, d, dtype=F32))
    deltas = jnp.tile(deltas, HY_ORDER).reshape(1, nd)
    nct = nd // tc
    const = lambda shape: pl.BlockSpec(shape, lambda c, kf: (0, 0))
    out = jax.ShapeDtypeStruct((seq_len, nd), F32)
    return pl.pallas_call(
        _hy_filter_kernel,
        grid=(nct, seq_len // tk),
        in_specs=[const((seq_len, emb)), const((emb, hid)), const((1, hid)), const((hid, hid)), const((1, hid)),
                  const((2, hid)),
                  pl.BlockSpec((hid, tc), lambda c, kf: (0, c)),
                  pl.BlockSpec((hid, tc), lambda c, kf: (0, nct + c)),
                  const((seq_len, 1)),
                  pl.BlockSpec((1, tc), lambda c, kf: (0, c)),
                  pl.BlockSpec((None, 2 * tk, seq_len), lambda c, kf: (kf, 0, 0))],
        out_specs=[pl.BlockSpec((tk, tc), lambda c, kf: (kf, c)),
                   pl.BlockSpec((tk, tc), lambda c, kf: (kf, c))],
        out_shape=[out, out],
        scratch_shapes=[pltpu.VMEM((seq_len, tc), BF16), pltpu.VMEM((seq_len, tc), BF16),
                        pltpu.VMEM((1, tc), F32), pltpu.VMEM((1, tc), F32)],
        compiler_params=_params("parallel", "arbitrary"),
        name="hyena_filter",
    )(feats, w1, f_b1.reshape(1, hid), f_w2, f_b2.reshape(1, hid), freq, f_w3, f_w3,
      tnorm.reshape(seq_len, 1), deltas, ana_mats)


def _hy_inproj_kernel(x_ref, g_ref, sh_ref, sc_ref, w_ref, cw_ref, cb_ref, o_ref, h_scr, *, n_ctx_blocks, ctx_len):
    j = pl.program_id(1)

    @pl.when(j == 0)
    def _():
        h_scr[...] = _norm_mod(x_ref[...], g_ref[...], sh_ref[0], sc_ref[0]).astype(BF16)

    y = jnp.dot(h_scr[...], w_ref[...], preferred_element_type=F32)
    n = y.shape[0]
    row = lax.broadcasted_iota(jnp.int32, y.shape, 0)
    is_ctx = pl.program_id(0) < n_ctx_blocks
    pos = row % ctx_len
    first = (row == 0) | (is_ctx & (pos == 0))
    last = (row == n - 1) | (is_ctx & (pos == ctx_len - 1))
    prev = jnp.where(first, 0.0, pltpu.roll(y, 1, 0))
    nxt = jnp.where(last, 0.0, pltpu.roll(y, n - 1, 0))
    o_ref[...] = prev * cw_ref[0:1, :] + y * cw_ref[1:2, :] + nxt * cw_ref[2:3, :] + cb_ref[...]


def _hy_inproj(x, g, mod, w, conv_w, conv_b, *, n_ctx_rows, lat_len, ctx_len, tn):
    t, d = x.shape
    n = w.shape[1]
    tm = lat_len
    row = functools.partial(_mod_row, tm=tm, n_ctx_rows=n_ctx_rows, lat_len=lat_len)
    return pl.pallas_call(
        functools.partial(_hy_inproj_kernel, n_ctx_blocks=n_ctx_rows // tm, ctx_len=ctx_len),
        grid=(t // tm, n // tn),
        in_specs=[pl.BlockSpec((tm, d), lambda i, j: (i, 0), pipeline_mode=pl.Buffered(1)),
                  pl.BlockSpec((1, d), lambda i, j: (0, 0)),
                  pl.BlockSpec((1, 1, d), lambda i, j: (row(i) * 6, 0, 0)),
                  pl.BlockSpec((1, 1, d), lambda i, j: (row(i) * 6 + 1, 0, 0)),
                  pl.BlockSpec((d, tn), lambda i, j: (0, j)),
                  pl.BlockSpec((3, tn), lambda i, j: (0, j)),
                  pl.BlockSpec((1, tn), lambda i, j: (0, j))],
        out_specs=pl.BlockSpec((tm, tn), lambda i, j: (i, j)),
        out_shape=jax.ShapeDtypeStruct((t, n), F32),
        scratch_shapes=[pltpu.VMEM((tm, d), BF16)],
        compiler_params=_params("parallel", "arbitrary"),
        name="hyena_inproj",
    )(x, g.reshape(1, d), mod, mod, w, conv_w, conv_b.reshape(1, n))


def _hy_conv_kernel(a_ref, g_ref, skip_ref, ana_ref, syn_ref, hr_ref, hi_ref, o_ref, acc_scr, u_scr=None, *,
                    seq_len):
    kf = pl.program_id(2)
    rows = a_ref.shape[0]
    tk = hr_ref.shape[0]
    chunk = 256

    @pl.when(kf == 0)
    def _():
        acc_scr[...] = jnp.zeros_like(acc_scr)
        if u_scr is not None:
            for r0 in range(0, rows, chunk):
                u_scr[r0:r0 + chunk, :] = a_ref[r0:r0 + chunk, :].astype(BF16)

    u_ref = a_ref if u_scr is None else u_scr

    hr = hr_ref[...]
    hi = hi_ref[...]
    first = (lax.broadcasted_iota(jnp.int32, hr.shape, 0) == 0) & (kf == 0)
    for seg in range(0, rows, seq_len):
        uf = jnp.dot(ana_ref[...], u_ref[seg:seg + seq_len, :], preferred_element_type=F32)
        ur, ui = uf[0:tk, :], uf[tk:, :]
        yr = ur * hr - ui * hi
        yi = ur * hi + ui * hr
        yr = jnp.where(first, 0.5 * (ur * hr), yr)
        yi = jnp.where(first, 0.5 * (ui * hi), yi)
        yf = jnp.concatenate([yr, yi], axis=0).astype(BF16)
        step = min(seq_len, 512)
        for n0 in range(0, seq_len, step):
            acc_scr[seg + n0:seg + n0 + step, :] += jnp.dot(syn_ref[n0:n0 + step, :], yf,
                                                            preferred_element_type=F32)

    @pl.when(kf == pl.num_programs(2) - 1)
    def _():
        for r0 in range(0, rows, chunk):
            rs = slice(r0, r0 + chunk)
            a = a_ref[rs, :].astype(F32)
            o_ref[rs, :] = (g_ref[rs, :] * (acc_scr[rs, :] * (1.0 / seq_len) + a * skip_ref[...])).astype(BF16)


def _hy_conv(a, a_part, a_base, g, g_part, g_base, skip_row, mats, hr, hi, order, *,
             n_blocks, rows, seq_len, tc):
    d = skip_row.shape[1]
    nct = d // tc
    ana_mats, syn_mats = mats
    kfc = ana_mats.shape[0]
    tk = ana_mats.shape[1] // 2

    def amap(part, base):
        return lambda b, c, kf: (base + b, part * nct + c)

    hmap = lambda b, c, kf: (kf, order * nct + c)
    return pl.pallas_call(
        functools.partial(_hy_conv_kernel, seq_len=seq_len),
        grid=(n_blocks, nct, kfc),
        in_specs=[pl.BlockSpec((rows, tc), amap(a_part, a_base)),
                  pl.BlockSpec((rows, tc), amap(g_part, g_base)),
                  pl.BlockSpec((1, tc), lambda b, c, kf: (0, c)),
                  pl.BlockSpec((None, 2 * tk, seq_len), lambda b, c, kf: (kf, 0, 0)),
                  pl.BlockSpec((None, seq_len, 2 * tk), lambda b, c, kf: (kf, 0, 0)),
                  pl.BlockSpec((tk, tc), hmap), pl.BlockSpec((tk, tc), hmap)],
        out_specs=pl.BlockSpec((rows, tc), lambda b, c, kf: (b, c)),
        out_shape=jax.ShapeDtypeStruct((n_blocks * rows, d), BF16),
        scratch_shapes=[pltpu.VMEM((rows, tc), F32)] + ([] if a.dtype == BF16 else [pltpu.VMEM((rows, tc), BF16)]),
        compiler_params=_params("parallel", "parallel", "arbitrary"),
        name="hyena_conv",
    )(a, g, skip_row, ana_mats, syn_mats, hr, hi)


def _final_norm_kernel(x_ref, g_ref, o_ref):
    x = x_ref[...]
    o_ref[...] = x * lax.rsqrt(jnp.mean(x * x, axis=-1, keepdims=True) + EPS) * g_ref[...]


def _final_norm(x, g, *, row0, rows, tm):
    d = x.shape[1]
    base = row0 // tm
    return pl.pallas_call(
        _final_norm_kernel,
        grid=(rows // tm,),
        in_specs=[pl.BlockSpec((tm, d), lambda i: (base + i, 0)),
                  pl.BlockSpec((1, d), lambda i: (0, 0))],
        out_specs=pl.BlockSpec((tm, d), lambda i: (i, 0)),
        out_shape=jax.ShapeDtypeStruct((rows, d), F32),
        compiler_params=_params("parallel"),
        name="final_norm",
    )(x, g.reshape(1, d))


def kernel(x_prompt, x_sample, cache_win_k, cache_win_v, cache_ax_k, cache_ax_v, c, c_ctx, norm_mix_g, norm_ffn_g, mod_w, mod_b, win_wqkv, win_wo, win_sink, hy_w_in, hy_conv_w, hy_conv_b, hy_f_w1, hy_f_b1, hy_f_w2, hy_f_b2, hy_f_w3, hy_freq, hy_skip, hy_wo, ax_wqkv, ax_q_g, ax_k_g, ax_wo, ffn_w_gu, ffn_w_down, final_g):
    n_ctx_seqs, ctx_len, d = x_prompt.shape
    lat_batch, lat_len, _ = x_sample.shape
    depth = mod_w.shape[0]
    past = cache_win_k.shape[2]
    n_ctx_rows = n_ctx_seqs * ctx_len
    n_lat_rows = lat_batch * lat_len
    tm = 512
    assert n_ctx_rows % tm == 0 and lat_len % tm == 0 and n_ctx_rows % lat_len == 0
    assert lat_batch + 1 <= MOD_ROWS
    geom = dict(n_ctx_rows=n_ctx_rows, lat_len=lat_len, tm=tm)

    x = jnp.concatenate([x_prompt.reshape(n_ctx_rows, d), x_sample.reshape(n_lat_rows, d)], axis=0)
    cvec = jnp.concatenate([c_ctx[None, :], c, jnp.zeros((MOD_ROWS - 1 - lat_batch, d), F32)], axis=0)
    mod_all = _adaln_all(cvec, mod_w, mod_b)
    cos, sin = _rope_tables(lat_len, tm)

    win_k, win_v, ax_k, ax_v = [], [], [], []
    kv_w = N_KV_HEADS * HEAD_DIM
    q_w = N_HEADS * HEAD_DIM
    for i in range(depth):
        mod = mod_all[i].reshape(MOD_ROWS * 6, 1, d)
        kind, j = i % 3, i // 3
        if kind == 1:
            u = _hy_inproj(x, norm_mix_g[i], mod, hy_w_in[j].astype(BF16), hy_conv_w[j], hy_conv_b[j],
                           n_ctx_rows=n_ctx_rows, lat_len=lat_len, ctx_len=ctx_len, tn=256)
            rows = lat_len
            zs = []
            for (base, nblk, sl) in ((0, n_ctx_rows // rows, ctx_len), (n_ctx_rows // rows, lat_batch, lat_len)):
                mats = _dft_mats(sl, min(512, sl))
                hr, hi = _hy_filters(sl, hy_f_w1[j], hy_f_b1[j], hy_f_w2[j], hy_f_b2[j], hy_f_w3[j], hy_freq[j],
                                     mats[0], tc=512)
                conv = functools.partial(_hy_conv, mats=mats, hr=hr, hi=hi, n_blocks=nblk, rows=rows, seq_len=sl,
                                         tc=512)
                z1 = conv(u, 0, base, u, 1, base, skip_row=hy_skip[j, 0].reshape(1, d), order=0)
                zs.append(conv(z1, 0, 0, u, 2, base, skip_row=hy_skip[j, 1].reshape(1, d), order=1))
            x = _proj_residual(zs[0], zs[1], hy_wo[j].astype(BF16), x, mod, 2, tn=1024, **geom)
        else:
            if kind == 0:
                wqkv, wo, sink, hg = win_wqkv[j], win_wo[j], win_sink[j], None
                ck, cv = cache_win_k[:, j], cache_win_v[:, j]
            else:
                wqkv, wo, sink = ax_wqkv[j], ax_wo[j], jnp.zeros((N_HEADS,), F32)
                hg = jnp.stack([ax_q_g[j], ax_k_g[j]], axis=0)
                ck, cv = cache_ax_k[:, j], cache_ax_v[:, j]
            ck = ck.reshape(lat_batch, past, kv_w)
            cv = cv.reshape(lat_batch, past, kv_w)
            y = _norm_mod_matmul(x, norm_mix_g[i], mod, 0, wqkv.astype(BF16), tn=1024, rope=(cos, sin, hg), **geom)
            o_c = _attn_ctx(y, sink, n_ctx_seqs=n_ctx_seqs, ctx_len=ctx_len, use_sink=kind == 0)
            if kind == 0:
                o_l = _attn_lat_win(y, sink, ck, cv, n_ctx_rows=n_ctx_rows, lat_batch=lat_batch, lat_len=lat_len)
            else:
                o_l = _attn_lat_full(y, ck, cv, n_ctx_rows=n_ctx_rows, lat_batch=lat_batch, lat_len=lat_len)
            k_c = y[:n_ctx_rows, q_w:q_w + kv_w].reshape(n_ctx_seqs, ctx_len, N_KV_HEADS, HEAD_DIM)
            v_c = y[:n_ctx_rows, q_w + kv_w:].reshape(n_ctx_seqs, ctx_len, N_KV_HEADS, HEAD_DIM)
            (win_k if kind == 0 else ax_k).append(k_c)
            (win_v if kind == 0 else ax_v).append(v_c)
            x = _proj_residual(o_c, o_l, wo.astype(BF16), x, mod, 2, tn=1024, **geom)
        x = _ffn(x, norm_ffn_g[i], mod, ffn_w_gu[i].astype(BF16), ffn_w_down[i].astype(BF16), tf=512, **geom)

    y_prompt = _final_norm(x, final_g, row0=0, rows=n_ctx_rows, tm=tm).reshape(n_ctx_seqs, ctx_len, d)
    y_sample = _final_norm(x, final_g, row0=n_ctx_rows, rows=n_lat_rows, tm=tm).reshape(lat_batch, lat_len, d)
    return (y_prompt, y_sample, jnp.stack(win_k, axis=1), jnp.stack(win_v, axis=1),
            jnp.stack(ax_k, axis=1), jnp.stack(ax_v, axis=1))
```

```python
import functools
import math

import jax
import jax.numpy as jnp
from jax import lax
from jax.experimental import pallas as pl
from jax.experimental.pallas import tpu as pltpu

F32 = jnp.float32
BF16 = jnp.bfloat16

N_HEADS = 16
N_KV_HEADS = 4
GROUP = N_HEADS // N_KV_HEADS
HEAD_DIM = 128
WINDOW = 128
BLOCK = 128
GRID_W = 64
ROPE_THETA = 10000.0
HY_ORDER = 2
HY_EMB = 33
HY_TARGET = 1e-2
HY_FAST_DECAY = 0.3
HY_SLOW_DECAY = 1.5
HY_MIN_DECAY = math.log(HY_TARGET) / HY_SLOW_DECAY
HY_MAX_DECAY = math.log(HY_TARGET) / HY_FAST_DECAY
EPS = 1e-6
NEG = -1e30
SCALE = HEAD_DIM ** -0.5
LOG2E = math.log2(math.e)
Q_STEP = 2 * BLOCK

LANES = 128
MOD_ROWS = 16
VMEM_LIMIT = 56 * 1024 * 1024

NT_DIMS = (((1,), (1,)), ((), ()))


def _params(*sem):
    return pltpu.CompilerParams(dimension_semantics=sem, vmem_limit_bytes=VMEM_LIMIT)


def _silu(x):
    return x * (1.0 / (1.0 + jnp.exp(-x)))


def _mod_row(i, tm, n_ctx_rows, lat_len):
    return jnp.maximum(i * tm - n_ctx_rows + lat_len, 0) // lat_len


def _mod_kernel(c_ref, w_ref, b_ref, o_ref):
    s = _silu(c_ref[...]).astype(BF16)
    o_ref[0] = jnp.dot(s, w_ref[0].astype(BF16), preferred_element_type=F32) + b_ref[0]


def _adaln_all(cvec, mod_w, mod_b):
    depth, d, n = mod_w.shape
    tn = 1024
    return pl.pallas_call(
        _mod_kernel,
        grid=(depth, n // tn),
        in_specs=[pl.BlockSpec((MOD_ROWS, d), lambda l, j: (0, 0)),
                  pl.BlockSpec((1, d, tn), lambda l, j: (l, 0, j)),
                  pl.BlockSpec((1, 1, tn), lambda l, j: (l, 0, j))],
        out_specs=pl.BlockSpec((1, MOD_ROWS, tn), lambda l, j: (l, 0, j)),
        out_shape=jax.ShapeDtypeStruct((depth, MOD_ROWS, n), F32),
        compiler_params=_params("parallel", "parallel"),
        name="adaln_mod",
    )(cvec, mod_w, mod_b.reshape(depth, 1, n))


def _norm_mod(x, g, shift, scale):
    y = x * lax.rsqrt(jnp.mean(x * x, axis=-1, keepdims=True) + EPS) * g
    return y * (1.0 + scale) + shift


def _rope_rotate(a, cos, sin_signed):
    lane = lax.broadcasted_iota(jnp.int32, a.shape, 1)
    partner = jnp.where((lane & 32) == 0, pltpu.roll(a, 96, 1), pltpu.roll(a, 32, 1))
    return a * cos + partner * sin_signed


def _qkv_kernel(x_ref, g_ref, sh_ref, sc_ref, w_ref, cos_ref, sin_ref, hg_ref, o_ref, *, qk_norm, tn, halves):
    tm = x_ref.shape[0]
    rh = tm // halves
    heads_per_tile = tn // HEAD_DIM
    for hf in range(halves):
        rs = slice(hf * rh, (hf + 1) * rh)
        h = _norm_mod(x_ref[rs, :], g_ref[...], sh_ref[0], sc_ref[0]).astype(BF16)
        cos = cos_ref[rs, :]
        sin = sin_ref[rs, :]
        for jt in range(w_ref.shape[1] // tn):
            acc = jnp.dot(h, w_ref[:, jt * tn:(jt + 1) * tn], preferred_element_type=F32)
            for hh in range(heads_per_tile):
                head = jt * heads_per_tile + hh
                a = acc[:, hh * HEAD_DIM:(hh + 1) * HEAD_DIM]
                if head < N_HEADS + N_KV_HEADS:
                    if qk_norm:
                        gain = hg_ref[0:1, :] if head < N_HEADS else hg_ref[1:2, :]
                        a = a * lax.rsqrt(jnp.mean(a * a, axis=-1, keepdims=True) + EPS) * gain
                    a = _rope_rotate(a, cos, sin)
                o_ref[rs, head * HEAD_DIM:(head + 1) * HEAD_DIM] = a


def _qkv_project(x, g, mod, w, cos, sin, hg, *, n_ctx_rows, lat_len, tm):
    t, d = x.shape
    n = w.shape[1]
    row = functools.partial(_mod_row, tm=tm, n_ctx_rows=n_ctx_rows, lat_len=lat_len)
    blocks_per_seq = lat_len // tm

    def tab(i):
        lat_blk = jnp.maximum(i * tm - n_ctx_rows, 0) // tm
        return (jnp.where(i * tm < n_ctx_rows, 0, 1 + lat_blk % blocks_per_seq), 0)

    qk_norm = hg is not None
    if hg is None:
        hg = jnp.ones((2, HEAD_DIM), F32)
    return pl.pallas_call(
        functools.partial(_qkv_kernel, qk_norm=qk_norm, tn=512, halves=2),
        grid=(t // tm,),
        in_specs=[pl.BlockSpec((tm, d), lambda i: (i, 0)),
                  pl.BlockSpec((1, d), lambda i: (0, 0)),
                  pl.BlockSpec((1, 1, d), lambda i: (row(i) * 6, 0, 0)),
                  pl.BlockSpec((1, 1, d), lambda i: (row(i) * 6 + 1, 0, 0)),
                  pl.BlockSpec((d, n), lambda i: (0, 0), pipeline_mode=pl.Buffered(1)),
                  pl.BlockSpec((tm, HEAD_DIM), tab), pl.BlockSpec((tm, HEAD_DIM), tab),
                  pl.BlockSpec((2, HEAD_DIM), lambda i: (0, 0))],
        out_specs=pl.BlockSpec((tm, n), lambda i: (i, 0)),
        out_shape=jax.ShapeDtypeStruct((t, n), F32),
        compiler_params=_params("parallel"),
        name="qkv_project",
    )(x, g.reshape(1, d), mod, mod, w, cos, sin, hg)


def _proj_res_kernel(ac_ref, al_ref, w_ref, x_ref, gate_ref, g2_ref, sh2_ref, sc2_ref, o_ref, h_ref, *,
                     n_ctx_blocks):
    i = pl.program_id(0)

    def emit(a_ref):
        xn = x_ref[...] + gate_ref[0] * jnp.dot(a_ref[...], w_ref[...], preferred_element_type=F32)
        o_ref[...] = xn
        h_ref[...] = _norm_mod(xn, g2_ref[...], sh2_ref[0], sc2_ref[0]).astype(BF16)

    pl.when(i < n_ctx_blocks)(lambda: emit(ac_ref))
    pl.when(i >= n_ctx_blocks)(lambda: emit(al_ref))


def _proj_residual(a_ctx, a_lat, w, x, mod, g2, *, n_ctx_rows, lat_len, tm):
    t, d = x.shape
    k = w.shape[0]
    ncb = n_ctx_rows // tm
    row = functools.partial(_mod_row, tm=tm, n_ctx_rows=n_ctx_rows, lat_len=lat_len)
    modspec = lambda which: pl.BlockSpec((1, 1, d), lambda i: (row(i) * 6 + which, 0, 0))
    return pl.pallas_call(
        functools.partial(_proj_res_kernel, n_ctx_blocks=ncb),
        grid=(t // tm,),
        in_specs=[pl.BlockSpec((tm, k), lambda i: (jnp.minimum(i, ncb - 1), 0)),
                  pl.BlockSpec((tm, k), lambda i: (jnp.maximum(i - ncb, 0), 0)),
                  pl.BlockSpec((k, d), lambda i: (0, 0), pipeline_mode=pl.Buffered(1)),
                  pl.BlockSpec((tm, d), lambda i: (i, 0)),
                  modspec(2),
                  pl.BlockSpec((1, d), lambda i: (0, 0)),
                  modspec(3), modspec(4)],
        out_specs=[pl.BlockSpec((tm, d), lambda i: (i, 0)), pl.BlockSpec((tm, d), lambda i: (i, 0))],
        out_shape=[jax.ShapeDtypeStruct((t, d), F32), jax.ShapeDtypeStruct((t, d), BF16)],
        compiler_params=_params("parallel"),
        name="proj_residual",
    )(a_ctx, a_lat, w, x, mod, g2.reshape(1, d), mod, mod)


def _ffn_kernel(x_ref, h_ref, gate_ref, wg_ref, wu_ref, wd_ref, o_ref, acc_scr):
    f = pl.program_id(1)

    @pl.when(f == 0)
    def _():
        acc_scr[...] = jnp.zeros_like(acc_scr)

    h = h_ref[...]
    gv = jnp.dot(h, wg_ref[...], preferred_element_type=F32)
    uv = jnp.dot(h, wu_ref[...], preferred_element_type=F32)
    act = (_silu(gv) * uv).astype(BF16)
    acc_scr[...] += jnp.dot(act, wd_ref[...], preferred_element_type=F32)

    @pl.when(f == pl.num_programs(1) - 1)
    def _():
        o_ref[...] = x_ref[...] + gate_ref[0] * acc_scr[...]


def _ffn(x, h, mod, w_gu, w_down, *, n_ctx_rows, lat_len, tm, tf):
    t, d = x.shape
    ff = w_down.shape[0]
    nf = ff // tf
    row = functools.partial(_mod_row, tm=tm, n_ctx_rows=n_ctx_rows, lat_len=lat_len)
    return pl.pallas_call(
        _ffn_kernel,
        grid=(t // tm, nf),
        in_specs=[pl.BlockSpec((tm, d), lambda i, f: (i, 0)),
                  pl.BlockSpec((tm, d), lambda i, f: (i, 0)),
                  pl.BlockSpec((1, 1, d), lambda i, f: (row(i) * 6 + 5, 0, 0)),
                  pl.BlockSpec((d, tf), lambda i, f: (0, f)),
                  pl.BlockSpec((d, tf), lambda i, f: (0, nf + f)),
                  pl.BlockSpec((tf, d), lambda i, f: (f, 0))],
        out_specs=pl.BlockSpec((tm, d), lambda i, f: (i, 0)),
        out_shape=jax.ShapeDtypeStruct((t, d), F32),
        scratch_shapes=[pltpu.VMEM((tm, d), F32)],
        compiler_params=_params("parallel", "arbitrary"),
        name="ffn_swiglu",
    )(x, h, mod, w_gu, w_gu, w_down)


def _stack_heads(q):
    return jnp.concatenate([q[:, h * HEAD_DIM:(h + 1) * HEAD_DIM] for h in range(GROUP)], axis=0)


def _unstack_heads(o, rows):
    return jnp.concatenate([o[h * rows:(h + 1) * rows] for h in range(GROUP)], axis=1)


def _sink_column(sink_ref, kh, rows):
    head = lax.broadcasted_iota(jnp.int32, (GROUP * rows, 1), 0) // rows
    col = jnp.zeros((GROUP * rows, 1), F32)
    for h in range(GROUP):
        col = jnp.where(head == h, sink_ref[kh * GROUP + h], col)
    return col


def _softmax_av(s, v, sink_col):
    m = jnp.max(s, axis=-1, keepdims=True)
    if sink_col is not None:
        m = jnp.maximum(m, sink_col)
    p = jnp.exp(s - m)
    l = jnp.sum(p, axis=-1, keepdims=True)
    if sink_col is not None:
        l = l + jnp.exp(sink_col - m)
    return jnp.dot(p.astype(BF16), v, preferred_element_type=F32) / l


def _attn_ctx_kernel(sink_ref, q_ref, k_ref, v_ref, o_ref, *, use_sink):
    rows = q_ref.shape[0]
    qs = _stack_heads(q_ref[...] * SCALE).astype(BF16)
    s = lax.dot_general(qs, k_ref[...].astype(BF16), NT_DIMS, preferred_element_type=F32)
    sink_col = _sink_column(sink_ref, pl.program_id(1), rows) if use_sink else None
    o = _softmax_av(s, v_ref[...].astype(BF16), sink_col)
    o_ref[...] = _unstack_heads(o, rows).astype(BF16)


def _attn_ctx(y, sink, *, n_ctx_seqs, ctx_len, use_sink):
    t = n_ctx_seqs * ctx_len
    qw = GROUP * HEAD_DIM
    return pl.pallas_call(
        functools.partial(_attn_ctx_kernel, use_sink=use_sink),
        grid=(n_ctx_seqs, N_KV_HEADS),
        in_specs=[pl.BlockSpec(memory_space=pltpu.SMEM),
                  pl.BlockSpec((ctx_len, qw), lambda b, kh: (b, kh)),
                  pl.BlockSpec((ctx_len, HEAD_DIM), lambda b, kh: (b, N_HEADS + kh)),
                  pl.BlockSpec((ctx_len, HEAD_DIM), lambda b, kh: (b, N_HEADS + N_KV_HEADS + kh))],
        out_specs=pl.BlockSpec((ctx_len, qw), lambda b, kh: (b, kh)),
        out_shape=jax.ShapeDtypeStruct((t, N_HEADS * HEAD_DIM), BF16),
        compiler_params=_params("parallel", "parallel"),
        name="attn_context",
    )(sink, y, y, y)


def _keys_by_queries_attention(q_ref, k_scr, vt_scr, s_scr, chunks, bias_ref, sink_ref, o_ref):
    qn = q_ref.shape[0]
    kh = pl.program_id(1)

    def logits(h, ci):
        st, sz = chunks[ci]
        q = (q_ref[:, h * HEAD_DIM:(h + 1) * HEAD_DIM] * (SCALE * LOG2E)).astype(BF16)
        st_ = lax.dot_general(k_scr[st:st + sz, :], q, NT_DIMS, preferred_element_type=F32)
        if bias_ref is not None and ci == 0:
            st_ = st_ + bias_ref[...]
        s_scr[st:st + sz, h * qn:(h + 1) * qn] = st_
        return jnp.max(st_, axis=0, keepdims=True)

    def weigh(h, ci, m):
        st, sz = chunks[ci]
        p = jnp.exp2(s_scr[st:st + sz, h * qn:(h + 1) * qn] - m)
        pv = jnp.dot(vt_scr[:, st:st + sz], p.astype(BF16), preferred_element_type=F32)
        return jnp.sum(p, axis=0, keepdims=True), pv

    n = len(chunks)
    maxes = [logits(0, ci) for ci in range(n)]
    for h in range(GROUP):
        m = functools.reduce(jnp.maximum, maxes)
        if sink_ref is not None:
            sink2 = sink_ref[kh * GROUP + h] * LOG2E
            m = jnp.maximum(m, sink2)
        maxes, l, acc = [], None, None
        for ci in range(n):
            if h + 1 < GROUP:
                maxes.append(logits(h + 1, ci))
            lc, pv = weigh(h, ci, m)
            l, acc = (lc, pv) if l is None else (l + lc, acc + pv)
        if sink_ref is not None:
            l = l + jnp.exp2(sink2 - m)
        o_ref[:, h * HEAD_DIM:(h + 1) * HEAD_DIM] = (acc * (1.0 / l)).T.astype(BF16)


def _attn_lat_win_kernel(sink_ref, q_ref, kp_ref, kc_ref, kn_ref, vp_ref, vc_ref, vn_ref, ck_ref, cv_ref, bias_ref,
                         o_ref, k_scr, vt_scr, s_scr):
    past = ck_ref.shape[0]
    win = Q_STEP + 2 * BLOCK

    @pl.when(pl.program_id(2) == 0)
    def _():
        k_scr[win:, :] = ck_ref[...].astype(BF16)
        vt_scr[:, win:] = cv_ref[...].T.astype(BF16)

    row = 0
    for k_ref, v_ref in ((kp_ref, vp_ref), (kc_ref, vc_ref), (kn_ref, vn_ref)):
        n = k_ref.shape[0]
        k_scr[row:row + n, :] = k_ref[...].astype(BF16)
        vt_scr[:, row:row + n] = v_ref[...].T.astype(BF16)
        row += n
    _keys_by_queries_attention(q_ref, k_scr, vt_scr, s_scr, ((0, win), (win, past)), bias_ref, sink_ref, o_ref)


def _window_bias(n_steps):
    win = Q_STEP + 2 * BLOCK
    c = jnp.arange(win, dtype=jnp.int32)[:, None]
    r = jnp.arange(Q_STEP, dtype=jnp.int32)[None, :]
    band = jnp.abs(r + BLOCK - c) <= WINDOW
    variants = [band & (c >= BLOCK), band, band & (c < win - BLOCK)]
    if n_steps == 1:
        variants = [variants[0] & (c < win - BLOCK)] * 3
    return jnp.where(jnp.stack(variants), 0.0, NEG).astype(F32)


def _attn_lat_win(y, sink, cache_k, cache_v, *, n_ctx_rows, lat_batch, lat_len):
    qw = GROUP * HEAD_DIM
    nb = lat_len // BLOCK
    ns = lat_len // Q_STEP
    per = Q_STEP // BLOCK
    base = n_ctx_rows // BLOCK
    past = cache_k.shape[1]
    n_keys = Q_STEP + 2 * BLOCK + past

    def edge(col0, blk):
        return pl.BlockSpec((BLOCK, HEAD_DIM),
                            lambda b, kh, qs: (base + b * nb + jnp.clip(qs * per + blk, 0, nb - 1), col0 + kh))

    def centre(col0):
        return pl.BlockSpec((Q_STEP, HEAD_DIM), lambda b, kh, qs: (base // per + b * ns + qs, col0 + kh))

    kcol, vcol = N_HEADS, N_HEADS + N_KV_HEADS
    cache_spec = pl.BlockSpec((None, past, HEAD_DIM), lambda b, kh, qs: (b, 0, kh))
    return pl.pallas_call(
        _attn_lat_win_kernel,
        grid=(lat_batch, N_KV_HEADS, ns),
        in_specs=[pl.BlockSpec(memory_space=pltpu.SMEM),
                  pl.BlockSpec((Q_STEP, qw), lambda b, kh, qs: (base // per + b * ns + qs, kh)),
                  edge(kcol, -1), centre(kcol), edge(kcol, per),
                  edge(vcol, -1), centre(vcol), edge(vcol, per),
                  cache_spec, cache_spec,
                  pl.BlockSpec((None, Q_STEP + 2 * BLOCK, Q_STEP),
                               lambda b, kh, qs: (jnp.where(qs == 0, 0, jnp.where(qs == ns - 1, 2, 1)), 0, 0))],
        out_specs=pl.BlockSpec((Q_STEP, qw), lambda b, kh, qs: (b * ns + qs, kh)),
        out_shape=jax.ShapeDtypeStruct((lat_batch * lat_len, N_HEADS * HEAD_DIM), BF16),
        scratch_shapes=[pltpu.VMEM((n_keys, HEAD_DIM), BF16), pltpu.VMEM((HEAD_DIM, n_keys), BF16),
                        pltpu.VMEM((n_keys, GROUP * Q_STEP), F32)],
        compiler_params=_params("parallel", "parallel", "arbitrary"),
        name="attn_latent_window",
    )(sink, y, y, y, y, y, y, y, cache_k, cache_v, _window_bias(ns))


def _attn_lat_full_kernel(q_ref, k_ref, v_ref, ck_ref, cv_ref, o_ref, k_scr, vt_scr, s_scr, *, chunk):
    lat_len = k_ref.shape[0]
    n_keys = k_scr.shape[0]

    @pl.when(pl.program_id(2) == 0)
    def _():
        k_scr[0:lat_len, :] = k_ref[...].astype(BF16)
        k_scr[lat_len:, :] = ck_ref[...].astype(BF16)
        vt_scr[:, 0:lat_len] = v_ref[...].T.astype(BF16)
        vt_scr[:, lat_len:] = cv_ref[...].T.astype(BF16)

    chunks = tuple((st, chunk) for st in range(0, n_keys, chunk))
    _keys_by_queries_attention(q_ref, k_scr, vt_scr, s_scr, chunks, None, None, o_ref)


def _attn_lat_full(y, cache_k, cache_v, *, n_ctx_rows, lat_batch, lat_len):
    qw = GROUP * HEAD_DIM
    ns = lat_len // Q_STEP
    base = n_ctx_rows // Q_STEP
    seq_base = n_ctx_rows // lat_len
    past = cache_k.shape[1]
    cache_spec = pl.BlockSpec((None, past, HEAD_DIM), lambda b, kh, qs: (b, 0, kh))
    n_keys = lat_len + past
    chunk = 512
    assert n_keys % chunk == 0
    return pl.pallas_call(
        functools.partial(_attn_lat_full_kernel, chunk=chunk),
        grid=(lat_batch, N_KV_HEADS, ns),
        in_specs=[pl.BlockSpec((Q_STEP, qw), lambda b, kh, qs: (base + b * ns + qs, kh)),
                  pl.BlockSpec((lat_len, HEAD_DIM), lambda b, kh, qs: (seq_base + b, N_HEADS + kh)),
                  pl.BlockSpec((lat_len, HEAD_DIM), lambda b, kh, qs: (seq_base + b, N_HEADS + N_KV_HEADS + kh)),
                  cache_spec, cache_spec],
        out_specs=pl.BlockSpec((Q_STEP, qw), lambda b, kh, qs: (b * ns + qs, kh)),
        out_shape=jax.ShapeDtypeStruct((lat_batch * lat_len, N_HEADS * HEAD_DIM), BF16),
        scratch_shapes=[pltpu.VMEM((n_keys, HEAD_DIM), BF16), pltpu.VMEM((HEAD_DIM, n_keys), BF16),
                        pltpu.VMEM((n_keys, GROUP * Q_STEP), F32)],
        compiler_params=_params("parallel", "parallel", "arbitrary"),
        name="attn_latent_full",
    )(y, y, y, cache_k, cache_v)


def _rope_tables(lat_len, tm):
    half, quarter = HEAD_DIM // 2, HEAD_DIM // 4
    n_rows = lat_len // GRID_W
    row = jnp.repeat(jnp.arange(n_rows, dtype=F32), GRID_W)
    col = jnp.tile(jnp.arange(GRID_W, dtype=F32), n_rows)
    inv = ROPE_THETA ** (-jnp.arange(0, half, 2, dtype=F32) / half)
    ang_r, ang_c = row[:, None] * inv, col[:, None] * inv
    cos = jnp.concatenate([jnp.cos(ang_r)] * 2 + [jnp.cos(ang_c)] * 2, axis=-1)
    sin = jnp.concatenate([-jnp.sin(ang_r), jnp.sin(ang_r), -jnp.sin(ang_c), jnp.sin(ang_c)], axis=-1)
    del quarter
    cos = jnp.concatenate([jnp.ones((tm, HEAD_DIM), F32), cos], axis=0)
    sin = jnp.concatenate([jnp.zeros((tm, HEAD_DIM), F32), sin], axis=0)
    return cos, sin


def _dft_mats(seq_len, tk):
    n = 2 * seq_len
    k = jnp.arange(seq_len, dtype=jnp.int32)
    ang = ((k[:, None] * k[None, :]) % n).astype(F32) * (2.0 * math.pi / n)
    fc = jnp.cos(ang)
    fs = -jnp.sin(ang)
    sign = jnp.where(k % 2 == 0, 1.0, -1.0).astype(F32)
    fs = jnp.where(k[:, None] == 0, sign[None, :], fs)
    kfc = seq_len // tk
    ana = jnp.concatenate([fc.reshape(kfc, tk, seq_len), fs.reshape(kfc, tk, seq_len)], axis=1).astype(BF16)
    ana = lax.optimization_barrier(ana)
    return ana, ana.transpose(0, 2, 1)


def _hy_filter_kernel(feat_ref, w1_ref, b1_ref, w2_ref, b2_ref, fq_ref, w3f_ref, w3b_ref, tn_ref, dl_ref,
                      ana_ref, hr_ref, hi_ref, p_scr, q_scr, inv_scr, nyq_scr):
    kf = pl.program_id(1)
    hp = lax.Precision.HIGHEST
    tk = hr_ref.shape[0]

    @pl.when(kf == 0)
    def _():
        a = jnp.dot(feat_ref[...], w1_ref[...], precision=hp, preferred_element_type=F32) + b1_ref[...]
        a = jnp.sin(fq_ref[0:1, :] * a)
        a = jnp.dot(a, w2_ref[...], precision=hp, preferred_element_type=F32) + b2_ref[...]
        a = jnp.sin(fq_ref[1:2, :] * a)
        decay = jnp.exp(-tn_ref[...] * dl_ref[...])
        fwd = jnp.dot(a, w3f_ref[...], precision=hp, preferred_element_type=F32) * decay
        bwd = jnp.dot(a, w3b_ref[...], precision=hp, preferred_element_type=F32) * decay
        rowi = lax.broadcasted_iota(jnp.int32, fwd.shape, 0)
        bwd = jnp.where(rowi == 0, 0.0, bwd)
        norm = jnp.sum(jnp.abs(fwd), axis=0, keepdims=True) + jnp.sum(jnp.abs(bwd), axis=0, keepdims=True) + EPS
        inv_scr[...] = 1.0 / norm
        p = fwd + bwd
        nyq_scr[...] = jnp.sum(jnp.where((rowi & 1) == 0, p, -p), axis=0, keepdims=True)
        p_scr[...] = p.astype(BF16)
        q_scr[...] = (fwd - bwd).astype(BF16)

    inv = inv_scr[...]
    hr_ref[...] = jnp.dot(ana_ref[0:tk, :], p_scr[...], preferred_element_type=F32) * inv
    hi = jnp.dot(ana_ref[tk:, :], q_scr[...], preferred_element_type=F32)
    rowk = lax.broadcasted_iota(jnp.int32, hi.shape, 0)
    hi = jnp.where((rowk == 0) & (kf == 0), nyq_scr[...], hi)
    hi_ref[...] = hi * inv


def _hy_filters(seq_len, f_w1, f_b1, f_w2, f_b2, f_w3, freq, ana_mats, *, tc):
    tk = ana_mats.shape[1] // 2
    hid = f_w1.shape[1]
    nd = f_w3.shape[1] // 2
    d = nd // HY_ORDER
    t = jnp.arange(seq_len, dtype=F32)
    tnorm = t / max(seq_len - 1, 1)
    bands = (HY_EMB - 1) // 2
    fb = jnp.linspace(1e-4, bands - 1, bands, dtype=F32)
    w = 2.0 * math.pi * t / seq_len
    feats = jnp.concatenate([tnorm[:, None], jnp.cos(w[:, None] * fb), -jnp.sin(w[:, None] * fb)], axis=-1)
    emb = 64
    feats = jnp.pad(feats, ((0, 0), (0, emb - HY_EMB)))
    w1 = jnp.pad(f_w1, ((0, emb - HY_EMB), (0, 0)))
    deltas = jnp.abs(jnp.linspace(HY_MIN_DECAY, HY_MAX_DECAY, d, dtype=F32))
    deltas = jnp.tile(deltas, HY_ORDER).reshape(1, nd)
    nct = nd // tc
    const = lambda shape: pl.BlockSpec(shape, lambda c, kf: (0, 0))
    out = jax.ShapeDtypeStruct((seq_len, nd), F32)
    return pl.pallas_call(
        _hy_filter_kernel,
        grid=(nct, seq_len // tk),
        in_specs=[const((seq_len, emb)), const((emb, hid)), const((1, hid)), const((hid, hid)), const((1, hid)),
                  const((2, hid)),
                  pl.BlockSpec((hid, tc), lambda c, kf: (0, c)),
                  pl.BlockSpec((hid, tc), lambda c, kf: (0, nct + c)),
                  const((seq_len, 1)),
                  pl.BlockSpec((1, tc), lambda c, kf: (0, c)),
                  pl.BlockSpec((None, 2 * tk, seq_len), lambda c, kf: (kf, 0, 0))],
        out_specs=[pl.BlockSpec((tk, tc), lambda c, kf: (kf, c)),
                   pl.BlockSpec((tk, tc), lambda c, kf: (kf, c))],
        out_shape=[out, out],
        scratch_shapes=[pltpu.VMEM((seq_len, tc), BF16), pltpu.VMEM((seq_len, tc), BF16),
                        pltpu.VMEM((1, tc), F32), pltpu.VMEM((1, tc), F32)],
        compiler_params=_params("parallel", "arbitrary"),
        name="hyena_filter",
    )(feats, w1, f_b1.reshape(1, hid), f_w2, f_b2.reshape(1, hid), freq, f_w3, f_w3,
      tnorm.reshape(seq_len, 1), deltas, ana_mats)


def _hy_inproj_kernel(x_ref, g_ref, sh_ref, sc_ref, w_ref, cw_ref, cb_ref, o_ref, h_scr, *, n_ctx_blocks, ctx_len):
    j = pl.program_id(1)

    @pl.when(j == 0)
    def _():
        h_scr[...] = _norm_mod(x_ref[...], g_ref[...], sh_ref[0], sc_ref[0]).astype(BF16)

    y = jnp.dot(h_scr[...], w_ref[...], preferred_element_type=F32)
    n = y.shape[0]
    row = lax.broadcasted_iota(jnp.int32, y.shape, 0)
    is_ctx = pl.program_id(0) < n_ctx_blocks
    pos = row % ctx_len
    first = (row == 0) | (is_ctx & (pos == 0))
    last = (row == n - 1) | (is_ctx & (pos == ctx_len - 1))
    prev = jnp.where(first, 0.0, pltpu.roll(y, 1, 0))
    nxt = jnp.where(last, 0.0, pltpu.roll(y, n - 1, 0))
    o_ref[...] = prev * cw_ref[0:1, :] + y * cw_ref[1:2, :] + nxt * cw_ref[2:3, :] + cb_ref[...]


def _hy_inproj(x, g, mod, w, conv_w, conv_b, *, n_ctx_rows, lat_len, ctx_len, tn):
    t, d = x.shape
    n = w.shape[1]
    tm = lat_len
    row = functools.partial(_mod_row, tm=tm, n_ctx_rows=n_ctx_rows, lat_len=lat_len)
    return pl.pallas_call(
        functools.partial(_hy_inproj_kernel, n_ctx_blocks=n_ctx_rows // tm, ctx_len=ctx_len),
        grid=(t // tm, n // tn),
        in_specs=[pl.BlockSpec((tm, d), lambda i, j: (i, 0), pipeline_mode=pl.Buffered(1)),
                  pl.BlockSpec((1, d), lambda i, j: (0, 0)),
                  pl.BlockSpec((1, 1, d), lambda i, j: (row(i) * 6, 0, 0)),
                  pl.BlockSpec((1, 1, d), lambda i, j: (row(i) * 6 + 1, 0, 0)),
                  pl.BlockSpec((d, tn), lambda i, j: (0, j)),
                  pl.BlockSpec((3, tn), lambda i, j: (0, j)),
                  pl.BlockSpec((1, tn), lambda i, j: (0, j))],
        out_specs=pl.BlockSpec((tm, tn), lambda i, j: (i, j)),
        out_shape=jax.ShapeDtypeStruct((t, n), F32),
        scratch_shapes=[pltpu.VMEM((tm, d), BF16)],
        compiler_params=_params("parallel", "arbitrary"),
        name="hyena_inproj",
    )(x, g.reshape(1, d), mod, mod, w, conv_w, conv_b.reshape(1, n))


def _hy_conv_kernel(a_ref, g_ref, skip_ref, ana_ref, syn_ref, hr_ref, hi_ref, o_ref, acc_scr, u_scr=None, *,
                    seq_len):
    kf = pl.program_id(2)
    rows = a_ref.shape[0]
    tk = hr_ref.shape[0]
    chunk = 256

    @pl.when(kf == 0)
    def _():
        acc_scr[...] = jnp.zeros_like(acc_scr)
        if u_scr is not None:
            for r0 in range(0, rows, chunk):
                u_scr[r0:r0 + chunk, :] = a_ref[r0:r0 + chunk, :].astype(BF16)

    u_ref = a_ref if u_scr is None else u_scr

    hr = hr_ref[...]
    hi = hi_ref[...]
    first = (lax.broadcasted_iota(jnp.int32, hr.shape, 0) == 0) & (kf == 0)
    for seg in range(0, rows, seq_len):
        uf = jnp.dot(ana_ref[...], u_ref[seg:seg + seq_len, :], preferred_element_type=F32)
        ur, ui = uf[0:tk, :], uf[tk:, :]
        yr = ur * hr - ui * hi
        yi = ur * hi + ui * hr
        yr = jnp.where(first, 0.5 * (ur * hr), yr)
        yi = jnp.where(first, 0.5 * (ui * hi), yi)
        yf = jnp.concatenate([yr, yi], axis=0).astype(BF16)
        step = min(seq_len, 512)
        for n0 in range(0, seq_len, step):
            acc_scr[seg + n0:seg + n0 + step, :] += jnp.dot(syn_ref[n0:n0 + step, :], yf,
                                                            preferred_element_type=F32)

    @pl.when(kf == pl.num_programs(2) - 1)
    def _():
        for r0 in range(0, rows, chunk):
            rs = slice(r0, r0 + chunk)
            a = a_ref[rs, :].astype(F32)
            o_ref[rs, :] = (g_ref[rs, :] * (acc_scr[rs, :] * (1.0 / seq_len) + a * skip_ref[...])).astype(BF16)


def _hy_conv(a, a_part, a_base, g, g_part, g_base, skip_row, mats, hr, hi, order, *,
             n_blocks, rows, seq_len, tc):
    d = skip_row.shape[1]
    nct = d // tc
    ana_mats, syn_mats = mats
    kfc = ana_mats.shape[0]
    tk = ana_mats.shape[1] // 2

    def amap(part, base):
        return lambda b, c, kf: (base + b, part * nct + c)

    hmap = lambda b, c, kf: (kf, order * nct + c)
    return pl.pallas_call(
        functools.partial(_hy_conv_kernel, seq_len=seq_len),
        grid=(n_blocks, nct, kfc),
        in_specs=[pl.BlockSpec((rows, tc), amap(a_part, a_base)),
                  pl.BlockSpec((rows, tc), amap(g_part, g_base)),
                  pl.BlockSpec((1, tc), lambda b, c, kf: (0, c)),
                  pl.BlockSpec((None, 2 * tk, seq_len), lambda b, c, kf: (kf, 0, 0)),
                  pl.BlockSpec((None, seq_len, 2 * tk), lambda b, c, kf: (kf, 0, 0)),
                  pl.BlockSpec((tk, tc), hmap), pl.BlockSpec((tk, tc), hmap)],
        out_specs=pl.BlockSpec((rows, tc), lambda b, c, kf: (b, c)),
        out_shape=jax.ShapeDtypeStruct((n_blocks * rows, d), BF16),
        scratch_shapes=[pltpu.VMEM((rows, tc), F32)] + ([] if a.dtype == BF16 else [pltpu.VMEM((rows, tc), BF16)]),
        compiler_params=_params("parallel", "parallel", "arbitrary"),
        name="hyena_conv",
    )(a, g, skip_row, ana_mats, syn_mats, hr, hi)


def _final_norm_kernel(x_ref, g_ref, o_ref):
    x = x_ref[...]
    o_ref[...] = x * lax.rsqrt(jnp.mean(x * x, axis=-1, keepdims=True) + EPS) * g_ref[...]


def _final_norm(x, g, *, row0, rows, tm):
    d = x.shape[1]
    base = row0 // tm
    return pl.pallas_call(
        _final_norm_kernel,
        grid=(rows // tm,),
        in_specs=[pl.BlockSpec((tm, d), lambda i: (base + i, 0)),
                  pl.BlockSpec((1, d), lambda i: (0, 0))],
        out_specs=pl.BlockSpec((tm, d), lambda i: (i, 0)),
        out_shape=jax.ShapeDtypeStruct((rows, d), F32),
        compiler_params=_params("parallel"),
        name="final_norm",
    )(x, g.reshape(1, d))


def kernel(x_prompt, x_sample, cache_win_k, cache_win_v, cache_ax_k, cache_ax_v, c, c_ctx, norm_mix_g, norm_ffn_g, mod_w, mod_b, win_wqkv, win_wo, win_sink, hy_w_in, hy_conv_w, hy_conv_b, hy_f_w1, hy_f_b1, hy_f_w2, hy_f_b2, hy_f_w3, hy_freq, hy_skip, hy_wo, ax_wqkv, ax_q_g, ax_k_g, ax_wo, ffn_w_gu, ffn_w_down, final_g):
    n_ctx_seqs, ctx_len, d = x_prompt.shape
    lat_batch, lat_len, _ = x_sample.shape
    depth = mod_w.shape[0]
    past = cache_win_k.shape[2]
    n_ctx_rows = n_ctx_seqs * ctx_len
    n_lat_rows = lat_batch * lat_len
    tm = 512
    assert n_ctx_rows % tm == 0 and lat_len % tm == 0 and n_ctx_rows % lat_len == 0
    assert lat_batch + 1 <= MOD_ROWS
    geom = dict(n_ctx_rows=n_ctx_rows, lat_len=lat_len, tm=tm)

    x = jnp.concatenate([x_prompt.reshape(n_ctx_rows, d), x_sample.reshape(n_lat_rows, d)], axis=0)
    cvec = jnp.concatenate([c_ctx[None, :], c, jnp.zeros((MOD_ROWS - 1 - lat_batch, d), F32)], axis=0)
    mod_all = _adaln_all(cvec, mod_w, mod_b)
    cos, sin = _rope_tables(lat_len, tm)

    win_k, win_v, ax_k, ax_v = [], [], [], []
    kv_w = N_KV_HEADS * HEAD_DIM
    q_w = N_HEADS * HEAD_DIM
    for i in range(depth):
        mod = mod_all[i].reshape(MOD_ROWS * 6, 1, d)
        kind, j = i % 3, i // 3
        if kind == 1:
            u = _hy_inproj(x, norm_mix_g[i], mod, hy_w_in[j].astype(BF16), hy_conv_w[j], hy_conv_b[j],
                           n_ctx_rows=n_ctx_rows, lat_len=lat_len, ctx_len=ctx_len, tn=256)
            rows = lat_len
            zs = []
            for (base, nblk, sl) in ((0, n_ctx_rows // rows, ctx_len), (n_ctx_rows // rows, lat_batch, lat_len)):
                mats = _dft_mats(sl, min(512, sl))
                hr, hi = _hy_filters(sl, hy_f_w1[j], hy_f_b1[j], hy_f_w2[j], hy_f_b2[j], hy_f_w3[j], hy_freq[j],
                                     mats[0], tc=512)
                conv = functools.partial(_hy_conv, mats=mats, hr=hr, hi=hi, n_blocks=nblk, rows=rows, seq_len=sl,
                                         tc=512)
                z1 = conv(u, 0, base, u, 1, base, skip_row=hy_skip[j, 0].reshape(1, d), order=0)
                zs.append(conv(z1, 0, 0, u, 2, base, skip_row=hy_skip[j, 1].reshape(1, d), order=1))
            x, h2 = _proj_residual(zs[0], zs[1], hy_wo[j].astype(BF16), x, mod, norm_ffn_g[i], **geom)
        else:
            if kind == 0:
                wqkv, wo, sink, hg = win_wqkv[j], win_wo[j], win_sink[j], None
                ck, cv = cache_win_k[:, j], cache_win_v[:, j]
            else:
                wqkv, wo, sink = ax_wqkv[j], ax_wo[j], jnp.zeros((N_HEADS,), F32)
                hg = jnp.stack([ax_q_g[j], ax_k_g[j]], axis=0)
                ck, cv = cache_ax_k[:, j], cache_ax_v[:, j]
            ck = ck.reshape(lat_batch, past, kv_w)
            cv = cv.reshape(lat_batch, past, kv_w)
            y = _qkv_project(x, norm_mix_g[i], mod, wqkv.astype(BF16), cos, sin, hg, **geom)
            o_c = _attn_ctx(y, sink, n_ctx_seqs=n_ctx_seqs, ctx_len=ctx_len, use_sink=kind == 0)
            if kind == 0:
                o_l = _attn_lat_win(y, sink, ck, cv, n_ctx_rows=n_ctx_rows, lat_batch=lat_batch, lat_len=lat_len)
            else:
                o_l = _attn_lat_full(y, ck, cv, n_ctx_rows=n_ctx_rows, lat_batch=lat_batch, lat_len=lat_len)
            k_c = y[:n_ctx_rows, q_w:q_w + kv_w].reshape(n_ctx_seqs, ctx_len, N_KV_HEADS, HEAD_DIM)
            v_c = y[:n_ctx_rows, q_w + kv_w:].reshape(n_ctx_seqs, ctx_len, N_KV_HEADS, HEAD_DIM)
            (win_k if kind == 0 else ax_k).append(k_c)
            (win_v if kind == 0 else ax_v).append(v_c)
            x, h2 = _proj_residual(o_c, o_l, wo.astype(BF16), x, mod, norm_ffn_g[i], **geom)
        x = _ffn(x, h2, mod, ffn_w_gu[i].astype(BF16), ffn_w_down[i].astype(BF16), tf=512, **geom)

    y_prompt = _final_norm(x, final_g, row0=0, rows=n_ctx_rows, tm=tm).reshape(n_ctx_seqs, ctx_len, d)
    y_sample = _final_norm(x, final_g, row0=n_ctx_rows, rows=n_lat_rows, tm=tm).reshape(lat_batch, lat_len, d)
    return (y_prompt, y_sample, jnp.stack(win_k, axis=1), jnp.stack(win_v, axis=1),
            jnp.stack(ax_k, axis=1), jnp.stack(ax_v, axis=1))
```

```python
import functools
import math

import jax
import jax.numpy as jnp
from jax import lax
from jax.experimental import pallas as pl
from jax.experimental.pallas import tpu as pltpu

F32 = jnp.float32
BF16 = jnp.bfloat16

N_HEADS = 16
N_KV_HEADS = 4
GROUP = N_HEADS // N_KV_HEADS
HEAD_DIM = 128
WINDOW = 128
BLOCK = 128
GRID_W = 64
ROPE_THETA = 10000.0
HY_ORDER = 2
HY_EMB = 33
HY_TARGET = 1e-2
HY_FAST_DECAY = 0.3
HY_SLOW_DECAY = 1.5
HY_MIN_DECAY = math.log(HY_TARGET) / HY_SLOW_DECAY
HY_MAX_DECAY = math.log(HY_TARGET) / HY_FAST_DECAY
EPS = 1e-6
NEG = -1e30
SCALE = HEAD_DIM ** -0.5
LOG2E = math.log2(math.e)
Q_STEP = 2 * BLOCK

LANES = 128
MOD_ROWS = 16
VMEM_LIMIT = 56 * 1024 * 1024

NT_DIMS = (((1,), (1,)), ((), ()))


def _params(*sem):
    return pltpu.CompilerParams(dimension_semantics=sem, vmem_limit_bytes=VMEM_LIMIT)


def _silu(x):
    return x * (1.0 / (1.0 + jnp.exp(-x)))


def _mod_row(i, tm, n_ctx_rows, lat_len):
    return jnp.maximum(i * tm - n_ctx_rows + lat_len, 0) // lat_len


def _mod_kernel(c_ref, w_ref, b_ref, o_ref):
    s = _silu(c_ref[...]).astype(BF16)
    o_ref[...] = jnp.dot(s, w_ref[0].astype(BF16), preferred_element_type=F32) + b_ref[0]


def _adaln_all(cvec, mod_w, mod_b):
    depth, d, n = mod_w.shape
    tn = 1024
    per = d // tn
    return pl.pallas_call(
        _mod_kernel,
        grid=(depth, n // tn),
        in_specs=[pl.BlockSpec((MOD_ROWS, d), lambda l, j: (0, 0)),
                  pl.BlockSpec((1, d, tn), lambda l, j: (l, 0, j)),
                  pl.BlockSpec((1, 1, tn), lambda l, j: (l, 0, j))],
        out_specs=pl.BlockSpec((None, MOD_ROWS, None, None, tn), lambda l, j: (l, 0, j // per, 0, j % per)),
        out_shape=jax.ShapeDtypeStruct((depth, MOD_ROWS, 6, 1, d), F32),
        compiler_params=_params("parallel", "parallel"),
        name="adaln_mod",
    )(cvec, mod_w, mod_b.reshape(depth, 1, n))


def _norm_mod(x, g, shift, scale):
    y = x * lax.rsqrt(jnp.mean(x * x, axis=-1, keepdims=True) + EPS) * g
    return y * (1.0 + scale) + shift


def _rope_rotate(a, cos, sin_signed):
    lane = lax.broadcasted_iota(jnp.int32, a.shape, 1)
    partner = jnp.where((lane & 32) == 0, pltpu.roll(a, 96, 1), pltpu.roll(a, 32, 1))
    return a * cos + partner * sin_signed


def _two_source_specs(x_ctx, x_lat, tm, n_ctx_blocks):
    d = x_ctx.shape[1]
    lat_base = n_ctx_blocks if x_lat is None else 0
    specs = [pl.BlockSpec((tm, d), lambda i: (jnp.minimum(i, n_ctx_blocks - 1), 0)),
             pl.BlockSpec((tm, d), lambda i: (lat_base + jnp.maximum(i - n_ctx_blocks, 0), 0))]
    return specs, [x_ctx, x_ctx if x_lat is None else x_lat]


def _qkv_kernel(xc_ref, xl_ref, g_ref, sh_ref, sc_ref, w_ref, cos_ref, sin_ref, hg_ref, o_ref, ks_ref, vs_ref, *,
                qk_norm, tn, halves, n_ctx_blocks):
    tm = xc_ref.shape[0]
    rh = tm // halves
    heads_per_tile = tn // HEAD_DIM
    kcol, vcol = N_HEADS * HEAD_DIM, (N_HEADS + N_KV_HEADS) * HEAD_DIM

    def body(x_ref, latent):
        for hf in range(halves):
            rs = slice(hf * rh, (hf + 1) * rh)
            h = _norm_mod(x_ref[rs, :], g_ref[...], sh_ref[0], sc_ref[0]).astype(BF16)
            for jt in range(w_ref.shape[1] // tn):
                acc = jnp.dot(h, w_ref[:, jt * tn:(jt + 1) * tn], preferred_element_type=F32)
                for hh in range(heads_per_tile):
                    head = jt * heads_per_tile + hh
                    a = acc[:, hh * HEAD_DIM:(hh + 1) * HEAD_DIM]
                    if head < N_HEADS + N_KV_HEADS:
                        if qk_norm:
                            gain = hg_ref[0:1, :] if head < N_HEADS else hg_ref[1:2, :]
                            a = a * lax.rsqrt(jnp.mean(a * a, axis=-1, keepdims=True) + EPS) * gain
                        if latent:
                            a = _rope_rotate(a, cos_ref[rs, :], sin_ref[rs, :])
                    o_ref[rs, head * HEAD_DIM:(head + 1) * HEAD_DIM] = a
        if not latent:
            ks_ref[...] = o_ref[:, kcol:vcol]
            vs_ref[...] = o_ref[:, vcol:]

    i = pl.program_id(0)
    pl.when(i < n_ctx_blocks)(lambda: body(xc_ref, False))
    pl.when(i >= n_ctx_blocks)(lambda: body(xl_ref, True))


def _qkv_project(x_ctx, x_lat, g, mod, w, cos, sin, hg, *, n_ctx_rows, lat_len, tm):
    d = x_ctx.shape[1]
    n = w.shape[1]
    ncb = n_ctx_rows // tm
    t = n_ctx_rows + (x_ctx.shape[0] - n_ctx_rows if x_lat is None else x_lat.shape[0])
    kv_w = N_KV_HEADS * HEAD_DIM
    row = functools.partial(_mod_row, tm=tm, n_ctx_rows=n_ctx_rows, lat_len=lat_len)
    tab = lambda i: ((jnp.maximum(i - ncb, 0)) % (lat_len // tm), 0)
    state = lambda i: (jnp.minimum(i, ncb - 1), 0)
    qk_norm = hg is not None
    if hg is None:
        hg = jnp.ones((2, HEAD_DIM), F32)
    x_specs, x_args = _two_source_specs(x_ctx, x_lat, tm, ncb)
    return pl.pallas_call(
        functools.partial(_qkv_kernel, qk_norm=qk_norm, tn=512, halves=2, n_ctx_blocks=ncb),
        grid=(t // tm,),
        in_specs=x_specs + [
            pl.BlockSpec((1, d), lambda i: (0, 0)),
            pl.BlockSpec((1, 1, d), lambda i: (row(i) * 6, 0, 0)),
            pl.BlockSpec((1, 1, d), lambda i: (row(i) * 6 + 1, 0, 0)),
            pl.BlockSpec((d, n), lambda i: (0, 0), pipeline_mode=pl.Buffered(1)),
            pl.BlockSpec((tm, HEAD_DIM), tab), pl.BlockSpec((tm, HEAD_DIM), tab),
            pl.BlockSpec((2, HEAD_DIM), lambda i: (0, 0))],
        out_specs=[pl.BlockSpec((tm, n), lambda i: (i, 0)),
                   pl.BlockSpec((tm, kv_w), state), pl.BlockSpec((tm, kv_w), state)],
        out_shape=[jax.ShapeDtypeStruct((t, n), F32), jax.ShapeDtypeStruct((n_ctx_rows, kv_w), F32),
                   jax.ShapeDtypeStruct((n_ctx_rows, kv_w), F32)],
        compiler_params=_params("arbitrary"),
        name="qkv_project",
    )(*x_args, g.reshape(1, d), mod, mod, w, cos, sin, hg)


def _proj_res_kernel(ac_ref, al_ref, xc_ref, xl_ref, w_ref, gate_ref, g2_ref, sh2_ref, sc2_ref, o_ref, h_ref, *,
                     n_ctx_blocks):
    i = pl.program_id(0)

    def emit(a_ref, x_ref):
        xn = x_ref[...] + gate_ref[0] * jnp.dot(a_ref[...], w_ref[...], preferred_element_type=F32)
        o_ref[...] = xn
        h_ref[...] = _norm_mod(xn, g2_ref[...], sh2_ref[0], sc2_ref[0]).astype(BF16)

    pl.when(i < n_ctx_blocks)(lambda: emit(ac_ref, xc_ref))
    pl.when(i >= n_ctx_blocks)(lambda: emit(al_ref, xl_ref))


def _proj_residual(a_ctx, a_lat, w, x_ctx, x_lat, mod, g2, *, n_ctx_rows, lat_len, tm):
    d = x_ctx.shape[1]
    k = w.shape[0]
    ncb = n_ctx_rows // tm
    t = a_ctx.shape[0] + a_lat.shape[0]
    row = functools.partial(_mod_row, tm=tm, n_ctx_rows=n_ctx_rows, lat_len=lat_len)
    modspec = lambda which: pl.BlockSpec((1, 1, d), lambda i: (row(i) * 6 + which, 0, 0))
    x_specs, x_args = _two_source_specs(x_ctx, x_lat, tm, ncb)
    return pl.pallas_call(
        functools.partial(_proj_res_kernel, n_ctx_blocks=ncb),
        grid=(t // tm,),
        in_specs=[pl.BlockSpec((tm, k), lambda i: (jnp.minimum(i, ncb - 1), 0)),
                  pl.BlockSpec((tm, k), lambda i: (jnp.maximum(i - ncb, 0), 0))]
        + x_specs
        + [pl.BlockSpec((k, d), lambda i: (0, 0), pipeline_mode=pl.Buffered(1)),
           modspec(2),
           pl.BlockSpec((1, d), lambda i: (0, 0)),
           modspec(3), modspec(4)],
        out_specs=[pl.BlockSpec((tm, d), lambda i: (i, 0)), pl.BlockSpec((tm, d), lambda i: (i, 0))],
        out_shape=[jax.ShapeDtypeStruct((t, d), F32), jax.ShapeDtypeStruct((t, d), BF16)],
        compiler_params=_params("parallel"),
        name="proj_residual",
    )(a_ctx, a_lat, *x_args, w, mod, g2.reshape(1, d), mod, mod)


def _ffn_kernel(x_ref, h_ref, gate_ref, wg_ref, wu_ref, wd_ref, o_ref):
    f = pl.program_id(1)

    @pl.when(f == 0)
    def _():
        o_ref[...] = jnp.zeros_like(o_ref)

    h = h_ref[...]
    gv = jnp.dot(h, wg_ref[...], preferred_element_type=F32)
    uv = jnp.dot(h, wu_ref[...], preferred_element_type=F32)
    act = (_silu(gv) * uv).astype(BF16)
    o_ref[...] += jnp.dot(act, wd_ref[...], preferred_element_type=F32)

    @pl.when(f == pl.num_programs(1) - 1)
    def _():
        o_ref[...] = x_ref[...] + gate_ref[0] * o_ref[...]


def _ffn(x, h, mod, w_gu, w_down, *, n_ctx_rows, lat_len, tm, tf):
    t, d = x.shape
    ff = w_down.shape[0]
    nf = ff // tf
    row = functools.partial(_mod_row, tm=tm, n_ctx_rows=n_ctx_rows, lat_len=lat_len)
    return pl.pallas_call(
        _ffn_kernel,
        grid=(t // tm, nf),
        in_specs=[pl.BlockSpec((tm, d), lambda i, f: (i, 0), pipeline_mode=pl.Buffered(1)),
                  pl.BlockSpec((tm, d), lambda i, f: (i, 0)),
                  pl.BlockSpec((1, 1, d), lambda i, f: (row(i) * 6 + 5, 0, 0)),
                  pl.BlockSpec((d, tf), lambda i, f: (0, f)),
                  pl.BlockSpec((d, tf), lambda i, f: (0, nf + f)),
                  pl.BlockSpec((tf, d), lambda i, f: (f, 0))],
        out_specs=pl.BlockSpec((tm, d), lambda i, f: (i, 0)),
        out_shape=jax.ShapeDtypeStruct((t, d), F32),
        compiler_params=_params("parallel", "arbitrary"),
        name="ffn_swiglu",
    )(x, h, mod, w_gu, w_gu, w_down)


def _stack_heads(q):
    return jnp.concatenate([q[:, h * HEAD_DIM:(h + 1) * HEAD_DIM] for h in range(GROUP)], axis=0)


def _unstack_heads(o, rows):
    return jnp.concatenate([o[h * rows:(h + 1) * rows] for h in range(GROUP)], axis=1)


def _sink_column(sink_ref, kh, rows):
    head = lax.broadcasted_iota(jnp.int32, (GROUP * rows, 1), 0) // rows
    col = jnp.zeros((GROUP * rows, 1), F32)
    for h in range(GROUP):
        col = jnp.where(head == h, sink_ref[kh * GROUP + h], col)
    return col


def _softmax_av(s, v, sink_col):
    m = jnp.max(s, axis=-1, keepdims=True)
    if sink_col is not None:
        m = jnp.maximum(m, sink_col)
    p = jnp.exp(s - m)
    l = jnp.sum(p, axis=-1, keepdims=True)
    if sink_col is not None:
        l = l + jnp.exp(sink_col - m)
    return jnp.dot(p.astype(BF16), v, preferred_element_type=F32) / l


def _attn_ctx_kernel(sink_ref, q_ref, k_ref, v_ref, o_ref, *, use_sink):
    rows = q_ref.shape[0]
    qs = _stack_heads(q_ref[...] * SCALE).astype(BF16)
    s = lax.dot_general(qs, k_ref[...].astype(BF16), NT_DIMS, preferred_element_type=F32)
    sink_col = _sink_column(sink_ref, pl.program_id(1), rows) if use_sink else None
    o = _softmax_av(s, v_ref[...].astype(BF16), sink_col)
    o_ref[...] = _unstack_heads(o, rows).astype(BF16)


def _attn_ctx(y, sink, *, n_ctx_seqs, ctx_len, use_sink):
    t = n_ctx_seqs * ctx_len
    qw = GROUP * HEAD_DIM
    return pl.pallas_call(
        functools.partial(_attn_ctx_kernel, use_sink=use_sink),
        grid=(n_ctx_seqs, N_KV_HEADS),
        in_specs=[pl.BlockSpec(memory_space=pltpu.SMEM),
                  pl.BlockSpec((ctx_len, qw), lambda b, kh: (b, kh)),
                  pl.BlockSpec((ctx_len, HEAD_DIM), lambda b, kh: (b, N_HEADS + kh)),
                  pl.BlockSpec((ctx_len, HEAD_DIM), lambda b, kh: (b, N_HEADS + N_KV_HEADS + kh))],
        out_specs=pl.BlockSpec((ctx_len, qw), lambda b, kh: (b, kh)),
        out_shape=jax.ShapeDtypeStruct((t, N_HEADS * HEAD_DIM), BF16),
        compiler_params=_params("parallel", "parallel"),
        name="attn_context",
    )(sink, y, y, y)


def _keys_by_queries_attention(q_ref, k_scr, vt_scr, s_scr, chunks, bias_ref, sink_ref, o_ref):
    qn = q_ref.shape[0]
    kh = pl.program_id(1)

    def logits(h, ci):
        st, sz = chunks[ci]
        q = (q_ref[:, h * HEAD_DIM:(h + 1) * HEAD_DIM] * (SCALE * LOG2E)).astype(BF16)
        st_ = lax.dot_general(k_scr[st:st + sz, :], q, NT_DIMS, preferred_element_type=F32)
        if bias_ref is not None and ci == 0:
            st_ = st_ + bias_ref[...]
        s_scr[st:st + sz, h * qn:(h + 1) * qn] = st_
        return jnp.max(st_, axis=0, keepdims=True)

    def weigh(h, ci, m):
        st, sz = chunks[ci]
        p = jnp.exp2(s_scr[st:st + sz, h * qn:(h + 1) * qn] - m)
        pv = jnp.dot(vt_scr[:, st:st + sz], p.astype(BF16), preferred_element_type=F32)
        return jnp.sum(p, axis=0, keepdims=True), pv

    n = len(chunks)
    maxes = [logits(0, ci) for ci in range(n)]
    for h in range(GROUP):
        m = functools.reduce(jnp.maximum, maxes)
        if sink_ref is not None:
            sink2 = sink_ref[kh * GROUP + h] * LOG2E
            m = jnp.maximum(m, sink2)
        maxes, l, acc = [], None, None
        for ci in range(n):
            if h + 1 < GROUP:
                maxes.append(logits(h + 1, ci))
            lc, pv = weigh(h, ci, m)
            l, acc = (lc, pv) if l is None else (l + lc, acc + pv)
        if sink_ref is not None:
            l = l + jnp.exp2(sink2 - m)
        o_ref[:, h * HEAD_DIM:(h + 1) * HEAD_DIM] = (acc * (1.0 / l)).T.astype(BF16)


def _attn_lat_win_kernel(sink_ref, q_ref, kp_ref, kc_ref, kn_ref, vp_ref, vc_ref, vn_ref, ck_ref, cv_ref, bias_ref,
                         o_ref, k_scr, vt_scr, s_scr):
    past = ck_ref.shape[0]
    win = Q_STEP + 2 * BLOCK

    @pl.when(pl.program_id(2) == 0)
    def _():
        k_scr[win:, :] = ck_ref[...].astype(BF16)
        vt_scr[:, win:] = cv_ref[...].T.astype(BF16)

    row = 0
    for k_ref, v_ref in ((kp_ref, vp_ref), (kc_ref, vc_ref), (kn_ref, vn_ref)):
        n = k_ref.shape[0]
        k_scr[row:row + n, :] = k_ref[...].astype(BF16)
        vt_scr[:, row:row + n] = v_ref[...].T.astype(BF16)
        row += n
    _keys_by_queries_attention(q_ref, k_scr, vt_scr, s_scr, ((0, win), (win, past)), bias_ref, sink_ref, o_ref)


def _window_bias(n_steps):
    win = Q_STEP + 2 * BLOCK
    c = jnp.arange(win, dtype=jnp.int32)[:, None]
    r = jnp.arange(Q_STEP, dtype=jnp.int32)[None, :]
    band = jnp.abs(r + BLOCK - c) <= WINDOW
    variants = [band & (c >= BLOCK), band, band & (c < win - BLOCK)]
    if n_steps == 1:
        variants = [variants[0] & (c < win - BLOCK)] * 3
    return jnp.where(jnp.stack(variants), 0.0, NEG).astype(F32)


def _cache_spec(cache, layer):
    return pl.BlockSpec((None, None, cache.shape[2], HEAD_DIM), lambda b, kh, qs: (b, layer, 0, kh))


def _attn_lat_win(y, sink, cache_k, cache_v, layer, *, n_ctx_rows, lat_batch, lat_len):
    qw = GROUP * HEAD_DIM
    nb = lat_len // BLOCK
    ns = lat_len // Q_STEP
    per = Q_STEP // BLOCK
    base = n_ctx_rows // BLOCK
    past = cache_k.shape[2]
    n_keys = Q_STEP + 2 * BLOCK + past

    def edge(col0, blk):
        return pl.BlockSpec((BLOCK, HEAD_DIM),
                            lambda b, kh, qs: (base + b * nb + jnp.clip(qs * per + blk, 0, nb - 1), col0 + kh))

    def centre(col0):
        return pl.BlockSpec((Q_STEP, HEAD_DIM), lambda b, kh, qs: (base // per + b * ns + qs, col0 + kh))

    kcol, vcol = N_HEADS, N_HEADS + N_KV_HEADS
    cache_spec = _cache_spec(cache_k, layer)
    return pl.pallas_call(
        _attn_lat_win_kernel,
        grid=(lat_batch, N_KV_HEADS, ns),
        in_specs=[pl.BlockSpec(memory_space=pltpu.SMEM),
                  pl.BlockSpec((Q_STEP, qw), lambda b, kh, qs: (base // per + b * ns + qs, kh)),
                  edge(kcol, -1), centre(kcol), edge(kcol, per),
                  edge(vcol, -1), centre(vcol), edge(vcol, per),
                  cache_spec, cache_spec,
                  pl.BlockSpec((None, Q_STEP + 2 * BLOCK, Q_STEP),
                               lambda b, kh, qs: (jnp.where(qs == 0, 0, jnp.where(qs == ns - 1, 2, 1)), 0, 0))],
        out_specs=pl.BlockSpec((Q_STEP, qw), lambda b, kh, qs: (b * ns + qs, kh)),
        out_shape=jax.ShapeDtypeStruct((lat_batch * lat_len, N_HEADS * HEAD_DIM), BF16),
        scratch_shapes=[pltpu.VMEM((n_keys, HEAD_DIM), BF16), pltpu.VMEM((HEAD_DIM, n_keys), BF16),
                        pltpu.VMEM((n_keys, GROUP * Q_STEP), F32)],
        compiler_params=_params("parallel", "parallel", "arbitrary"),
        name="attn_latent_window",
    )(sink, y, y, y, y, y, y, y, cache_k, cache_v, _window_bias(ns))


def _attn_lat_full_kernel(q_ref, k_ref, v_ref, ck_ref, cv_ref, o_ref, k_scr, vt_scr, s_scr, *, chunk):
    lat_len = k_ref.shape[0]
    n_keys = k_scr.shape[0]

    @pl.when(pl.program_id(2) == 0)
    def _():
        k_scr[0:lat_len, :] = k_ref[...].astype(BF16)
        k_scr[lat_len:, :] = ck_ref[...].astype(BF16)
        vt_scr[:, 0:lat_len] = v_ref[...].T.astype(BF16)
        vt_scr[:, lat_len:] = cv_ref[...].T.astype(BF16)

    chunks = tuple((st, chunk) for st in range(0, n_keys, chunk))
    _keys_by_queries_attention(q_ref, k_scr, vt_scr, s_scr, chunks, None, None, o_ref)


def _attn_lat_full(y, cache_k, cache_v, layer, *, n_ctx_rows, lat_batch, lat_len):
    qw = GROUP * HEAD_DIM
    ns = lat_len // Q_STEP
    base = n_ctx_rows // Q_STEP
    seq_base = n_ctx_rows // lat_len
    past = cache_k.shape[2]
    cache_spec = _cache_spec(cache_k, layer)
    n_keys = lat_len + past
    chunk = 512
    assert n_keys % chunk == 0
    return pl.pallas_call(
        functools.partial(_attn_lat_full_kernel, chunk=chunk),
        grid=(lat_batch, N_KV_HEADS, ns),
        in_specs=[pl.BlockSpec((Q_STEP, qw), lambda b, kh, qs: (base + b * ns + qs, kh)),
                  pl.BlockSpec((lat_len, HEAD_DIM), lambda b, kh, qs: (seq_base + b, N_HEADS + kh)),
                  pl.BlockSpec((lat_len, HEAD_DIM), lambda b, kh, qs: (seq_base + b, N_HEADS + N_KV_HEADS + kh)),
                  cache_spec, cache_spec],
        out_specs=pl.BlockSpec((Q_STEP, qw), lambda b, kh, qs: (b * ns + qs, kh)),
        out_shape=jax.ShapeDtypeStruct((lat_batch * lat_len, N_HEADS * HEAD_DIM), BF16),
        scratch_shapes=[pltpu.VMEM((n_keys, HEAD_DIM), BF16), pltpu.VMEM((HEAD_DIM, n_keys), BF16),
                        pltpu.VMEM((n_keys, GROUP * Q_STEP), F32)],
        compiler_params=_params("parallel", "parallel", "arbitrary"),
        name="attn_latent_full",
    )(y, y, y, cache_k, cache_v)


def _rope_tables(lat_len):
    half = HEAD_DIM // 2
    n_rows = lat_len // GRID_W
    row = jnp.repeat(jnp.arange(n_rows, dtype=F32), GRID_W)
    col = jnp.tile(jnp.arange(GRID_W, dtype=F32), n_rows)
    inv = ROPE_THETA ** (-jnp.arange(0, half, 2, dtype=F32) / half)
    ang_r, ang_c = row[:, None] * inv, col[:, None] * inv
    cos = jnp.concatenate([jnp.cos(ang_r)] * 2 + [jnp.cos(ang_c)] * 2, axis=-1)
    sin = jnp.concatenate([-jnp.sin(ang_r), jnp.sin(ang_r), -jnp.sin(ang_c), jnp.sin(ang_c)], axis=-1)
    return cos, sin


def _dft_mats(seq_len, tk):
    n = 2 * seq_len
    k = jnp.arange(seq_len, dtype=jnp.int32)
    step = 64
    a = jnp.arange(seq_len // step, dtype=jnp.int32) * step
    b = jnp.arange(step, dtype=jnp.int32)
    ang_a = ((k[:, None] * a[None, :]) % n).astype(F32) * (2.0 * math.pi / n)
    ang_b = ((k[:, None] * b[None, :]) % n).astype(F32) * (2.0 * math.pi / n)
    ca, sa = jnp.cos(ang_a)[:, :, None], jnp.sin(ang_a)[:, :, None]
    cb, sb = jnp.cos(ang_b)[:, None, :], jnp.sin(ang_b)[:, None, :]
    fc = (ca * cb - sa * sb).reshape(seq_len, seq_len)
    fs = -(sa * cb + ca * sb).reshape(seq_len, seq_len)
    sign = jnp.where(k % 2 == 0, 1.0, -1.0).astype(F32)
    fs = jnp.where(k[:, None] == 0, sign[None, :], fs)
    kfc = seq_len // tk
    ana = jnp.concatenate([fc.reshape(kfc, tk, seq_len), fs.reshape(kfc, tk, seq_len)], axis=1).astype(BF16)
    ana = lax.optimization_barrier(ana)
    return ana, ana.transpose(0, 2, 1)


def _hy_filter_kernel(feat_ref, w1_ref, b1_ref, w2_ref, b2_ref, fq_ref, w3f_ref, w3b_ref, tn_ref, dl_ref,
                      ana_ref, hr_ref, hi_ref, p_scr, q_scr, inv_scr, nyq_scr):
    kf = pl.program_id(1)
    hp = lax.Precision.HIGHEST
    tk = hr_ref.shape[0]

    @pl.when(kf == 0)
    def _():
        a = jnp.dot(feat_ref[...], w1_ref[...], precision=hp, preferred_element_type=F32) + b1_ref[...]
        a = jnp.sin(fq_ref[0:1, :] * a)
        a = jnp.dot(a, w2_ref[...], precision=hp, preferred_element_type=F32) + b2_ref[...]
        a = jnp.sin(fq_ref[1:2, :] * a)
        decay = jnp.exp(-tn_ref[...] * dl_ref[...])
        fwd = jnp.dot(a, w3f_ref[...], precision=hp, preferred_element_type=F32) * decay
        bwd = jnp.dot(a, w3b_ref[...], precision=hp, preferred_element_type=F32) * decay
        rowi = lax.broadcasted_iota(jnp.int32, fwd.shape, 0)
        bwd = jnp.where(rowi == 0, 0.0, bwd)
        norm = jnp.sum(jnp.abs(fwd), axis=0, keepdims=True) + jnp.sum(jnp.abs(bwd), axis=0, keepdims=True) + EPS
        inv_scr[...] = 1.0 / norm
        p = fwd + bwd
        nyq_scr[...] = jnp.sum(jnp.where((rowi & 1) == 0, p, -p), axis=0, keepdims=True)
        p_scr[...] = p.astype(BF16)
        q_scr[...] = (fwd - bwd).astype(BF16)

    inv = inv_scr[...]
    hr_ref[...] = jnp.dot(ana_ref[0:tk, :], p_scr[...], preferred_element_type=F32) * inv
    hi = jnp.dot(ana_ref[tk:, :], q_scr[...], preferred_element_type=F32)
    rowk = lax.broadcasted_iota(jnp.int32, hi.shape, 0)
    hi = jnp.where((rowk == 0) & (kf == 0), nyq_scr[...], hi)
    hi_ref[...] = hi * inv


def _hy_filters(seq_len, f_w1, f_b1, f_w2, f_b2, f_w3, freq, ana_mats, *, tc):
    tk = ana_mats.shape[1] // 2
    hid = f_w1.shape[1]
    nd = f_w3.shape[1] // 2
    d = nd // HY_ORDER
    t = jnp.arange(seq_len, dtype=F32)
    tnorm = t / max(seq_len - 1, 1)
    bands = (HY_EMB - 1) // 2
    fb = jnp.linspace(1e-4, bands - 1, bands, dtype=F32)
    w = 2.0 * math.pi * t / seq_len
    feats = jnp.concatenate([tnorm[:, None], jnp.cos(w[:, None] * fb), -jnp.sin(w[:, None] * fb)], axis=-1)
    emb = 64
    feats = jnp.pad(feats, ((0, 0), (0, emb - HY_EMB)))
    w1 = jnp.pad(f_w1, ((0, emb - HY_EMB), (0, 0)))
    deltas = jnp.abs(jnp.linspace(HY_MIN_DECAY, HY_MAX_DECAY, d, dtype=F32))
    deltas = jnp.tile(deltas, HY_ORDER).reshape(1, nd)
    nct = nd // tc
    const = lambda shape: pl.BlockSpec(shape, lambda c, kf: (0, 0))
    out = jax.ShapeDtypeStruct((seq_len, nd), F32)
    return pl.pallas_call(
        _hy_filter_kernel,
        grid=(nct, seq_len // tk),
        in_specs=[const((seq_len, emb)), const((emb, hid)), const((1, hid)), const((hid, hid)), const((1, hid)),
                  const((2, hid)),
                  pl.BlockSpec((hid, tc), lambda c, kf: (0, c)),
                  pl.BlockSpec((hid, tc), lambda c, kf: (0, nct + c)),
                  const((seq_len, 1)),
                  pl.BlockSpec((1, tc), lambda c, kf: (0, c)),
                  pl.BlockSpec((None, 2 * tk, seq_len), lambda c, kf: (kf, 0, 0))],
        out_specs=[pl.BlockSpec((tk, tc), lambda c, kf: (kf, c)),
                   pl.BlockSpec((tk, tc), lambda c, kf: (kf, c))],
        out_shape=[out, out],
        scratch_shapes=[pltpu.VMEM((seq_len, tc), BF16), pltpu.VMEM((seq_len, tc), BF16),
                        pltpu.VMEM((1, tc), F32), pltpu.VMEM((1, tc), F32)],
        compiler_params=_params("parallel", "arbitrary"),
        name="hyena_filter",
    )(feats, w1, f_b1.reshape(1, hid), f_w2, f_b2.reshape(1, hid), freq, f_w3, f_w3,
      tnorm.reshape(seq_len, 1), deltas, ana_mats)


def _hy_inproj_kernel(x_ref, g_ref, sh_ref, sc_ref, w_ref, cw_ref, cb_ref, o_ref, h_scr, *, n_ctx_blocks, ctx_len):
    j = pl.program_id(1)

    @pl.when(j == 0)
    def _():
        h_scr[...] = _norm_mod(x_ref[...], g_ref[...], sh_ref[0], sc_ref[0]).astype(BF16)

    y = jnp.dot(h_scr[...], w_ref[...], preferred_element_type=F32)
    n = y.shape[0]
    row = lax.broadcasted_iota(jnp.int32, y.shape, 0)
    is_ctx = pl.program_id(0) < n_ctx_blocks
    pos = row % ctx_len
    first = (row == 0) | (is_ctx & (pos == 0))
    last = (row == n - 1) | (is_ctx & (pos == ctx_len - 1))
    prev = jnp.where(first, 0.0, pltpu.roll(y, 1, 0))
    nxt = jnp.where(last, 0.0, pltpu.roll(y, n - 1, 0))
    o_ref[...] = prev * cw_ref[0:1, :] + y * cw_ref[1:2, :] + nxt * cw_ref[2:3, :] + cb_ref[...]


def _hy_inproj(x, g, mod, w, conv_w, conv_b, *, n_ctx_rows, lat_len, ctx_len, tn):
    t, d = x.shape
    n = w.shape[1]
    tm = lat_len
    row = functools.partial(_mod_row, tm=tm, n_ctx_rows=n_ctx_rows, lat_len=lat_len)
    return pl.pallas_call(
        functools.partial(_hy_inproj_kernel, n_ctx_blocks=n_ctx_rows // tm, ctx_len=ctx_len),
        grid=(t // tm, n // tn),
        in_specs=[pl.BlockSpec((tm, d), lambda i, j: (i, 0), pipeline_mode=pl.Buffered(1)),
                  pl.BlockSpec((1, d), lambda i, j: (0, 0)),
                  pl.BlockSpec((1, 1, d), lambda i, j: (row(i) * 6, 0, 0)),
                  pl.BlockSpec((1, 1, d), lambda i, j: (row(i) * 6 + 1, 0, 0)),
                  pl.BlockSpec((d, tn), lambda i, j: (0, j)),
                  pl.BlockSpec((3, tn), lambda i, j: (0, j)),
                  pl.BlockSpec((1, tn), lambda i, j: (0, j))],
        out_specs=pl.BlockSpec((tm, tn), lambda i, j: (i, j)),
        out_shape=jax.ShapeDtypeStruct((t, n), F32),
        scratch_shapes=[pltpu.VMEM((tm, d), BF16)],
        compiler_params=_params("parallel", "arbitrary"),
        name="hyena_inproj",
    )(x, g.reshape(1, d), mod, mod, w, conv_w, conv_b.reshape(1, n))


def _hy_conv_kernel(a_ref, g_ref, skip_ref, ana_ref, syn_ref, hr_ref, hi_ref, o_ref, acc_scr, u_scr=None, *,
                    seq_len):
    kf = pl.program_id(2)
    rows = a_ref.shape[0]
    tk = hr_ref.shape[0]
    chunk = 256

    @pl.when(kf == 0)
    def _():
        acc_scr[...] = jnp.zeros_like(acc_scr)
        if u_scr is not None:
            for r0 in range(0, rows, chunk):
                u_scr[r0:r0 + chunk, :] = a_ref[r0:r0 + chunk, :].astype(BF16)

    u_ref = a_ref if u_scr is None else u_scr

    hr = hr_ref[...]
    hi = hi_ref[...]
    first = (lax.broadcasted_iota(jnp.int32, hr.shape, 0) == 0) & (kf == 0)
    for seg in range(0, rows, seq_len):
        uf = jnp.dot(ana_ref[...], u_ref[seg:seg + seq_len, :], preferred_element_type=F32)
        ur, ui = uf[0:tk, :], uf[tk:, :]
        yr = ur * hr - ui * hi
        yi = ur * hi + ui * hr
        yr = jnp.where(first, 0.5 * (ur * hr), yr)
        yi = jnp.where(first, 0.5 * (ui * hi), yi)
        yf = jnp.concatenate([yr, yi], axis=0).astype(BF16)
        step = min(seq_len, 512)
        for n0 in range(0, seq_len, step):
            acc_scr[seg + n0:seg + n0 + step, :] += jnp.dot(syn_ref[n0:n0 + step, :], yf,
                                                            preferred_element_type=F32)

    @pl.when(kf == pl.num_programs(2) - 1)
    def _():
        for r0 in range(0, rows, chunk):
            rs = slice(r0, r0 + chunk)
            a = a_ref[rs, :].astype(F32)
            o_ref[rs, :] = (g_ref[rs, :] * (acc_scr[rs, :] * (1.0 / seq_len) + a * skip_ref[...])).astype(BF16)


def _hy_conv(a, a_part, a_base, g, g_part, g_base, skip_row, mats, hr, hi, order, *,
             n_blocks, rows, seq_len, tc):
    d = skip_row.shape[1]
    nct = d // tc
    ana_mats, syn_mats = mats
    kfc = ana_mats.shape[0]
    tk = ana_mats.shape[1] // 2

    def amap(part, base):
        return lambda b, c, kf: (base + b, part * nct + c)

    hmap = lambda b, c, kf: (kf, order * nct + c)
    return pl.pallas_call(
        functools.partial(_hy_conv_kernel, seq_len=seq_len),
        grid=(n_blocks, nct, kfc),
        in_specs=[pl.BlockSpec((rows, tc), amap(a_part, a_base)),
                  pl.BlockSpec((rows, tc), amap(g_part, g_base)),
                  pl.BlockSpec((1, tc), lambda b, c, kf: (0, c)),
                  pl.BlockSpec((None, 2 * tk, seq_len), lambda b, c, kf: (kf, 0, 0)),
                  pl.BlockSpec((None, seq_len, 2 * tk), lambda b, c, kf: (kf, 0, 0)),
                  pl.BlockSpec((tk, tc), hmap), pl.BlockSpec((tk, tc), hmap)],
        out_specs=pl.BlockSpec((rows, tc), lambda b, c, kf: (b, c)),
        out_shape=jax.ShapeDtypeStruct((n_blocks * rows, d), BF16),
        scratch_shapes=[pltpu.VMEM((rows, tc), F32)] + ([] if a.dtype == BF16 else [pltpu.VMEM((rows, tc), BF16)]),
        compiler_params=_params("parallel", "parallel", "arbitrary"),
        name="hyena_conv",
    )(a, g, skip_row, ana_mats, syn_mats, hr, hi)


def _final_norm_kernel(x_ref, g_ref, o_ref):
    x = x_ref[...]
    o_ref[...] = x * lax.rsqrt(jnp.mean(x * x, axis=-1, keepdims=True) + EPS) * g_ref[...]


def _final_norm(x, g, *, row0, rows, tm):
    d = x.shape[1]
    base = row0 // tm
    return pl.pallas_call(
        _final_norm_kernel,
        grid=(rows // tm,),
        in_specs=[pl.BlockSpec((tm, d), lambda i: (base + i, 0)),
                  pl.BlockSpec((1, d), lambda i: (0, 0))],
        out_specs=pl.BlockSpec((tm, d), lambda i: (i, 0)),
        out_shape=jax.ShapeDtypeStruct((rows, d), F32),
        compiler_params=_params("parallel"),
        name="final_norm",
    )(x, g.reshape(1, d))


def kernel(x_prompt, x_sample, cache_win_k, cache_win_v, cache_ax_k, cache_ax_v, c, c_ctx, norm_mix_g, norm_ffn_g, mod_w, mod_b, win_wqkv, win_wo, win_sink, hy_w_in, hy_conv_w, hy_conv_b, hy_f_w1, hy_f_b1, hy_f_w2, hy_f_b2, hy_f_w3, hy_freq, hy_skip, hy_wo, ax_wqkv, ax_q_g, ax_k_g, ax_wo, ffn_w_gu, ffn_w_down, final_g):
    n_ctx_seqs, ctx_len, d = x_prompt.shape
    lat_batch, lat_len, _ = x_sample.shape
    depth = mod_w.shape[0]
    past = cache_win_k.shape[2]
    n_ctx_rows = n_ctx_seqs * ctx_len
    n_lat_rows = lat_batch * lat_len
    tm = 512
    tm_ffn = 1024 if (n_ctx_rows % 1024 == 0 and lat_len % 1024 == 0) else tm
    assert n_ctx_rows % tm == 0 and lat_len % tm == 0 and n_ctx_rows % lat_len == 0
    assert lat_batch + 1 <= MOD_ROWS
    geom = dict(n_ctx_rows=n_ctx_rows, lat_len=lat_len, tm=tm)

    x, x_lat = x_prompt.reshape(n_ctx_rows, d), x_sample.reshape(n_lat_rows, d)
    cvec = jnp.concatenate([c_ctx[None, :], c, jnp.zeros((MOD_ROWS - 1 - lat_batch, d), F32)], axis=0)
    mod_all = _adaln_all(cvec, mod_w, mod_b)
    cos, sin = _rope_tables(lat_len)

    win_k, win_v, ax_k, ax_v = [], [], [], []
    kv_w = N_KV_HEADS * HEAD_DIM
    state_shape = (n_ctx_seqs, ctx_len, N_KV_HEADS, HEAD_DIM)
    for i in range(depth):
        mod = mod_all[i].reshape(MOD_ROWS * 6, 1, d)
        kind, j = i % 3, i // 3
        if kind == 1:
            assert x_lat is None
            u = _hy_inproj(x, norm_mix_g[i], mod, hy_w_in[j].astype(BF16), hy_conv_w[j], hy_conv_b[j],
                           n_ctx_rows=n_ctx_rows, lat_len=lat_len, ctx_len=ctx_len, tn=256)
            rows = lat_len
            zs = []
            for (base, nblk, sl) in ((0, n_ctx_rows // rows, ctx_len), (n_ctx_rows // rows, lat_batch, lat_len)):
                mats = _dft_mats(sl, min(512, sl))
                hr, hi = _hy_filters(sl, hy_f_w1[j], hy_f_b1[j], hy_f_w2[j], hy_f_b2[j], hy_f_w3[j], hy_freq[j],
                                     mats[0], tc=512)
                conv = functools.partial(_hy_conv, mats=mats, hr=hr, hi=hi, n_blocks=nblk, rows=rows, seq_len=sl,
                                         tc=512)
                z1 = conv(u, 0, base, u, 1, base, skip_row=hy_skip[j, 0].reshape(1, d), order=0)
                zs.append(conv(z1, 0, 0, u, 2, base, skip_row=hy_skip[j, 1].reshape(1, d), order=1))
            x, h2 = _proj_residual(zs[0], zs[1], hy_wo[j].astype(BF16), x, None, mod, norm_ffn_g[i], **geom)
        else:
            if kind == 0:
                wqkv, wo, sink, hg = win_wqkv[j], win_wo[j], win_sink[j], None
                ck, cv = cache_win_k, cache_win_v
            else:
                wqkv, wo, sink = ax_wqkv[j], ax_wo[j], jnp.zeros((N_HEADS,), F32)
                hg = jnp.stack([ax_q_g[j], ax_k_g[j]], axis=0)
                ck, cv = cache_ax_k, cache_ax_v
            ck = ck[:, j:j + 1].reshape(lat_batch, 1, past, kv_w)
            cv = cv[:, j:j + 1].reshape(lat_batch, 1, past, kv_w)
            y, k_c, v_c = _qkv_project(x, x_lat, norm_mix_g[i], mod, wqkv.astype(BF16), cos, sin, hg, **geom)
            o_c = _attn_ctx(y, sink, n_ctx_seqs=n_ctx_seqs, ctx_len=ctx_len, use_sink=kind == 0)
            lat_geom = dict(n_ctx_rows=n_ctx_rows, lat_batch=lat_batch, lat_len=lat_len)
            if kind == 0:
                o_l = _attn_lat_win(y, sink, ck, cv, 0, **lat_geom)
            else:
                o_l = _attn_lat_full(y, ck, cv, 0, **lat_geom)
            (win_k if kind == 0 else ax_k).append(k_c.reshape(state_shape))
            (win_v if kind == 0 else ax_v).append(v_c.reshape(state_shape))
            x, h2 = _proj_residual(o_c, o_l, wo.astype(BF16), x, x_lat, mod, norm_ffn_g[i], **geom)
            x_lat = None
        x = _ffn(x, h2, mod, ffn_w_gu[i].astype(BF16), ffn_w_down[i].astype(BF16), tf=512,
                 n_ctx_rows=n_ctx_rows, lat_len=lat_len, tm=tm_ffn)

    y_prompt = _final_norm(x, final_g, row0=0, rows=n_ctx_rows, tm=tm).reshape(n_ctx_seqs, ctx_len, d)
    y_sample = _final_norm(x, final_g, row0=n_ctx_rows, rows=n_lat_rows, tm=tm).reshape(lat_batch, lat_len, d)
    return (y_prompt, y_sample, jnp.stack(win_k, axis=1), jnp.stack(win_v, axis=1),
            jnp.stack(ax_k, axis=1), jnp.stack(ax_v, axis=1))
```

```python
import functools
import math

import jax
import jax.numpy as jnp
from jax import lax
from jax.experimental import pallas as pl
from jax.experimental.pallas import tpu as pltpu

F32 = jnp.float32
BF16 = jnp.bfloat16

N_HEADS = 16
N_KV_HEADS = 4
GROUP = N_HEADS // N_KV_HEADS
HEAD_DIM = 128
WINDOW = 128
BLOCK = 128
GRID_W = 64
ROPE_THETA = 10000.0
HY_ORDER = 2
HY_EMB = 33
HY_TARGET = 1e-2
HY_FAST_DECAY = 0.3
HY_SLOW_DECAY = 1.5
HY_MIN_DECAY = math.log(HY_TARGET) / HY_SLOW_DECAY
HY_MAX_DECAY = math.log(HY_TARGET) / HY_FAST_DECAY
EPS = 1e-6
NEG = -1e30
SCALE = HEAD_DIM ** -0.5
LOG2E = math.log2(math.e)
Q_STEP = 2 * BLOCK

LANES = 128
PAD = 8
MOD_ROWS = 16
VMEM_LIMIT = 56 * 1024 * 1024

NT_DIMS = (((1,), (1,)), ((), ()))

def _params(*sem):
    return pltpu.CompilerParams(dimension_semantics=sem, vmem_limit_bytes=VMEM_LIMIT)


def _silu(x):
    return x * (1.0 / (1.0 + jnp.exp(-x)))


def _mod_row(i, tm, n_ctx_rows, lat_len):
    return jnp.maximum(i * tm - n_ctx_rows + lat_len, 0) // lat_len


def _mod_kernel(c_ref, w_ref, b_ref, o_ref):
    s = _silu(c_ref[...]).astype(BF16)
    o_ref[...] = jnp.dot(s, w_ref[0].astype(BF16), preferred_element_type=F32) + b_ref[0]


def _adaln_all(cvec, mod_w, mod_b):
    depth, d, n = mod_w.shape
    tn = 1024
    per = d // tn
    return pl.pallas_call(
        _mod_kernel,
        grid=(depth, n // tn),
        in_specs=[pl.BlockSpec((MOD_ROWS, d), lambda l, j: (0, 0)),
                  pl.BlockSpec((1, d, tn), lambda l, j: (l, 0, j)),
                  pl.BlockSpec((1, 1, tn), lambda l, j: (l, 0, j))],
        out_specs=pl.BlockSpec((None, MOD_ROWS, None, None, tn), lambda l, j: (l, 0, j // per, 0, j % per)),
        out_shape=jax.ShapeDtypeStruct((depth, MOD_ROWS, 6, 1, d), F32),
        compiler_params=_params("parallel", "parallel"),
        name="adaln_mod",
    )(cvec, mod_w, mod_b.reshape(depth, 1, n))


def _norm_mod(x, g, shift, scale):
    y = x * lax.rsqrt(jnp.mean(x * x, axis=-1, keepdims=True) + EPS) * g
    return y * (1.0 + scale) + shift


def _rope_rotate(a, cos, sin_signed):
    lane = lax.broadcasted_iota(jnp.int32, a.shape, 1)
    partner = jnp.where((lane & 32) == 0, pltpu.roll(a, 96, 1), pltpu.roll(a, 32, 1))
    return a * cos + partner * sin_signed


def _two_source_specs(x_ctx, x_lat, tm, n_ctx_blocks):
    d = x_ctx.shape[1]
    lat_base = n_ctx_blocks if x_lat is None else 0
    specs = [pl.BlockSpec((tm, d), lambda i: (jnp.minimum(i, n_ctx_blocks - 1), 0)),
             pl.BlockSpec((tm, d), lambda i: (lat_base + jnp.maximum(i - n_ctx_blocks, 0), 0))]
    return specs, [x_ctx, x_ctx if x_lat is None else x_lat]


def _qkv_kernel(xc_ref, xl_ref, g_ref, sh_ref, sc_ref, w_ref, cos_ref, sin_ref, hg_ref, o_ref, ks_ref, vs_ref, *,
                qk_norm, tn, halves, n_ctx_blocks):
    tm = xc_ref.shape[0]
    rh = tm // halves
    heads_per_tile = tn // HEAD_DIM
    kcol, vcol = N_HEADS * HEAD_DIM, (N_HEADS + N_KV_HEADS) * HEAD_DIM

    def body(x_ref, latent):
        for hf in range(halves):
            rs = slice(hf * rh, (hf + 1) * rh)
            h = _norm_mod(x_ref[rs, :], g_ref[...], sh_ref[0], sc_ref[0]).astype(BF16)
            for jt in range(w_ref.shape[1] // tn):
                acc = jnp.dot(h, w_ref[:, jt * tn:(jt + 1) * tn], preferred_element_type=F32)
                for hh in range(heads_per_tile):
                    head = jt * heads_per_tile + hh
                    a = acc[:, hh * HEAD_DIM:(hh + 1) * HEAD_DIM]
                    if head < N_HEADS + N_KV_HEADS:
                        if qk_norm:
                            gain = hg_ref[0:1, :] if head < N_HEADS else hg_ref[1:2, :]
                            a = a * lax.rsqrt(jnp.mean(a * a, axis=-1, keepdims=True) + EPS) * gain
                        if latent:
                            a = _rope_rotate(a, cos_ref[rs, :], sin_ref[rs, :])
                    o_ref[rs, head * HEAD_DIM:(head + 1) * HEAD_DIM] = a
        if not latent:
            seqs, ctx_len = ks_ref.shape[0], ks_ref.shape[1]
            for s in range(seqs):
                for kvh in range(N_KV_HEADS):
                    rows = slice(s * ctx_len, (s + 1) * ctx_len)
                    ks_ref[s, :, kvh, :] = o_ref[rows, kcol + kvh * HEAD_DIM:kcol + (kvh + 1) * HEAD_DIM]
                    vs_ref[s, :, kvh, :] = o_ref[rows, vcol + kvh * HEAD_DIM:vcol + (kvh + 1) * HEAD_DIM]

    i = pl.program_id(0)
    pl.when(i < n_ctx_blocks)(lambda: body(xc_ref, False))
    pl.when(i >= n_ctx_blocks)(lambda: body(xl_ref, True))


def _qkv_project(x_ctx, x_lat, g, mod, w, layer, cos, sin, hg, *, n_ctx_rows, lat_len, ctx_len, tm):
    d = x_ctx.shape[1]
    seqs = tm // ctx_len
    state_shape = (n_ctx_rows // ctx_len, ctx_len, N_KV_HEADS, HEAD_DIM)
    state_spec = pl.BlockSpec((seqs, ctx_len, N_KV_HEADS, HEAD_DIM),
                              lambda i: (jnp.minimum(i, n_ctx_rows // tm - 1), 0, 0, 0))
    n = w.shape[2]
    ncb = n_ctx_rows // tm
    t = n_ctx_rows + (x_ctx.shape[0] - n_ctx_rows if x_lat is None else x_lat.shape[0])
    row = functools.partial(_mod_row, tm=tm, n_ctx_rows=n_ctx_rows, lat_len=lat_len)
    tab = lambda i: ((jnp.maximum(i - ncb, 0)) % (lat_len // tm), 0)
    qk_norm = hg is not None
    if hg is None:
        hg = jnp.ones((2, HEAD_DIM), F32)
    x_specs, x_args = _two_source_specs(x_ctx, x_lat, tm, ncb)
    return pl.pallas_call(
        functools.partial(_qkv_kernel, qk_norm=qk_norm, tn=512, halves=2, n_ctx_blocks=ncb),
        grid=(t // tm,),
        in_specs=x_specs + [
            pl.BlockSpec((1, d), lambda i: (0, 0)),
            pl.BlockSpec((1, 1, d), lambda i: (row(i) * 6, 0, 0)),
            pl.BlockSpec((1, 1, d), lambda i: (row(i) * 6 + 1, 0, 0)),
            pl.BlockSpec((None, d, n), lambda i: (layer, 0, 0), pipeline_mode=pl.Buffered(1)),
            pl.BlockSpec((tm, HEAD_DIM), tab), pl.BlockSpec((tm, HEAD_DIM), tab),
            pl.BlockSpec((2, HEAD_DIM), lambda i: (0, 0))],
        out_specs=[pl.BlockSpec((tm, n), lambda i: (i, 0)), state_spec, state_spec],
        out_shape=[jax.ShapeDtypeStruct((t, n), F32), jax.ShapeDtypeStruct(state_shape, F32),
                   jax.ShapeDtypeStruct(state_shape, F32)],
        compiler_params=_params("arbitrary"),
        name="qkv_project",
    )(*x_args, g.reshape(1, d), mod, mod, w, cos, sin, hg)


def _proj_res_kernel(ac_ref, al_ref, xc_ref, xl_ref, w_ref, gate_ref, g2_ref, sh2_ref, sc2_ref, o_ref, h_ref, *,
                     n_ctx_blocks):
    i = pl.program_id(0)

    def emit(a_ref, x_ref):
        xn = x_ref[...] + gate_ref[0] * jnp.dot(a_ref[...], w_ref[...], preferred_element_type=F32)
        o_ref[...] = xn
        h_ref[...] = _norm_mod(xn, g2_ref[...], sh2_ref[0], sc2_ref[0]).astype(BF16)

    pl.when(i < n_ctx_blocks)(lambda: emit(ac_ref, xc_ref))
    pl.when(i >= n_ctx_blocks)(lambda: emit(al_ref, xl_ref))


def _proj_residual(a_ctx, a_lat, w, layer, x_ctx, x_lat, mod, g2, *, n_ctx_rows, lat_len, tm):
    d = x_ctx.shape[1]
    k = w.shape[1]
    ncb = n_ctx_rows // tm
    t = a_ctx.shape[0] + a_lat.shape[0]
    row = functools.partial(_mod_row, tm=tm, n_ctx_rows=n_ctx_rows, lat_len=lat_len)
    modspec = lambda which: pl.BlockSpec((1, 1, d), lambda i: (row(i) * 6 + which, 0, 0))
    x_specs, x_args = _two_source_specs(x_ctx, x_lat, tm, ncb)
    return pl.pallas_call(
        functools.partial(_proj_res_kernel, n_ctx_blocks=ncb),
        grid=(t // tm,),
        in_specs=[pl.BlockSpec((tm, k), lambda i: (jnp.minimum(i, ncb - 1), 0)),
                  pl.BlockSpec((tm, k), lambda i: (jnp.maximum(i - ncb, 0), 0))]
        + x_specs
        + [pl.BlockSpec((None, k, d), lambda i: (layer, 0, 0), pipeline_mode=pl.Buffered(1)),
           modspec(2),
           pl.BlockSpec((1, d), lambda i: (0, 0)),
           modspec(3), modspec(4)],
        out_specs=[pl.BlockSpec((tm, d), lambda i: (i, 0)), pl.BlockSpec((tm, d), lambda i: (i, 0))],
        out_shape=[jax.ShapeDtypeStruct((t, d), F32), jax.ShapeDtypeStruct((t, d), BF16)],
        compiler_params=_params("parallel"),
        name="proj_residual",
    )(a_ctx, a_lat, *x_args, w, mod, g2.reshape(1, d), mod, mod)


def _ffn_kernel(x_ref, h_ref, gate_ref, wg_ref, wu_ref, wd_ref, o_ref):
    f = pl.program_id(1)

    @pl.when(f == 0)
    def _():
        o_ref[...] = jnp.zeros_like(o_ref)

    h = h_ref[...]
    gv = jnp.dot(h, wg_ref[...], preferred_element_type=F32)
    uv = jnp.dot(h, wu_ref[...], preferred_element_type=F32)
    act = (_silu(gv) * uv).astype(BF16)
    o_ref[...] += jnp.dot(act, wd_ref[...], preferred_element_type=F32)

    @pl.when(f == pl.num_programs(1) - 1)
    def _():
        o_ref[...] = x_ref[...] + gate_ref[0] * o_ref[...]


def _ffn(x, h, mod, w_gu, w_down, layer, *, n_ctx_rows, lat_len, tm, tf):
    t, d = x.shape
    ff = w_down.shape[1]
    nf = ff // tf
    row = functools.partial(_mod_row, tm=tm, n_ctx_rows=n_ctx_rows, lat_len=lat_len)
    return pl.pallas_call(
        _ffn_kernel,
        grid=(t // tm, nf),
        in_specs=[pl.BlockSpec((tm, d), lambda i, f: (i, 0)),
                  pl.BlockSpec((tm, d), lambda i, f: (i, 0)),
                  pl.BlockSpec((1, 1, d), lambda i, f: (row(i) * 6 + 5, 0, 0)),
                  pl.BlockSpec((None, d, tf), lambda i, f: (layer, 0, f)),
                  pl.BlockSpec((None, d, tf), lambda i, f: (layer, 0, nf + f)),
                  pl.BlockSpec((None, tf, d), lambda i, f: (layer, f, 0))],
        out_specs=pl.BlockSpec((tm, d), lambda i, f: (i, 0)),
        out_shape=jax.ShapeDtypeStruct((t, d), F32),
        compiler_params=_params("parallel", "arbitrary"),
        name="ffn_swiglu",
    )(x, h, mod, w_gu, w_gu, w_down)


def _stack_heads(q):
    return jnp.concatenate([q[:, h * HEAD_DIM:(h + 1) * HEAD_DIM] for h in range(GROUP)], axis=0)


def _unstack_heads(o, rows):
    return jnp.concatenate([o[h * rows:(h + 1) * rows] for h in range(GROUP)], axis=1)


def _sink_column(sink_ref, kh, rows):
    head = lax.broadcasted_iota(jnp.int32, (GROUP * rows, 1), 0) // rows
    col = jnp.zeros((GROUP * rows, 1), F32)
    for h in range(GROUP):
        col = jnp.where(head == h, sink_ref[kh * GROUP + h], col)
    return col


def _softmax_av(s, v, sink_col):
    m = jnp.max(s, axis=-1, keepdims=True)
    if sink_col is not None:
        m = jnp.maximum(m, sink_col)
    p = jnp.exp(s - m)
    l = jnp.sum(p, axis=-1, keepdims=True)
    if sink_col is not None:
        l = l + jnp.exp(sink_col - m)
    return jnp.dot(p.astype(BF16), v, preferred_element_type=F32) / l


def _attn_ctx_kernel(sink_ref, q_ref, k_ref, v_ref, o_ref, *, use_sink):
    rows = q_ref.shape[0]
    qs = _stack_heads(q_ref[...] * SCALE).astype(BF16)
    s = lax.dot_general(qs, k_ref[...].astype(BF16), NT_DIMS, preferred_element_type=F32)
    sink_col = _sink_column(sink_ref, pl.program_id(1), rows) if use_sink else None
    o = _softmax_av(s, v_ref[...].astype(BF16), sink_col)
    o_ref[...] = _unstack_heads(o, rows).astype(BF16)


def _attn_ctx(y, sink, *, n_ctx_seqs, ctx_len, use_sink):
    t = n_ctx_seqs * ctx_len
    qw = GROUP * HEAD_DIM
    return pl.pallas_call(
        functools.partial(_attn_ctx_kernel, use_sink=use_sink),
        grid=(n_ctx_seqs, N_KV_HEADS),
        in_specs=[pl.BlockSpec(memory_space=pltpu.SMEM),
                  pl.BlockSpec((ctx_len, qw), lambda b, kh: (b, kh)),
                  pl.BlockSpec((ctx_len, HEAD_DIM), lambda b, kh: (b, N_HEADS + kh)),
                  pl.BlockSpec((ctx_len, HEAD_DIM), lambda b, kh: (b, N_HEADS + N_KV_HEADS + kh))],
        out_specs=pl.BlockSpec((ctx_len, qw), lambda b, kh: (b, kh)),
        out_shape=jax.ShapeDtypeStruct((t, N_HEADS * HEAD_DIM), BF16),
        compiler_params=_params("parallel", "parallel"),
        name="attn_context",
    )(sink, y, y, y)


def _keys_by_queries_attention(q_ref, k_scr, vt_scr, s_scr, chunks, bias_ref, sink_ref, o_ref):
    qn = q_ref.shape[0]
    kh = pl.program_id(1)

    def logits(h, ci):
        st, sz = chunks[ci]
        q = (q_ref[:, h * HEAD_DIM:(h + 1) * HEAD_DIM] * (SCALE * LOG2E)).astype(BF16)
        st_ = lax.dot_general(k_scr[st:st + sz, :], q, NT_DIMS, preferred_element_type=F32)
        if bias_ref is not None and ci == 0:
            st_ = st_ + bias_ref[...]
        s_scr[st:st + sz, h * qn:(h + 1) * qn] = st_
        return jnp.max(st_, axis=0, keepdims=True)

    def weigh(h, ci, m):
        st, sz = chunks[ci]
        p = jnp.exp2(s_scr[st:st + sz, h * qn:(h + 1) * qn] - m)
        pv = jnp.dot(vt_scr[:, st:st + sz], p.astype(BF16), preferred_element_type=F32)
        return jnp.sum(p, axis=0, keepdims=True), pv

    n = len(chunks)
    maxes = [logits(0, ci) for ci in range(n)]
    for h in range(GROUP):
        m = functools.reduce(jnp.maximum, maxes)
        if sink_ref is not None:
            sink2 = sink_ref[kh * GROUP + h] * LOG2E
            m = jnp.maximum(m, sink2)
        maxes, l, acc = [], None, None
        for ci in range(n):
            if h + 1 < GROUP:
                maxes.append(logits(h + 1, ci))
            lc, pv = weigh(h, ci, m)
            l, acc = (lc, pv) if l is None else (l + lc, acc + pv)
        if sink_ref is not None:
            l = l + jnp.exp2(sink2 - m)
        o_ref[:, h * HEAD_DIM:(h + 1) * HEAD_DIM] = (acc * (1.0 / l)).T.astype(BF16)


def _attn_lat_win_kernel(sink_ref, q_ref, kp_ref, kc_ref, kn_ref, vp_ref, vc_ref, vn_ref, ck_ref, cv_ref, bias_ref,
                         o_ref, k_scr, vt_scr, s_scr):
    past = ck_ref.shape[0]
    win = Q_STEP + 2 * BLOCK

    @pl.when(pl.program_id(2) == 0)
    def _():
        k_scr[win:, :] = ck_ref[...].astype(BF16)
        vt_scr[:, win:] = cv_ref[...].T.astype(BF16)

    row = 0
    for k_ref, v_ref in ((kp_ref, vp_ref), (kc_ref, vc_ref), (kn_ref, vn_ref)):
        n = k_ref.shape[0]
        k_scr[row:row + n, :] = k_ref[...].astype(BF16)
        vt_scr[:, row:row + n] = v_ref[...].T.astype(BF16)
        row += n
    _keys_by_queries_attention(q_ref, k_scr, vt_scr, s_scr, ((0, win), (win, past)), bias_ref, sink_ref, o_ref)


def _window_bias(n_steps):
    win = Q_STEP + 2 * BLOCK
    c = jnp.arange(win, dtype=jnp.int32)[:, None]
    r = jnp.arange(Q_STEP, dtype=jnp.int32)[None, :]
    band = jnp.abs(r + BLOCK - c) <= WINDOW
    variants = [band & (c >= BLOCK), band, band & (c < win - BLOCK)]
    if n_steps == 1:
        variants = [variants[0] & (c < win - BLOCK)] * 3
    return jnp.where(jnp.stack(variants), 0.0, NEG).astype(F32)


def _cache_spec(cache, layer):
    return pl.BlockSpec((None, None, cache.shape[2], HEAD_DIM), lambda b, kh, qs: (b, layer, 0, kh))


def _attn_lat_win(y, sink, cache_k, cache_v, layer, *, n_ctx_rows, lat_batch, lat_len):
    qw = GROUP * HEAD_DIM
    nb = lat_len // BLOCK
    ns = lat_len // Q_STEP
    per = Q_STEP // BLOCK
    base = n_ctx_rows // BLOCK
    past = cache_k.shape[2]
    n_keys = Q_STEP + 2 * BLOCK + past

    def edge(col0, blk):
        return pl.BlockSpec((BLOCK, HEAD_DIM),
                            lambda b, kh, qs: (base + b * nb + jnp.clip(qs * per + blk, 0, nb - 1), col0 + kh))

    def centre(col0):
        return pl.BlockSpec((Q_STEP, HEAD_DIM), lambda b, kh, qs: (base // per + b * ns + qs, col0 + kh))

    kcol, vcol = N_HEADS, N_HEADS + N_KV_HEADS
    cache_spec = _cache_spec(cache_k, layer)
    return pl.pallas_call(
        _attn_lat_win_kernel,
        grid=(lat_batch, N_KV_HEADS, ns),
        in_specs=[pl.BlockSpec(memory_space=pltpu.SMEM),
                  pl.BlockSpec((Q_STEP, qw), lambda b, kh, qs: (base // per + b * ns + qs, kh)),
                  edge(kcol, -1), centre(kcol), edge(kcol, per),
                  edge(vcol, -1), centre(vcol), edge(vcol, per),
                  cache_spec, cache_spec,
                  pl.BlockSpec((None, Q_STEP + 2 * BLOCK, Q_STEP),
                               lambda b, kh, qs: (jnp.where(qs == 0, 0, jnp.where(qs == ns - 1, 2, 1)), 0, 0))],
        out_specs=pl.BlockSpec((Q_STEP, qw), lambda b, kh, qs: (b * ns + qs, kh)),
        out_shape=jax.ShapeDtypeStruct((lat_batch * lat_len, N_HEADS * HEAD_DIM), BF16),
        scratch_shapes=[pltpu.VMEM((n_keys, HEAD_DIM), BF16), pltpu.VMEM((HEAD_DIM, n_keys), BF16),
                        pltpu.VMEM((n_keys, GROUP * Q_STEP), F32)],
        compiler_params=_params("parallel", "parallel", "arbitrary"),
        name="attn_latent_window",
    )(sink, y, y, y, y, y, y, y, cache_k, cache_v, _window_bias(ns))


def _attn_lat_full_kernel(q_ref, k_ref, v_ref, ck_ref, cv_ref, o_ref, k_scr, vt_scr, s_scr, *, chunk):
    lat_len = k_ref.shape[0]
    n_keys = k_scr.shape[0]

    @pl.when(pl.program_id(2) == 0)
    def _():
        k_scr[0:lat_len, :] = k_ref[...].astype(BF16)
        k_scr[lat_len:, :] = ck_ref[...].astype(BF16)
        vt_scr[:, 0:lat_len] = v_ref[...].T.astype(BF16)
        vt_scr[:, lat_len:] = cv_ref[...].T.astype(BF16)

    chunks = tuple((st, chunk) for st in range(0, n_keys, chunk))
    _keys_by_queries_attention(q_ref, k_scr, vt_scr, s_scr, chunks, None, None, o_ref)


def _attn_lat_full(y, cache_k, cache_v, layer, *, n_ctx_rows, lat_batch, lat_len):
    qw = GROUP * HEAD_DIM
    ns = lat_len // Q_STEP
    base = n_ctx_rows // Q_STEP
    seq_base = n_ctx_rows // lat_len
    past = cache_k.shape[2]
    cache_spec = _cache_spec(cache_k, layer)
    n_keys = lat_len + past
    chunk = 512
    assert n_keys % chunk == 0
    return pl.pallas_call(
        functools.partial(_attn_lat_full_kernel, chunk=chunk),
        grid=(lat_batch, N_KV_HEADS, ns),
        in_specs=[pl.BlockSpec((Q_STEP, qw), lambda b, kh, qs: (base + b * ns + qs, kh)),
                  pl.BlockSpec((lat_len, HEAD_DIM), lambda b, kh, qs: (seq_base + b, N_HEADS + kh)),
                  pl.BlockSpec((lat_len, HEAD_DIM), lambda b, kh, qs: (seq_base + b, N_HEADS + N_KV_HEADS + kh)),
                  cache_spec, cache_spec],
        out_specs=pl.BlockSpec((Q_STEP, qw), lambda b, kh, qs: (b * ns + qs, kh)),
        out_shape=jax.ShapeDtypeStruct((lat_batch * lat_len, N_HEADS * HEAD_DIM), BF16),
        scratch_shapes=[pltpu.VMEM((n_keys, HEAD_DIM), BF16), pltpu.VMEM((HEAD_DIM, n_keys), BF16),
                        pltpu.VMEM((n_keys, GROUP * Q_STEP), F32)],
        compiler_params=_params("parallel", "parallel", "arbitrary"),
        name="attn_latent_full",
    )(y, y, y, cache_k, cache_v)


def _rope_tables(lat_len):
    half = HEAD_DIM // 2
    n_rows = lat_len // GRID_W
    row = jnp.repeat(jnp.arange(n_rows, dtype=F32), GRID_W)
    col = jnp.tile(jnp.arange(GRID_W, dtype=F32), n_rows)
    inv = ROPE_THETA ** (-jnp.arange(0, half, 2, dtype=F32) / half)
    ang_r, ang_c = row[:, None] * inv, col[:, None] * inv
    cos = jnp.concatenate([jnp.cos(ang_r)] * 2 + [jnp.cos(ang_c)] * 2, axis=-1)
    sin = jnp.concatenate([-jnp.sin(ang_r), jnp.sin(ang_r), -jnp.sin(ang_c), jnp.sin(ang_c)], axis=-1)
    return cos, sin


def _dft_mats(seq_len, tk):
    n = 2 * seq_len
    k = jnp.arange(seq_len, dtype=jnp.int32)
    step = 64
    a = jnp.arange(seq_len // step, dtype=jnp.int32) * step
    b = jnp.arange(step, dtype=jnp.int32)
    ang_a = ((k[:, None] * a[None, :]) % n).astype(F32) * (2.0 * math.pi / n)
    ang_b = ((k[:, None] * b[None, :]) % n).astype(F32) * (2.0 * math.pi / n)
    ca, sa = jnp.cos(ang_a)[:, :, None], jnp.sin(ang_a)[:, :, None]
    cb, sb = jnp.cos(ang_b)[:, None, :], jnp.sin(ang_b)[:, None, :]
    fc = (ca * cb - sa * sb).reshape(seq_len, seq_len)
    fs = -(sa * cb + ca * sb).reshape(seq_len, seq_len)
    sign = jnp.where(k % 2 == 0, 1.0, -1.0).astype(F32)
    fs = jnp.where(k[:, None] == 0, sign[None, :], fs)
    kfc = seq_len // tk
    ana = jnp.concatenate([fc.reshape(kfc, tk, seq_len), fs.reshape(kfc, tk, seq_len)], axis=1).astype(BF16)
    ana = lax.optimization_barrier(ana)
    return ana, ana.transpose(0, 2, 1)


def _hy_filter_kernel(feat_ref, w1_ref, b1_ref, w2_ref, b2_ref, fq_ref, w3f_ref, w3b_ref, tn_ref, dl_ref,
                      ana_ref, hr_ref, hi_ref, p_scr, q_scr, inv_scr, nyq_scr):
    kf = pl.program_id(1)
    hp = lax.Precision.HIGHEST
    tk = hr_ref.shape[0]

    @pl.when(kf == 0)
    def _():
        a = jnp.dot(feat_ref[...], w1_ref[...], precision=hp, preferred_element_type=F32) + b1_ref[...]
        a = jnp.sin(fq_ref[0:1, :] * a)
        a = jnp.dot(a, w2_ref[...], precision=hp, preferred_element_type=F32) + b2_ref[...]
        a = jnp.sin(fq_ref[1:2, :] * a)
        decay = jnp.exp(-tn_ref[...] * dl_ref[...])
        fwd = jnp.dot(a, w3f_ref[...], precision=hp, preferred_element_type=F32) * decay
        bwd = jnp.dot(a, w3b_ref[...], precision=hp, preferred_element_type=F32) * decay
        rowi = lax.broadcasted_iota(jnp.int32, fwd.shape, 0)
        bwd = jnp.where(rowi == 0, 0.0, bwd)
        norm = jnp.sum(jnp.abs(fwd), axis=0, keepdims=True) + jnp.sum(jnp.abs(bwd), axis=0, keepdims=True) + EPS
        inv_scr[...] = 1.0 / norm
        p = fwd + bwd
        nyq_scr[...] = jnp.sum(jnp.where((rowi & 1) == 0, p, -p), axis=0, keepdims=True)
        p_scr[...] = p.astype(BF16)
        q_scr[...] = (fwd - bwd).astype(BF16)

    inv = inv_scr[...]
    hr_ref[...] = jnp.dot(ana_ref[0:tk, :], p_scr[...], preferred_element_type=F32) * inv
    hi = jnp.dot(ana_ref[tk:, :], q_scr[...], preferred_element_type=F32)
    rowk = lax.broadcasted_iota(jnp.int32, hi.shape, 0)
    hi = jnp.where((rowk == 0) & (kf == 0), nyq_scr[...], hi)
    hi_ref[...] = hi * inv


def _hy_filters(seq_len, f_w1, f_b1, f_w2, f_b2, f_w3, freq, ana_mats, *, tc):
    tk = ana_mats.shape[1] // 2
    hid = f_w1.shape[1]
    nd = f_w3.shape[1] // 2
    d = nd // HY_ORDER
    t = jnp.arange(seq_len, dtype=F32)
    tnorm = t / max(seq_len - 1, 1)
    bands = (HY_EMB - 1) // 2
    fb = jnp.linspace(1e-4, bands - 1, bands, dtype=F32)
    w = 2.0 * math.pi * t / seq_len
    feats = jnp.concatenate([tnorm[:, None], jnp.cos(w[:, None] * fb), -jnp.sin(w[:, None] * fb)], axis=-1)
    emb = 64
    feats = jnp.pad(feats, ((0, 0), (0, emb - HY_EMB)))
    w1 = jnp.pad(f_w1, ((0, emb - HY_EMB), (0, 0)))
    deltas = jnp.abs(jnp.linspace(HY_MIN_DECAY, HY_MAX_DECAY, d, dtype=F32))
    deltas = jnp.tile(deltas, HY_ORDER).reshape(1, nd)
    nct = nd // tc
    const = lambda shape: pl.BlockSpec(shape, lambda c, kf: (0, 0))
    out = jax.ShapeDtypeStruct((seq_len, nd), F32)
    return pl.pallas_call(
        _hy_filter_kernel,
        grid=(nct, seq_len // tk),
        in_specs=[const((seq_len, emb)), const((emb, hid)), const((1, hid)), const((hid, hid)), const((1, hid)),
                  const((2, hid)),
                  pl.BlockSpec((hid, tc), lambda c, kf: (0, c)),
                  pl.BlockSpec((hid, tc), lambda c, kf: (0, nct + c)),
                  const((seq_len, 1)),
                  pl.BlockSpec((1, tc), lambda c, kf: (0, c)),
                  pl.BlockSpec((None, 2 * tk, seq_len), lambda c, kf: (kf, 0, 0))],
        out_specs=[pl.BlockSpec((tk, tc), lambda c, kf: (kf, c)),
                   pl.BlockSpec((tk, tc), lambda c, kf: (kf, c))],
        out_shape=[out, out],
        scratch_shapes=[pltpu.VMEM((seq_len, tc), BF16), pltpu.VMEM((seq_len, tc), BF16),
                        pltpu.VMEM((1, tc), F32), pltpu.VMEM((1, tc), F32)],
        compiler_params=_params("parallel", "arbitrary"),
        name="hyena_filter",
    )(feats, w1, f_b1.reshape(1, hid), f_w2, f_b2.reshape(1, hid), freq, f_w3, f_w3,
      tnorm.reshape(seq_len, 1), deltas, ana_mats)


def _hy_inproj_kernel(x_ref, g_ref, sh_ref, sc_ref, w_ref, cw_ref, cb_ref, o_ref, h_scr, y_scr, *, n_ctx_blocks,
                      ctx_len):
    j = pl.program_id(1)
    n = x_ref.shape[0]

    @pl.when(j == 0)
    def _():
        for r0 in range(0, n, 256):
            rs = slice(r0, r0 + 256)
            h_scr[rs, :] = _norm_mod(x_ref[rs, :], g_ref[...], sh_ref[0], sc_ref[0]).astype(BF16)
        y_scr[0:PAD, :] = jnp.zeros((PAD, y_scr.shape[1]), F32)
        y_scr[PAD + n:, :] = jnp.zeros((PAD, y_scr.shape[1]), F32)

    is_ctx = pl.program_id(0) < n_ctx_blocks
    chunk = 256

    def product(r0):
        y_scr[PAD + r0:PAD + r0 + chunk, :] = jnp.dot(h_scr[r0:r0 + chunk, :], w_ref[...],
                                                      preferred_element_type=F32)

    product(0)
    for r0 in range(0, n, chunk):
        if r0 + chunk < n:
            product(r0 + chunk)
        pos = (r0 + lax.broadcasted_iota(jnp.int32, (chunk, y_scr.shape[1]), 0)) % ctx_len
        prev = jnp.where(is_ctx & (pos == 0), 0.0, y_scr[PAD - 1 + r0:PAD - 1 + r0 + chunk, :])
        cur = y_scr[PAD + r0:PAD + r0 + chunk, :]
        nxt = jnp.where(is_ctx & (pos == ctx_len - 1), 0.0, y_scr[PAD + 1 + r0:PAD + 1 + r0 + chunk, :])
        o_ref[r0:r0 + chunk, :] = (prev * cw_ref[0:1, :] + cur * cw_ref[1:2, :] + nxt * cw_ref[2:3, :]
                                   + cb_ref[...])


def _hy_inproj(x, g, mod, w, conv_w, conv_b, *, n_ctx_rows, lat_len, ctx_len, tn):
    t, d = x.shape
    n = w.shape[1]
    tm = lat_len
    row = functools.partial(_mod_row, tm=tm, n_ctx_rows=n_ctx_rows, lat_len=lat_len)
    return pl.pallas_call(
        functools.partial(_hy_inproj_kernel, n_ctx_blocks=n_ctx_rows // tm, ctx_len=ctx_len),
        grid=(t // tm, n // tn),
        in_specs=[pl.BlockSpec((tm, d), lambda i, j: (i, 0), pipeline_mode=pl.Buffered(1)),
                  pl.BlockSpec((1, d), lambda i, j: (0, 0)),
                  pl.BlockSpec((1, 1, d), lambda i, j: (row(i) * 6, 0, 0)),
                  pl.BlockSpec((1, 1, d), lambda i, j: (row(i) * 6 + 1, 0, 0)),
                  pl.BlockSpec((d, tn), lambda i, j: (0, j)),
                  pl.BlockSpec((3, tn), lambda i, j: (0, j)),
                  pl.BlockSpec((1, tn), lambda i, j: (0, j))],
        out_specs=pl.BlockSpec((tm, tn), lambda i, j: (i, j)),
        out_shape=jax.ShapeDtypeStruct((t, n), F32),
        scratch_shapes=[pltpu.VMEM((tm, d), BF16), pltpu.VMEM((tm + 2 * PAD, tn), F32)],
        compiler_params=_params("parallel", "arbitrary"),
        name="hyena_inproj",
    )(x, g.reshape(1, d), mod, mod, w, conv_w, conv_b.reshape(1, n))


def _hy_conv_kernel(a_ref, g_ref, skip_ref, ana_ref, syn_ref, hr_ref, hi_ref, o_ref, acc_scr, u_scr=None, *,
                    seq_len):
    kf = pl.program_id(2)
    rows = a_ref.shape[0]
    tk = hr_ref.shape[0]
    chunk = 256

    @pl.when(kf == 0)
    def _():
        acc_scr[...] = jnp.zeros_like(acc_scr)
        if u_scr is not None:
            for r0 in range(0, rows, chunk):
                u_scr[r0:r0 + chunk, :] = a_ref[r0:r0 + chunk, :].astype(BF16)

    u_ref = a_ref if u_scr is None else u_scr

    hr = hr_ref[...]
    hi = hi_ref[...]
    first = (lax.broadcasted_iota(jnp.int32, hr.shape, 0) == 0) & (kf == 0)
    for seg in range(0, rows, seq_len):
        uf = jnp.dot(ana_ref[...], u_ref[seg:seg + seq_len, :], preferred_element_type=F32)
        ur, ui = uf[0:tk, :], uf[tk:, :]
        yr = ur * hr - ui * hi
        yi = ur * hi + ui * hr
        yr = jnp.where(first, 0.5 * (ur * hr), yr)
        yi = jnp.where(first, 0.5 * (ui * hi), yi)
        yf = jnp.concatenate([yr, yi], axis=0).astype(BF16)
        step = min(seq_len, 512)
        for n0 in range(0, seq_len, step):
            acc_scr[seg + n0:seg + n0 + step, :] += jnp.dot(syn_ref[n0:n0 + step, :], yf,
                                                            preferred_element_type=F32)

    @pl.when(kf == pl.num_programs(2) - 1)
    def _():
        for r0 in range(0, rows, chunk):
            rs = slice(r0, r0 + chunk)
            a = a_ref[rs, :].astype(F32)
            o_ref[rs, :] = (g_ref[rs, :] * (acc_scr[rs, :] * (1.0 / seq_len) + a * skip_ref[...])).astype(BF16)


def _hy_conv(a, a_part, a_base, g, g_part, g_base, skip_row, mats, hr, hi, order, *,
             n_blocks, rows, seq_len, tc):
    d = skip_row.shape[1]
    nct = d // tc
    ana_mats, syn_mats = mats
    kfc = ana_mats.shape[0]
    tk = ana_mats.shape[1] // 2

    def amap(part, base):
        return lambda b, c, kf: (base + b, part * nct + c)

    hmap = lambda b, c, kf: (kf, order * nct + c)
    return pl.pallas_call(
        functools.partial(_hy_conv_kernel, seq_len=seq_len),
        grid=(n_blocks, nct, kfc),
        in_specs=[pl.BlockSpec((rows, tc), amap(a_part, a_base)),
                  pl.BlockSpec((rows, tc), amap(g_part, g_base)),
                  pl.BlockSpec((1, tc), lambda b, c, kf: (0, c)),
                  pl.BlockSpec((None, 2 * tk, seq_len), lambda b, c, kf: (kf, 0, 0)),
                  pl.BlockSpec((None, seq_len, 2 * tk), lambda b, c, kf: (kf, 0, 0)),
                  pl.BlockSpec((tk, tc), hmap), pl.BlockSpec((tk, tc), hmap)],
        out_specs=pl.BlockSpec((rows, tc), lambda b, c, kf: (b, c)),
        out_shape=jax.ShapeDtypeStruct((n_blocks * rows, d), BF16),
        scratch_shapes=[pltpu.VMEM((rows, tc), F32)] + ([] if a.dtype == BF16 else [pltpu.VMEM((rows, tc), BF16)]),
        compiler_params=_params("parallel", "parallel", "arbitrary"),
        name="hyena_conv",
    )(a, g, skip_row, ana_mats, syn_mats, hr, hi)


def _final_norm_kernel(x_ref, g_ref, o_ref):
    x = x_ref[...]
    o_ref[...] = x * lax.rsqrt(jnp.mean(x * x, axis=-1, keepdims=True) + EPS) * g_ref[...]


def _final_norm(x, g, *, row0, rows, tm):
    d = x.shape[1]
    base = row0 // tm
    return pl.pallas_call(
        _final_norm_kernel,
        grid=(rows // tm,),
        in_specs=[pl.BlockSpec((tm, d), lambda i: (base + i, 0)),
                  pl.BlockSpec((1, d), lambda i: (0, 0))],
        out_specs=pl.BlockSpec((tm, d), lambda i: (i, 0)),
        out_shape=jax.ShapeDtypeStruct((rows, d), F32),
        compiler_params=_params("parallel"),
        name="final_norm",
    )(x, g.reshape(1, d))


def kernel(x_prompt, x_sample, cache_win_k, cache_win_v, cache_ax_k, cache_ax_v, c, c_ctx, norm_mix_g, norm_ffn_g, mod_w, mod_b, win_wqkv, win_wo, win_sink, hy_w_in, hy_conv_w, hy_conv_b, hy_f_w1, hy_f_b1, hy_f_w2, hy_f_b2, hy_f_w3, hy_freq, hy_skip, hy_wo, ax_wqkv, ax_q_g, ax_k_g, ax_wo, ffn_w_gu, ffn_w_down, final_g):
    n_ctx_seqs, ctx_len, d = x_prompt.shape
    lat_batch, lat_len, _ = x_sample.shape
    depth = mod_w.shape[0]
    past = cache_win_k.shape[2]
    n_ctx_rows = n_ctx_seqs * ctx_len
    n_lat_rows = lat_batch * lat_len
    tm = 512
    assert n_ctx_rows % tm == 0 and lat_len % tm == 0 and n_ctx_rows % lat_len == 0
    assert lat_batch + 1 <= MOD_ROWS
    geom = dict(n_ctx_rows=n_ctx_rows, lat_len=lat_len, tm=tm)

    x, x_lat = x_prompt.reshape(n_ctx_rows, d), x_sample.reshape(n_lat_rows, d)
    cvec = jnp.concatenate([c_ctx[None, :], c, jnp.zeros((MOD_ROWS - 1 - lat_batch, d), F32)], axis=0)
    mod_all = _adaln_all(cvec, mod_w, mod_b)
    cos, sin = _rope_tables(lat_len)

    w_gu, w_down = ffn_w_gu.astype(BF16), ffn_w_down.astype(BF16)
    wqkv_all = {0: win_wqkv.astype(BF16), 2: ax_wqkv.astype(BF16)}
    wo_all = {0: win_wo.astype(BF16), 1: hy_wo.astype(BF16), 2: ax_wo.astype(BF16)}

    win_k, win_v, ax_k, ax_v = [], [], [], []
    kv_w = N_KV_HEADS * HEAD_DIM
    state_shape = (n_ctx_seqs, ctx_len, N_KV_HEADS, HEAD_DIM)
    for i in range(depth):
        mod = mod_all[i].reshape(MOD_ROWS * 6, 1, d)
        kind, j = i % 3, i // 3
        if kind == 1:
            assert x_lat is None
            u = _hy_inproj(x, norm_mix_g[i], mod, hy_w_in[j].astype(BF16), hy_conv_w[j], hy_conv_b[j],
                           n_ctx_rows=n_ctx_rows, lat_len=lat_len, ctx_len=ctx_len, tn=512)
            rows = lat_len
            zs = []
            for (base, nblk, sl) in ((0, n_ctx_rows // rows, ctx_len), (n_ctx_rows // rows, lat_batch, lat_len)):
                mats = _dft_mats(sl, min(512, sl))
                hr, hi = _hy_filters(sl, hy_f_w1[j], hy_f_b1[j], hy_f_w2[j], hy_f_b2[j], hy_f_w3[j], hy_freq[j],
                                     mats[0], tc=512)
                conv = functools.partial(_hy_conv, mats=mats, hr=hr, hi=hi, n_blocks=nblk, rows=rows, seq_len=sl,
                                         tc=512)
                z1 = conv(u, 0, base, u, 1, base, skip_row=hy_skip[j, 0].reshape(1, d), order=0)
                zs.append(conv(z1, 0, 0, u, 2, base, skip_row=hy_skip[j, 1].reshape(1, d), order=1))
            x, h2 = _proj_residual(zs[0], zs[1], wo_all[kind], j, x, None, mod, norm_ffn_g[i], **geom)
        else:
            if kind == 0:
                sink, hg = win_sink[j], None
                ck, cv = cache_win_k, cache_win_v
            else:
                sink = jnp.zeros((N_HEADS,), F32)
                hg = jnp.stack([ax_q_g[j], ax_k_g[j]], axis=0)
                ck, cv = cache_ax_k, cache_ax_v
            ck = ck[:, j:j + 1].reshape(lat_batch, 1, past, kv_w)
            cv = cv[:, j:j + 1].reshape(lat_batch, 1, past, kv_w)
            y, k_c, v_c = _qkv_project(x, x_lat, norm_mix_g[i], mod, wqkv_all[kind], j, cos, sin, hg,
                                       ctx_len=ctx_len, **geom)
            o_c = _attn_ctx(y, sink, n_ctx_seqs=n_ctx_seqs, ctx_len=ctx_len, use_sink=kind == 0)
            lat_geom = dict(n_ctx_rows=n_ctx_rows, lat_batch=lat_batch, lat_len=lat_len)
            if kind == 0:
                o_l = _attn_lat_win(y, sink, ck, cv, 0, **lat_geom)
            else:
                o_l = _attn_lat_full(y, ck, cv, 0, **lat_geom)
            (win_k if kind == 0 else ax_k).append(k_c)
            (win_v if kind == 0 else ax_v).append(v_c)
            x, h2 = _proj_residual(o_c, o_l, wo_all[kind], j, x, x_lat, mod, norm_ffn_g[i], **geom)
            x_lat = None
        x = _ffn(x, h2, mod, w_gu, w_down, i, tf=512, **geom)

    y_prompt = _final_norm(x, final_g, row0=0, rows=n_ctx_rows, tm=tm).reshape(n_ctx_seqs, ctx_len, d)
    y_sample = _final_norm(x, final_g, row0=n_ctx_rows, rows=n_lat_rows, tm=tm).reshape(lat_batch, lat_len, d)
    return (y_prompt, y_sample, jnp.stack(win_k, axis=1), jnp.stack(win_v, axis=1),
            jnp.stack(ax_k, axis=1), jnp.stack(ax_v, axis=1))
```

```python
import functools
import math

import jax
import jax.numpy as jnp
from jax import lax
from jax.experimental import pallas as pl
from jax.experimental.pallas import tpu as pltpu

F32 = jnp.float32
BF16 = jnp.bfloat16

N_HEADS = 16
N_KV_HEADS = 4
GROUP = N_HEADS // N_KV_HEADS
HEAD_DIM = 128
WINDOW = 128
BLOCK = 128
GRID_W = 64
ROPE_THETA = 10000.0
HY_ORDER = 2
HY_EMB = 33
HY_TARGET = 1e-2
HY_FAST_DECAY = 0.3
HY_SLOW_DECAY = 1.5
HY_MIN_DECAY = math.log(HY_TARGET) / HY_SLOW_DECAY
HY_MAX_DECAY = math.log(HY_TARGET) / HY_FAST_DECAY
EPS = 1e-6
NEG = -1e30
SCALE = HEAD_DIM ** -0.5
LOG2E = math.log2(math.e)
Q_STEP = 2 * BLOCK

LANES = 128
PAD = 8
MOD_ROWS = 16
VMEM_LIMIT = 56 * 1024 * 1024

NT_DIMS = (((1,), (1,)), ((), ()))

def _params(*sem):
    return pltpu.CompilerParams(dimension_semantics=sem, vmem_limit_bytes=VMEM_LIMIT)


def _silu(x):
    return x * (1.0 / (1.0 + jnp.exp(-x)))


def _mod_row(i, tm, n_ctx_rows, lat_len):
    return jnp.maximum(i * tm - n_ctx_rows + lat_len, 0) // lat_len


def _mod_kernel(c_ref, w_ref, b_ref, o_ref):
    s = _silu(c_ref[...]).astype(BF16)
    o_ref[...] = jnp.dot(s, w_ref[0].astype(BF16), preferred_element_type=F32) + b_ref[0]


def _adaln_all(cvec, mod_w, mod_b):
    depth, d, n = mod_w.shape
    tn = 1024
    per = d // tn
    return pl.pallas_call(
        _mod_kernel,
        grid=(depth, n // tn),
        in_specs=[pl.BlockSpec((MOD_ROWS, d), lambda l, j: (0, 0)),
                  pl.BlockSpec((1, d, tn), lambda l, j: (l, 0, j)),
                  pl.BlockSpec((1, 1, tn), lambda l, j: (l, 0, j))],
        out_specs=pl.BlockSpec((None, MOD_ROWS, None, None, tn), lambda l, j: (l, 0, j // per, 0, j % per)),
        out_shape=jax.ShapeDtypeStruct((depth, MOD_ROWS, 6, 1, d), F32),
        compiler_params=_params("parallel", "parallel"),
        name="adaln_mod",
    )(cvec, mod_w, mod_b.reshape(depth, 1, n))


def _norm_mod(x, g, shift, scale):
    y = x * lax.rsqrt(jnp.mean(x * x, axis=-1, keepdims=True) + EPS) * g
    return y * (1.0 + scale) + shift


def _rope_rotate(a, cos, sin_signed):
    lane = lax.broadcasted_iota(jnp.int32, a.shape, 1)
    partner = jnp.where((lane & 32) == 0, pltpu.roll(a, 96, 1), pltpu.roll(a, 32, 1))
    return a * cos + partner * sin_signed


def _two_source_specs(x_ctx, x_lat, tm, n_ctx_blocks):
    d = x_ctx.shape[1]
    lat_base = n_ctx_blocks if x_lat is None else 0
    specs = [pl.BlockSpec((tm, d), lambda i: (jnp.minimum(i, n_ctx_blocks - 1), 0)),
             pl.BlockSpec((tm, d), lambda i: (lat_base + jnp.maximum(i - n_ctx_blocks, 0), 0))]
    return specs, [x_ctx, x_ctx if x_lat is None else x_lat]


def _qkv_kernel(*refs, qk_norm, tn, halves, n_ctx_blocks, state_slot):
    xc_ref, xl_ref, g_ref, sh_ref, sc_ref, w_ref, cos_ref, sin_ref, hg_ref = refs[:9]
    o_ref, ks_ref, vs_ref = refs[-3:]
    tm = xc_ref.shape[0]
    rh = tm // halves
    heads_per_tile = tn // HEAD_DIM
    kcol, vcol = N_HEADS * HEAD_DIM, (N_HEADS + N_KV_HEADS) * HEAD_DIM

    def body(x_ref, latent):
        for hf in range(halves):
            rs = slice(hf * rh, (hf + 1) * rh)
            h = _norm_mod(x_ref[rs, :], g_ref[...], sh_ref[0], sc_ref[0]).astype(BF16)
            for jt in range(w_ref.shape[1] // tn):
                acc = jnp.dot(h, w_ref[:, jt * tn:(jt + 1) * tn], preferred_element_type=F32)
                for hh in range(heads_per_tile):
                    head = jt * heads_per_tile + hh
                    a = acc[:, hh * HEAD_DIM:(hh + 1) * HEAD_DIM]
                    if head < N_HEADS + N_KV_HEADS:
                        if qk_norm:
                            gain = hg_ref[0:1, :] if head < N_HEADS else hg_ref[1:2, :]
                            a = a * lax.rsqrt(jnp.mean(a * a, axis=-1, keepdims=True) + EPS) * gain
                        if latent:
                            a = _rope_rotate(a, cos_ref[rs, :], sin_ref[rs, :])
                    o_ref[rs, head * HEAD_DIM:(head + 1) * HEAD_DIM] = a
        if not latent:
            seqs, ctx_len = ks_ref.shape[0], ks_ref.shape[-3]
            for ref, col in ((ks_ref, kcol), (vs_ref, vcol)):
                if state_slot is not None:
                    for other in range(ref.shape[1]):
                        if other != state_slot:
                            ref[:, other] = jnp.zeros(ref.shape[:1] + ref.shape[2:], F32)
                for s in range(seqs):
                    for kvh in range(N_KV_HEADS):
                        val = o_ref[s * ctx_len:(s + 1) * ctx_len, col + kvh * HEAD_DIM:col + (kvh + 1) * HEAD_DIM]
                        if state_slot is None:
                            ref[s, :, kvh, :] = val
                        else:
                            ref[s, state_slot, :, kvh, :] = val

    i = pl.program_id(0)
    pl.when(i < n_ctx_blocks)(lambda: body(xc_ref, False))
    pl.when(i >= n_ctx_blocks)(lambda: body(xl_ref, True))


def _qkv_project(x_ctx, x_lat, g, mod, w, layer, cos, sin, hg, state, *, n_ctx_rows, lat_len, ctx_len, tm):
    d = x_ctx.shape[1]
    seqs = tm // ctx_len
    n_layers = w.shape[0]
    state_shape = (n_ctx_rows // ctx_len, n_layers, ctx_len, N_KV_HEADS, HEAD_DIM)
    last_ctx = n_ctx_rows // tm - 1
    if state is None:
        state_spec = pl.BlockSpec((seqs, n_layers, ctx_len, N_KV_HEADS, HEAD_DIM),
                                  lambda i: (jnp.minimum(i, last_ctx), 0, 0, 0, 0))
        state_specs, state_args, aliases = [], [], {}
    else:
        state_spec = pl.BlockSpec((seqs, None, ctx_len, N_KV_HEADS, HEAD_DIM),
                                  lambda i: (jnp.minimum(i, last_ctx), layer, 0, 0, 0))
        state_specs, state_args = [pl.BlockSpec(memory_space=pl.ANY)] * 2, list(state)
        aliases = {9: 1, 10: 2}
    n = w.shape[2]
    ncb = n_ctx_rows // tm
    t = n_ctx_rows + (x_ctx.shape[0] - n_ctx_rows if x_lat is None else x_lat.shape[0])
    row = functools.partial(_mod_row, tm=tm, n_ctx_rows=n_ctx_rows, lat_len=lat_len)
    tab = lambda i: ((jnp.maximum(i - ncb, 0)) % (lat_len // tm), 0)
    qk_norm = hg is not None
    if hg is None:
        hg = jnp.ones((2, HEAD_DIM), F32)
    x_specs, x_args = _two_source_specs(x_ctx, x_lat, tm, ncb)
    return pl.pallas_call(
        functools.partial(_qkv_kernel, qk_norm=qk_norm, tn=512, halves=2, n_ctx_blocks=ncb,
                          state_slot=layer if state is None else None),
        grid=(t // tm,),
        in_specs=x_specs + [
            pl.BlockSpec((1, d), lambda i: (0, 0)),
            pl.BlockSpec((1, 1, d), lambda i: (row(i) * 6, 0, 0)),
            pl.BlockSpec((1, 1, d), lambda i: (row(i) * 6 + 1, 0, 0)),
            pl.BlockSpec((None, d, n), lambda i: (layer, 0, 0), pipeline_mode=pl.Buffered(1)),
            pl.BlockSpec((tm, HEAD_DIM), tab), pl.BlockSpec((tm, HEAD_DIM), tab),
            pl.BlockSpec((2, HEAD_DIM), lambda i: (0, 0))] + state_specs,
        out_specs=[pl.BlockSpec((tm, n), lambda i: (i, 0)), state_spec, state_spec],
        out_shape=[jax.ShapeDtypeStruct((t, n), F32), jax.ShapeDtypeStruct(state_shape, F32),
                   jax.ShapeDtypeStruct(state_shape, F32)],
        input_output_aliases=aliases,
        compiler_params=_params("arbitrary"),
        name="qkv_project",
    )(*x_args, g.reshape(1, d), mod, mod, w, cos, sin, hg, *state_args)


def _proj_res_kernel(ac_ref, al_ref, xc_ref, xl_ref, w_ref, gate_ref, g2_ref, sh2_ref, sc2_ref, o_ref, h_ref, *,
                     n_ctx_blocks):
    i = pl.program_id(0)

    def emit(a_ref, x_ref):
        xn = x_ref[...] + gate_ref[0] * jnp.dot(a_ref[...], w_ref[...], preferred_element_type=F32)
        o_ref[...] = xn
        h_ref[...] = _norm_mod(xn, g2_ref[...], sh2_ref[0], sc2_ref[0]).astype(BF16)

    pl.when(i < n_ctx_blocks)(lambda: emit(ac_ref, xc_ref))
    pl.when(i >= n_ctx_blocks)(lambda: emit(al_ref, xl_ref))


def _proj_residual(a_ctx, a_lat, w, layer, x_ctx, x_lat, mod, g2, *, n_ctx_rows, lat_len, tm):
    d = x_ctx.shape[1]
    k = w.shape[1]
    ncb = n_ctx_rows // tm
    t = a_ctx.shape[0] + a_lat.shape[0]
    row = functools.partial(_mod_row, tm=tm, n_ctx_rows=n_ctx_rows, lat_len=lat_len)
    modspec = lambda which: pl.BlockSpec((1, 1, d), lambda i: (row(i) * 6 + which, 0, 0))
    x_specs, x_args = _two_source_specs(x_ctx, x_lat, tm, ncb)
    return pl.pallas_call(
        functools.partial(_proj_res_kernel, n_ctx_blocks=ncb),
        grid=(t // tm,),
        in_specs=[pl.BlockSpec((tm, k), lambda i: (jnp.minimum(i, ncb - 1), 0)),
                  pl.BlockSpec((tm, k), lambda i: (jnp.maximum(i - ncb, 0), 0))]
        + x_specs
        + [pl.BlockSpec((None, k, d), lambda i: (layer, 0, 0), pipeline_mode=pl.Buffered(1)),
           modspec(2),
           pl.BlockSpec((1, d), lambda i: (0, 0)),
           modspec(3), modspec(4)],
        out_specs=[pl.BlockSpec((tm, d), lambda i: (i, 0)), pl.BlockSpec((tm, d), lambda i: (i, 0))],
        out_shape=[jax.ShapeDtypeStruct((t, d), F32), jax.ShapeDtypeStruct((t, d), BF16)],
        compiler_params=_params("parallel"),
        name="proj_residual",
    )(a_ctx, a_lat, *x_args, w, mod, g2.reshape(1, d), mod, mod)


def _ffn_kernel(x_ref, h_ref, gate_ref, wg_ref, wu_ref, wd_ref, o_ref):
    f = pl.program_id(1)

    @pl.when(f == 0)
    def _():
        o_ref[...] = jnp.zeros_like(o_ref)

    h = h_ref[...]
    gv = jnp.dot(h, wg_ref[...], preferred_element_type=F32)
    uv = jnp.dot(h, wu_ref[...], preferred_element_type=F32)
    act = (_silu(gv) * uv).astype(BF16)
    o_ref[...] += jnp.dot(act, wd_ref[...], preferred_element_type=F32)

    @pl.when(f == pl.num_programs(1) - 1)
    def _():
        o_ref[...] = x_ref[...] + gate_ref[0] * o_ref[...]


def _ffn(x, h, mod, w_gu, w_down, layer, *, n_ctx_rows, lat_len, tm, tf):
    t, d = x.shape
    ff = w_down.shape[1]
    nf = ff // tf
    row = functools.partial(_mod_row, tm=tm, n_ctx_rows=n_ctx_rows, lat_len=lat_len)
    return pl.pallas_call(
        _ffn_kernel,
        grid=(t // tm, nf),
        in_specs=[pl.BlockSpec((tm, d), lambda i, f: (i, 0)),
                  pl.BlockSpec((tm, d), lambda i, f: (i, 0)),
                  pl.BlockSpec((1, 1, d), lambda i, f: (row(i) * 6 + 5, 0, 0)),
                  pl.BlockSpec((None, d, tf), lambda i, f: (layer, 0, f)),
                  pl.BlockSpec((None, d, tf), lambda i, f: (layer, 0, nf + f)),
                  pl.BlockSpec((None, tf, d), lambda i, f: (layer, f, 0))],
        out_specs=pl.BlockSpec((tm, d), lambda i, f: (i, 0)),
        out_shape=jax.ShapeDtypeStruct((t, d), F32),
        compiler_params=_params("parallel", "arbitrary"),
        name="ffn_swiglu",
    )(x, h, mod, w_gu, w_gu, w_down)


def _stack_heads(q):
    return jnp.concatenate([q[:, h * HEAD_DIM:(h + 1) * HEAD_DIM] for h in range(GROUP)], axis=0)


def _unstack_heads(o, rows):
    return jnp.concatenate([o[h * rows:(h + 1) * rows] for h in range(GROUP)], axis=1)


def _sink_column(sink_ref, kh, rows):
    head = lax.broadcasted_iota(jnp.int32, (GROUP * rows, 1), 0) // rows
    col = jnp.zeros((GROUP * rows, 1), F32)
    for h in range(GROUP):
        col = jnp.where(head == h, sink_ref[kh * GROUP + h], col)
    return col


def _softmax_av(s, v, sink_col):
    m = jnp.max(s, axis=-1, keepdims=True)
    if sink_col is not None:
        m = jnp.maximum(m, sink_col)
    p = jnp.exp(s - m)
    l = jnp.sum(p, axis=-1, keepdims=True)
    if sink_col is not None:
        l = l + jnp.exp(sink_col - m)
    return jnp.dot(p.astype(BF16), v, preferred_element_type=F32) / l


def _attn_ctx_kernel(sink_ref, q_ref, k_ref, v_ref, o_ref, *, use_sink):
    rows = q_ref.shape[0]
    qw = GROUP * HEAD_DIM
    for kh in range(N_KV_HEADS):
        qs = _stack_heads(q_ref[:, kh * qw:(kh + 1) * qw] * SCALE).astype(BF16)
        k = k_ref[:, kh * HEAD_DIM:(kh + 1) * HEAD_DIM].astype(BF16)
        v = v_ref[:, kh * HEAD_DIM:(kh + 1) * HEAD_DIM].astype(BF16)
        s = lax.dot_general(qs, k, NT_DIMS, preferred_element_type=F32)
        sink_col = _sink_column(sink_ref, kh, rows) if use_sink else None
        o_ref[:, kh * qw:(kh + 1) * qw] = _unstack_heads(_softmax_av(s, v, sink_col), rows).astype(BF16)


def _attn_ctx(y, sink, *, n_ctx_seqs, ctx_len, use_sink):
    t = n_ctx_seqs * ctx_len
    q_w, kv_w = N_HEADS * HEAD_DIM, N_KV_HEADS * HEAD_DIM
    return pl.pallas_call(
        functools.partial(_attn_ctx_kernel, use_sink=use_sink),
        grid=(n_ctx_seqs,),
        in_specs=[pl.BlockSpec(memory_space=pltpu.SMEM),
                  pl.BlockSpec((ctx_len, q_w), lambda b: (b, 0)),
                  pl.BlockSpec((ctx_len, kv_w), lambda b: (b, q_w // kv_w)),
                  pl.BlockSpec((ctx_len, kv_w), lambda b: (b, q_w // kv_w + 1))],
        out_specs=pl.BlockSpec((ctx_len, q_w), lambda b: (b, 0)),
        out_shape=jax.ShapeDtypeStruct((t, q_w), BF16),
        compiler_params=_params("parallel"),
        name="attn_context",
    )(sink, y, y, y)


def _keys_by_queries_attention(q_ref, k_scr, vt_scr, s_scr, chunks, bias_ref, sink_ref, o_ref):
    qn = q_ref.shape[0]
    kh = pl.program_id(1)

    def logits(h, ci):
        st, sz = chunks[ci]
        q = (q_ref[:, h * HEAD_DIM:(h + 1) * HEAD_DIM] * (SCALE * LOG2E)).astype(BF16)
        st_ = lax.dot_general(k_scr[st:st + sz, :], q, NT_DIMS, preferred_element_type=F32)
        if bias_ref is not None and ci == 0:
            st_ = st_ + bias_ref[...]
        s_scr[st:st + sz, h * qn:(h + 1) * qn] = st_
        return jnp.max(st_, axis=0, keepdims=True)

    def weigh(h, ci, m):
        st, sz = chunks[ci]
        p = jnp.exp2(s_scr[st:st + sz, h * qn:(h + 1) * qn] - m)
        pv = jnp.dot(vt_scr[:, st:st + sz], p.astype(BF16), preferred_element_type=F32)
        return jnp.sum(p, axis=0, keepdims=True), pv

    n = len(chunks)
    maxes = [logits(0, ci) for ci in range(n)]
    for h in range(GROUP):
        m = functools.reduce(jnp.maximum, maxes)
        if sink_ref is not None:
            sink2 = sink_ref[kh * GROUP + h] * LOG2E
            m = jnp.maximum(m, sink2)
        maxes, l, acc = [], None, None
        for ci in range(n):
            if h + 1 < GROUP:
                maxes.append(logits(h + 1, ci))
            lc, pv = weigh(h, ci, m)
            l, acc = (lc, pv) if l is None else (l + lc, acc + pv)
        if sink_ref is not None:
            l = l + jnp.exp2(sink2 - m)
        o_ref[:, h * HEAD_DIM:(h + 1) * HEAD_DIM] = (acc * (1.0 / l)).T.astype(BF16)


def _attn_lat_win_kernel(sink_ref, q_ref, kp_ref, kc_ref, kn_ref, vp_ref, vc_ref, vn_ref, ck_ref, cv_ref, bias_ref,
                         o_ref, k_scr, vt_scr, s_scr):
    past = ck_ref.shape[0]
    win = Q_STEP + 2 * BLOCK

    @pl.when(pl.program_id(2) == 0)
    def _():
        k_scr[win:, :] = ck_ref[...].astype(BF16)
        vt_scr[:, win:] = cv_ref[...].T.astype(BF16)

    row = 0
    for k_ref, v_ref in ((kp_ref, vp_ref), (kc_ref, vc_ref), (kn_ref, vn_ref)):
        n = k_ref.shape[0]
        k_scr[row:row + n, :] = k_ref[...].astype(BF16)
        vt_scr[:, row:row + n] = v_ref[...].T.astype(BF16)
        row += n
    _keys_by_queries_attention(q_ref, k_scr, vt_scr, s_scr, ((0, win), (win, past)), bias_ref, sink_ref, o_ref)


def _window_bias(n_steps):
    win = Q_STEP + 2 * BLOCK
    c = jnp.arange(win, dtype=jnp.int32)[:, None]
    r = jnp.arange(Q_STEP, dtype=jnp.int32)[None, :]
    band = jnp.abs(r + BLOCK - c) <= WINDOW
    variants = [band & (c >= BLOCK), band, band & (c < win - BLOCK)]
    if n_steps == 1:
        variants = [variants[0] & (c < win - BLOCK)] * 3
    return jnp.where(jnp.stack(variants), 0.0, NEG).astype(F32)


def _cache_spec(cache, layer):
    return pl.BlockSpec((None, None, cache.shape[2], HEAD_DIM), lambda b, kh, qs: (b, layer, 0, kh))


def _attn_lat_win(y, sink, cache_k, cache_v, layer, *, n_ctx_rows, lat_batch, lat_len):
    qw = GROUP * HEAD_DIM
    nb = lat_len // BLOCK
    ns = lat_len // Q_STEP
    per = Q_STEP // BLOCK
    base = n_ctx_rows // BLOCK
    past = cache_k.shape[2]
    n_keys = Q_STEP + 2 * BLOCK + past

    def edge(col0, blk):
        return pl.BlockSpec((BLOCK, HEAD_DIM),
                            lambda b, kh, qs: (base + b * nb + jnp.clip(qs * per + blk, 0, nb - 1), col0 + kh))

    def centre(col0):
        return pl.BlockSpec((Q_STEP, HEAD_DIM), lambda b, kh, qs: (base // per + b * ns + qs, col0 + kh))

    kcol, vcol = N_HEADS, N_HEADS + N_KV_HEADS
    cache_spec = _cache_spec(cache_k, layer)
    return pl.pallas_call(
        _attn_lat_win_kernel,
        grid=(lat_batch, N_KV_HEADS, ns),
        in_specs=[pl.BlockSpec(memory_space=pltpu.SMEM),
                  pl.BlockSpec((Q_STEP, qw), lambda b, kh, qs: (base // per + b * ns + qs, kh)),
                  edge(kcol, -1), centre(kcol), edge(kcol, per),
                  edge(vcol, -1), centre(vcol), edge(vcol, per),
                  cache_spec, cache_spec,
                  pl.BlockSpec((None, Q_STEP + 2 * BLOCK, Q_STEP),
                               lambda b, kh, qs: (jnp.where(qs == 0, 0, jnp.where(qs == ns - 1, 2, 1)), 0, 0))],
        out_specs=pl.BlockSpec((Q_STEP, qw), lambda b, kh, qs: (b * ns + qs, kh)),
        out_shape=jax.ShapeDtypeStruct((lat_batch * lat_len, N_HEADS * HEAD_DIM), BF16),
        scratch_shapes=[pltpu.VMEM((n_keys, HEAD_DIM), BF16), pltpu.VMEM((HEAD_DIM, n_keys), BF16),
                        pltpu.VMEM((n_keys, GROUP * Q_STEP), F32)],
        compiler_params=_params("parallel", "parallel", "arbitrary"),
        name="attn_latent_window",
    )(sink, y, y, y, y, y, y, y, cache_k, cache_v, _window_bias(ns))


def _attn_lat_full_kernel(q_ref, k_ref, v_ref, ck_ref, cv_ref, o_ref, k_scr, vt_scr, s_scr, *, chunk):
    lat_len = k_ref.shape[0]
    n_keys = k_scr.shape[0]

    @pl.when(pl.program_id(2) == 0)
    def _():
        k_scr[0:lat_len, :] = k_ref[...].astype(BF16)
        k_scr[lat_len:, :] = ck_ref[...].astype(BF16)
        vt_scr[:, 0:lat_len] = v_ref[...].T.astype(BF16)
        vt_scr[:, lat_len:] = cv_ref[...].T.astype(BF16)

    chunks = tuple((st, chunk) for st in range(0, n_keys, chunk))
    _keys_by_queries_attention(q_ref, k_scr, vt_scr, s_scr, chunks, None, None, o_ref)


def _attn_lat_full(y, cache_k, cache_v, layer, *, n_ctx_rows, lat_batch, lat_len):
    qw = GROUP * HEAD_DIM
    ns = lat_len // Q_STEP
    base = n_ctx_rows // Q_STEP
    seq_base = n_ctx_rows // lat_len
    past = cache_k.shape[2]
    cache_spec = _cache_spec(cache_k, layer)
    n_keys = lat_len + past
    chunk = 512
    assert n_keys % chunk == 0
    return pl.pallas_call(
        functools.partial(_attn_lat_full_kernel, chunk=chunk),
        grid=(lat_batch, N_KV_HEADS, ns),
        in_specs=[pl.BlockSpec((Q_STEP, qw), lambda b, kh, qs: (base + b * ns + qs, kh)),
                  pl.BlockSpec((lat_len, HEAD_DIM), lambda b, kh, qs: (seq_base + b, N_HEADS + kh)),
                  pl.BlockSpec((lat_len, HEAD_DIM), lambda b, kh, qs: (seq_base + b, N_HEADS + N_KV_HEADS + kh)),
                  cache_spec, cache_spec],
        out_specs=pl.BlockSpec((Q_STEP, qw), lambda b, kh, qs: (b * ns + qs, kh)),
        out_shape=jax.ShapeDtypeStruct((lat_batch * lat_len, N_HEADS * HEAD_DIM), BF16),
        scratch_shapes=[pltpu.VMEM((n_keys, HEAD_DIM), BF16), pltpu.VMEM((HEAD_DIM, n_keys), BF16),
                        pltpu.VMEM((n_keys, GROUP * Q_STEP), F32)],
        compiler_params=_params("parallel", "parallel", "arbitrary"),
        name="attn_latent_full",
    )(y, y, y, cache_k, cache_v)


def _rope_tables(lat_len):
    half = HEAD_DIM // 2
    n_rows = lat_len // GRID_W
    row = jnp.repeat(jnp.arange(n_rows, dtype=F32), GRID_W)
    col = jnp.tile(jnp.arange(GRID_W, dtype=F32), n_rows)
    inv = ROPE_THETA ** (-jnp.arange(0, half, 2, dtype=F32) / half)
    ang_r, ang_c = row[:, None] * inv, col[:, None] * inv
    cos = jnp.concatenate([jnp.cos(ang_r)] * 2 + [jnp.cos(ang_c)] * 2, axis=-1)
    sin = jnp.concatenate([-jnp.sin(ang_r), jnp.sin(ang_r), -jnp.sin(ang_c), jnp.sin(ang_c)], axis=-1)
    return cos, sin


def _dft_mats(seq_len, tk):
    n = 2 * seq_len
    k = jnp.arange(seq_len, dtype=jnp.int32)
    step = 64
    a = jnp.arange(seq_len // step, dtype=jnp.int32) * step
    b = jnp.arange(step, dtype=jnp.int32)
    ang_a = ((k[:, None] * a[None, :]) % n).astype(F32) * (2.0 * math.pi / n)
    ang_b = ((k[:, None] * b[None, :]) % n).astype(F32) * (2.0 * math.pi / n)
    ca, sa = jnp.cos(ang_a)[:, :, None], jnp.sin(ang_a)[:, :, None]
    cb, sb = jnp.cos(ang_b)[:, None, :], jnp.sin(ang_b)[:, None, :]
    fc = (ca * cb - sa * sb).reshape(seq_len, seq_len)
    fs = -(sa * cb + ca * sb).reshape(seq_len, seq_len)
    sign = jnp.where(k % 2 == 0, 1.0, -1.0).astype(F32)
    fs = jnp.where(k[:, None] == 0, sign[None, :], fs)
    kfc = seq_len // tk
    ana = jnp.concatenate([fc.reshape(kfc, tk, seq_len), fs.reshape(kfc, tk, seq_len)], axis=1).astype(BF16)
    ana = lax.optimization_barrier(ana)
    return ana, ana.transpose(0, 2, 1)


def _hy_filter_kernel(feat_ref, w1_ref, b1_ref, w2_ref, b2_ref, fq_ref, w3f_ref, w3b_ref, tn_ref, dl_ref,
                      ana_ref, hr_ref, hi_ref, p_scr, q_scr, inv_scr, nyq_scr):
    kf = pl.program_id(1)
    hp = lax.Precision.HIGHEST
    tk = hr_ref.shape[0]

    @pl.when(kf == 0)
    def _():
        a = jnp.dot(feat_ref[...], w1_ref[...], precision=hp, preferred_element_type=F32) + b1_ref[...]
        a = jnp.sin(fq_ref[0:1, :] * a)
        a = jnp.dot(a, w2_ref[...], precision=hp, preferred_element_type=F32) + b2_ref[...]
        a = jnp.sin(fq_ref[1:2, :] * a)
        decay = jnp.exp(-tn_ref[...] * dl_ref[...])
        fwd = jnp.dot(a, w3f_ref[...], precision=hp, preferred_element_type=F32) * decay
        bwd = jnp.dot(a, w3b_ref[...], precision=hp, preferred_element_type=F32) * decay
        rowi = lax.broadcasted_iota(jnp.int32, fwd.shape, 0)
        bwd = jnp.where(rowi == 0, 0.0, bwd)
        norm = jnp.sum(jnp.abs(fwd), axis=0, keepdims=True) + jnp.sum(jnp.abs(bwd), axis=0, keepdims=True) + EPS
        inv_scr[...] = 1.0 / norm
        p = fwd + bwd
        nyq_scr[...] = jnp.sum(jnp.where((rowi & 1) == 0, p, -p), axis=0, keepdims=True)
        p_scr[...] = p.astype(BF16)
        q_scr[...] = (fwd - bwd).astype(BF16)

    inv = inv_scr[...]
    hr_ref[...] = jnp.dot(ana_ref[0:tk, :], p_scr[...], preferred_element_type=F32) * inv
    hi = jnp.dot(ana_ref[tk:, :], q_scr[...], preferred_element_type=F32)
    rowk = lax.broadcasted_iota(jnp.int32, hi.shape, 0)
    hi = jnp.where((rowk == 0) & (kf == 0), nyq_scr[...], hi)
    hi_ref[...] = hi * inv


def _hy_filters(seq_len, f_w1, f_b1, f_w2, f_b2, f_w3, freq, ana_mats, *, tc):
    tk = ana_mats.shape[1] // 2
    hid = f_w1.shape[1]
    nd = f_w3.shape[1] // 2
    d = nd // HY_ORDER
    t = jnp.arange(seq_len, dtype=F32)
    tnorm = t / max(seq_len - 1, 1)
    bands = (HY_EMB - 1) // 2
    fb = jnp.linspace(1e-4, bands - 1, bands, dtype=F32)
    w = 2.0 * math.pi * t / seq_len
    feats = jnp.concatenate([tnorm[:, None], jnp.cos(w[:, None] * fb), -jnp.sin(w[:, None] * fb)], axis=-1)
    emb = 64
    feats = jnp.pad(feats, ((0, 0), (0, emb - HY_EMB)))
    w1 = jnp.pad(f_w1, ((0, emb - HY_EMB), (0, 0)))
    deltas = jnp.abs(jnp.linspace(HY_MIN_DECAY, HY_MAX_DECAY, d, dtype=F32))
    deltas = jnp.tile(deltas, HY_ORDER).reshape(1, nd)
    nct = nd // tc
    const = lambda shape: pl.BlockSpec(shape, lambda c, kf: (0, 0))
    out = jax.ShapeDtypeStruct((seq_len, nd), F32)
    return pl.pallas_call(
        _hy_filter_kernel,
        grid=(nct, seq_len // tk),
        in_specs=[const((seq_len, emb)), const((emb, hid)), const((1, hid)), const((hid, hid)), const((1, hid)),
                  const((2, hid)),
                  pl.BlockSpec((hid, tc), lambda c, kf: (0, c)),
                  pl.BlockSpec((hid, tc), lambda c, kf: (0, nct + c)),
                  const((seq_len, 1)),
                  pl.BlockSpec((1, tc), lambda c, kf: (0, c)),
                  pl.BlockSpec((None, 2 * tk, seq_len), lambda c, kf: (kf, 0, 0))],
        out_specs=[pl.BlockSpec((tk, tc), lambda c, kf: (kf, c)),
                   pl.BlockSpec((tk, tc), lambda c, kf: (kf, c))],
        out_shape=[out, out],
        scratch_shapes=[pltpu.VMEM((seq_len, tc), BF16), pltpu.VMEM((seq_len, tc), BF16),
                        pltpu.VMEM((1, tc), F32), pltpu.VMEM((1, tc), F32)],
        compiler_params=_params("parallel", "arbitrary"),
        name="hyena_filter",
    )(feats, w1, f_b1.reshape(1, hid), f_w2, f_b2.reshape(1, hid), freq, f_w3, f_w3,
      tnorm.reshape(seq_len, 1), deltas, ana_mats)


def _hy_inproj_kernel(x_ref, g_ref, sh_ref, sc_ref, w_ref, cw_ref, cb_ref, o_ref, h_scr, y_scr, *, n_ctx_blocks,
                      ctx_len):
    j = pl.program_id(1)
    n = x_ref.shape[0]

    @pl.when(j == 0)
    def _():
        for r0 in range(0, n, 256):
            rs = slice(r0, r0 + 256)
            h_scr[rs, :] = _norm_mod(x_ref[rs, :], g_ref[...], sh_ref[0], sc_ref[0]).astype(BF16)
        y_scr[0:PAD, :] = jnp.zeros((PAD, y_scr.shape[1]), F32)
        y_scr[PAD + n:, :] = jnp.zeros((PAD, y_scr.shape[1]), F32)

    is_ctx = pl.program_id(0) < n_ctx_blocks
    chunk = 256

    def product(r0):
        y_scr[PAD + r0:PAD + r0 + chunk, :] = jnp.dot(h_scr[r0:r0 + chunk, :], w_ref[...],
                                                      preferred_element_type=F32)

    product(0)
    for r0 in range(0, n, chunk):
        if r0 + chunk < n:
            product(r0 + chunk)
        pos = (r0 + lax.broadcasted_iota(jnp.int32, (chunk, y_scr.shape[1]), 0)) % ctx_len
        prev = jnp.where(is_ctx & (pos == 0), 0.0, y_scr[PAD - 1 + r0:PAD - 1 + r0 + chunk, :])
        cur = y_scr[PAD + r0:PAD + r0 + chunk, :]
        nxt = jnp.where(is_ctx & (pos == ctx_len - 1), 0.0, y_scr[PAD + 1 + r0:PAD + 1 + r0 + chunk, :])
        o_ref[r0:r0 + chunk, :] = (prev * cw_ref[0:1, :] + cur * cw_ref[1:2, :] + nxt * cw_ref[2:3, :]
                                   + cb_ref[...])


def _hy_inproj(x, g, mod, w, conv_w, conv_b, *, n_ctx_rows, lat_len, ctx_len, tn):
    t, d = x.shape
    n = w.shape[1]
    tm = lat_len
    row = functools.partial(_mod_row, tm=tm, n_ctx_rows=n_ctx_rows, lat_len=lat_len)
    return pl.pallas_call(
        functools.partial(_hy_inproj_kernel, n_ctx_blocks=n_ctx_rows // tm, ctx_len=ctx_len),
        grid=(t // tm, n // tn),
        in_specs=[pl.BlockSpec((tm, d), lambda i, j: (i, 0), pipeline_mode=pl.Buffered(1)),
                  pl.BlockSpec((1, d), lambda i, j: (0, 0)),
                  pl.BlockSpec((1, 1, d), lambda i, j: (row(i) * 6, 0, 0)),
                  pl.BlockSpec((1, 1, d), lambda i, j: (row(i) * 6 + 1, 0, 0)),
                  pl.BlockSpec((d, tn), lambda i, j: (0, j)),
                  pl.BlockSpec((3, tn), lambda i, j: (0, j)),
                  pl.BlockSpec((1, tn), lambda i, j: (0, j))],
        out_specs=pl.BlockSpec((tm, tn), lambda i, j: (i, j)),
        out_shape=jax.ShapeDtypeStruct((t, n), F32),
        scratch_shapes=[pltpu.VMEM((tm, d), BF16), pltpu.VMEM((tm + 2 * PAD, tn), F32)],
        compiler_params=_params("parallel", "arbitrary"),
        name="hyena_inproj",
    )(x, g.reshape(1, d), mod, mod, w, conv_w, conv_b.reshape(1, n))


def _hy_conv_kernel(a_ref, g_ref, skip_ref, ana_ref, syn_ref, hr_ref, hi_ref, o_ref, acc_scr, u_scr=None, *,
                    seq_len):
    kf = pl.program_id(2)
    rows = a_ref.shape[0]
    tk = hr_ref.shape[0]
    chunk = 256

    @pl.when(kf == 0)
    def _():
        acc_scr[...] = jnp.zeros_like(acc_scr)
        if u_scr is not None:
            for r0 in range(0, rows, chunk):
                u_scr[r0:r0 + chunk, :] = a_ref[r0:r0 + chunk, :].astype(BF16)

    u_ref = a_ref if u_scr is None else u_scr

    hr = hr_ref[...]
    hi = hi_ref[...]
    first = (lax.broadcasted_iota(jnp.int32, hr.shape, 0) == 0) & (kf == 0)
    for seg in range(0, rows, seq_len):
        uf = jnp.dot(ana_ref[...], u_ref[seg:seg + seq_len, :], preferred_element_type=F32)
        ur, ui = uf[0:tk, :], uf[tk:, :]
        yr = ur * hr - ui * hi
        yi = ur * hi + ui * hr
        yr = jnp.where(first, 0.5 * (ur * hr), yr)
        yi = jnp.where(first, 0.5 * (ui * hi), yi)
        yf = jnp.concatenate([yr, yi], axis=0).astype(BF16)
        step = min(seq_len, 512)
        for n0 in range(0, seq_len, step):
            acc_scr[seg + n0:seg + n0 + step, :] += jnp.dot(syn_ref[n0:n0 + step, :], yf,
                                                            preferred_element_type=F32)

    @pl.when(kf == pl.num_programs(2) - 1)
    def _():
        for r0 in range(0, rows, chunk):
            rs = slice(r0, r0 + chunk)
            a = a_ref[rs, :].astype(F32)
            o_ref[rs, :] = (g_ref[rs, :] * (acc_scr[rs, :] * (1.0 / seq_len) + a * skip_ref[...])).astype(BF16)


def _hy_conv(a, a_part, a_base, g, g_part, g_base, skip_row, mats, hr, hi, order, *,
             n_blocks, rows, seq_len, tc):
    d = skip_row.shape[1]
    nct = d // tc
    ana_mats, syn_mats = mats
    kfc = ana_mats.shape[0]
    tk = ana_mats.shape[1] // 2

    def amap(part, base):
        return lambda b, c, kf: (base + b, part * nct + c)

    hmap = lambda b, c, kf: (kf, order * nct + c)
    return pl.pallas_call(
        functools.partial(_hy_conv_kernel, seq_len=seq_len),
        grid=(n_blocks, nct, kfc),
        in_specs=[pl.BlockSpec((rows, tc), amap(a_part, a_base)),
                  pl.BlockSpec((rows, tc), amap(g_part, g_base)),
                  pl.BlockSpec((1, tc), lambda b, c, kf: (0, c)),
                  pl.BlockSpec((None, 2 * tk, seq_len), lambda b, c, kf: (kf, 0, 0)),
                  pl.BlockSpec((None, seq_len, 2 * tk), lambda b, c, kf: (kf, 0, 0)),
                  pl.BlockSpec((tk, tc), hmap), pl.BlockSpec((tk, tc), hmap)],
        out_specs=pl.BlockSpec((rows, tc), lambda b, c, kf: (b, c)),
        out_shape=jax.ShapeDtypeStruct((n_blocks * rows, d), BF16),
        scratch_shapes=[pltpu.VMEM((rows, tc), F32)] + ([] if a.dtype == BF16 else [pltpu.VMEM((rows, tc), BF16)]),
        compiler_params=_params("parallel", "parallel", "arbitrary"),
        name="hyena_conv",
    )(a, g, skip_row, ana_mats, syn_mats, hr, hi)


def _final_norm_kernel(x_ref, g_ref, o_ref):
    x = x_ref[...]
    o_ref[...] = x * lax.rsqrt(jnp.mean(x * x, axis=-1, keepdims=True) + EPS) * g_ref[...]


def _final_norm(x, g, *, row0, rows, tm):
    d = x.shape[1]
    base = row0 // tm
    return pl.pallas_call(
        _final_norm_kernel,
        grid=(rows // tm,),
        in_specs=[pl.BlockSpec((tm, d), lambda i: (base + i, 0)),
                  pl.BlockSpec((1, d), lambda i: (0, 0))],
        out_specs=pl.BlockSpec((tm, d), lambda i: (i, 0)),
        out_shape=jax.ShapeDtypeStruct((rows, d), F32),
        compiler_params=_params("parallel"),
        name="final_norm",
    )(x, g.reshape(1, d))


def kernel(x_prompt, x_sample, cache_win_k, cache_win_v, cache_ax_k, cache_ax_v, c, c_ctx, norm_mix_g, norm_ffn_g, mod_w, mod_b, win_wqkv, win_wo, win_sink, hy_w_in, hy_conv_w, hy_conv_b, hy_f_w1, hy_f_b1, hy_f_w2, hy_f_b2, hy_f_w3, hy_freq, hy_skip, hy_wo, ax_wqkv, ax_q_g, ax_k_g, ax_wo, ffn_w_gu, ffn_w_down, final_g):
    n_ctx_seqs, ctx_len, d = x_prompt.shape
    lat_batch, lat_len, _ = x_sample.shape
    depth = mod_w.shape[0]
    past = cache_win_k.shape[2]
    n_ctx_rows = n_ctx_seqs * ctx_len
    n_lat_rows = lat_batch * lat_len
    tm = 512
    assert n_ctx_rows % tm == 0 and lat_len % tm == 0 and n_ctx_rows % lat_len == 0
    assert lat_batch + 1 <= MOD_ROWS
    geom = dict(n_ctx_rows=n_ctx_rows, lat_len=lat_len, tm=tm)

    x, x_lat = x_prompt.reshape(n_ctx_rows, d), x_sample.reshape(n_lat_rows, d)
    cvec = jnp.concatenate([c_ctx[None, :], c, jnp.zeros((MOD_ROWS - 1 - lat_batch, d), F32)], axis=0)
    mod_all = _adaln_all(cvec, mod_w, mod_b)
    cos, sin = _rope_tables(lat_len)

    w_gu, w_down = ffn_w_gu.astype(BF16), ffn_w_down.astype(BF16)
    wqkv_all = {0: win_wqkv.astype(BF16), 2: ax_wqkv.astype(BF16)}
    wo_all = {0: win_wo.astype(BF16), 1: hy_wo.astype(BF16), 2: ax_wo.astype(BF16)}

    states = {0: None, 2: None}
    kv_w = N_KV_HEADS * HEAD_DIM
    for i in range(depth):
        mod = mod_all[i].reshape(MOD_ROWS * 6, 1, d)
        kind, j = i % 3, i // 3
        if kind == 1:
            assert x_lat is None
            u = _hy_inproj(x, norm_mix_g[i], mod, hy_w_in[j].astype(BF16), hy_conv_w[j], hy_conv_b[j],
                           n_ctx_rows=n_ctx_rows, lat_len=lat_len, ctx_len=ctx_len, tn=512)
            rows = lat_len
            zs = []
            for (base, nblk, sl) in ((0, n_ctx_rows // rows, ctx_len), (n_ctx_rows // rows, lat_batch, lat_len)):
                mats = _dft_mats(sl, min(512, sl))
                hr, hi = _hy_filters(sl, hy_f_w1[j], hy_f_b1[j], hy_f_w2[j], hy_f_b2[j], hy_f_w3[j], hy_freq[j],
                                     mats[0], tc=512)
                conv = functools.partial(_hy_conv, mats=mats, hr=hr, hi=hi, n_blocks=nblk, rows=rows, seq_len=sl,
                                         tc=512)
                z1 = conv(u, 0, base, u, 1, base, skip_row=hy_skip[j, 0].reshape(1, d), order=0)
                zs.append(conv(z1, 0, 0, u, 2, base, skip_row=hy_skip[j, 1].reshape(1, d), order=1))
            x, h2 = _proj_residual(zs[0], zs[1], wo_all[kind], j, x, None, mod, norm_ffn_g[i], **geom)
        else:
            if kind == 0:
                sink, hg = win_sink[j], None
                ck, cv = cache_win_k, cache_win_v
            else:
                sink = jnp.zeros((N_HEADS,), F32)
                hg = jnp.stack([ax_q_g[j], ax_k_g[j]], axis=0)
                ck, cv = cache_ax_k, cache_ax_v
            ck = ck[:, j:j + 1].reshape(lat_batch, 1, past, kv_w)
            cv = cv[:, j:j + 1].reshape(lat_batch, 1, past, kv_w)
            y, k_c, v_c = _qkv_project(x, x_lat, norm_mix_g[i], mod, wqkv_all[kind], j, cos, sin, hg, states[kind],
                                       ctx_len=ctx_len, **geom)
            states[kind] = (k_c, v_c)
            o_c = _attn_ctx(y, sink, n_ctx_seqs=n_ctx_seqs, ctx_len=ctx_len, use_sink=kind == 0)
            lat_geom = dict(n_ctx_rows=n_ctx_rows, lat_batch=lat_batch, lat_len=lat_len)
            if kind == 0:
                o_l = _attn_lat_win(y, sink, ck, cv, 0, **lat_geom)
            else:
                o_l = _attn_lat_full(y, ck, cv, 0, **lat_geom)
            x, h2 = _proj_residual(o_c, o_l, wo_all[kind], j, x, x_lat, mod, norm_ffn_g[i], **geom)
            x_lat = None
        x = _ffn(x, h2, mod, w_gu, w_down, i, tf=512, **geom)

    y_prompt = _final_norm(x, final_g, row0=0, rows=n_ctx_rows, tm=tm).reshape(n_ctx_seqs, ctx_len, d)
    y_sample = _final_norm(x, final_g, row0=n_ctx_rows, rows=n_lat_rows, tm=tm).reshape(lat_batch, lat_len, d)
    return (y_prompt, y_sample) + states[0] + states[2]
```

```python
import functools
import math

import jax
import jax.numpy as jnp
from jax import lax
from jax.experimental import pallas as pl
from jax.experimental.pallas import tpu as pltpu

F32 = jnp.float32
BF16 = jnp.bfloat16

N_HEADS = 16
N_KV_HEADS = 4
GROUP = N_HEADS // N_KV_HEADS
HEAD_DIM = 128
WINDOW = 128
BLOCK = 128
GRID_W = 64
ROPE_THETA = 10000.0
HY_ORDER = 2
HY_EMB = 33
HY_TARGET = 1e-2
HY_FAST_DECAY = 0.3
HY_SLOW_DECAY = 1.5
HY_MIN_DECAY = math.log(HY_TARGET) / HY_SLOW_DECAY
HY_MAX_DECAY = math.log(HY_TARGET) / HY_FAST_DECAY
EPS = 1e-6
NEG = -1e30
SCALE = HEAD_DIM ** -0.5
LOG2E = math.log2(math.e)
Q_STEP = 2 * BLOCK

LANES = 128
PAD = 8
MOD_ROWS = 16
VMEM_LIMIT = 56 * 1024 * 1024

NT_DIMS = (((1,), (1,)), ((), ()))

def _params(*sem):
    return pltpu.CompilerParams(dimension_semantics=sem, vmem_limit_bytes=VMEM_LIMIT)


def _silu(x):
    return x * (1.0 / (1.0 + jnp.exp(-x)))


def _mod_row(i, tm, n_ctx_rows, lat_len):
    return jnp.maximum(i * tm - n_ctx_rows + lat_len, 0) // lat_len


def _mod_kernel(c_ref, w_ref, b_ref, o_ref):
    s = _silu(c_ref[...]).astype(BF16)
    o_ref[...] = jnp.dot(s, w_ref[0].astype(BF16), preferred_element_type=F32) + b_ref[0]


def _adaln_all(cvec, mod_w, mod_b):
    depth, d, n = mod_w.shape
    tn = 1024
    per = d // tn
    return pl.pallas_call(
        _mod_kernel,
        grid=(depth, n // tn),
        in_specs=[pl.BlockSpec((MOD_ROWS, d), lambda l, j: (0, 0)),
                  pl.BlockSpec((1, d, tn), lambda l, j: (l, 0, j)),
                  pl.BlockSpec((1, 1, tn), lambda l, j: (l, 0, j))],
        out_specs=pl.BlockSpec((None, MOD_ROWS, None, None, tn), lambda l, j: (l, 0, j // per, 0, j % per)),
        out_shape=jax.ShapeDtypeStruct((depth, MOD_ROWS, 6, 1, d), F32),
        compiler_params=_params("parallel", "parallel"),
        name="adaln_mod",
    )(cvec, mod_w, mod_b.reshape(depth, 1, n))


def _norm_mod(x, g, shift, scale):
    y = x * lax.rsqrt(jnp.mean(x * x, axis=-1, keepdims=True) + EPS) * g
    return y * (1.0 + scale) + shift


def _rope_rotate(a, cos, sin_signed):
    lane = lax.broadcasted_iota(jnp.int32, a.shape, 1)
    partner = jnp.where((lane & 32) == 0, pltpu.roll(a, 96, 1), pltpu.roll(a, 32, 1))
    return a * cos + partner * sin_signed


def _two_source_specs(x_ctx, x_lat, tm, n_ctx_blocks):
    d = x_ctx.shape[1]
    lat_base = n_ctx_blocks if x_lat is None else 0
    specs = [pl.BlockSpec((tm, d), lambda i: (jnp.minimum(i, n_ctx_blocks - 1), 0)),
             pl.BlockSpec((tm, d), lambda i: (lat_base + jnp.maximum(i - n_ctx_blocks, 0), 0))]
    return specs, [x_ctx, x_ctx if x_lat is None else x_lat]


def _qkv_kernel(*refs, qk_norm, tn, halves, n_ctx_blocks, state_slot):
    xc_ref, xl_ref, g_ref, sh_ref, sc_ref, w_ref, cos_ref, sin_ref, hg_ref = refs[:9]
    o_ref, ks_ref, vs_ref = refs[-3:]
    tm = xc_ref.shape[0]
    rh = tm // halves
    heads_per_tile = tn // HEAD_DIM
    kcol, vcol = N_HEADS * HEAD_DIM, (N_HEADS + N_KV_HEADS) * HEAD_DIM

    def body(x_ref, latent):
        for hf in range(halves):
            rs = slice(hf * rh, (hf + 1) * rh)
            h = _norm_mod(x_ref[rs, :], g_ref[...], sh_ref[0], sc_ref[0]).astype(BF16)
            for jt in range(w_ref.shape[1] // tn):
                acc = jnp.dot(h, w_ref[:, jt * tn:(jt + 1) * tn], preferred_element_type=F32)
                for hh in range(heads_per_tile):
                    head = jt * heads_per_tile + hh
                    a = acc[:, hh * HEAD_DIM:(hh + 1) * HEAD_DIM]
                    if head < N_HEADS + N_KV_HEADS:
                        if qk_norm:
                            gain = hg_ref[0:1, :] if head < N_HEADS else hg_ref[1:2, :]
                            a = a * lax.rsqrt(jnp.mean(a * a, axis=-1, keepdims=True) + EPS) * gain
                        if latent:
                            a = _rope_rotate(a, cos_ref[rs, :], sin_ref[rs, :])
                    o_ref[rs, head * HEAD_DIM:(head + 1) * HEAD_DIM] = a
        if not latent:
            seqs, ctx_len = ks_ref.shape[0], ks_ref.shape[-3]
            for ref, col in ((ks_ref, kcol), (vs_ref, vcol)):
                if state_slot is not None:
                    for other in range(ref.shape[1]):
                        if other != state_slot:
                            ref[:, other] = jnp.zeros(ref.shape[:1] + ref.shape[2:], F32)
                for s in range(seqs):
                    for kvh in range(N_KV_HEADS):
                        val = o_ref[s * ctx_len:(s + 1) * ctx_len, col + kvh * HEAD_DIM:col + (kvh + 1) * HEAD_DIM]
                        if state_slot is None:
                            ref[s, :, kvh, :] = val
                        else:
                            ref[s, state_slot, :, kvh, :] = val

    i = pl.program_id(0)
    pl.when(i < n_ctx_blocks)(lambda: body(xc_ref, False))
    pl.when(i >= n_ctx_blocks)(lambda: body(xl_ref, True))


def _qkv_project(x_ctx, x_lat, g, mod, w, layer, cos, sin, hg, state, *, n_ctx_rows, lat_len, ctx_len, tm):
    d = x_ctx.shape[1]
    seqs = tm // ctx_len
    n_layers = w.shape[0]
    state_shape = (n_ctx_rows // ctx_len, n_layers, ctx_len, N_KV_HEADS, HEAD_DIM)
    last_ctx = n_ctx_rows // tm - 1
    if state is None:
        state_spec = pl.BlockSpec((seqs, n_layers, ctx_len, N_KV_HEADS, HEAD_DIM),
                                  lambda i: (jnp.minimum(i, last_ctx), 0, 0, 0, 0))
        state_specs, state_args, aliases = [], [], {}
    else:
        state_spec = pl.BlockSpec((seqs, None, ctx_len, N_KV_HEADS, HEAD_DIM),
                                  lambda i: (jnp.minimum(i, last_ctx), layer, 0, 0, 0))
        state_specs, state_args = [pl.BlockSpec(memory_space=pl.ANY)] * 2, list(state)
        aliases = {9: 1, 10: 2}
    n = w.shape[2]
    ncb = n_ctx_rows // tm
    t = n_ctx_rows + (x_ctx.shape[0] - n_ctx_rows if x_lat is None else x_lat.shape[0])
    row = functools.partial(_mod_row, tm=tm, n_ctx_rows=n_ctx_rows, lat_len=lat_len)
    tab = lambda i: ((jnp.maximum(i - ncb, 0)) % (lat_len // tm), 0)
    qk_norm = hg is not None
    if hg is None:
        hg = jnp.ones((2, HEAD_DIM), F32)
    x_specs, x_args = _two_source_specs(x_ctx, x_lat, tm, ncb)
    return pl.pallas_call(
        functools.partial(_qkv_kernel, qk_norm=qk_norm, tn=512, halves=2, n_ctx_blocks=ncb,
                          state_slot=layer if state is None else None),
        grid=(t // tm,),
        in_specs=x_specs + [
            pl.BlockSpec((1, d), lambda i: (0, 0)),
            pl.BlockSpec((1, 1, d), lambda i: (row(i) * 6, 0, 0)),
            pl.BlockSpec((1, 1, d), lambda i: (row(i) * 6 + 1, 0, 0)),
            pl.BlockSpec((None, d, n), lambda i: (layer, 0, 0), pipeline_mode=pl.Buffered(1)),
            pl.BlockSpec((tm, HEAD_DIM), tab), pl.BlockSpec((tm, HEAD_DIM), tab),
            pl.BlockSpec((2, HEAD_DIM), lambda i: (0, 0))] + state_specs,
        out_specs=[pl.BlockSpec((tm, n), lambda i: (i, 0)), state_spec, state_spec],
        out_shape=[jax.ShapeDtypeStruct((t, n), F32), jax.ShapeDtypeStruct(state_shape, F32),
                   jax.ShapeDtypeStruct(state_shape, F32)],
        input_output_aliases=aliases,
        compiler_params=_params("arbitrary"),
        name="qkv_project",
    )(*x_args, g.reshape(1, d), mod, mod, w, cos, sin, hg, *state_args)


def _proj_res_kernel(ac_ref, al_ref, xc_ref, xl_ref, w_ref, gate_ref, g2_ref, sh2_ref, sc2_ref, o_ref, h_ref, *,
                     n_ctx_blocks):
    i = pl.program_id(0)

    def emit(a_ref, x_ref):
        xn = x_ref[...] + gate_ref[0] * jnp.dot(a_ref[...], w_ref[...], preferred_element_type=F32)
        o_ref[...] = xn
        h_ref[...] = _norm_mod(xn, g2_ref[...], sh2_ref[0], sc2_ref[0]).astype(BF16)

    pl.when(i < n_ctx_blocks)(lambda: emit(ac_ref, xc_ref))
    pl.when(i >= n_ctx_blocks)(lambda: emit(al_ref, xl_ref))


def _proj_residual(a_ctx, a_lat, w, layer, x_ctx, x_lat, mod, g2, *, n_ctx_rows, lat_len, tm):
    d = x_ctx.shape[1]
    k = w.shape[1]
    ncb = n_ctx_rows // tm
    t = a_ctx.shape[0] + a_lat.shape[0]
    row = functools.partial(_mod_row, tm=tm, n_ctx_rows=n_ctx_rows, lat_len=lat_len)
    modspec = lambda which: pl.BlockSpec((1, 1, d), lambda i: (row(i) * 6 + which, 0, 0))
    x_specs, x_args = _two_source_specs(x_ctx, x_lat, tm, ncb)
    return pl.pallas_call(
        functools.partial(_proj_res_kernel, n_ctx_blocks=ncb),
        grid=(t // tm,),
        in_specs=[pl.BlockSpec((tm, k), lambda i: (jnp.minimum(i, ncb - 1), 0)),
                  pl.BlockSpec((tm, k), lambda i: (jnp.maximum(i - ncb, 0), 0))]
        + x_specs
        + [pl.BlockSpec((None, k, d), lambda i: (layer, 0, 0), pipeline_mode=pl.Buffered(1)),
           modspec(2),
           pl.BlockSpec((1, d), lambda i: (0, 0)),
           modspec(3), modspec(4)],
        out_specs=[pl.BlockSpec((tm, d), lambda i: (i, 0)), pl.BlockSpec((tm, d), lambda i: (i, 0))],
        out_shape=[jax.ShapeDtypeStruct((t, d), F32), jax.ShapeDtypeStruct((t, d), BF16)],
        compiler_params=_params("parallel"),
        name="proj_residual",
    )(a_ctx, a_lat, *x_args, w, mod, g2.reshape(1, d), mod, mod)


def _ffn_kernel(x_ref, h_ref, gate_ref, wg_ref, wu_ref, wd_ref, o_ref):
    f = pl.program_id(1)

    @pl.when(f == 0)
    def _():
        o_ref[...] = jnp.zeros_like(o_ref)

    h = h_ref[...]
    gv = jnp.dot(h, wg_ref[...], preferred_element_type=F32)
    uv = jnp.dot(h, wu_ref[...], preferred_element_type=F32)
    act = (_silu(gv) * uv).astype(BF16)
    o_ref[...] += jnp.dot(act, wd_ref[...], preferred_element_type=F32)

    @pl.when(f == pl.num_programs(1) - 1)
    def _():
        o_ref[...] = x_ref[...] + gate_ref[0] * o_ref[...]


def _ffn(x, h, mod, w_gu, w_down, layer, *, n_ctx_rows, lat_len, tm, tf):
    t, d = x.shape
    ff = w_down.shape[1]
    nf = ff // tf
    row = functools.partial(_mod_row, tm=tm, n_ctx_rows=n_ctx_rows, lat_len=lat_len)
    return pl.pallas_call(
        _ffn_kernel,
        grid=(t // tm, nf),
        in_specs=[pl.BlockSpec((tm, d), lambda i, f: (i, 0)),
                  pl.BlockSpec((tm, d), lambda i, f: (i, 0)),
                  pl.BlockSpec((1, 1, d), lambda i, f: (row(i) * 6 + 5, 0, 0)),
                  pl.BlockSpec((None, d, tf), lambda i, f: (layer, 0, f)),
                  pl.BlockSpec((None, d, tf), lambda i, f: (layer, 0, nf + f)),
                  pl.BlockSpec((None, tf, d), lambda i, f: (layer, f, 0))],
        out_specs=pl.BlockSpec((tm, d), lambda i, f: (i, 0)),
        out_shape=jax.ShapeDtypeStruct((t, d), F32),
        compiler_params=_params("parallel", "arbitrary"),
        name="ffn_swiglu",
    )(x, h, mod, w_gu, w_gu, w_down)


def _stack_heads(q):
    return jnp.concatenate([q[:, h * HEAD_DIM:(h + 1) * HEAD_DIM] for h in range(GROUP)], axis=0)


def _unstack_heads(o, rows):
    return jnp.concatenate([o[h * rows:(h + 1) * rows] for h in range(GROUP)], axis=1)


def _sink_column(sink_ref, kh, rows):
    head = lax.broadcasted_iota(jnp.int32, (GROUP * rows, 1), 0) // rows
    col = jnp.zeros((GROUP * rows, 1), F32)
    for h in range(GROUP):
        col = jnp.where(head == h, sink_ref[kh * GROUP + h], col)
    return col


def _softmax_av(s, v, sink_col):
    m = jnp.max(s, axis=-1, keepdims=True)
    if sink_col is not None:
        m = jnp.maximum(m, sink_col)
    p = jnp.exp(s - m)
    l = jnp.sum(p, axis=-1, keepdims=True)
    if sink_col is not None:
        l = l + jnp.exp(sink_col - m)
    return jnp.dot(p.astype(BF16), v, preferred_element_type=F32) / l


def _attn_ctx_kernel(sink_ref, q_ref, k_ref, v_ref, o_ref, *, use_sink):
    rows = q_ref.shape[0]
    qw = GROUP * HEAD_DIM
    for kh in range(N_KV_HEADS):
        qs = _stack_heads(q_ref[:, kh * qw:(kh + 1) * qw] * SCALE).astype(BF16)
        k = k_ref[:, kh * HEAD_DIM:(kh + 1) * HEAD_DIM].astype(BF16)
        v = v_ref[:, kh * HEAD_DIM:(kh + 1) * HEAD_DIM].astype(BF16)
        s = lax.dot_general(qs, k, NT_DIMS, preferred_element_type=F32)
        sink_col = _sink_column(sink_ref, kh, rows) if use_sink else None
        o_ref[:, kh * qw:(kh + 1) * qw] = _unstack_heads(_softmax_av(s, v, sink_col), rows).astype(BF16)


def _attn_ctx(y, sink, *, n_ctx_seqs, ctx_len, use_sink):
    t = n_ctx_seqs * ctx_len
    q_w, kv_w = N_HEADS * HEAD_DIM, N_KV_HEADS * HEAD_DIM
    return pl.pallas_call(
        functools.partial(_attn_ctx_kernel, use_sink=use_sink),
        grid=(n_ctx_seqs,),
        in_specs=[pl.BlockSpec(memory_space=pltpu.SMEM),
                  pl.BlockSpec((ctx_len, q_w), lambda b: (b, 0)),
                  pl.BlockSpec((ctx_len, kv_w), lambda b: (b, q_w // kv_w)),
                  pl.BlockSpec((ctx_len, kv_w), lambda b: (b, q_w // kv_w + 1))],
        out_specs=pl.BlockSpec((ctx_len, q_w), lambda b: (b, 0)),
        out_shape=jax.ShapeDtypeStruct((t, q_w), BF16),
        compiler_params=_params("parallel"),
        name="attn_context",
    )(sink, y, y, y)


def _keys_by_queries_attention(q_ref, k_scr, vt_scr, s_scr, chunks, bias_ref, sink_ref, o_ref):
    qn = q_ref.shape[0]
    kh = pl.program_id(1)

    def logits(h, ci):
        st, sz = chunks[ci]
        q = (q_ref[:, h * HEAD_DIM:(h + 1) * HEAD_DIM] * (SCALE * LOG2E)).astype(BF16)
        st_ = lax.dot_general(k_scr[st:st + sz, :], q, NT_DIMS, preferred_element_type=F32)
        if bias_ref is not None and ci == 0:
            st_ = st_ + bias_ref[...]
        s_scr[st:st + sz, h * qn:(h + 1) * qn] = st_
        return jnp.max(st_, axis=0, keepdims=True)

    def weigh(h, ci, m):
        st, sz = chunks[ci]
        p = jnp.exp2(s_scr[st:st + sz, h * qn:(h + 1) * qn] - m)
        pv = jnp.dot(vt_scr[:, st:st + sz], p.astype(BF16), preferred_element_type=F32)
        return jnp.sum(p, axis=0, keepdims=True), pv

    n = len(chunks)
    maxes = [logits(0, ci) for ci in range(n)]
    for h in range(GROUP):
        m = functools.reduce(jnp.maximum, maxes)
        if sink_ref is not None:
            sink2 = sink_ref[kh * GROUP + h] * LOG2E
            m = jnp.maximum(m, sink2)
        maxes, l, acc = [], None, None
        for ci in range(n):
            if h + 1 < GROUP:
                maxes.append(logits(h + 1, ci))
            lc, pv = weigh(h, ci, m)
            l, acc = (lc, pv) if l is None else (l + lc, acc + pv)
        if sink_ref is not None:
            l = l + jnp.exp2(sink2 - m)
        o_ref[:, h * HEAD_DIM:(h + 1) * HEAD_DIM] = (acc * (1.0 / l)).T.astype(BF16)


def _attn_lat_win_kernel(sink_ref, q_ref, kp_ref, kc_ref, kn_ref, vp_ref, vc_ref, vn_ref, ck_ref, cv_ref, bias_ref,
                         o_ref, k_scr, vt_scr, s_scr):
    past = ck_ref.shape[0]
    win = Q_STEP + 2 * BLOCK

    @pl.when(pl.program_id(2) == 0)
    def _():
        k_scr[win:, :] = ck_ref[...].astype(BF16)
        vt_scr[:, win:] = cv_ref[...].T.astype(BF16)

    row = 0
    for k_ref, v_ref in ((kp_ref, vp_ref), (kc_ref, vc_ref), (kn_ref, vn_ref)):
        n = k_ref.shape[0]
        k_scr[row:row + n, :] = k_ref[...].astype(BF16)
        vt_scr[:, row:row + n] = v_ref[...].T.astype(BF16)
        row += n
    _keys_by_queries_attention(q_ref, k_scr, vt_scr, s_scr, ((0, win), (win, past)), bias_ref, sink_ref, o_ref)


def _window_bias(n_steps):
    win = Q_STEP + 2 * BLOCK
    c = jnp.arange(win, dtype=jnp.int32)[:, None]
    r = jnp.arange(Q_STEP, dtype=jnp.int32)[None, :]
    band = jnp.abs(r + BLOCK - c) <= WINDOW
    variants = [band & (c >= BLOCK), band, band & (c < win - BLOCK)]
    if n_steps == 1:
        variants = [variants[0] & (c < win - BLOCK)] * 3
    return jnp.where(jnp.stack(variants), 0.0, NEG).astype(F32)


def _cache_spec(cache, layer):
    return pl.BlockSpec((None, None, cache.shape[2], HEAD_DIM), lambda b, kh, qs: (b, layer, 0, kh))


def _attn_lat_win(y, sink, cache_k, cache_v, layer, *, n_ctx_rows, lat_batch, lat_len):
    qw = GROUP * HEAD_DIM
    nb = lat_len // BLOCK
    ns = lat_len // Q_STEP
    per = Q_STEP // BLOCK
    base = n_ctx_rows // BLOCK
    past = cache_k.shape[2]
    n_keys = Q_STEP + 2 * BLOCK + past

    def edge(col0, blk):
        return pl.BlockSpec((BLOCK, HEAD_DIM),
                            lambda b, kh, qs: (base + b * nb + jnp.clip(qs * per + blk, 0, nb - 1), col0 + kh))

    def centre(col0):
        return pl.BlockSpec((Q_STEP, HEAD_DIM), lambda b, kh, qs: (base // per + b * ns + qs, col0 + kh))

    kcol, vcol = N_HEADS, N_HEADS + N_KV_HEADS
    cache_spec = _cache_spec(cache_k, layer)
    return pl.pallas_call(
        _attn_lat_win_kernel,
        grid=(lat_batch, N_KV_HEADS, ns),
        in_specs=[pl.BlockSpec(memory_space=pltpu.SMEM),
                  pl.BlockSpec((Q_STEP, qw), lambda b, kh, qs: (base // per + b * ns + qs, kh)),
                  edge(kcol, -1), centre(kcol), edge(kcol, per),
                  edge(vcol, -1), centre(vcol), edge(vcol, per),
                  cache_spec, cache_spec,
                  pl.BlockSpec((None, Q_STEP + 2 * BLOCK, Q_STEP),
                               lambda b, kh, qs: (jnp.where(qs == 0, 0, jnp.where(qs == ns - 1, 2, 1)), 0, 0))],
        out_specs=pl.BlockSpec((Q_STEP, qw), lambda b, kh, qs: (b * ns + qs, kh)),
        out_shape=jax.ShapeDtypeStruct((lat_batch * lat_len, N_HEADS * HEAD_DIM), BF16),
        scratch_shapes=[pltpu.VMEM((n_keys, HEAD_DIM), BF16), pltpu.VMEM((HEAD_DIM, n_keys), BF16),
                        pltpu.VMEM((n_keys, GROUP * Q_STEP), F32)],
        compiler_params=_params("parallel", "parallel", "arbitrary"),
        name="attn_latent_window",
    )(sink, y, y, y, y, y, y, y, cache_k, cache_v, _window_bias(ns))


def _attn_lat_full_kernel(q_ref, k_ref, v_ref, ck_ref, cv_ref, o_ref, k_scr, vt_scr, s_scr, *, chunk):
    lat_len = k_ref.shape[0]
    n_keys = k_scr.shape[0]

    @pl.when(pl.program_id(2) == 0)
    def _():
        k_scr[0:lat_len, :] = k_ref[...].astype(BF16)
        k_scr[lat_len:, :] = ck_ref[...].astype(BF16)
        vt_scr[:, 0:lat_len] = v_ref[...].T.astype(BF16)
        vt_scr[:, lat_len:] = cv_ref[...].T.astype(BF16)

    chunks = tuple((st, chunk) for st in range(0, n_keys, chunk))
    _keys_by_queries_attention(q_ref, k_scr, vt_scr, s_scr, chunks, None, None, o_ref)


def _attn_lat_full(y, cache_k, cache_v, layer, *, n_ctx_rows, lat_batch, lat_len):
    qw = GROUP * HEAD_DIM
    ns = lat_len // Q_STEP
    base = n_ctx_rows // Q_STEP
    seq_base = n_ctx_rows // lat_len
    past = cache_k.shape[2]
    cache_spec = _cache_spec(cache_k, layer)
    n_keys = lat_len + past
    chunk = 512
    assert n_keys % chunk == 0
    return pl.pallas_call(
        functools.partial(_attn_lat_full_kernel, chunk=chunk),
        grid=(lat_batch, N_KV_HEADS, ns),
        in_specs=[pl.BlockSpec((Q_STEP, qw), lambda b, kh, qs: (base + b * ns + qs, kh)),
                  pl.BlockSpec((lat_len, HEAD_DIM), lambda b, kh, qs: (seq_base + b, N_HEADS + kh)),
                  pl.BlockSpec((lat_len, HEAD_DIM), lambda b, kh, qs: (seq_base + b, N_HEADS + N_KV_HEADS + kh)),
                  cache_spec, cache_spec],
        out_specs=pl.BlockSpec((Q_STEP, qw), lambda b, kh, qs: (b * ns + qs, kh)),
        out_shape=jax.ShapeDtypeStruct((lat_batch * lat_len, N_HEADS * HEAD_DIM), BF16),
        scratch_shapes=[pltpu.VMEM((n_keys, HEAD_DIM), BF16), pltpu.VMEM((HEAD_DIM, n_keys), BF16),
                        pltpu.VMEM((n_keys, GROUP * Q_STEP), F32)],
        compiler_params=_params("parallel", "parallel", "arbitrary"),
        name="attn_latent_full",
    )(y, y, y, cache_k, cache_v)


def _rope_tables(lat_len):
    half = HEAD_DIM // 2
    n_rows = lat_len // GRID_W
    row = jnp.repeat(jnp.arange(n_rows, dtype=F32), GRID_W)
    col = jnp.tile(jnp.arange(GRID_W, dtype=F32), n_rows)
    inv = ROPE_THETA ** (-jnp.arange(0, half, 2, dtype=F32) / half)
    ang_r, ang_c = row[:, None] * inv, col[:, None] * inv
    cos = jnp.concatenate([jnp.cos(ang_r)] * 2 + [jnp.cos(ang_c)] * 2, axis=-1)
    sin = jnp.concatenate([-jnp.sin(ang_r), jnp.sin(ang_r), -jnp.sin(ang_c), jnp.sin(ang_c)], axis=-1)
    return cos, sin


def _dft_mats(seq_len, tk):
    n = 2 * seq_len
    k = jnp.arange(seq_len, dtype=jnp.int32)
    step = 64
    a = jnp.arange(seq_len // step, dtype=jnp.int32) * step
    b = jnp.arange(step, dtype=jnp.int32)
    ang_a = ((k[:, None] * a[None, :]) % n).astype(F32) * (2.0 * math.pi / n)
    ang_b = ((k[:, None] * b[None, :]) % n).astype(F32) * (2.0 * math.pi / n)
    ca, sa = jnp.cos(ang_a)[:, :, None], jnp.sin(ang_a)[:, :, None]
    cb, sb = jnp.cos(ang_b)[:, None, :], jnp.sin(ang_b)[:, None, :]
    fc = (ca * cb - sa * sb).reshape(seq_len, seq_len)
    fs = -(sa * cb + ca * sb).reshape(seq_len, seq_len)
    sign = jnp.where(k % 2 == 0, 1.0, -1.0).astype(F32)
    fs = jnp.where(k[:, None] == 0, sign[None, :], fs)
    kfc = seq_len // tk
    ana = jnp.concatenate([fc.reshape(kfc, tk, seq_len), fs.reshape(kfc, tk, seq_len)], axis=1).astype(BF16)
    ana = lax.optimization_barrier(ana)
    return ana, ana.transpose(0, 2, 1)


def _hy_filter_kernel(feat_ref, tn_ref, w1_ref, b1_ref, w2_ref, b2_ref, fq_ref, w3f_ref, w3b_ref, dl_ref, ana_ref,
                      h_ref, pq_scr, inv_scr, nyq_scr):
    kf = pl.program_id(1)
    hp = lax.Precision.HIGHEST
    tk = h_ref.shape[1]

    @pl.when(kf == 0)
    def _():
        def taps(g):
            a = jnp.dot(feat_ref[g], w1_ref[...], precision=hp, preferred_element_type=F32) + b1_ref[...]
            a = jnp.sin(fq_ref[0:1, :] * a)
            a = jnp.dot(a, w2_ref[...], precision=hp, preferred_element_type=F32) + b2_ref[...]
            a = jnp.sin(fq_ref[1:2, :] * a)
            decay = jnp.exp(-tn_ref[g] * dl_ref[...])
            return (jnp.dot(a, w3f_ref[...], precision=hp, preferred_element_type=F32) * decay,
                    jnp.dot(a, w3b_ref[...], precision=hp, preferred_element_type=F32) * decay)

        fwd0, bwd0 = taps(0)
        fwd1, bwd1 = taps(1)
        fwdm, bwdm = taps(2)
        rowi = lax.broadcasted_iota(jnp.int32, fwd0.shape, 0)
        first = rowi == 0
        bwd0 = jnp.where(first, 0.0, bwd0)
        absum = lambda v: jnp.sum(jnp.abs(v), axis=0, keepdims=True)
        inv_scr[...] = 1.0 / (absum(fwd0) + absum(fwd1) + absum(bwd0) + absum(bwd1) + EPS)
        parts = ((fwd0, bwd0),
                 (bwdm, jnp.where(first, 0.0, bwd1)),
                 (fwd1, jnp.where(first, 0.0, fwdm)))
        for n, (pos, neg) in enumerate(parts):
            p = pos + neg
            nyq_scr[n:n + 1, :] = jnp.sum(jnp.where((rowi & 1) == 0, p, -p), axis=0, keepdims=True)
            pq_scr[2 * n] = p.astype(BF16)
            pq_scr[2 * n + 1] = (pos - neg).astype(BF16)

    inv = inv_scr[...]
    for n in range(3):
        h_ref[2 * n] = jnp.dot(ana_ref[0:tk, :], pq_scr[2 * n], preferred_element_type=F32) * inv
        hi = jnp.dot(ana_ref[tk:, :], pq_scr[2 * n + 1], preferred_element_type=F32)
        rowk = lax.broadcasted_iota(jnp.int32, hi.shape, 0)
        hi = jnp.where((rowk == 0) & (kf == 0), nyq_scr[n:n + 1, :], hi)
        h_ref[2 * n + 1] = hi * inv


def _hy_filters(seq_len, f_w1, f_b1, f_w2, f_b2, f_w3, freq, ana_mats, *, tc):
    half = seq_len // 2
    tk = ana_mats.shape[1] // 2
    hid = f_w1.shape[1]
    nd = f_w3.shape[1] // 2
    d = nd // HY_ORDER
    t = jnp.arange(seq_len, dtype=F32)
    tnorm = t / max(seq_len - 1, 1)
    bands = (HY_EMB - 1) // 2
    fb = jnp.linspace(1e-4, bands - 1, bands, dtype=F32)
    w = 2.0 * math.pi * t / seq_len
    feats = jnp.concatenate([tnorm[:, None], jnp.cos(w[:, None] * fb), -jnp.sin(w[:, None] * fb)], axis=-1)
    emb = 64
    feats = jnp.pad(feats, ((0, 0), (0, emb - HY_EMB)))
    w1 = jnp.pad(f_w1, ((0, emb - HY_EMB), (0, 0)))
    mirror = half - jnp.arange(half)
    feats3 = jnp.stack([feats[:half], feats[half:], feats[mirror]])
    tn3 = jnp.stack([tnorm[:half], tnorm[half:], tnorm[mirror]]).reshape(3, half, 1)
    deltas = jnp.abs(jnp.linspace(HY_MIN_DECAY, HY_MAX_DECAY, d, dtype=F32))
    deltas = jnp.tile(deltas, HY_ORDER).reshape(1, nd)
    nct = nd // tc
    const = lambda shape: pl.BlockSpec(shape, lambda c, kf: (0,) * len(shape))
    return pl.pallas_call(
        _hy_filter_kernel,
        grid=(nct, half // tk),
        in_specs=[const((3, half, emb)), const((3, half, 1)),
                  const((emb, hid)), const((1, hid)), const((hid, hid)), const((1, hid)), const((2, hid)),
                  pl.BlockSpec((hid, tc), lambda c, kf: (0, c)),
                  pl.BlockSpec((hid, tc), lambda c, kf: (0, nct + c)),
                  pl.BlockSpec((1, tc), lambda c, kf: (0, c)),
                  pl.BlockSpec((None, 2 * tk, half), lambda c, kf: (kf, 0, 0))],
        out_specs=pl.BlockSpec((6, tk, tc), lambda c, kf: (0, kf, c)),
        out_shape=jax.ShapeDtypeStruct((6, half, nd), F32),
        scratch_shapes=[pltpu.VMEM((6, half, tc), BF16), pltpu.VMEM((1, tc), F32), pltpu.VMEM((8, tc), F32)],
        compiler_params=_params("parallel", "arbitrary"),
        name="hyena_filter",
    )(feats3, tn3, w1, f_b1.reshape(1, hid), f_w2, f_b2.reshape(1, hid), freq, f_w3, f_w3, deltas, ana_mats)


def _hy_inproj_kernel(x_ref, g_ref, sh_ref, sc_ref, w_ref, cw_ref, cb_ref, o_ref, h_scr, y_scr, *, n_ctx_blocks,
                      ctx_len):
    j = pl.program_id(1)
    n = x_ref.shape[0]

    @pl.when(j == 0)
    def _():
        for r0 in range(0, n, 256):
            rs = slice(r0, r0 + 256)
            h_scr[rs, :] = _norm_mod(x_ref[rs, :], g_ref[...], sh_ref[0], sc_ref[0]).astype(BF16)
        y_scr[0:PAD, :] = jnp.zeros((PAD, y_scr.shape[1]), F32)
        y_scr[PAD + n:, :] = jnp.zeros((PAD, y_scr.shape[1]), F32)

    is_ctx = pl.program_id(0) < n_ctx_blocks
    chunk = 256

    def product(r0):
        y_scr[PAD + r0:PAD + r0 + chunk, :] = jnp.dot(h_scr[r0:r0 + chunk, :], w_ref[...],
                                                      preferred_element_type=F32)

    product(0)
    for r0 in range(0, n, chunk):
        if r0 + chunk < n:
            product(r0 + chunk)
        pos = (r0 + lax.broadcasted_iota(jnp.int32, (chunk, y_scr.shape[1]), 0)) % ctx_len
        prev = jnp.where(is_ctx & (pos == 0), 0.0, y_scr[PAD - 1 + r0:PAD - 1 + r0 + chunk, :])
        cur = y_scr[PAD + r0:PAD + r0 + chunk, :]
        nxt = jnp.where(is_ctx & (pos == ctx_len - 1), 0.0, y_scr[PAD + 1 + r0:PAD + 1 + r0 + chunk, :])
        o_ref[r0:r0 + chunk, :] = (prev * cw_ref[0:1, :] + cur * cw_ref[1:2, :] + nxt * cw_ref[2:3, :]
                                   + cb_ref[...])


def _hy_inproj(x, g, mod, w, conv_w, conv_b, *, n_ctx_rows, lat_len, ctx_len, tn):
    t, d = x.shape
    n = w.shape[1]
    tm = lat_len
    row = functools.partial(_mod_row, tm=tm, n_ctx_rows=n_ctx_rows, lat_len=lat_len)
    return pl.pallas_call(
        functools.partial(_hy_inproj_kernel, n_ctx_blocks=n_ctx_rows // tm, ctx_len=ctx_len),
        grid=(t // tm, n // tn),
        in_specs=[pl.BlockSpec((tm, d), lambda i, j: (i, 0), pipeline_mode=pl.Buffered(1)),
                  pl.BlockSpec((1, d), lambda i, j: (0, 0)),
                  pl.BlockSpec((1, 1, d), lambda i, j: (row(i) * 6, 0, 0)),
                  pl.BlockSpec((1, 1, d), lambda i, j: (row(i) * 6 + 1, 0, 0)),
                  pl.BlockSpec((d, tn), lambda i, j: (0, j)),
                  pl.BlockSpec((3, tn), lambda i, j: (0, j)),
                  pl.BlockSpec((1, tn), lambda i, j: (0, j))],
        out_specs=pl.BlockSpec((tm, tn), lambda i, j: (i, j)),
        out_shape=jax.ShapeDtypeStruct((t, n), F32),
        scratch_shapes=[pltpu.VMEM((tm, d), BF16), pltpu.VMEM((tm + 2 * PAD, tn), F32)],
        compiler_params=_params("parallel", "arbitrary"),
        name="hyena_inproj",
    )(x, g.reshape(1, d), mod, mod, w, conv_w, conv_b.reshape(1, n))


def _hy_conv_kernel(a_ref, g_ref, skip_ref, ana_ref, syn_ref, h_ref, o_ref, acc_scr, u_scr=None, *, seq_len):
    kf = pl.program_id(2)
    rows = a_ref.shape[0]
    tk = h_ref.shape[1]
    half = seq_len // 2
    chunk = 256

    @pl.when(kf == 0)
    def _():
        acc_scr[...] = jnp.zeros_like(acc_scr)
        if u_scr is not None:
            for r0 in range(0, rows, chunk):
                u_scr[r0:r0 + chunk, :] = a_ref[r0:r0 + chunk, :].astype(BF16)

    u_ref = a_ref if u_scr is None else u_scr

    first = (lax.broadcasted_iota(jnp.int32, (tk, a_ref.shape[1]), 0) == 0) & (kf == 0)

    def respond(uf, n):
        ur, ui = uf[0:tk, :], uf[tk:, :]
        hr, hi = h_ref[2 * n], h_ref[2 * n + 1]
        yr = jnp.where(first, 0.5 * (ur * hr), ur * hr - ui * hi)
        yi = jnp.where(first, 0.5 * (ui * hi), ur * hi + ui * hr)
        return yr, yi

    step = min(half, 512)
    for seg in range(0, rows, seq_len):
        uf1 = jnp.dot(ana_ref[...], u_ref[seg:seg + half, :], preferred_element_type=F32)
        uf2 = jnp.dot(ana_ref[...], u_ref[seg + half:seg + seq_len, :], preferred_element_type=F32)
        for dst, (n1, n2) in ((seg, (0, 1)), (seg + half, (2, 0))):
            y1r, y1i = respond(uf1, n1)
            y2r, y2i = respond(uf2, n2)
            yf = jnp.concatenate([y1r + y2r, y1i + y2i], axis=0).astype(BF16)
            for n0 in range(0, half, step):
                acc_scr[dst + n0:dst + n0 + step, :] += jnp.dot(syn_ref[n0:n0 + step, :], yf,
                                                                preferred_element_type=F32)

    @pl.when(kf == pl.num_programs(2) - 1)
    def _():
        for r0 in range(0, rows, chunk):
            rs = slice(r0, r0 + chunk)
            a = a_ref[rs, :].astype(F32)
            o_ref[rs, :] = (g_ref[rs, :] * (acc_scr[rs, :] * (1.0 / half) + a * skip_ref[...])).astype(BF16)


def _hy_conv(a, a_part, a_base, g, g_part, g_base, skip_row, mats, h, order, *, n_blocks, rows, seq_len, tc):
    d = skip_row.shape[1]
    nct = d // tc
    ana_mats, syn_mats = mats
    kfc = ana_mats.shape[0]
    tk = ana_mats.shape[1] // 2

    def amap(part, base):
        return lambda b, c, kf: (base + b, part * nct + c)

    half = seq_len // 2
    return pl.pallas_call(
        functools.partial(_hy_conv_kernel, seq_len=seq_len),
        grid=(n_blocks, nct, kfc),
        in_specs=[pl.BlockSpec((rows, tc), amap(a_part, a_base)),
                  pl.BlockSpec((rows, tc), amap(g_part, g_base)),
                  pl.BlockSpec((1, tc), lambda b, c, kf: (0, c)),
                  pl.BlockSpec((None, 2 * tk, half), lambda b, c, kf: (kf, 0, 0)),
                  pl.BlockSpec((None, half, 2 * tk), lambda b, c, kf: (kf, 0, 0)),
                  pl.BlockSpec((6, tk, tc), lambda b, c, kf: (0, kf, order * nct + c))],
        out_specs=pl.BlockSpec((rows, tc), lambda b, c, kf: (b, c)),
        out_shape=jax.ShapeDtypeStruct((n_blocks * rows, d), BF16),
        scratch_shapes=[pltpu.VMEM((rows, tc), F32)] + ([] if a.dtype == BF16 else [pltpu.VMEM((rows, tc), BF16)]),
        compiler_params=_params("parallel", "parallel", "arbitrary"),
        name="hyena_conv",
    )(a, g, skip_row, ana_mats, syn_mats, h)


def _final_norm_kernel(x_ref, g_ref, o_ref):
    x = x_ref[...]
    o_ref[...] = x * lax.rsqrt(jnp.mean(x * x, axis=-1, keepdims=True) + EPS) * g_ref[...]


def _final_norm(x, g, *, row0, rows, tm):
    d = x.shape[1]
    base = row0 // tm
    return pl.pallas_call(
        _final_norm_kernel,
        grid=(rows // tm,),
        in_specs=[pl.BlockSpec((tm, d), lambda i: (base + i, 0)),
                  pl.BlockSpec((1, d), lambda i: (0, 0))],
        out_specs=pl.BlockSpec((tm, d), lambda i: (i, 0)),
        out_shape=jax.ShapeDtypeStruct((rows, d), F32),
        compiler_params=_params("parallel"),
        name="final_norm",
    )(x, g.reshape(1, d))


def kernel(x_prompt, x_sample, cache_win_k, cache_win_v, cache_ax_k, cache_ax_v, c, c_ctx, norm_mix_g, norm_ffn_g, mod_w, mod_b, win_wqkv, win_wo, win_sink, hy_w_in, hy_conv_w, hy_conv_b, hy_f_w1, hy_f_b1, hy_f_w2, hy_f_b2, hy_f_w3, hy_freq, hy_skip, hy_wo, ax_wqkv, ax_q_g, ax_k_g, ax_wo, ffn_w_gu, ffn_w_down, final_g):
    n_ctx_seqs, ctx_len, d = x_prompt.shape
    lat_batch, lat_len, _ = x_sample.shape
    depth = mod_w.shape[0]
    past = cache_win_k.shape[2]
    n_ctx_rows = n_ctx_seqs * ctx_len
    n_lat_rows = lat_batch * lat_len
    tm = 512
    assert n_ctx_rows % tm == 0 and lat_len % tm == 0 and n_ctx_rows % lat_len == 0
    assert lat_batch + 1 <= MOD_ROWS
    geom = dict(n_ctx_rows=n_ctx_rows, lat_len=lat_len, tm=tm)

    x, x_lat = x_prompt.reshape(n_ctx_rows, d), x_sample.reshape(n_lat_rows, d)
    cvec = jnp.concatenate([c_ctx[None, :], c, jnp.zeros((MOD_ROWS - 1 - lat_batch, d), F32)], axis=0)
    mod_all = _adaln_all(cvec, mod_w, mod_b)
    cos, sin = _rope_tables(lat_len)

    w_gu, w_down = ffn_w_gu.astype(BF16), ffn_w_down.astype(BF16)
    wqkv_all = {0: win_wqkv.astype(BF16), 2: ax_wqkv.astype(BF16)}
    wo_all = {0: win_wo.astype(BF16), 1: hy_wo.astype(BF16), 2: ax_wo.astype(BF16)}

    states = {0: None, 2: None}
    kv_w = N_KV_HEADS * HEAD_DIM
    for i in range(depth):
        mod = mod_all[i].reshape(MOD_ROWS * 6, 1, d)
        kind, j = i % 3, i // 3
        if kind == 1:
            assert x_lat is None
            u = _hy_inproj(x, norm_mix_g[i], mod, hy_w_in[j].astype(BF16), hy_conv_w[j], hy_conv_b[j],
                           n_ctx_rows=n_ctx_rows, lat_len=lat_len, ctx_len=ctx_len, tn=512)
            rows = lat_len
            zs = []
            for (base, nblk, sl) in ((0, n_ctx_rows // rows, ctx_len), (n_ctx_rows // rows, lat_batch, lat_len)):
                mats = _dft_mats(sl // 2, min(512, sl // 2))
                resp = _hy_filters(sl, hy_f_w1[j], hy_f_b1[j], hy_f_w2[j], hy_f_b2[j], hy_f_w3[j], hy_freq[j],
                                   mats[0], tc=512)
                conv = functools.partial(_hy_conv, mats=mats, h=resp, n_blocks=nblk, rows=rows, seq_len=sl, tc=512)
                z1 = conv(u, 0, base, u, 1, base, skip_row=hy_skip[j, 0].reshape(1, d), order=0)
                zs.append(conv(z1, 0, 0, u, 2, base, skip_row=hy_skip[j, 1].reshape(1, d), order=1))
            x, h2 = _proj_residual(zs[0], zs[1], wo_all[kind], j, x, None, mod, norm_ffn_g[i], **geom)
        else:
            if kind == 0:
                sink, hg = win_sink[j], None
                ck, cv = cache_win_k, cache_win_v
            else:
                sink = jnp.zeros((N_HEADS,), F32)
                hg = jnp.stack([ax_q_g[j], ax_k_g[j]], axis=0)
                ck, cv = cache_ax_k, cache_ax_v
            ck = ck[:, j:j + 1].reshape(lat_batch, 1, past, kv_w)
            cv = cv[:, j:j + 1].reshape(lat_batch, 1, past, kv_w)
            y, k_c, v_c = _qkv_project(x, x_lat, norm_mix_g[i], mod, wqkv_all[kind], j, cos, sin, hg, states[kind],
                                       ctx_len=ctx_len, **geom)
            states[kind] = (k_c, v_c)
            o_c = _attn_ctx(y, sink, n_ctx_seqs=n_ctx_seqs, ctx_len=ctx_len, use_sink=kind == 0)
            lat_geom = dict(n_ctx_rows=n_ctx_rows, lat_batch=lat_batch, lat_len=lat_len)
            if kind == 0:
                o_l = _attn_lat_win(y, sink, ck, cv, 0, **lat_geom)
            else:
                o_l = _attn_lat_full(y, ck, cv, 0, **lat_geom)
            x, h2 = _proj_residual(o_c, o_l, wo_all[kind], j, x, x_lat, mod, norm_ffn_g[i], **geom)
            x_lat = None
        x = _ffn(x, h2, mod, w_gu, w_down, i, tf=512, **geom)

    y_prompt = _final_norm(x, final_g, row0=0, rows=n_ctx_rows, tm=tm).reshape(n_ctx_seqs, ctx_len, d)
    y_sample = _final_norm(x, final_g, row0=n_ctx_rows, rows=n_lat_rows, tm=tm).reshape(lat_batch, lat_len, d)
    return (y_prompt, y_sample) + states[0] + states[2]
```

```python
import functools
import math

import jax
import jax.numpy as jnp
from jax import lax
from jax.experimental import pallas as pl
from jax.experimental.pallas import tpu as pltpu

F32 = jnp.float32
BF16 = jnp.bfloat16

N_HEADS = 16
N_KV_HEADS = 4
GROUP = N_HEADS // N_KV_HEADS
HEAD_DIM = 128
WINDOW = 128
BLOCK = 128
GRID_W = 64
ROPE_THETA = 10000.0
HY_ORDER = 2
HY_EMB = 33
HY_TARGET = 1e-2
HY_FAST_DECAY = 0.3
HY_SLOW_DECAY = 1.5
HY_MIN_DECAY = math.log(HY_TARGET) / HY_SLOW_DECAY
HY_MAX_DECAY = math.log(HY_TARGET) / HY_FAST_DECAY
EPS = 1e-6
NEG = -1e30
SCALE = HEAD_DIM ** -0.5
LOG2E = math.log2(math.e)
Q_STEP = 4 * BLOCK

LANES = 128
PAD = 8
MOD_ROWS = 16
VMEM_LIMIT = 56 * 1024 * 1024

NT_DIMS = (((1,), (1,)), ((), ()))

def _params(*sem):
    return pltpu.CompilerParams(dimension_semantics=sem, vmem_limit_bytes=VMEM_LIMIT)


def _silu(x):
    return x * (1.0 / (1.0 + jnp.exp(-x)))


def _mod_row(i, tm, n_ctx_rows, lat_len):
    return jnp.maximum(i * tm - n_ctx_rows + lat_len, 0) // lat_len


def _mod_kernel(c_ref, w_ref, b_ref, o_ref):
    s = _silu(c_ref[...]).astype(BF16)
    o_ref[...] = jnp.dot(s, w_ref[0].astype(BF16), preferred_element_type=F32) + b_ref[0]


def _adaln_all(cvec, mod_w, mod_b):
    depth, d, n = mod_w.shape
    tn = 1024
    per = d // tn
    return pl.pallas_call(
        _mod_kernel,
        grid=(depth, n // tn),
        in_specs=[pl.BlockSpec((MOD_ROWS, d), lambda l, j: (0, 0)),
                  pl.BlockSpec((1, d, tn), lambda l, j: (l, 0, j)),
                  pl.BlockSpec((1, 1, tn), lambda l, j: (l, 0, j))],
        out_specs=pl.BlockSpec((None, MOD_ROWS, None, None, tn), lambda l, j: (l, 0, j // per, 0, j % per)),
        out_shape=jax.ShapeDtypeStruct((depth, MOD_ROWS, 6, 1, d), F32),
        compiler_params=_params("parallel", "parallel"),
        name="adaln_mod",
    )(cvec, mod_w, mod_b.reshape(depth, 1, n))


def _norm_mod(x, g, shift, scale):
    y = x * lax.rsqrt(jnp.mean(x * x, axis=-1, keepdims=True) + EPS) * g
    return y * (1.0 + scale) + shift


def _rope_rotate(a, cos, sin_signed):
    lane = lax.broadcasted_iota(jnp.int32, a.shape, 1)
    partner = jnp.where((lane & 32) == 0, pltpu.roll(a, 96, 1), pltpu.roll(a, 32, 1))
    return a * cos + partner * sin_signed


def _two_source_specs(x_ctx, x_lat, tm, n_ctx_blocks):
    d = x_ctx.shape[1]
    lat_base = n_ctx_blocks if x_lat is None else 0
    specs = [pl.BlockSpec((tm, d), lambda i: (jnp.minimum(i, n_ctx_blocks - 1), 0)),
             pl.BlockSpec((tm, d), lambda i: (lat_base + jnp.maximum(i - n_ctx_blocks, 0), 0))]
    return specs, [x_ctx, x_ctx if x_lat is None else x_lat]


def _qkv_kernel(*refs, qk_norm, tn, halves, n_ctx_blocks, state_slot):
    xc_ref, xl_ref, g_ref, sh_ref, sc_ref, w_ref, cos_ref, sin_ref, hg_ref = refs[:9]
    o_ref, ks_ref, vs_ref = refs[-3:]
    tm = xc_ref.shape[0]
    rh = tm // halves
    heads_per_tile = tn // HEAD_DIM
    kcol, vcol = N_HEADS * HEAD_DIM, (N_HEADS + N_KV_HEADS) * HEAD_DIM

    def body(x_ref, latent):
        for hf in range(halves):
            rs = slice(hf * rh, (hf + 1) * rh)
            h = _norm_mod(x_ref[rs, :], g_ref[...], sh_ref[0], sc_ref[0]).astype(BF16)
            for jt in range(w_ref.shape[1] // tn):
                acc = jnp.dot(h, w_ref[:, jt * tn:(jt + 1) * tn], preferred_element_type=F32)
                for hh in range(heads_per_tile):
                    head = jt * heads_per_tile + hh
                    a = acc[:, hh * HEAD_DIM:(hh + 1) * HEAD_DIM]
                    if head < N_HEADS + N_KV_HEADS:
                        if qk_norm:
                            gain = hg_ref[0:1, :] if head < N_HEADS else hg_ref[1:2, :]
                            a = a * lax.rsqrt(jnp.mean(a * a, axis=-1, keepdims=True) + EPS) * gain
                        if latent:
                            a = _rope_rotate(a, cos_ref[rs, :], sin_ref[rs, :])
                    o_ref[rs, head * HEAD_DIM:(head + 1) * HEAD_DIM] = a
        if not latent:
            seqs, ctx_len = ks_ref.shape[0], ks_ref.shape[-3]
            for ref, col in ((ks_ref, kcol), (vs_ref, vcol)):
                if state_slot is not None:
                    for other in range(ref.shape[1]):
                        if other != state_slot:
                            ref[:, other] = jnp.zeros(ref.shape[:1] + ref.shape[2:], F32)
                for s in range(seqs):
                    for kvh in range(N_KV_HEADS):
                        val = o_ref[s * ctx_len:(s + 1) * ctx_len, col + kvh * HEAD_DIM:col + (kvh + 1) * HEAD_DIM]
                        if state_slot is None:
                            ref[s, :, kvh, :] = val
                        else:
                            ref[s, state_slot, :, kvh, :] = val

    i = pl.program_id(0)
    pl.when(i < n_ctx_blocks)(lambda: body(xc_ref, False))
    pl.when(i >= n_ctx_blocks)(lambda: body(xl_ref, True))


def _qkv_project(x_ctx, x_lat, g, mod, w, layer, cos, sin, hg, state, *, n_ctx_rows, lat_len, ctx_len, tm):
    d = x_ctx.shape[1]
    seqs = tm // ctx_len
    n_layers = w.shape[0]
    state_shape = (n_ctx_rows // ctx_len, n_layers, ctx_len, N_KV_HEADS, HEAD_DIM)
    last_ctx = n_ctx_rows // tm - 1
    if state is None:
        state_spec = pl.BlockSpec((seqs, n_layers, ctx_len, N_KV_HEADS, HEAD_DIM),
                                  lambda i: (jnp.minimum(i, last_ctx), 0, 0, 0, 0))
        state_specs, state_args, aliases = [], [], {}
    else:
        state_spec = pl.BlockSpec((seqs, None, ctx_len, N_KV_HEADS, HEAD_DIM),
                                  lambda i: (jnp.minimum(i, last_ctx), layer, 0, 0, 0))
        state_specs, state_args = [pl.BlockSpec(memory_space=pl.ANY)] * 2, list(state)
        aliases = {9: 1, 10: 2}
    n = w.shape[2]
    ncb = n_ctx_rows // tm
    t = n_ctx_rows + (x_ctx.shape[0] - n_ctx_rows if x_lat is None else x_lat.shape[0])
    row = functools.partial(_mod_row, tm=tm, n_ctx_rows=n_ctx_rows, lat_len=lat_len)
    tab = lambda i: ((jnp.maximum(i - ncb, 0)) % (lat_len // tm), 0)
    qk_norm = hg is not None
    if hg is None:
        hg = jnp.ones((2, HEAD_DIM), F32)
    x_specs, x_args = _two_source_specs(x_ctx, x_lat, tm, ncb)
    return pl.pallas_call(
        functools.partial(_qkv_kernel, qk_norm=qk_norm, tn=512, halves=2, n_ctx_blocks=ncb,
                          state_slot=layer if state is None else None),
        grid=(t // tm,),
        in_specs=x_specs + [
            pl.BlockSpec((1, d), lambda i: (0, 0)),
            pl.BlockSpec((1, 1, d), lambda i: (row(i) * 6, 0, 0)),
            pl.BlockSpec((1, 1, d), lambda i: (row(i) * 6 + 1, 0, 0)),
            pl.BlockSpec((None, d, n), lambda i: (layer, 0, 0), pipeline_mode=pl.Buffered(1)),
            pl.BlockSpec((tm, HEAD_DIM), tab), pl.BlockSpec((tm, HEAD_DIM), tab),
            pl.BlockSpec((2, HEAD_DIM), lambda i: (0, 0))] + state_specs,
        out_specs=[pl.BlockSpec((tm, n), lambda i: (i, 0)), state_spec, state_spec],
        out_shape=[jax.ShapeDtypeStruct((t, n), F32), jax.ShapeDtypeStruct(state_shape, F32),
                   jax.ShapeDtypeStruct(state_shape, F32)],
        input_output_aliases=aliases,
        compiler_params=_params("arbitrary"),
        name="qkv_project",
    )(*x_args, g.reshape(1, d), mod, mod, w, cos, sin, hg, *state_args)


def _proj_res_kernel(ac_ref, al_ref, xc_ref, xl_ref, w_ref, gate_ref, g2_ref, sh2_ref, sc2_ref, o_ref, h_ref, *,
                     n_ctx_blocks):
    i = pl.program_id(0)

    def emit(a_ref, x_ref):
        xn = x_ref[...] + gate_ref[0] * jnp.dot(a_ref[...], w_ref[...], preferred_element_type=F32)
        o_ref[...] = xn
        h_ref[...] = _norm_mod(xn, g2_ref[...], sh2_ref[0], sc2_ref[0]).astype(BF16)

    pl.when(i < n_ctx_blocks)(lambda: emit(ac_ref, xc_ref))
    pl.when(i >= n_ctx_blocks)(lambda: emit(al_ref, xl_ref))


def _proj_residual(a_ctx, a_lat, w, layer, x_ctx, x_lat, mod, g2, *, n_ctx_rows, lat_len, tm):
    d = x_ctx.shape[1]
    k = w.shape[1]
    ncb = n_ctx_rows // tm
    t = a_ctx.shape[0] + a_lat.shape[0]
    row = functools.partial(_mod_row, tm=tm, n_ctx_rows=n_ctx_rows, lat_len=lat_len)
    modspec = lambda which: pl.BlockSpec((1, 1, d), lambda i: (row(i) * 6 + which, 0, 0))
    x_specs, x_args = _two_source_specs(x_ctx, x_lat, tm, ncb)
    return pl.pallas_call(
        functools.partial(_proj_res_kernel, n_ctx_blocks=ncb),
        grid=(t // tm,),
        in_specs=[pl.BlockSpec((tm, k), lambda i: (jnp.minimum(i, ncb - 1), 0)),
                  pl.BlockSpec((tm, k), lambda i: (jnp.maximum(i - ncb, 0), 0))]
        + x_specs
        + [pl.BlockSpec((None, k, d), lambda i: (layer, 0, 0), pipeline_mode=pl.Buffered(1)),
           modspec(2),
           pl.BlockSpec((1, d), lambda i: (0, 0)),
           modspec(3), modspec(4)],
        out_specs=[pl.BlockSpec((tm, d), lambda i: (i, 0)), pl.BlockSpec((tm, d), lambda i: (i, 0))],
        out_shape=[jax.ShapeDtypeStruct((t, d), F32), jax.ShapeDtypeStruct((t, d), BF16)],
        compiler_params=_params("parallel"),
        name="proj_residual",
    )(a_ctx, a_lat, *x_args, w, mod, g2.reshape(1, d), mod, mod)


def _ffn_kernel(x_ref, h_ref, gate_ref, wg_ref, wu_ref, wd_ref, o_ref):
    f = pl.program_id(1)

    @pl.when(f == 0)
    def _():
        o_ref[...] = jnp.zeros_like(o_ref)

    h = h_ref[...]
    gv = jnp.dot(h, wg_ref[...], preferred_element_type=F32)
    uv = jnp.dot(h, wu_ref[...], preferred_element_type=F32)
    act = (_silu(gv) * uv).astype(BF16)
    o_ref[...] += jnp.dot(act, wd_ref[...], preferred_element_type=F32)

    @pl.when(f == pl.num_programs(1) - 1)
    def _():
        o_ref[...] = x_ref[...] + gate_ref[0] * o_ref[...]


def _ffn(x, h, mod, w_gu, w_down, layer, *, n_ctx_rows, lat_len, tm, tf):
    t, d = x.shape
    ff = w_down.shape[1]
    nf = ff // tf
    row = functools.partial(_mod_row, tm=tm, n_ctx_rows=n_ctx_rows, lat_len=lat_len)
    return pl.pallas_call(
        _ffn_kernel,
        grid=(t // tm, nf),
        in_specs=[pl.BlockSpec((tm, d), lambda i, f: (i, 0)),
                  pl.BlockSpec((tm, d), lambda i, f: (i, 0)),
                  pl.BlockSpec((1, 1, d), lambda i, f: (row(i) * 6 + 5, 0, 0)),
                  pl.BlockSpec((None, d, tf), lambda i, f: (layer, 0, f)),
                  pl.BlockSpec((None, d, tf), lambda i, f: (layer, 0, nf + f)),
                  pl.BlockSpec((None, tf, d), lambda i, f: (layer, f, 0))],
        out_specs=pl.BlockSpec((tm, d), lambda i, f: (i, 0)),
        out_shape=jax.ShapeDtypeStruct((t, d), F32),
        compiler_params=_params("parallel", "arbitrary"),
        name="ffn_swiglu",
    )(x, h, mod, w_gu, w_gu, w_down)


def _stack_heads(q):
    return jnp.concatenate([q[:, h * HEAD_DIM:(h + 1) * HEAD_DIM] for h in range(GROUP)], axis=0)


def _unstack_heads(o, rows):
    return jnp.concatenate([o[h * rows:(h + 1) * rows] for h in range(GROUP)], axis=1)


def _sink_column(sink_ref, kh, rows):
    head = lax.broadcasted_iota(jnp.int32, (GROUP * rows, 1), 0) // rows
    col = jnp.zeros((GROUP * rows, 1), F32)
    for h in range(GROUP):
        col = jnp.where(head == h, sink_ref[kh * GROUP + h], col)
    return col


def _softmax_av(s, v, sink_col):
    m = jnp.max(s, axis=-1, keepdims=True)
    if sink_col is not None:
        m = jnp.maximum(m, sink_col)
    p = jnp.exp(s - m)
    l = jnp.sum(p, axis=-1, keepdims=True)
    if sink_col is not None:
        l = l + jnp.exp(sink_col - m)
    return jnp.dot(p.astype(BF16), v, preferred_element_type=F32) / l


def _attn_ctx_kernel(sink_ref, q_ref, k_ref, v_ref, o_ref, *, use_sink):
    rows = q_ref.shape[0]
    qw = GROUP * HEAD_DIM
    for kh in range(N_KV_HEADS):
        qs = _stack_heads(q_ref[:, kh * qw:(kh + 1) * qw] * SCALE).astype(BF16)
        k = k_ref[:, kh * HEAD_DIM:(kh + 1) * HEAD_DIM].astype(BF16)
        v = v_ref[:, kh * HEAD_DIM:(kh + 1) * HEAD_DIM].astype(BF16)
        s = lax.dot_general(qs, k, NT_DIMS, preferred_element_type=F32)
        sink_col = _sink_column(sink_ref, kh, rows) if use_sink else None
        o_ref[:, kh * qw:(kh + 1) * qw] = _unstack_heads(_softmax_av(s, v, sink_col), rows).astype(BF16)


def _attn_ctx(y, sink, *, n_ctx_seqs, ctx_len, use_sink):
    t = n_ctx_seqs * ctx_len
    q_w, kv_w = N_HEADS * HEAD_DIM, N_KV_HEADS * HEAD_DIM
    return pl.pallas_call(
        functools.partial(_attn_ctx_kernel, use_sink=use_sink),
        grid=(n_ctx_seqs,),
        in_specs=[pl.BlockSpec(memory_space=pltpu.SMEM),
                  pl.BlockSpec((ctx_len, q_w), lambda b: (b, 0)),
                  pl.BlockSpec((ctx_len, kv_w), lambda b: (b, q_w // kv_w)),
                  pl.BlockSpec((ctx_len, kv_w), lambda b: (b, q_w // kv_w + 1))],
        out_specs=pl.BlockSpec((ctx_len, q_w), lambda b: (b, 0)),
        out_shape=jax.ShapeDtypeStruct((t, q_w), BF16),
        compiler_params=_params("parallel"),
        name="attn_context",
    )(sink, y, y, y)


def _keys_by_queries_attention(q_ref, k_scr, vt_scr, s_scr, chunks, bias_ref, sink_ref, o_ref):
    qn = q_ref.shape[0]
    kh = pl.program_id(1)

    def logits(h, ci):
        st, sz = chunks[ci]
        q = (q_ref[:, h * HEAD_DIM:(h + 1) * HEAD_DIM] * (SCALE * LOG2E)).astype(BF16)
        st_ = lax.dot_general(k_scr[st:st + sz, :], q, NT_DIMS, preferred_element_type=F32)
        if bias_ref is not None and ci == 0:
            st_ = st_ + bias_ref[...]
        s_scr[st:st + sz, h * qn:(h + 1) * qn] = st_
        return jnp.max(st_, axis=0, keepdims=True)

    def weigh(h, ci, m):
        st, sz = chunks[ci]
        p = jnp.exp2(s_scr[st:st + sz, h * qn:(h + 1) * qn] - m)
        pv = jnp.dot(vt_scr[:, st:st + sz], p.astype(BF16), preferred_element_type=F32)
        return jnp.sum(p, axis=0, keepdims=True), pv

    n = len(chunks)
    maxes = [logits(0, ci) for ci in range(n)]
    for h in range(GROUP):
        m = functools.reduce(jnp.maximum, maxes)
        if sink_ref is not None:
            sink2 = sink_ref[kh * GROUP + h] * LOG2E
            m = jnp.maximum(m, sink2)
        maxes, l, acc = [], None, None
        for ci in range(n):
            if h + 1 < GROUP:
                maxes.append(logits(h + 1, ci))
            lc, pv = weigh(h, ci, m)
            l, acc = (lc, pv) if l is None else (l + lc, acc + pv)
        if sink_ref is not None:
            l = l + jnp.exp2(sink2 - m)
        o_ref[:, h * HEAD_DIM:(h + 1) * HEAD_DIM] = (acc * (1.0 / l)).T.astype(BF16)


def _attn_lat_win_kernel(sink_ref, q_ref, kp_ref, kc_ref, kn_ref, vp_ref, vc_ref, vn_ref, ck_ref, cv_ref, bias_ref,
                         o_ref, k_scr, vt_scr, s_scr):
    past = ck_ref.shape[0]
    win = Q_STEP + 2 * BLOCK

    @pl.when(pl.program_id(2) == 0)
    def _():
        k_scr[win:, :] = ck_ref[...].astype(BF16)
        vt_scr[:, win:] = cv_ref[...].T.astype(BF16)

    row = 0
    for k_ref, v_ref in ((kp_ref, vp_ref), (kc_ref, vc_ref), (kn_ref, vn_ref)):
        n = k_ref.shape[0]
        k_scr[row:row + n, :] = k_ref[...].astype(BF16)
        vt_scr[:, row:row + n] = v_ref[...].T.astype(BF16)
        row += n
    _keys_by_queries_attention(q_ref, k_scr, vt_scr, s_scr, ((0, win), (win, past)), bias_ref, sink_ref, o_ref)


def _window_bias(n_steps):
    win = Q_STEP + 2 * BLOCK
    c = jnp.arange(win, dtype=jnp.int32)[:, None]
    r = jnp.arange(Q_STEP, dtype=jnp.int32)[None, :]
    band = jnp.abs(r + BLOCK - c) <= WINDOW
    variants = [band & (c >= BLOCK), band, band & (c < win - BLOCK)]
    if n_steps == 1:
        variants = [variants[0] & (c < win - BLOCK)] * 3
    return jnp.where(jnp.stack(variants), 0.0, NEG).astype(F32)


def _cache_spec(cache, layer):
    return pl.BlockSpec((None, None, cache.shape[2], HEAD_DIM), lambda b, kh, qs: (b, layer, 0, kh))


def _attn_lat_win(y, sink, cache_k, cache_v, layer, *, n_ctx_rows, lat_batch, lat_len):
    qw = GROUP * HEAD_DIM
    nb = lat_len // BLOCK
    ns = lat_len // Q_STEP
    per = Q_STEP // BLOCK
    base = n_ctx_rows // BLOCK
    past = cache_k.shape[2]
    n_keys = Q_STEP + 2 * BLOCK + past

    def edge(col0, blk):
        return pl.BlockSpec((BLOCK, HEAD_DIM),
                            lambda b, kh, qs: (base + b * nb + jnp.clip(qs * per + blk, 0, nb - 1), col0 + kh))

    def centre(col0):
        return pl.BlockSpec((Q_STEP, HEAD_DIM), lambda b, kh, qs: (base // per + b * ns + qs, col0 + kh))

    kcol, vcol = N_HEADS, N_HEADS + N_KV_HEADS
    cache_spec = _cache_spec(cache_k, layer)
    return pl.pallas_call(
        _attn_lat_win_kernel,
        grid=(lat_batch, N_KV_HEADS, ns),
        in_specs=[pl.BlockSpec(memory_space=pltpu.SMEM),
                  pl.BlockSpec((Q_STEP, qw), lambda b, kh, qs: (base // per + b * ns + qs, kh)),
                  edge(kcol, -1), centre(kcol), edge(kcol, per),
                  edge(vcol, -1), centre(vcol), edge(vcol, per),
                  cache_spec, cache_spec,
                  pl.BlockSpec((None, Q_STEP + 2 * BLOCK, Q_STEP),
                               lambda b, kh, qs: (jnp.where(qs == 0, 0, jnp.where(qs == ns - 1, 2, 1)), 0, 0))],
        out_specs=pl.BlockSpec((Q_STEP, qw), lambda b, kh, qs: (b * ns + qs, kh)),
        out_shape=jax.ShapeDtypeStruct((lat_batch * lat_len, N_HEADS * HEAD_DIM), BF16),
        scratch_shapes=[pltpu.VMEM((n_keys, HEAD_DIM), BF16), pltpu.VMEM((HEAD_DIM, n_keys), BF16),
                        pltpu.VMEM((n_keys, GROUP * Q_STEP), F32)],
        compiler_params=_params("parallel", "parallel", "arbitrary"),
        name="attn_latent_window",
    )(sink, y, y, y, y, y, y, y, cache_k, cache_v, _window_bias(ns))


def _attn_lat_full_kernel(q_ref, k_ref, v_ref, ck_ref, cv_ref, o_ref, k_scr, vt_scr, s_scr, *, chunk):
    lat_len = k_ref.shape[0]
    n_keys = k_scr.shape[0]

    @pl.when(pl.program_id(2) == 0)
    def _():
        k_scr[0:lat_len, :] = k_ref[...].astype(BF16)
        k_scr[lat_len:, :] = ck_ref[...].astype(BF16)
        vt_scr[:, 0:lat_len] = v_ref[...].T.astype(BF16)
        vt_scr[:, lat_len:] = cv_ref[...].T.astype(BF16)

    chunks = tuple((st, chunk) for st in range(0, n_keys, chunk))
    _keys_by_queries_attention(q_ref, k_scr, vt_scr, s_scr, chunks, None, None, o_ref)


def _attn_lat_full(y, cache_k, cache_v, layer, *, n_ctx_rows, lat_batch, lat_len):
    qw = GROUP * HEAD_DIM
    ns = lat_len // Q_STEP
    base = n_ctx_rows // Q_STEP
    seq_base = n_ctx_rows // lat_len
    past = cache_k.shape[2]
    cache_spec = _cache_spec(cache_k, layer)
    n_keys = lat_len + past
    chunk = 512
    assert n_keys % chunk == 0
    return pl.pallas_call(
        functools.partial(_attn_lat_full_kernel, chunk=chunk),
        grid=(lat_batch, N_KV_HEADS, ns),
        in_specs=[pl.BlockSpec((Q_STEP, qw), lambda b, kh, qs: (base + b * ns + qs, kh)),
                  pl.BlockSpec((lat_len, HEAD_DIM), lambda b, kh, qs: (seq_base + b, N_HEADS + kh)),
                  pl.BlockSpec((lat_len, HEAD_DIM), lambda b, kh, qs: (seq_base + b, N_HEADS + N_KV_HEADS + kh)),
                  cache_spec, cache_spec],
        out_specs=pl.BlockSpec((Q_STEP, qw), lambda b, kh, qs: (b * ns + qs, kh)),
        out_shape=jax.ShapeDtypeStruct((lat_batch * lat_len, N_HEADS * HEAD_DIM), BF16),
        scratch_shapes=[pltpu.VMEM((n_keys, HEAD_DIM), BF16), pltpu.VMEM((HEAD_DIM, n_keys), BF16),
                        pltpu.VMEM((n_keys, GROUP * Q_STEP), F32)],
        compiler_params=_params("parallel", "parallel", "arbitrary"),
        name="attn_latent_full",
    )(y, y, y, cache_k, cache_v)


def _rope_tables(lat_len):
    half = HEAD_DIM // 2
    n_rows = lat_len // GRID_W
    row = jnp.repeat(jnp.arange(n_rows, dtype=F32), GRID_W)
    col = jnp.tile(jnp.arange(GRID_W, dtype=F32), n_rows)
    inv = ROPE_THETA ** (-jnp.arange(0, half, 2, dtype=F32) / half)
    ang_r, ang_c = row[:, None] * inv, col[:, None] * inv
    cos = jnp.concatenate([jnp.cos(ang_r)] * 2 + [jnp.cos(ang_c)] * 2, axis=-1)
    sin = jnp.concatenate([-jnp.sin(ang_r), jnp.sin(ang_r), -jnp.sin(ang_c), jnp.sin(ang_c)], axis=-1)
    return cos, sin


def _dft_mats(seq_len, tk):
    n = 2 * seq_len
    k = jnp.arange(seq_len, dtype=jnp.int32)
    step = 64
    a = jnp.arange(seq_len // step, dtype=jnp.int32) * step
    b = jnp.arange(step, dtype=jnp.int32)
    ang_a = ((k[:, None] * a[None, :]) % n).astype(F32) * (2.0 * math.pi / n)
    ang_b = ((k[:, None] * b[None, :]) % n).astype(F32) * (2.0 * math.pi / n)
    ca, sa = jnp.cos(ang_a)[:, :, None], jnp.sin(ang_a)[:, :, None]
    cb, sb = jnp.cos(ang_b)[:, None, :], jnp.sin(ang_b)[:, None, :]
    fc = (ca * cb - sa * sb).reshape(seq_len, seq_len)
    fs = -(sa * cb + ca * sb).reshape(seq_len, seq_len)
    sign = jnp.where(k % 2 == 0, 1.0, -1.0).astype(F32)
    fs = jnp.where(k[:, None] == 0, sign[None, :], fs)
    kfc = seq_len // tk
    ana = jnp.concatenate([fc.reshape(kfc, tk, seq_len), fs.reshape(kfc, tk, seq_len)], axis=1).astype(BF16)
    ana = lax.optimization_barrier(ana)
    return ana, ana.transpose(0, 2, 1)


def _hy_filter_kernel(feat_ref, tn_ref, w1_ref, b1_ref, w2_ref, b2_ref, fq_ref, w3f_ref, w3b_ref, dl_ref, ana_ref,
                      h_ref, pq_scr, inv_scr, nyq_scr):
    kf = pl.program_id(1)
    hp = lax.Precision.HIGHEST
    tk = h_ref.shape[1]

    @pl.when(kf == 0)
    def _():
        def taps(g):
            a = jnp.dot(feat_ref[g], w1_ref[...], precision=hp, preferred_element_type=F32) + b1_ref[...]
            a = jnp.sin(fq_ref[0:1, :] * a)
            a = jnp.dot(a, w2_ref[...], precision=hp, preferred_element_type=F32) + b2_ref[...]
            a = jnp.sin(fq_ref[1:2, :] * a)
            decay = jnp.exp(-tn_ref[g] * dl_ref[...])
            a = a.astype(BF16)
            return (jnp.dot(a, w3f_ref[...].astype(BF16), preferred_element_type=F32) * decay,
                    jnp.dot(a, w3b_ref[...].astype(BF16), preferred_element_type=F32) * decay)

        fwd0, bwd0 = taps(0)
        fwd1, bwd1 = taps(1)
        fwdm, bwdm = taps(2)
        rowi = lax.broadcasted_iota(jnp.int32, fwd0.shape, 0)
        first = rowi == 0
        bwd0 = jnp.where(first, 0.0, bwd0)
        absum = lambda v: jnp.sum(jnp.abs(v), axis=0, keepdims=True)
        inv_scr[...] = 1.0 / (absum(fwd0) + absum(fwd1) + absum(bwd0) + absum(bwd1) + EPS)
        parts = ((fwd0, bwd0),
                 (bwdm, jnp.where(first, 0.0, bwd1)),
                 (fwd1, jnp.where(first, 0.0, fwdm)))
        for n, (pos, neg) in enumerate(parts):
            p = pos + neg
            nyq_scr[n:n + 1, :] = jnp.sum(jnp.where((rowi & 1) == 0, p, -p), axis=0, keepdims=True)
            pq_scr[2 * n] = p.astype(BF16)
            pq_scr[2 * n + 1] = (pos - neg).astype(BF16)

    inv = inv_scr[...]
    for n in range(3):
        h_ref[2 * n] = jnp.dot(ana_ref[0:tk, :], pq_scr[2 * n], preferred_element_type=F32) * inv
        hi = jnp.dot(ana_ref[tk:, :], pq_scr[2 * n + 1], preferred_element_type=F32)
        rowk = lax.broadcasted_iota(jnp.int32, hi.shape, 0)
        hi = jnp.where((rowk == 0) & (kf == 0), nyq_scr[n:n + 1, :], hi)
        h_ref[2 * n + 1] = hi * inv


def _hy_filters(seq_len, f_w1, f_b1, f_w2, f_b2, f_w3, freq, ana_mats, *, tc):
    half = seq_len // 2
    tk = ana_mats.shape[1] // 2
    hid = f_w1.shape[1]
    nd = f_w3.shape[1] // 2
    d = nd // HY_ORDER
    t = jnp.arange(seq_len, dtype=F32)
    tnorm = t / max(seq_len - 1, 1)
    bands = (HY_EMB - 1) // 2
    fb = jnp.linspace(1e-4, bands - 1, bands, dtype=F32)
    w = 2.0 * math.pi * t / seq_len
    feats = jnp.concatenate([tnorm[:, None], jnp.cos(w[:, None] * fb), -jnp.sin(w[:, None] * fb)], axis=-1)
    emb = 64
    feats = jnp.pad(feats, ((0, 0), (0, emb - HY_EMB)))
    w1 = jnp.pad(f_w1, ((0, emb - HY_EMB), (0, 0)))
    mirror = half - jnp.arange(half)
    feats3 = jnp.stack([feats[:half], feats[half:], feats[mirror]])
    tn3 = jnp.stack([tnorm[:half], tnorm[half:], tnorm[mirror]]).reshape(3, half, 1)
    deltas = jnp.abs(jnp.linspace(HY_MIN_DECAY, HY_MAX_DECAY, d, dtype=F32))
    deltas = jnp.tile(deltas, HY_ORDER).reshape(1, nd)
    nct = nd // tc
    const = lambda shape: pl.BlockSpec(shape, lambda c, kf: (0,) * len(shape))
    return pl.pallas_call(
        _hy_filter_kernel,
        grid=(nct, half // tk),
        in_specs=[const((3, half, emb)), const((3, half, 1)),
                  const((emb, hid)), const((1, hid)), const((hid, hid)), const((1, hid)), const((2, hid)),
                  pl.BlockSpec((hid, tc), lambda c, kf: (0, c)),
                  pl.BlockSpec((hid, tc), lambda c, kf: (0, nct + c)),
                  pl.BlockSpec((1, tc), lambda c, kf: (0, c)),
                  pl.BlockSpec((None, 2 * tk, half), lambda c, kf: (kf, 0, 0))],
        out_specs=pl.BlockSpec((6, tk, tc), lambda c, kf: (0, kf, c)),
        out_shape=jax.ShapeDtypeStruct((6, half, nd), F32),
        scratch_shapes=[pltpu.VMEM((6, half, tc), BF16), pltpu.VMEM((1, tc), F32), pltpu.VMEM((8, tc), F32)],
        compiler_params=_params("parallel", "arbitrary"),
        name="hyena_filter",
    )(feats3, tn3, w1, f_b1.reshape(1, hid), f_w2, f_b2.reshape(1, hid), freq, f_w3, f_w3, deltas, ana_mats)


def _hy_inproj_kernel(x_ref, g_ref, sh_ref, sc_ref, w_ref, cw_ref, cb_ref, o_ref, h_scr, y_scr, *, n_ctx_blocks,
                      ctx_len):
    j = pl.program_id(1)
    n = x_ref.shape[0]

    @pl.when(j == 0)
    def _():
        for r0 in range(0, n, 256):
            rs = slice(r0, r0 + 256)
            h_scr[rs, :] = _norm_mod(x_ref[rs, :], g_ref[...], sh_ref[0], sc_ref[0]).astype(BF16)
        y_scr[0:PAD, :] = jnp.zeros((PAD, y_scr.shape[1]), F32)
        y_scr[PAD + n:, :] = jnp.zeros((PAD, y_scr.shape[1]), F32)

    is_ctx = pl.program_id(0) < n_ctx_blocks
    chunk = 256

    def product(r0):
        y_scr[PAD + r0:PAD + r0 + chunk, :] = jnp.dot(h_scr[r0:r0 + chunk, :], w_ref[...],
                                                      preferred_element_type=F32)

    product(0)
    for r0 in range(0, n, chunk):
        if r0 + chunk < n:
            product(r0 + chunk)
        pos = (r0 + lax.broadcasted_iota(jnp.int32, (chunk, y_scr.shape[1]), 0)) % ctx_len
        prev = jnp.where(is_ctx & (pos == 0), 0.0, y_scr[PAD - 1 + r0:PAD - 1 + r0 + chunk, :])
        cur = y_scr[PAD + r0:PAD + r0 + chunk, :]
        nxt = jnp.where(is_ctx & (pos == ctx_len - 1), 0.0, y_scr[PAD + 1 + r0:PAD + 1 + r0 + chunk, :])
        o_ref[r0:r0 + chunk, :] = (prev * cw_ref[0:1, :] + cur * cw_ref[1:2, :] + nxt * cw_ref[2:3, :]
                                   + cb_ref[...])


def _hy_inproj(x, g, mod, w, conv_w, conv_b, *, n_ctx_rows, lat_len, ctx_len, tn):
    t, d = x.shape
    n = w.shape[1]
    tm = lat_len
    row = functools.partial(_mod_row, tm=tm, n_ctx_rows=n_ctx_rows, lat_len=lat_len)
    return pl.pallas_call(
        functools.partial(_hy_inproj_kernel, n_ctx_blocks=n_ctx_rows // tm, ctx_len=ctx_len),
        grid=(t // tm, n // tn),
        in_specs=[pl.BlockSpec((tm, d), lambda i, j: (i, 0), pipeline_mode=pl.Buffered(1)),
                  pl.BlockSpec((1, d), lambda i, j: (0, 0)),
                  pl.BlockSpec((1, 1, d), lambda i, j: (row(i) * 6, 0, 0)),
                  pl.BlockSpec((1, 1, d), lambda i, j: (row(i) * 6 + 1, 0, 0)),
                  pl.BlockSpec((d, tn), lambda i, j: (0, j)),
                  pl.BlockSpec((3, tn), lambda i, j: (0, j)),
                  pl.BlockSpec((1, tn), lambda i, j: (0, j))],
        out_specs=pl.BlockSpec((tm, tn), lambda i, j: (i, j)),
        out_shape=jax.ShapeDtypeStruct((t, n), F32),
        scratch_shapes=[pltpu.VMEM((tm, d), BF16), pltpu.VMEM((tm + 2 * PAD, tn), F32)],
        compiler_params=_params("parallel", "arbitrary"),
        name="hyena_inproj",
    )(x, g.reshape(1, d), mod, mod, w, conv_w, conv_b.reshape(1, n))


def _hy_conv_kernel(a_ref, g_ref, skip_ref, ana_ref, syn_ref, h_ref, o_ref, acc_scr, u_scr=None, *, seq_len):
    kf = pl.program_id(2)
    rows = a_ref.shape[0]
    tk = h_ref.shape[1]
    half = seq_len // 2
    chunk = 256

    @pl.when(kf == 0)
    def _():
        acc_scr[...] = jnp.zeros_like(acc_scr)
        if u_scr is not None:
            for r0 in range(0, rows, chunk):
                u_scr[r0:r0 + chunk, :] = a_ref[r0:r0 + chunk, :].astype(BF16)

    u_ref = a_ref if u_scr is None else u_scr

    first = (lax.broadcasted_iota(jnp.int32, (tk, a_ref.shape[1]), 0) == 0) & (kf == 0)

    def respond(uf, n):
        ur, ui = uf[0:tk, :], uf[tk:, :]
        hr, hi = h_ref[2 * n], h_ref[2 * n + 1]
        yr = jnp.where(first, 0.5 * (ur * hr), ur * hr - ui * hi)
        yi = jnp.where(first, 0.5 * (ui * hi), ur * hi + ui * hr)
        return yr, yi

    step = min(half, 512)
    for seg in range(0, rows, seq_len):
        uf1 = jnp.dot(ana_ref[...], u_ref[seg:seg + half, :], preferred_element_type=F32)
        uf2 = jnp.dot(ana_ref[...], u_ref[seg + half:seg + seq_len, :], preferred_element_type=F32)
        for dst, (n1, n2) in ((seg, (0, 1)), (seg + half, (2, 0))):
            y1r, y1i = respond(uf1, n1)
            y2r, y2i = respond(uf2, n2)
            yf = jnp.concatenate([y1r + y2r, y1i + y2i], axis=0).astype(BF16)
            for n0 in range(0, half, step):
                acc_scr[dst + n0:dst + n0 + step, :] += jnp.dot(syn_ref[n0:n0 + step, :], yf,
                                                                preferred_element_type=F32)

    @pl.when(kf == pl.num_programs(2) - 1)
    def _():
        for r0 in range(0, rows, chunk):
            rs = slice(r0, r0 + chunk)
            a = a_ref[rs, :].astype(F32)
            o_ref[rs, :] = (g_ref[rs, :] * (acc_scr[rs, :] * (1.0 / half) + a * skip_ref[...])).astype(BF16)


def _hy_conv(a, a_part, a_base, g, g_part, g_base, skip_row, mats, h, order, *, n_blocks, rows, seq_len, tc):
    d = skip_row.shape[1]
    nct = d // tc
    ana_mats, syn_mats = mats
    kfc = ana_mats.shape[0]
    tk = ana_mats.shape[1] // 2

    def amap(part, base):
        return lambda b, c, kf: (base + b, part * nct + c)

    half = seq_len // 2
    return pl.pallas_call(
        functools.partial(_hy_conv_kernel, seq_len=seq_len),
        grid=(n_blocks, nct, kfc),
        in_specs=[pl.BlockSpec((rows, tc), amap(a_part, a_base)),
                  pl.BlockSpec((rows, tc), amap(g_part, g_base)),
                  pl.BlockSpec((1, tc), lambda b, c, kf: (0, c)),
                  pl.BlockSpec((None, 2 * tk, half), lambda b, c, kf: (kf, 0, 0)),
                  pl.BlockSpec((None, half, 2 * tk), lambda b, c, kf: (kf, 0, 0)),
                  pl.BlockSpec((6, tk, tc), lambda b, c, kf: (0, kf, order * nct + c))],
        out_specs=pl.BlockSpec((rows, tc), lambda b, c, kf: (b, c)),
        out_shape=jax.ShapeDtypeStruct((n_blocks * rows, d), BF16),
        scratch_shapes=[pltpu.VMEM((rows, tc), F32)] + ([] if a.dtype == BF16 else [pltpu.VMEM((rows, tc), BF16)]),
        compiler_params=_params("parallel", "parallel", "arbitrary"),
        name="hyena_conv",
    )(a, g, skip_row, ana_mats, syn_mats, h)


def _final_norm_kernel(x_ref, g_ref, o_ref):
    x = x_ref[...]
    o_ref[...] = x * lax.rsqrt(jnp.mean(x * x, axis=-1, keepdims=True) + EPS) * g_ref[...]


def _final_norm(x, g, *, row0, rows, tm):
    d = x.shape[1]
    base = row0 // tm
    return pl.pallas_call(
        _final_norm_kernel,
        grid=(rows // tm,),
        in_specs=[pl.BlockSpec((tm, d), lambda i: (base + i, 0)),
                  pl.BlockSpec((1, d), lambda i: (0, 0))],
        out_specs=pl.BlockSpec((tm, d), lambda i: (i, 0)),
        out_shape=jax.ShapeDtypeStruct((rows, d), F32),
        compiler_params=_params("parallel"),
        name="final_norm",
    )(x, g.reshape(1, d))


def kernel(x_prompt, x_sample, cache_win_k, cache_win_v, cache_ax_k, cache_ax_v, c, c_ctx, norm_mix_g, norm_ffn_g, mod_w, mod_b, win_wqkv, win_wo, win_sink, hy_w_in, hy_conv_w, hy_conv_b, hy_f_w1, hy_f_b1, hy_f_w2, hy_f_b2, hy_f_w3, hy_freq, hy_skip, hy_wo, ax_wqkv, ax_q_g, ax_k_g, ax_wo, ffn_w_gu, ffn_w_down, final_g):
    n_ctx_seqs, ctx_len, d = x_prompt.shape
    lat_batch, lat_len, _ = x_sample.shape
    depth = mod_w.shape[0]
    past = cache_win_k.shape[2]
    n_ctx_rows = n_ctx_seqs * ctx_len
    n_lat_rows = lat_batch * lat_len
    tm = 512
    assert n_ctx_rows % tm == 0 and lat_len % tm == 0 and n_ctx_rows % lat_len == 0
    assert lat_batch + 1 <= MOD_ROWS
    geom = dict(n_ctx_rows=n_ctx_rows, lat_len=lat_len, tm=tm)

    x, x_lat = x_prompt.reshape(n_ctx_rows, d), x_sample.reshape(n_lat_rows, d)
    cvec = jnp.concatenate([c_ctx[None, :], c, jnp.zeros((MOD_ROWS - 1 - lat_batch, d), F32)], axis=0)
    mod_all = _adaln_all(cvec, mod_w, mod_b)
    cos, sin = _rope_tables(lat_len)

    w_gu, w_down = ffn_w_gu.astype(BF16), ffn_w_down.astype(BF16)
    wqkv_all = {0: win_wqkv.astype(BF16), 2: ax_wqkv.astype(BF16)}
    wo_all = {0: win_wo.astype(BF16), 1: hy_wo.astype(BF16), 2: ax_wo.astype(BF16)}

    states = {0: None, 2: None}
    kv_w = N_KV_HEADS * HEAD_DIM
    for i in range(depth):
        mod = mod_all[i].reshape(MOD_ROWS * 6, 1, d)
        kind, j = i % 3, i // 3
        if kind == 1:
            assert x_lat is None
            u = _hy_inproj(x, norm_mix_g[i], mod, hy_w_in[j].astype(BF16), hy_conv_w[j], hy_conv_b[j],
                           n_ctx_rows=n_ctx_rows, lat_len=lat_len, ctx_len=ctx_len, tn=512)
            rows = lat_len
            zs = []
            for (base, nblk, sl) in ((0, n_ctx_rows // rows, ctx_len), (n_ctx_rows // rows, lat_batch, lat_len)):
                mats = _dft_mats(sl // 2, min(512, sl // 2))
                resp = _hy_filters(sl, hy_f_w1[j], hy_f_b1[j], hy_f_w2[j], hy_f_b2[j], hy_f_w3[j], hy_freq[j],
                                   mats[0], tc=512)
                conv = functools.partial(_hy_conv, mats=mats, h=resp, n_blocks=nblk, rows=rows, seq_len=sl, tc=512)
                z1 = conv(u, 0, base, u, 1, base, skip_row=hy_skip[j, 0].reshape(1, d), order=0)
                zs.append(conv(z1, 0, 0, u, 2, base, skip_row=hy_skip[j, 1].reshape(1, d), order=1))
            x, h2 = _proj_residual(zs[0], zs[1], wo_all[kind], j, x, None, mod, norm_ffn_g[i], **geom)
        else:
            if kind == 0:
                sink, hg = win_sink[j], None
                ck, cv = cache_win_k, cache_win_v
            else:
                sink = jnp.zeros((N_HEADS,), F32)
                hg = jnp.stack([ax_q_g[j], ax_k_g[j]], axis=0)
                ck, cv = cache_ax_k, cache_ax_v
            ck = ck[:, j:j + 1].reshape(lat_batch, 1, past, kv_w)
            cv = cv[:, j:j + 1].reshape(lat_batch, 1, past, kv_w)
            y, k_c, v_c = _qkv_project(x, x_lat, norm_mix_g[i], mod, wqkv_all[kind], j, cos, sin, hg, states[kind],
                                       ctx_len=ctx_len, **geom)
            states[kind] = (k_c, v_c)
            o_c = _attn_ctx(y, sink, n_ctx_seqs=n_ctx_seqs, ctx_len=ctx_len, use_sink=kind == 0)
            lat_geom = dict(n_ctx_rows=n_ctx_rows, lat_batch=lat_batch, lat_len=lat_len)
            if kind == 0:
                o_l = _attn_lat_win(y, sink, ck, cv, 0, **lat_geom)
            else:
                o_l = _attn_lat_full(y, ck, cv, 0, **lat_geom)
            x, h2 = _proj_residual(o_c, o_l, wo_all[kind], j, x, x_lat, mod, norm_ffn_g[i], **geom)
            x_lat = None
        x = _ffn(x, h2, mod, w_gu, w_down, i, tf=512, **geom)

    y_prompt = _final_norm(x, final_g, row0=0, rows=n_ctx_rows, tm=tm).reshape(n_ctx_seqs, ctx_len, d)
    y_sample = _final_norm(x, final_g, row0=n_ctx_rows, rows=n_lat_rows, tm=tm).reshape(lat_batch, lat_len, d)
    return (y_prompt, y_sample) + states[0] + states[2]
```

```python
import functools
import math

import jax
import jax.numpy as jnp
from jax import lax
from jax.experimental import pallas as pl
from jax.experimental.pallas import tpu as pltpu

F32 = jnp.float32
BF16 = jnp.bfloat16

N_HEADS = 16
N_KV_HEADS = 4
GROUP = N_HEADS // N_KV_HEADS
HEAD_DIM = 128
WINDOW = 128
BLOCK = 128
GRID_W = 64
ROPE_THETA = 10000.0
HY_ORDER = 2
HY_EMB = 33
HY_TARGET = 1e-2
HY_FAST_DECAY = 0.3
HY_SLOW_DECAY = 1.5
HY_MIN_DECAY = math.log(HY_TARGET) / HY_SLOW_DECAY
HY_MAX_DECAY = math.log(HY_TARGET) / HY_FAST_DECAY
EPS = 1e-6
NEG = -1e30
SCALE = HEAD_DIM ** -0.5
LOG2E = math.log2(math.e)
PAIR = 2
Q_STEP = 4 * BLOCK

LANES = 128
PAD = 8
MOD_ROWS = 16
VMEM_LIMIT = 56 * 1024 * 1024

NT_DIMS = (((1,), (1,)), ((), ()))

def _params(*sem):
    return pltpu.CompilerParams(dimension_semantics=sem, vmem_limit_bytes=VMEM_LIMIT)


def _silu(x):
    return x * (1.0 / (1.0 + jnp.exp(-x)))


def _mod_row(i, tm, n_ctx_rows, lat_len):
    return jnp.maximum(i * tm - n_ctx_rows + lat_len, 0) // lat_len


def _mod_kernel(c_ref, w_ref, b_ref, o_ref):
    s = _silu(c_ref[...]).astype(BF16)
    o_ref[...] = jnp.dot(s, w_ref[0].astype(BF16), preferred_element_type=F32) + b_ref[0]


def _adaln_all(cvec, mod_w, mod_b):
    depth, d, n = mod_w.shape
    tn = 1024
    per = d // tn
    return pl.pallas_call(
        _mod_kernel,
        grid=(depth, n // tn),
        in_specs=[pl.BlockSpec((MOD_ROWS, d), lambda l, j: (0, 0)),
                  pl.BlockSpec((1, d, tn), lambda l, j: (l, 0, j)),
                  pl.BlockSpec((1, 1, tn), lambda l, j: (l, 0, j))],
        out_specs=pl.BlockSpec((None, MOD_ROWS, None, None, tn), lambda l, j: (l, 0, j // per, 0, j % per)),
        out_shape=jax.ShapeDtypeStruct((depth, MOD_ROWS, 6, 1, d), F32),
        compiler_params=_params("parallel", "parallel"),
        name="adaln_mod",
    )(cvec, mod_w, mod_b.reshape(depth, 1, n))


def _norm_mod(x, g, shift, scale):
    y = x * lax.rsqrt(jnp.mean(x * x, axis=-1, keepdims=True) + EPS) * g
    return y * (1.0 + scale) + shift


def _rope_rotate(a, cos, sin_signed):
    lane = lax.broadcasted_iota(jnp.int32, a.shape, 1)
    partner = jnp.where((lane & 32) == 0, pltpu.roll(a, 96, 1), pltpu.roll(a, 32, 1))
    return a * cos + partner * sin_signed


def _two_source_specs(x_ctx, x_lat, tm, n_ctx_blocks):
    d = x_ctx.shape[1]
    lat_base = n_ctx_blocks if x_lat is None else 0
    specs = [pl.BlockSpec((tm, d), lambda i: (jnp.minimum(i, n_ctx_blocks - 1), 0)),
             pl.BlockSpec((tm, d), lambda i: (lat_base + jnp.maximum(i - n_ctx_blocks, 0), 0))]
    return specs, [x_ctx, x_ctx if x_lat is None else x_lat]


def _qkv_kernel(*refs, qk_norm, tn, halves, n_ctx_blocks, state_slot):
    xc_ref, xl_ref, g_ref, sh_ref, sc_ref, w_ref, cos_ref, sin_ref, hg_ref = refs[:9]
    o_ref, ks_ref, vs_ref = refs[-3:]
    tm = xc_ref.shape[0]
    rh = tm // halves
    heads_per_tile = tn // HEAD_DIM
    kcol, vcol = N_HEADS * HEAD_DIM, (N_HEADS + N_KV_HEADS) * HEAD_DIM

    def body(x_ref, latent):
        for hf in range(halves):
            rs = slice(hf * rh, (hf + 1) * rh)
            h = _norm_mod(x_ref[rs, :], g_ref[...], sh_ref[0], sc_ref[0]).astype(BF16)
            for jt in range(w_ref.shape[1] // tn):
                acc = jnp.dot(h, w_ref[:, jt * tn:(jt + 1) * tn], preferred_element_type=F32)
                for hh in range(heads_per_tile):
                    head = jt * heads_per_tile + hh
                    a = acc[:, hh * HEAD_DIM:(hh + 1) * HEAD_DIM]
                    if head < N_HEADS + N_KV_HEADS:
                        if qk_norm:
                            gain = hg_ref[0:1, :] if head < N_HEADS else hg_ref[1:2, :]
                            a = a * lax.rsqrt(jnp.mean(a * a, axis=-1, keepdims=True) + EPS) * gain
                        if latent:
                            a = _rope_rotate(a, cos_ref[rs, :], sin_ref[rs, :])
                    o_ref[rs, head * HEAD_DIM:(head + 1) * HEAD_DIM] = a
        if not latent:
            seqs, ctx_len = ks_ref.shape[0], ks_ref.shape[-3]
            for ref, col in ((ks_ref, kcol), (vs_ref, vcol)):
                if state_slot is not None:
                    for other in range(ref.shape[1]):
                        if other != state_slot:
                            ref[:, other] = jnp.zeros(ref.shape[:1] + ref.shape[2:], F32)
                for s in range(seqs):
                    for kvh in range(N_KV_HEADS):
                        val = o_ref[s * ctx_len:(s + 1) * ctx_len, col + kvh * HEAD_DIM:col + (kvh + 1) * HEAD_DIM]
                        if state_slot is None:
                            ref[s, :, kvh, :] = val
                        else:
                            ref[s, state_slot, :, kvh, :] = val

    i = pl.program_id(0)
    pl.when(i < n_ctx_blocks)(lambda: body(xc_ref, False))
    pl.when(i >= n_ctx_blocks)(lambda: body(xl_ref, True))


def _qkv_project(x_ctx, x_lat, g, mod, w, layer, cos, sin, hg, state, *, n_ctx_rows, lat_len, ctx_len, tm):
    d = x_ctx.shape[1]
    seqs = tm // ctx_len
    n_layers = w.shape[0]
    state_shape = (n_ctx_rows // ctx_len, n_layers, ctx_len, N_KV_HEADS, HEAD_DIM)
    last_ctx = n_ctx_rows // tm - 1
    if state is None:
        state_spec = pl.BlockSpec((seqs, n_layers, ctx_len, N_KV_HEADS, HEAD_DIM),
                                  lambda i: (jnp.minimum(i, last_ctx), 0, 0, 0, 0))
        state_specs, state_args, aliases = [], [], {}
    else:
        state_spec = pl.BlockSpec((seqs, None, ctx_len, N_KV_HEADS, HEAD_DIM),
                                  lambda i: (jnp.minimum(i, last_ctx), layer, 0, 0, 0))
        state_specs, state_args = [pl.BlockSpec(memory_space=pl.ANY)] * 2, list(state)
        aliases = {9: 1, 10: 2}
    n = w.shape[2]
    ncb = n_ctx_rows // tm
    t = n_ctx_rows + (x_ctx.shape[0] - n_ctx_rows if x_lat is None else x_lat.shape[0])
    row = functools.partial(_mod_row, tm=tm, n_ctx_rows=n_ctx_rows, lat_len=lat_len)
    tab = lambda i: ((jnp.maximum(i - ncb, 0)) % (lat_len // tm), 0)
    qk_norm = hg is not None
    if hg is None:
        hg = jnp.ones((2, HEAD_DIM), F32)
    x_specs, x_args = _two_source_specs(x_ctx, x_lat, tm, ncb)
    return pl.pallas_call(
        functools.partial(_qkv_kernel, qk_norm=qk_norm, tn=512, halves=2, n_ctx_blocks=ncb,
                          state_slot=layer if state is None else None),
        grid=(t // tm,),
        in_specs=x_specs + [
            pl.BlockSpec((1, d), lambda i: (0, 0)),
            pl.BlockSpec((1, 1, d), lambda i: (row(i) * 6, 0, 0)),
            pl.BlockSpec((1, 1, d), lambda i: (row(i) * 6 + 1, 0, 0)),
            pl.BlockSpec((None, d, n), lambda i: (layer, 0, 0), pipeline_mode=pl.Buffered(1)),
            pl.BlockSpec((tm, HEAD_DIM), tab), pl.BlockSpec((tm, HEAD_DIM), tab),
            pl.BlockSpec((2, HEAD_DIM), lambda i: (0, 0))] + state_specs,
        out_specs=[pl.BlockSpec((tm, n), lambda i: (i, 0)), state_spec, state_spec],
        out_shape=[jax.ShapeDtypeStruct((t, n), F32), jax.ShapeDtypeStruct(state_shape, F32),
                   jax.ShapeDtypeStruct(state_shape, F32)],
        input_output_aliases=aliases,
        compiler_params=_params("arbitrary"),
        name="qkv_project",
    )(*x_args, g.reshape(1, d), mod, mod, w, cos, sin, hg, *state_args)


def _proj_res_kernel(ac_ref, al_ref, xc_ref, xl_ref, w_ref, gate_ref, g2_ref, sh2_ref, sc2_ref, o_ref, h_ref, *,
                     n_ctx_blocks):
    i = pl.program_id(0)

    def emit(a_ref, x_ref):
        xn = x_ref[...] + gate_ref[0] * jnp.dot(a_ref[...], w_ref[...], preferred_element_type=F32)
        o_ref[...] = xn
        h_ref[...] = _norm_mod(xn, g2_ref[...], sh2_ref[0], sc2_ref[0]).astype(BF16)

    pl.when(i < n_ctx_blocks)(lambda: emit(ac_ref, xc_ref))
    pl.when(i >= n_ctx_blocks)(lambda: emit(al_ref, xl_ref))


def _proj_residual(a_ctx, a_lat, w, layer, x_ctx, x_lat, mod, g2, *, n_ctx_rows, lat_len, tm):
    d = x_ctx.shape[1]
    k = w.shape[1]
    ncb = n_ctx_rows // tm
    t = a_ctx.shape[0] + a_lat.shape[0]
    row = functools.partial(_mod_row, tm=tm, n_ctx_rows=n_ctx_rows, lat_len=lat_len)
    modspec = lambda which: pl.BlockSpec((1, 1, d), lambda i: (row(i) * 6 + which, 0, 0))
    x_specs, x_args = _two_source_specs(x_ctx, x_lat, tm, ncb)
    return pl.pallas_call(
        functools.partial(_proj_res_kernel, n_ctx_blocks=ncb),
        grid=(t // tm,),
        in_specs=[pl.BlockSpec((tm, k), lambda i: (jnp.minimum(i, ncb - 1), 0)),
                  pl.BlockSpec((tm, k), lambda i: (jnp.maximum(i - ncb, 0), 0))]
        + x_specs
        + [pl.BlockSpec((None, k, d), lambda i: (layer, 0, 0), pipeline_mode=pl.Buffered(1)),
           modspec(2),
           pl.BlockSpec((1, d), lambda i: (0, 0)),
           modspec(3), modspec(4)],
        out_specs=[pl.BlockSpec((tm, d), lambda i: (i, 0)), pl.BlockSpec((tm, d), lambda i: (i, 0))],
        out_shape=[jax.ShapeDtypeStruct((t, d), F32), jax.ShapeDtypeStruct((t, d), BF16)],
        compiler_params=_params("parallel"),
        name="proj_residual",
    )(a_ctx, a_lat, *x_args, w, mod, g2.reshape(1, d), mod, mod)


def _ffn_kernel(x_ref, h_ref, gate_ref, wg_ref, wu_ref, wd_ref, o_ref):
    f = pl.program_id(1)

    @pl.when(f == 0)
    def _():
        o_ref[...] = jnp.zeros_like(o_ref)

    h = h_ref[...]
    gv = jnp.dot(h, wg_ref[...], preferred_element_type=F32)
    uv = jnp.dot(h, wu_ref[...], preferred_element_type=F32)
    act = (_silu(gv) * uv).astype(BF16)
    o_ref[...] += jnp.dot(act, wd_ref[...], preferred_element_type=F32)

    @pl.when(f == pl.num_programs(1) - 1)
    def _():
        o_ref[...] = x_ref[...] + gate_ref[0] * o_ref[...]


def _ffn(x, h, mod, w_gu, w_down, layer, *, n_ctx_rows, lat_len, tm, tf):
    t, d = x.shape
    ff = w_down.shape[1]
    nf = ff // tf
    row = functools.partial(_mod_row, tm=tm, n_ctx_rows=n_ctx_rows, lat_len=lat_len)
    return pl.pallas_call(
        _ffn_kernel,
        grid=(t // tm, nf),
        in_specs=[pl.BlockSpec((tm, d), lambda i, f: (i, 0)),
                  pl.BlockSpec((tm, d), lambda i, f: (i, 0)),
                  pl.BlockSpec((1, 1, d), lambda i, f: (row(i) * 6 + 5, 0, 0)),
                  pl.BlockSpec((None, d, tf), lambda i, f: (layer, 0, f)),
                  pl.BlockSpec((None, d, tf), lambda i, f: (layer, 0, nf + f)),
                  pl.BlockSpec((None, tf, d), lambda i, f: (layer, f, 0))],
        out_specs=pl.BlockSpec((tm, d), lambda i, f: (i, 0)),
        out_shape=jax.ShapeDtypeStruct((t, d), F32),
        compiler_params=_params("parallel", "arbitrary"),
        name="ffn_swiglu",
    )(x, h, mod, w_gu, w_gu, w_down)


def _stack_heads(q):
    return jnp.concatenate([q[:, h * HEAD_DIM:(h + 1) * HEAD_DIM] for h in range(GROUP)], axis=0)


def _unstack_heads(o, rows):
    return jnp.concatenate([o[h * rows:(h + 1) * rows] for h in range(GROUP)], axis=1)


def _sink_column(sink_ref, kh, rows):
    head = lax.broadcasted_iota(jnp.int32, (GROUP * rows, 1), 0) // rows
    col = jnp.zeros((GROUP * rows, 1), F32)
    for h in range(GROUP):
        col = jnp.where(head == h, sink_ref[kh * GROUP + h], col)
    return col


def _softmax_av(s, v, sink_col):
    m = jnp.max(s, axis=-1, keepdims=True)
    if sink_col is not None:
        m = jnp.maximum(m, sink_col)
    p = jnp.exp(s - m)
    l = jnp.sum(p, axis=-1, keepdims=True)
    if sink_col is not None:
        l = l + jnp.exp(sink_col - m)
    return jnp.dot(p.astype(BF16), v, preferred_element_type=F32) / l


def _attn_ctx_kernel(sink_ref, q_ref, k_ref, v_ref, o_ref, *, use_sink):
    rows = q_ref.shape[0]
    qw = GROUP * HEAD_DIM
    for kh in range(N_KV_HEADS):
        qs = _stack_heads(q_ref[:, kh * qw:(kh + 1) * qw] * SCALE).astype(BF16)
        k = k_ref[:, kh * HEAD_DIM:(kh + 1) * HEAD_DIM].astype(BF16)
        v = v_ref[:, kh * HEAD_DIM:(kh + 1) * HEAD_DIM].astype(BF16)
        s = lax.dot_general(qs, k, NT_DIMS, preferred_element_type=F32)
        sink_col = _sink_column(sink_ref, kh, rows) if use_sink else None
        o_ref[:, kh * qw:(kh + 1) * qw] = _unstack_heads(_softmax_av(s, v, sink_col), rows).astype(BF16)


def _attn_ctx(y, sink, *, n_ctx_seqs, ctx_len, use_sink):
    t = n_ctx_seqs * ctx_len
    q_w, kv_w = N_HEADS * HEAD_DIM, N_KV_HEADS * HEAD_DIM
    return pl.pallas_call(
        functools.partial(_attn_ctx_kernel, use_sink=use_sink),
        grid=(n_ctx_seqs,),
        in_specs=[pl.BlockSpec(memory_space=pltpu.SMEM),
                  pl.BlockSpec((ctx_len, q_w), lambda b: (b, 0)),
                  pl.BlockSpec((ctx_len, kv_w), lambda b: (b, q_w // kv_w)),
                  pl.BlockSpec((ctx_len, kv_w), lambda b: (b, q_w // kv_w + 1))],
        out_specs=pl.BlockSpec((ctx_len, q_w), lambda b: (b, 0)),
        out_shape=jax.ShapeDtypeStruct((t, q_w), BF16),
        compiler_params=_params("parallel"),
        name="attn_context",
    )(sink, y, y, y)


def _keys_by_queries_attention(q_ref, k_scr, vt_scr, s_scr, chunks, bias_ref, sink_ref, o_ref):
    qn = q_ref.shape[0]
    kh = pl.program_id(1)

    def logits(h, ci):
        st, sz = chunks[ci]
        q = (q_ref[:, h * HEAD_DIM:(h + 1) * HEAD_DIM] * (SCALE * LOG2E)).astype(BF16)
        st_ = lax.dot_general(k_scr[st:st + sz, :], q, NT_DIMS, preferred_element_type=F32)
        if bias_ref is not None and ci == 0:
            st_ = st_ + bias_ref[...]
        s_scr[st:st + sz, h * qn:(h + 1) * qn] = st_
        return jnp.max(st_, axis=0, keepdims=True)

    def weigh(h, ci, m):
        st, sz = chunks[ci]
        p = jnp.exp2(s_scr[st:st + sz, h * qn:(h + 1) * qn] - m)
        pv = jnp.dot(vt_scr[:, st:st + sz], p.astype(BF16), preferred_element_type=F32)
        return jnp.sum(p, axis=0, keepdims=True), pv

    n = len(chunks)
    maxes = [logits(0, ci) for ci in range(n)]
    for h in range(GROUP):
        m = functools.reduce(jnp.maximum, maxes)
        if sink_ref is not None:
            sink2 = sink_ref[kh * GROUP + h] * LOG2E
            m = jnp.maximum(m, sink2)
        maxes, l, acc = [], None, None
        for ci in range(n):
            if h + 1 < GROUP:
                maxes.append(logits(h + 1, ci))
            lc, pv = weigh(h, ci, m)
            l, acc = (lc, pv) if l is None else (l + lc, acc + pv)
        if sink_ref is not None:
            l = l + jnp.exp2(sink2 - m)
        o_ref[:, h * HEAD_DIM:(h + 1) * HEAD_DIM] = (acc * (1.0 / l)).T.astype(BF16)


def _attn_lat_win_kernel(sink_ref, q_ref, kp_ref, kc_ref, kn_ref, vp_ref, vc_ref, vn_ref, ck_ref, cv_ref, bias_ref,
                         o_ref, k_scr, vt_scr, s_scr):
    past = ck_ref.shape[0]
    win = Q_STEP + 2 * BLOCK

    @pl.when(pl.program_id(2) == 0)
    def _():
        k_scr[win:, :] = ck_ref[...].astype(BF16)
        vt_scr[:, win:] = cv_ref[...].T.astype(BF16)

    row = 0
    for k_ref, v_ref in ((kp_ref, vp_ref), (kc_ref, vc_ref), (kn_ref, vn_ref)):
        n = k_ref.shape[0]
        k_scr[row:row + n, :] = k_ref[...].astype(BF16)
        vt_scr[:, row:row + n] = v_ref[...].T.astype(BF16)
        row += n

    kh, step, n_steps = pl.program_id(1), pl.program_id(2), pl.num_programs(2)
    span, lanes, nq = 3 * BLOCK, PAIR * BLOCK, Q_STEP // BLOCK
    groups = [(qb, p) for qb in range(nq) for p in range(GROUP // PAIR)]

    def logits(g):
        qb, p = groups[g]
        rows = slice(qb * BLOCK, (qb + 1) * BLOCK)
        q = jnp.concatenate([q_ref[rows, h * HEAD_DIM:(h + 1) * HEAD_DIM] for h in range(PAIR * p, PAIR * (p + 1))],
                            axis=0)
        q = (q * (SCALE * LOG2E)).astype(BF16)
        edge = 1
        if qb == 0:
            edge = jnp.where(step == 0, 0, 1)
        elif qb == nq - 1:
            edge = jnp.where(step == n_steps - 1, 2, 1)
        sw = lax.dot_general(k_scr[qb * BLOCK:qb * BLOCK + span, :], q, NT_DIMS,
                             preferred_element_type=F32) + bias_ref[edge]
        sc = lax.dot_general(k_scr[win:, :], q, NT_DIMS, preferred_element_type=F32)
        s_scr[0:span, g * lanes:(g + 1) * lanes] = sw
        s_scr[span:, g * lanes:(g + 1) * lanes] = sc
        return jnp.maximum(jnp.max(sw, axis=0, keepdims=True), jnp.max(sc, axis=0, keepdims=True))

    head = lax.broadcasted_iota(jnp.int32, (1, lanes), 1) // BLOCK
    m_next = logits(0)
    for g, (qb, p) in enumerate(groups):
        sink2 = jnp.zeros((1, lanes), F32)
        for hh in range(PAIR):
            sink2 = jnp.where(head == hh, sink_ref[kh * GROUP + PAIR * p + hh] * LOG2E, sink2)
        m = jnp.maximum(m_next, sink2)
        if g + 1 < len(groups):
            m_next = logits(g + 1)
        pw = jnp.exp2(s_scr[0:span, g * lanes:(g + 1) * lanes] - m)
        pc = jnp.exp2(s_scr[span:, g * lanes:(g + 1) * lanes] - m)
        pv = (jnp.dot(vt_scr[:, qb * BLOCK:qb * BLOCK + span], pw.astype(BF16), preferred_element_type=F32)
              + jnp.dot(vt_scr[:, win:], pc.astype(BF16), preferred_element_type=F32))
        l = (jnp.sum(pw, axis=0, keepdims=True) + jnp.sum(pc, axis=0, keepdims=True) + jnp.exp2(sink2 - m))
        ot = pv * (1.0 / l)
        for hh in range(PAIR):
            col = (PAIR * p + hh) * HEAD_DIM
            o_ref[qb * BLOCK:(qb + 1) * BLOCK, col:col + HEAD_DIM] = ot[:, hh * BLOCK:(hh + 1) * BLOCK].T.astype(BF16)


def _window_bias():
    c = jnp.arange(3 * BLOCK, dtype=jnp.int32)[:, None]
    r = jnp.arange(PAIR * BLOCK, dtype=jnp.int32)[None, :] % BLOCK
    band = jnp.abs(r + BLOCK - c) <= WINDOW
    variants = [band & (c >= BLOCK), band, band & (c < 2 * BLOCK)]
    return jnp.where(jnp.stack(variants), 0.0, NEG).astype(F32)


def _cache_spec(cache, layer):
    return pl.BlockSpec((None, None, cache.shape[2], HEAD_DIM), lambda b, kh, qs: (b, layer, 0, kh))


def _attn_lat_win(y, sink, cache_k, cache_v, layer, *, n_ctx_rows, lat_batch, lat_len):
    qw = GROUP * HEAD_DIM
    nb = lat_len // BLOCK
    ns = lat_len // Q_STEP
    per = Q_STEP // BLOCK
    base = n_ctx_rows // BLOCK
    past = cache_k.shape[2]
    n_keys = Q_STEP + 2 * BLOCK + past

    def edge(col0, blk):
        return pl.BlockSpec((BLOCK, HEAD_DIM),
                            lambda b, kh, qs: (base + b * nb + jnp.clip(qs * per + blk, 0, nb - 1), col0 + kh))

    def centre(col0):
        return pl.BlockSpec((Q_STEP, HEAD_DIM), lambda b, kh, qs: (base // per + b * ns + qs, col0 + kh))

    kcol, vcol = N_HEADS, N_HEADS + N_KV_HEADS
    cache_spec = _cache_spec(cache_k, layer)
    return pl.pallas_call(
        _attn_lat_win_kernel,
        grid=(lat_batch, N_KV_HEADS, ns),
        in_specs=[pl.BlockSpec(memory_space=pltpu.SMEM),
                  pl.BlockSpec((Q_STEP, qw), lambda b, kh, qs: (base // per + b * ns + qs, kh)),
                  edge(kcol, -1), centre(kcol), edge(kcol, per),
                  edge(vcol, -1), centre(vcol), edge(vcol, per),
                  cache_spec, cache_spec,
                  pl.BlockSpec((3, 3 * BLOCK, PAIR * BLOCK), lambda b, kh, qs: (0, 0, 0))],
        out_specs=pl.BlockSpec((Q_STEP, qw), lambda b, kh, qs: (b * ns + qs, kh)),
        out_shape=jax.ShapeDtypeStruct((lat_batch * lat_len, N_HEADS * HEAD_DIM), BF16),
        scratch_shapes=[pltpu.VMEM((n_keys, HEAD_DIM), BF16), pltpu.VMEM((HEAD_DIM, n_keys), BF16),
                        pltpu.VMEM((3 * BLOCK + past, GROUP * Q_STEP), F32)],
        compiler_params=_params("parallel", "parallel", "arbitrary"),
        name="attn_latent_window",
    )(sink, y, y, y, y, y, y, y, cache_k, cache_v, _window_bias())


def _attn_lat_full_kernel(q_ref, k_ref, v_ref, ck_ref, cv_ref, o_ref, k_scr, vt_scr, s_scr, *, chunk):
    lat_len = k_ref.shape[0]
    n_keys = k_scr.shape[0]

    @pl.when(pl.program_id(2) == 0)
    def _():
        k_scr[0:lat_len, :] = k_ref[...].astype(BF16)
        k_scr[lat_len:, :] = ck_ref[...].astype(BF16)
        vt_scr[:, 0:lat_len] = v_ref[...].T.astype(BF16)
        vt_scr[:, lat_len:] = cv_ref[...].T.astype(BF16)

    chunks = tuple((st, chunk) for st in range(0, n_keys, chunk))
    _keys_by_queries_attention(q_ref, k_scr, vt_scr, s_scr, chunks, None, None, o_ref)


def _attn_lat_full(y, cache_k, cache_v, layer, *, n_ctx_rows, lat_batch, lat_len):
    qw = GROUP * HEAD_DIM
    ns = lat_len // Q_STEP
    base = n_ctx_rows // Q_STEP
    seq_base = n_ctx_rows // lat_len
    past = cache_k.shape[2]
    cache_spec = _cache_spec(cache_k, layer)
    n_keys = lat_len + past
    chunk = 512
    assert n_keys % chunk == 0
    return pl.pallas_call(
        functools.partial(_attn_lat_full_kernel, chunk=chunk),
        grid=(lat_batch, N_KV_HEADS, ns),
        in_specs=[pl.BlockSpec((Q_STEP, qw), lambda b, kh, qs: (base + b * ns + qs, kh)),
                  pl.BlockSpec((lat_len, HEAD_DIM), lambda b, kh, qs: (seq_base + b, N_HEADS + kh)),
                  pl.BlockSpec((lat_len, HEAD_DIM), lambda b, kh, qs: (seq_base + b, N_HEADS + N_KV_HEADS + kh)),
                  cache_spec, cache_spec],
        out_specs=pl.BlockSpec((Q_STEP, qw), lambda b, kh, qs: (b * ns + qs, kh)),
        out_shape=jax.ShapeDtypeStruct((lat_batch * lat_len, N_HEADS * HEAD_DIM), BF16),
        scratch_shapes=[pltpu.VMEM((n_keys, HEAD_DIM), BF16), pltpu.VMEM((HEAD_DIM, n_keys), BF16),
                        pltpu.VMEM((n_keys, GROUP * Q_STEP), F32)],
        compiler_params=_params("parallel", "parallel", "arbitrary"),
        name="attn_latent_full",
    )(y, y, y, cache_k, cache_v)


def _rope_tables(lat_len):
    half = HEAD_DIM // 2
    n_rows = lat_len // GRID_W
    row = jnp.repeat(jnp.arange(n_rows, dtype=F32), GRID_W)
    col = jnp.tile(jnp.arange(GRID_W, dtype=F32), n_rows)
    inv = ROPE_THETA ** (-jnp.arange(0, half, 2, dtype=F32) / half)
    ang_r, ang_c = row[:, None] * inv, col[:, None] * inv
    cos = jnp.concatenate([jnp.cos(ang_r)] * 2 + [jnp.cos(ang_c)] * 2, axis=-1)
    sin = jnp.concatenate([-jnp.sin(ang_r), jnp.sin(ang_r), -jnp.sin(ang_c), jnp.sin(ang_c)], axis=-1)
    return cos, sin


def _dft_mats(seq_len, tk):
    n = 2 * seq_len
    k = jnp.arange(seq_len, dtype=jnp.int32)
    step = 64
    a = jnp.arange(seq_len // step, dtype=jnp.int32) * step
    b = jnp.arange(step, dtype=jnp.int32)
    ang_a = ((k[:, None] * a[None, :]) % n).astype(F32) * (2.0 * math.pi / n)
    ang_b = ((k[:, None] * b[None, :]) % n).astype(F32) * (2.0 * math.pi / n)
    ca, sa = jnp.cos(ang_a)[:, :, None], jnp.sin(ang_a)[:, :, None]
    cb, sb = jnp.cos(ang_b)[:, None, :], jnp.sin(ang_b)[:, None, :]
    fc = (ca * cb - sa * sb).reshape(seq_len, seq_len)
    fs = -(sa * cb + ca * sb).reshape(seq_len, seq_len)
    sign = jnp.where(k % 2 == 0, 1.0, -1.0).astype(F32)
    fs = jnp.where(k[:, None] == 0, sign[None, :], fs)
    kfc = seq_len // tk
    ana = jnp.concatenate([fc.reshape(kfc, tk, seq_len), fs.reshape(kfc, tk, seq_len)], axis=1).astype(BF16)
    ana = lax.optimization_barrier(ana)
    return ana, ana.transpose(0, 2, 1)


def _hy_filter_kernel(feat_ref, tn_ref, w1_ref, b1_ref, w2_ref, b2_ref, fq_ref, w3f_ref, w3b_ref, dl_ref, ana_ref,
                      h_ref, pq_scr, inv_scr, nyq_scr):
    kf = pl.program_id(1)
    hp = lax.Precision.HIGHEST
    tk = h_ref.shape[1]

    @pl.when(kf == 0)
    def _():
        def taps(g):
            a = jnp.dot(feat_ref[g], w1_ref[...], precision=hp, preferred_element_type=F32) + b1_ref[...]
            a = jnp.sin(fq_ref[0:1, :] * a)
            a = jnp.dot(a, w2_ref[...], precision=hp, preferred_element_type=F32) + b2_ref[...]
            a = jnp.sin(fq_ref[1:2, :] * a)
            decay = jnp.exp(-tn_ref[g] * dl_ref[...])
            a = a.astype(BF16)
            return (jnp.dot(a, w3f_ref[...].astype(BF16), preferred_element_type=F32) * decay,
                    jnp.dot(a, w3b_ref[...].astype(BF16), preferred_element_type=F32) * decay)

        fwd0, bwd0 = taps(0)
        fwd1, bwd1 = taps(1)
        fwdm, bwdm = taps(2)
        rowi = lax.broadcasted_iota(jnp.int32, fwd0.shape, 0)
        first = rowi == 0
        bwd0 = jnp.where(first, 0.0, bwd0)
        absum = lambda v: jnp.sum(jnp.abs(v), axis=0, keepdims=True)
        inv_scr[...] = 1.0 / (absum(fwd0) + absum(fwd1) + absum(bwd0) + absum(bwd1) + EPS)
        parts = ((fwd0, bwd0),
                 (bwdm, jnp.where(first, 0.0, bwd1)),
                 (fwd1, jnp.where(first, 0.0, fwdm)))
        for n, (pos, neg) in enumerate(parts):
            p = pos + neg
            nyq_scr[n:n + 1, :] = jnp.sum(jnp.where((rowi & 1) == 0, p, -p), axis=0, keepdims=True)
            pq_scr[2 * n] = p.astype(BF16)
            pq_scr[2 * n + 1] = (pos - neg).astype(BF16)

    inv = inv_scr[...]
    for n in range(3):
        h_ref[2 * n] = jnp.dot(ana_ref[0:tk, :], pq_scr[2 * n], preferred_element_type=F32) * inv
        hi = jnp.dot(ana_ref[tk:, :], pq_scr[2 * n + 1], preferred_element_type=F32)
        rowk = lax.broadcasted_iota(jnp.int32, hi.shape, 0)
        hi = jnp.where((rowk == 0) & (kf == 0), nyq_scr[n:n + 1, :], hi)
        h_ref[2 * n + 1] = hi * inv


def _hy_filters(seq_len, f_w1, f_b1, f_w2, f_b2, f_w3, freq, ana_mats, *, tc):
    half = seq_len // 2
    tk = ana_mats.shape[1] // 2
    hid = f_w1.shape[1]
    nd = f_w3.shape[1] // 2
    d = nd // HY_ORDER
    t = jnp.arange(seq_len, dtype=F32)
    tnorm = t / max(seq_len - 1, 1)
    bands = (HY_EMB - 1) // 2
    fb = jnp.linspace(1e-4, bands - 1, bands, dtype=F32)
    w = 2.0 * math.pi * t / seq_len
    feats = jnp.concatenate([tnorm[:, None], jnp.cos(w[:, None] * fb), -jnp.sin(w[:, None] * fb)], axis=-1)
    emb = 64
    feats = jnp.pad(feats, ((0, 0), (0, emb - HY_EMB)))
    w1 = jnp.pad(f_w1, ((0, emb - HY_EMB), (0, 0)))
    mirror = half - jnp.arange(half)
    feats3 = jnp.stack([feats[:half], feats[half:], feats[mirror]])
    tn3 = jnp.stack([tnorm[:half], tnorm[half:], tnorm[mirror]]).reshape(3, half, 1)
    deltas = jnp.abs(jnp.linspace(HY_MIN_DECAY, HY_MAX_DECAY, d, dtype=F32))
    deltas = jnp.tile(deltas, HY_ORDER).reshape(1, nd)
    nct = nd // tc
    const = lambda shape: pl.BlockSpec(shape, lambda c, kf: (0,) * len(shape))
    return pl.pallas_call(
        _hy_filter_kernel,
        grid=(nct, half // tk),
        in_specs=[const((3, half, emb)), const((3, half, 1)),
                  const((emb, hid)), const((1, hid)), const((hid, hid)), const((1, hid)), const((2, hid)),
                  pl.BlockSpec((hid, tc), lambda c, kf: (0, c)),
                  pl.BlockSpec((hid, tc), lambda c, kf: (0, nct + c)),
                  pl.BlockSpec((1, tc), lambda c, kf: (0, c)),
                  pl.BlockSpec((None, 2 * tk, half), lambda c, kf: (kf, 0, 0))],
        out_specs=pl.BlockSpec((6, tk, tc), lambda c, kf: (0, kf, c)),
        out_shape=jax.ShapeDtypeStruct((6, half, nd), F32),
        scratch_shapes=[pltpu.VMEM((6, half, tc), BF16), pltpu.VMEM((1, tc), F32), pltpu.VMEM((8, tc), F32)],
        compiler_params=_params("parallel", "arbitrary"),
        name="hyena_filter",
    )(feats3, tn3, w1, f_b1.reshape(1, hid), f_w2, f_b2.reshape(1, hid), freq, f_w3, f_w3, deltas, ana_mats)


def _hy_inproj_kernel(x_ref, g_ref, sh_ref, sc_ref, w_ref, cw_ref, cb_ref, o_ref, h_scr, y_scr, *, n_ctx_blocks,
                      ctx_len):
    j = pl.program_id(1)
    n = x_ref.shape[0]

    @pl.when(j == 0)
    def _():
        for r0 in range(0, n, 256):
            rs = slice(r0, r0 + 256)
            h_scr[rs, :] = _norm_mod(x_ref[rs, :], g_ref[...], sh_ref[0], sc_ref[0]).astype(BF16)
        y_scr[0:PAD, :] = jnp.zeros((PAD, y_scr.shape[1]), F32)
        y_scr[PAD + n:, :] = jnp.zeros((PAD, y_scr.shape[1]), F32)

    is_ctx = pl.program_id(0) < n_ctx_blocks
    chunk = 256

    def product(r0):
        y_scr[PAD + r0:PAD + r0 + chunk, :] = jnp.dot(h_scr[r0:r0 + chunk, :], w_ref[...],
                                                      preferred_element_type=F32)

    product(0)
    for r0 in range(0, n, chunk):
        if r0 + chunk < n:
            product(r0 + chunk)
        pos = (r0 + lax.broadcasted_iota(jnp.int32, (chunk, y_scr.shape[1]), 0)) % ctx_len
        prev = jnp.where(is_ctx & (pos == 0), 0.0, y_scr[PAD - 1 + r0:PAD - 1 + r0 + chunk, :])
        cur = y_scr[PAD + r0:PAD + r0 + chunk, :]
        nxt = jnp.where(is_ctx & (pos == ctx_len - 1), 0.0, y_scr[PAD + 1 + r0:PAD + 1 + r0 + chunk, :])
        o_ref[r0:r0 + chunk, :] = (prev * cw_ref[0:1, :] + cur * cw_ref[1:2, :] + nxt * cw_ref[2:3, :]
                                   + cb_ref[...])


def _hy_inproj(x, g, mod, w, conv_w, conv_b, *, n_ctx_rows, lat_len, ctx_len, tn):
    t, d = x.shape
    n = w.shape[1]
    tm = lat_len
    row = functools.partial(_mod_row, tm=tm, n_ctx_rows=n_ctx_rows, lat_len=lat_len)
    return pl.pallas_call(
        functools.partial(_hy_inproj_kernel, n_ctx_blocks=n_ctx_rows // tm, ctx_len=ctx_len),
        grid=(t // tm, n // tn),
        in_specs=[pl.BlockSpec((tm, d), lambda i, j: (i, 0), pipeline_mode=pl.Buffered(1)),
                  pl.BlockSpec((1, d), lambda i, j: (0, 0)),
                  pl.BlockSpec((1, 1, d), lambda i, j: (row(i) * 6, 0, 0)),
                  pl.BlockSpec((1, 1, d), lambda i, j: (row(i) * 6 + 1, 0, 0)),
                  pl.BlockSpec((d, tn), lambda i, j: (0, j)),
                  pl.BlockSpec((3, tn), lambda i, j: (0, j)),
                  pl.BlockSpec((1, tn), lambda i, j: (0, j))],
        out_specs=pl.BlockSpec((tm, tn), lambda i, j: (i, j)),
        out_shape=jax.ShapeDtypeStruct((t, n), F32),
        scratch_shapes=[pltpu.VMEM((tm, d), BF16), pltpu.VMEM((tm + 2 * PAD, tn), F32)],
        compiler_params=_params("parallel", "arbitrary"),
        name="hyena_inproj",
    )(x, g.reshape(1, d), mod, mod, w, conv_w, conv_b.reshape(1, n))


def _hy_conv_kernel(a_ref, g_ref, skip_ref, ana_ref, syn_ref, h_ref, o_ref, acc_scr, u_scr=None, *, seq_len):
    kf = pl.program_id(2)
    rows = a_ref.shape[0]
    tk = h_ref.shape[1]
    half = seq_len // 2
    chunk = 256

    @pl.when(kf == 0)
    def _():
        acc_scr[...] = jnp.zeros_like(acc_scr)
        if u_scr is not None:
            for r0 in range(0, rows, chunk):
                u_scr[r0:r0 + chunk, :] = a_ref[r0:r0 + chunk, :].astype(BF16)

    u_ref = a_ref if u_scr is None else u_scr

    first = (lax.broadcasted_iota(jnp.int32, (tk, a_ref.shape[1]), 0) == 0) & (kf == 0)

    def respond(uf, n):
        ur, ui = uf[0:tk, :], uf[tk:, :]
        hr, hi = h_ref[2 * n], h_ref[2 * n + 1]
        yr = jnp.where(first, 0.5 * (ur * hr), ur * hr - ui * hi)
        yi = jnp.where(first, 0.5 * (ui * hi), ur * hi + ui * hr)
        return yr, yi

    step = min(half, 512)
    for seg in range(0, rows, seq_len):
        uf1 = jnp.dot(ana_ref[...], u_ref[seg:seg + half, :], preferred_element_type=F32)
        uf2 = jnp.dot(ana_ref[...], u_ref[seg + half:seg + seq_len, :], preferred_element_type=F32)
        for dst, (n1, n2) in ((seg, (0, 1)), (seg + half, (2, 0))):
            y1r, y1i = respond(uf1, n1)
            y2r, y2i = respond(uf2, n2)
            yf = jnp.concatenate([y1r + y2r, y1i + y2i], axis=0).astype(BF16)
            for n0 in range(0, half, step):
                acc_scr[dst + n0:dst + n0 + step, :] += jnp.dot(syn_ref[n0:n0 + step, :], yf,
                                                                preferred_element_type=F32)

    @pl.when(kf == pl.num_programs(2) - 1)
    def _():
        for r0 in range(0, rows, chunk):
            rs = slice(r0, r0 + chunk)
            a = a_ref[rs, :].astype(F32)
            o_ref[rs, :] = (g_ref[rs, :] * (acc_scr[rs, :] * (1.0 / half) + a * skip_ref[...])).astype(BF16)


def _hy_conv(a, a_part, a_base, g, g_part, g_base, skip_row, mats, h, order, *, n_blocks, rows, seq_len, tc):
    d = skip_row.shape[1]
    nct = d // tc
    ana_mats, syn_mats = mats
    kfc = ana_mats.shape[0]
    tk = ana_mats.shape[1] // 2

    def amap(part, base):
        return lambda b, c, kf: (base + b, part * nct + c)

    half = seq_len // 2
    return pl.pallas_call(
        functools.partial(_hy_conv_kernel, seq_len=seq_len),
        grid=(n_blocks, nct, kfc),
        in_specs=[pl.BlockSpec((rows, tc), amap(a_part, a_base)),
                  pl.BlockSpec((rows, tc), amap(g_part, g_base)),
                  pl.BlockSpec((1, tc), lambda b, c, kf: (0, c)),
                  pl.BlockSpec((None, 2 * tk, half), lambda b, c, kf: (kf, 0, 0)),
                  pl.BlockSpec((None, half, 2 * tk), lambda b, c, kf: (kf, 0, 0)),
                  pl.BlockSpec((6, tk, tc), lambda b, c, kf: (0, kf, order * nct + c))],
        out_specs=pl.BlockSpec((rows, tc), lambda b, c, kf: (b, c)),
        out_shape=jax.ShapeDtypeStruct((n_blocks * rows, d), BF16),
        scratch_shapes=[pltpu.VMEM((rows, tc), F32)] + ([] if a.dtype == BF16 else [pltpu.VMEM((rows, tc), BF16)]),
        compiler_params=_params("parallel", "parallel", "arbitrary"),
        name="hyena_conv",
    )(a, g, skip_row, ana_mats, syn_mats, h)


def _final_norm_kernel(x_ref, g_ref, o_ref):
    x = x_ref[...]
    o_ref[...] = x * lax.rsqrt(jnp.mean(x * x, axis=-1, keepdims=True) + EPS) * g_ref[...]


def _final_norm(x, g, *, row0, rows, tm):
    d = x.shape[1]
    base = row0 // tm
    return pl.pallas_call(
        _final_norm_kernel,
        grid=(rows // tm,),
        in_specs=[pl.BlockSpec((tm, d), lambda i: (base + i, 0)),
                  pl.BlockSpec((1, d), lambda i: (0, 0))],
        out_specs=pl.BlockSpec((tm, d), lambda i: (i, 0)),
        out_shape=jax.ShapeDtypeStruct((rows, d), F32),
        compiler_params=_params("parallel"),
        name="final_norm",
    )(x, g.reshape(1, d))


def kernel(x_prompt, x_sample, cache_win_k, cache_win_v, cache_ax_k, cache_ax_v, c, c_ctx, norm_mix_g, norm_ffn_g, mod_w, mod_b, win_wqkv, win_wo, win_sink, hy_w_in, hy_conv_w, hy_conv_b, hy_f_w1, hy_f_b1, hy_f_w2, hy_f_b2, hy_f_w3, hy_freq, hy_skip, hy_wo, ax_wqkv, ax_q_g, ax_k_g, ax_wo, ffn_w_gu, ffn_w_down, final_g):
    n_ctx_seqs, ctx_len, d = x_prompt.shape
    lat_batch, lat_len, _ = x_sample.shape
    depth = mod_w.shape[0]
    past = cache_win_k.shape[2]
    n_ctx_rows = n_ctx_seqs * ctx_len
    n_lat_rows = lat_batch * lat_len
    tm = 512
    assert n_ctx_rows % tm == 0 and lat_len % tm == 0 and n_ctx_rows % lat_len == 0
    assert lat_batch + 1 <= MOD_ROWS
    geom = dict(n_ctx_rows=n_ctx_rows, lat_len=lat_len, tm=tm)

    x, x_lat = x_prompt.reshape(n_ctx_rows, d), x_sample.reshape(n_lat_rows, d)
    cvec = jnp.concatenate([c_ctx[None, :], c, jnp.zeros((MOD_ROWS - 1 - lat_batch, d), F32)], axis=0)
    mod_all = _adaln_all(cvec, mod_w, mod_b)
    cos, sin = _rope_tables(lat_len)

    w_gu, w_down = ffn_w_gu.astype(BF16), ffn_w_down.astype(BF16)
    wqkv_all = {0: win_wqkv.astype(BF16), 2: ax_wqkv.astype(BF16)}
    wo_all = {0: win_wo.astype(BF16), 1: hy_wo.astype(BF16), 2: ax_wo.astype(BF16)}

    states = {0: None, 2: None}
    kv_w = N_KV_HEADS * HEAD_DIM
    for i in range(depth):
        mod = mod_all[i].reshape(MOD_ROWS * 6, 1, d)
        kind, j = i % 3, i // 3
        if kind == 1:
            assert x_lat is None
            u = _hy_inproj(x, norm_mix_g[i], mod, hy_w_in[j].astype(BF16), hy_conv_w[j], hy_conv_b[j],
                           n_ctx_rows=n_ctx_rows, lat_len=lat_len, ctx_len=ctx_len, tn=512)
            rows = lat_len
            zs = []
            for (base, nblk, sl) in ((0, n_ctx_rows // rows, ctx_len), (n_ctx_rows // rows, lat_batch, lat_len)):
                mats = _dft_mats(sl // 2, min(512, sl // 2))
                resp = _hy_filters(sl, hy_f_w1[j], hy_f_b1[j], hy_f_w2[j], hy_f_b2[j], hy_f_w3[j], hy_freq[j],
                                   mats[0], tc=512)
                conv = functools.partial(_hy_conv, mats=mats, h=resp, n_blocks=nblk, rows=rows, seq_len=sl, tc=512)
                z1 = conv(u, 0, base, u, 1, base, skip_row=hy_skip[j, 0].reshape(1, d), order=0)
                zs.append(conv(z1, 0, 0, u, 2, base, skip_row=hy_skip[j, 1].reshape(1, d), order=1))
            x, h2 = _proj_residual(zs[0], zs[1], wo_all[kind], j, x, None, mod, norm_ffn_g[i], **geom)
        else:
            if kind == 0:
                sink, hg = win_sink[j], None
                ck, cv = cache_win_k, cache_win_v
            else:
                sink = jnp.zeros((N_HEADS,), F32)
                hg = jnp.stack([ax_q_g[j], ax_k_g[j]], axis=0)
                ck, cv = cache_ax_k, cache_ax_v
            ck = ck[:, j:j + 1].reshape(lat_batch, 1, past, kv_w)
            cv = cv[:, j:j + 1].reshape(lat_batch, 1, past, kv_w)
            y, k_c, v_c = _qkv_project(x, x_lat, norm_mix_g[i], mod, wqkv_all[kind], j, cos, sin, hg, states[kind],
                                       ctx_len=ctx_len, **geom)
            states[kind] = (k_c, v_c)
            o_c = _attn_ctx(y, sink, n_ctx_seqs=n_ctx_seqs, ctx_len=ctx_len, use_sink=kind == 0)
            lat_geom = dict(n_ctx_rows=n_ctx_rows, lat_batch=lat_batch, lat_len=lat_len)
            if kind == 0:
                o_l = _attn_lat_win(y, sink, ck, cv, 0, **lat_geom)
            else:
                o_l = _attn_lat_full(y, ck, cv, 0, **lat_geom)
            x, h2 = _proj_residual(o_c, o_l, wo_all[kind], j, x, x_lat, mod, norm_ffn_g[i], **geom)
            x_lat = None
        x = _ffn(x, h2, mod, w_gu, w_down, i, tf=512, **geom)

    y_prompt = _final_norm(x, final_g, row0=0, rows=n_ctx_rows, tm=tm).reshape(n_ctx_seqs, ctx_len, d)
    y_sample = _final_norm(x, final_g, row0=n_ctx_rows, rows=n_lat_rows, tm=tm).reshape(lat_batch, lat_len, d)
    return (y_prompt, y_sample) + states[0] + states[2]
```

```python
import functools
import math

import jax
import jax.numpy as jnp
from jax import lax
from jax.experimental import pallas as pl
from jax.experimental.pallas import tpu as pltpu

F32 = jnp.float32
BF16 = jnp.bfloat16

N_HEADS = 16
N_KV_HEADS = 4
GROUP = N_HEADS // N_KV_HEADS
HEAD_DIM = 128
WINDOW = 128
BLOCK = 128
GRID_W = 64
ROPE_THETA = 10000.0
HY_ORDER = 2
HY_EMB = 33
HY_TARGET = 1e-2
HY_FAST_DECAY = 0.3
HY_SLOW_DECAY = 1.5
HY_MIN_DECAY = math.log(HY_TARGET) / HY_SLOW_DECAY
HY_MAX_DECAY = math.log(HY_TARGET) / HY_FAST_DECAY
EPS = 1e-6
NEG = -1e30
SCALE = HEAD_DIM ** -0.5
LOG2E = math.log2(math.e)
PAIR = 2
Q_STEP = 4 * BLOCK

LANES = 128
HALO = 16
MOD_ROWS = 16
VMEM_LIMIT = 56 * 1024 * 1024

NT_DIMS = (((1,), (1,)), ((), ()))

def _params(*sem):
    return pltpu.CompilerParams(dimension_semantics=sem, vmem_limit_bytes=VMEM_LIMIT)


def _silu(x):
    return x * (1.0 / (1.0 + jnp.exp(-x)))


def _mod_row(i, tm, n_ctx_rows, lat_len):
    return jnp.maximum(i * tm - n_ctx_rows + lat_len, 0) // lat_len


def _mod_kernel(c_ref, w_ref, b_ref, o_ref):
    s = _silu(c_ref[...]).astype(BF16)
    o_ref[...] = jnp.dot(s, w_ref[0].astype(BF16), preferred_element_type=F32) + b_ref[0]


def _adaln_all(cvec, mod_w, mod_b):
    depth, d, n = mod_w.shape
    tn = 1024
    per = d // tn
    return pl.pallas_call(
        _mod_kernel,
        grid=(depth, n // tn),
        in_specs=[pl.BlockSpec((MOD_ROWS, d), lambda l, j: (0, 0)),
                  pl.BlockSpec((1, d, tn), lambda l, j: (l, 0, j)),
                  pl.BlockSpec((1, 1, tn), lambda l, j: (l, 0, j))],
        out_specs=pl.BlockSpec((None, MOD_ROWS, None, None, tn), lambda l, j: (l, 0, j // per, 0, j % per)),
        out_shape=jax.ShapeDtypeStruct((depth, MOD_ROWS, 6, 1, d), F32),
        compiler_params=_params("parallel", "parallel"),
        name="adaln_mod",
    )(cvec, mod_w, mod_b.reshape(depth, 1, n))


def _norm_mod(x, g, shift, scale):
    y = x * lax.rsqrt(jnp.mean(x * x, axis=-1, keepdims=True) + EPS) * g
    return y * (1.0 + scale) + shift


def _rope_rotate(a, cos, sin_signed):
    lane = lax.broadcasted_iota(jnp.int32, a.shape, 1)
    partner = jnp.where((lane & 32) == 0, pltpu.roll(a, 96, 1), pltpu.roll(a, 32, 1))
    return a * cos + partner * sin_signed


def _two_source_specs(x_ctx, x_lat, tm, n_ctx_blocks):
    d = x_ctx.shape[1]
    lat_base = n_ctx_blocks if x_lat is None else 0
    specs = [pl.BlockSpec((tm, d), lambda i: (jnp.minimum(i, n_ctx_blocks - 1), 0)),
             pl.BlockSpec((tm, d), lambda i: (lat_base + jnp.maximum(i - n_ctx_blocks, 0), 0))]
    return specs, [x_ctx, x_ctx if x_lat is None else x_lat]


def _qkv_kernel(*refs, qk_norm, tn, halves, n_ctx_blocks, state_slot):
    xc_ref, xl_ref, g_ref, sh_ref, sc_ref, w_ref, cos_ref, sin_ref, hg_ref = refs[:9]
    o_ref, ks_ref, vs_ref = refs[-3:]
    tm = xc_ref.shape[0]
    rh = tm // halves
    heads_per_tile = tn // HEAD_DIM
    kcol, vcol = N_HEADS * HEAD_DIM, (N_HEADS + N_KV_HEADS) * HEAD_DIM

    def body(x_ref, latent):
        for hf in range(halves):
            rs = slice(hf * rh, (hf + 1) * rh)
            h = _norm_mod(x_ref[rs, :], g_ref[...], sh_ref[0], sc_ref[0]).astype(BF16)
            for jt in range(w_ref.shape[1] // tn):
                acc = jnp.dot(h, w_ref[:, jt * tn:(jt + 1) * tn], preferred_element_type=F32)
                for hh in range(heads_per_tile):
                    head = jt * heads_per_tile + hh
                    a = acc[:, hh * HEAD_DIM:(hh + 1) * HEAD_DIM]
                    if head < N_HEADS + N_KV_HEADS:
                        if qk_norm:
                            gain = hg_ref[0:1, :] if head < N_HEADS else hg_ref[1:2, :]
                            a = a * lax.rsqrt(jnp.mean(a * a, axis=-1, keepdims=True) + EPS) * gain
                        if latent:
                            a = _rope_rotate(a, cos_ref[rs, :], sin_ref[rs, :])
                    o_ref[rs, head * HEAD_DIM:(head + 1) * HEAD_DIM] = a
        if not latent:
            seqs, ctx_len = ks_ref.shape[0], ks_ref.shape[-3]
            for ref, col in ((ks_ref, kcol), (vs_ref, vcol)):
                if state_slot is not None:
                    for other in range(ref.shape[1]):
                        if other != state_slot:
                            ref[:, other] = jnp.zeros(ref.shape[:1] + ref.shape[2:], F32)
                for s in range(seqs):
                    for kvh in range(N_KV_HEADS):
                        val = o_ref[s * ctx_len:(s + 1) * ctx_len, col + kvh * HEAD_DIM:col + (kvh + 1) * HEAD_DIM]
                        if state_slot is None:
                            ref[s, :, kvh, :] = val
                        else:
                            ref[s, state_slot, :, kvh, :] = val

    i = pl.program_id(0)
    pl.when(i < n_ctx_blocks)(lambda: body(xc_ref, False))
    pl.when(i >= n_ctx_blocks)(lambda: body(xl_ref, True))


def _qkv_project(x_ctx, x_lat, g, mod, w, layer, cos, sin, hg, state, *, n_ctx_rows, lat_len, ctx_len, tm):
    d = x_ctx.shape[1]
    seqs = tm // ctx_len
    n_layers = w.shape[0]
    state_shape = (n_ctx_rows // ctx_len, n_layers, ctx_len, N_KV_HEADS, HEAD_DIM)
    last_ctx = n_ctx_rows // tm - 1
    if state is None:
        state_spec = pl.BlockSpec((seqs, n_layers, ctx_len, N_KV_HEADS, HEAD_DIM),
                                  lambda i: (jnp.minimum(i, last_ctx), 0, 0, 0, 0))
        state_specs, state_args, aliases = [], [], {}
    else:
        state_spec = pl.BlockSpec((seqs, None, ctx_len, N_KV_HEADS, HEAD_DIM),
                                  lambda i: (jnp.minimum(i, last_ctx), layer, 0, 0, 0))
        state_specs, state_args = [pl.BlockSpec(memory_space=pl.ANY)] * 2, list(state)
        aliases = {9: 1, 10: 2}
    n = w.shape[2]
    ncb = n_ctx_rows // tm
    t = n_ctx_rows + (x_ctx.shape[0] - n_ctx_rows if x_lat is None else x_lat.shape[0])
    row = functools.partial(_mod_row, tm=tm, n_ctx_rows=n_ctx_rows, lat_len=lat_len)
    tab = lambda i: ((jnp.maximum(i - ncb, 0)) % (lat_len // tm), 0)
    qk_norm = hg is not None
    if hg is None:
        hg = jnp.ones((2, HEAD_DIM), F32)
    x_specs, x_args = _two_source_specs(x_ctx, x_lat, tm, ncb)
    return pl.pallas_call(
        functools.partial(_qkv_kernel, qk_norm=qk_norm, tn=512, halves=2, n_ctx_blocks=ncb,
                          state_slot=layer if state is None else None),
        grid=(t // tm,),
        in_specs=x_specs + [
            pl.BlockSpec((1, d), lambda i: (0, 0)),
            pl.BlockSpec((1, 1, d), lambda i: (row(i) * 6, 0, 0)),
            pl.BlockSpec((1, 1, d), lambda i: (row(i) * 6 + 1, 0, 0)),
            pl.BlockSpec((None, d, n), lambda i: (layer, 0, 0), pipeline_mode=pl.Buffered(1)),
            pl.BlockSpec((tm, HEAD_DIM), tab), pl.BlockSpec((tm, HEAD_DIM), tab),
            pl.BlockSpec((2, HEAD_DIM), lambda i: (0, 0))] + state_specs,
        out_specs=[pl.BlockSpec((tm, n), lambda i: (i, 0)), state_spec, state_spec],
        out_shape=[jax.ShapeDtypeStruct((t, n), F32), jax.ShapeDtypeStruct(state_shape, F32),
                   jax.ShapeDtypeStruct(state_shape, F32)],
        input_output_aliases=aliases,
        compiler_params=_params("arbitrary"),
        name="qkv_project",
    )(*x_args, g.reshape(1, d), mod, mod, w, cos, sin, hg, *state_args)


def _proj_res_kernel(ac_ref, al_ref, xc_ref, xl_ref, w_ref, gate_ref, g2_ref, sh2_ref, sc2_ref, o_ref, h_ref, *,
                     n_ctx_blocks):
    i = pl.program_id(0)

    def emit(a_ref, x_ref):
        xn = x_ref[...] + gate_ref[0] * jnp.dot(a_ref[...], w_ref[...], preferred_element_type=F32)
        o_ref[...] = xn
        h_ref[...] = _norm_mod(xn, g2_ref[...], sh2_ref[0], sc2_ref[0]).astype(BF16)

    pl.when(i < n_ctx_blocks)(lambda: emit(ac_ref, xc_ref))
    pl.when(i >= n_ctx_blocks)(lambda: emit(al_ref, xl_ref))


def _proj_residual(a_ctx, a_lat, w, layer, x_ctx, x_lat, mod, g2, *, n_ctx_rows, lat_len, tm):
    d = x_ctx.shape[1]
    k = w.shape[1]
    ncb = n_ctx_rows // tm
    t = a_ctx.shape[0] + a_lat.shape[0]
    row = functools.partial(_mod_row, tm=tm, n_ctx_rows=n_ctx_rows, lat_len=lat_len)
    modspec = lambda which: pl.BlockSpec((1, 1, d), lambda i: (row(i) * 6 + which, 0, 0))
    x_specs, x_args = _two_source_specs(x_ctx, x_lat, tm, ncb)
    return pl.pallas_call(
        functools.partial(_proj_res_kernel, n_ctx_blocks=ncb),
        grid=(t // tm,),
        in_specs=[pl.BlockSpec((tm, k), lambda i: (jnp.minimum(i, ncb - 1), 0)),
                  pl.BlockSpec((tm, k), lambda i: (jnp.maximum(i - ncb, 0), 0))]
        + x_specs
        + [pl.BlockSpec((None, k, d), lambda i: (layer, 0, 0), pipeline_mode=pl.Buffered(1)),
           modspec(2),
           pl.BlockSpec((1, d), lambda i: (0, 0)),
           modspec(3), modspec(4)],
        out_specs=[pl.BlockSpec((tm, d), lambda i: (i, 0)), pl.BlockSpec((tm, d), lambda i: (i, 0))],
        out_shape=[jax.ShapeDtypeStruct((t, d), F32), jax.ShapeDtypeStruct((t, d), BF16)],
        compiler_params=_params("parallel"),
        name="proj_residual",
    )(a_ctx, a_lat, *x_args, w, mod, g2.reshape(1, d), mod, mod)


def _ffn_kernel(x_ref, h_ref, gate_ref, wg_ref, wu_ref, wd_ref, o_ref):
    f = pl.program_id(1)

    @pl.when(f == 0)
    def _():
        o_ref[...] = jnp.zeros_like(o_ref)

    h = h_ref[...]
    gv = jnp.dot(h, wg_ref[...], preferred_element_type=F32)
    uv = jnp.dot(h, wu_ref[...], preferred_element_type=F32)
    act = (_silu(gv) * uv).astype(BF16)
    o_ref[...] += jnp.dot(act, wd_ref[...], preferred_element_type=F32)

    @pl.when(f == pl.num_programs(1) - 1)
    def _():
        o_ref[...] = x_ref[...] + gate_ref[0] * o_ref[...]


def _ffn(x, h, mod, w_gu, w_down, layer, *, n_ctx_rows, lat_len, tm, tf):
    t, d = x.shape
    ff = w_down.shape[1]
    nf = ff // tf
    row = functools.partial(_mod_row, tm=tm, n_ctx_rows=n_ctx_rows, lat_len=lat_len)
    return pl.pallas_call(
        _ffn_kernel,
        grid=(t // tm, nf),
        in_specs=[pl.BlockSpec((tm, d), lambda i, f: (i, 0)),
                  pl.BlockSpec((tm, d), lambda i, f: (i, 0)),
                  pl.BlockSpec((1, 1, d), lambda i, f: (row(i) * 6 + 5, 0, 0)),
                  pl.BlockSpec((None, d, tf), lambda i, f: (layer, 0, f)),
                  pl.BlockSpec((None, d, tf), lambda i, f: (layer, 0, nf + f)),
                  pl.BlockSpec((None, tf, d), lambda i, f: (layer, f, 0))],
        out_specs=pl.BlockSpec((tm, d), lambda i, f: (i, 0)),
        out_shape=jax.ShapeDtypeStruct((t, d), F32),
        compiler_params=_params("parallel", "arbitrary"),
        name="ffn_swiglu",
    )(x, h, mod, w_gu, w_gu, w_down)


def _stack_heads(q):
    return jnp.concatenate([q[:, h * HEAD_DIM:(h + 1) * HEAD_DIM] for h in range(GROUP)], axis=0)


def _unstack_heads(o, rows):
    return jnp.concatenate([o[h * rows:(h + 1) * rows] for h in range(GROUP)], axis=1)


def _sink_column(sink_ref, kh, rows):
    head = lax.broadcasted_iota(jnp.int32, (GROUP * rows, 1), 0) // rows
    col = jnp.zeros((GROUP * rows, 1), F32)
    for h in range(GROUP):
        col = jnp.where(head == h, sink_ref[kh * GROUP + h], col)
    return col


def _softmax_av(s, v, sink_col):
    m = jnp.max(s, axis=-1, keepdims=True)
    if sink_col is not None:
        m = jnp.maximum(m, sink_col)
    p = jnp.exp(s - m)
    l = jnp.sum(p, axis=-1, keepdims=True)
    if sink_col is not None:
        l = l + jnp.exp(sink_col - m)
    return jnp.dot(p.astype(BF16), v, preferred_element_type=F32) / l


def _attn_ctx_kernel(sink_ref, q_ref, k_ref, v_ref, o_ref, *, use_sink):
    rows = q_ref.shape[0]
    qw = GROUP * HEAD_DIM
    for kh in range(N_KV_HEADS):
        qs = _stack_heads(q_ref[:, kh * qw:(kh + 1) * qw] * SCALE).astype(BF16)
        k = k_ref[:, kh * HEAD_DIM:(kh + 1) * HEAD_DIM].astype(BF16)
        v = v_ref[:, kh * HEAD_DIM:(kh + 1) * HEAD_DIM].astype(BF16)
        s = lax.dot_general(qs, k, NT_DIMS, preferred_element_type=F32)
        sink_col = _sink_column(sink_ref, kh, rows) if use_sink else None
        o_ref[:, kh * qw:(kh + 1) * qw] = _unstack_heads(_softmax_av(s, v, sink_col), rows).astype(BF16)


def _attn_ctx(y, sink, *, n_ctx_seqs, ctx_len, use_sink):
    t = n_ctx_seqs * ctx_len
    q_w, kv_w = N_HEADS * HEAD_DIM, N_KV_HEADS * HEAD_DIM
    return pl.pallas_call(
        functools.partial(_attn_ctx_kernel, use_sink=use_sink),
        grid=(n_ctx_seqs,),
        in_specs=[pl.BlockSpec(memory_space=pltpu.SMEM),
                  pl.BlockSpec((ctx_len, q_w), lambda b: (b, 0)),
                  pl.BlockSpec((ctx_len, kv_w), lambda b: (b, q_w // kv_w)),
                  pl.BlockSpec((ctx_len, kv_w), lambda b: (b, q_w // kv_w + 1))],
        out_specs=pl.BlockSpec((ctx_len, q_w), lambda b: (b, 0)),
        out_shape=jax.ShapeDtypeStruct((t, q_w), BF16),
        compiler_params=_params("parallel"),
        name="attn_context",
    )(sink, y, y, y)


def _keys_by_queries_attention(q_ref, k_scr, vt_scr, s_scr, chunks, bias_ref, sink_ref, o_ref):
    qn = q_ref.shape[0]
    kh = pl.program_id(1)

    def logits(h, ci):
        st, sz = chunks[ci]
        q = (q_ref[:, h * HEAD_DIM:(h + 1) * HEAD_DIM] * (SCALE * LOG2E)).astype(BF16)
        st_ = lax.dot_general(k_scr[st:st + sz, :], q, NT_DIMS, preferred_element_type=F32)
        if bias_ref is not None and ci == 0:
            st_ = st_ + bias_ref[...]
        s_scr[st:st + sz, h * qn:(h + 1) * qn] = st_
        return jnp.max(st_, axis=0, keepdims=True)

    def weigh(h, ci, m):
        st, sz = chunks[ci]
        p = jnp.exp2(s_scr[st:st + sz, h * qn:(h + 1) * qn] - m)
        pv = jnp.dot(vt_scr[:, st:st + sz], p.astype(BF16), preferred_element_type=F32)
        return jnp.sum(p, axis=0, keepdims=True), pv

    n = len(chunks)
    maxes = [logits(0, ci) for ci in range(n)]
    for h in range(GROUP):
        m = functools.reduce(jnp.maximum, maxes)
        if sink_ref is not None:
            sink2 = sink_ref[kh * GROUP + h] * LOG2E
            m = jnp.maximum(m, sink2)
        maxes, l, acc = [], None, None
        for ci in range(n):
            if h + 1 < GROUP:
                maxes.append(logits(h + 1, ci))
            lc, pv = weigh(h, ci, m)
            l, acc = (lc, pv) if l is None else (l + lc, acc + pv)
        if sink_ref is not None:
            l = l + jnp.exp2(sink2 - m)
        o_ref[:, h * HEAD_DIM:(h + 1) * HEAD_DIM] = (acc * (1.0 / l)).T.astype(BF16)


def _attn_lat_win_kernel(sink_ref, q_ref, kp_ref, kc_ref, kn_ref, vp_ref, vc_ref, vn_ref, ck_ref, cv_ref, bias_ref,
                         o_ref, k_scr, vt_scr, s_scr):
    past = ck_ref.shape[0]
    win = Q_STEP + 2 * BLOCK

    @pl.when(pl.program_id(2) == 0)
    def _():
        k_scr[win:, :] = ck_ref[...].astype(BF16)
        vt_scr[:, win:] = cv_ref[...].T.astype(BF16)

    row = 0
    for k_ref, v_ref in ((kp_ref, vp_ref), (kc_ref, vc_ref), (kn_ref, vn_ref)):
        n = k_ref.shape[0]
        k_scr[row:row + n, :] = k_ref[...].astype(BF16)
        vt_scr[:, row:row + n] = v_ref[...].T.astype(BF16)
        row += n

    kh, step, n_steps = pl.program_id(1), pl.program_id(2), pl.num_programs(2)
    span, lanes, nq = 3 * BLOCK, PAIR * BLOCK, Q_STEP // BLOCK
    groups = [(qb, p) for qb in range(nq) for p in range(GROUP // PAIR)]

    def logits(g):
        qb, p = groups[g]
        rows = slice(qb * BLOCK, (qb + 1) * BLOCK)
        q = jnp.concatenate([q_ref[rows, h * HEAD_DIM:(h + 1) * HEAD_DIM] for h in range(PAIR * p, PAIR * (p + 1))],
                            axis=0)
        q = (q * (SCALE * LOG2E)).astype(BF16)
        edge = 1
        if qb == 0:
            edge = jnp.where(step == 0, 0, 1)
        elif qb == nq - 1:
            edge = jnp.where(step == n_steps - 1, 2, 1)
        sw = lax.dot_general(k_scr[qb * BLOCK:qb * BLOCK + span, :], q, NT_DIMS,
                             preferred_element_type=F32) + bias_ref[edge]
        sc = lax.dot_general(k_scr[win:, :], q, NT_DIMS, preferred_element_type=F32)
        s_scr[0:span, g * lanes:(g + 1) * lanes] = sw
        s_scr[span:, g * lanes:(g + 1) * lanes] = sc
        return jnp.maximum(jnp.max(sw, axis=0, keepdims=True), jnp.max(sc, axis=0, keepdims=True))

    head = lax.broadcasted_iota(jnp.int32, (1, lanes), 1) // BLOCK
    m_next = logits(0)
    for g, (qb, p) in enumerate(groups):
        sink2 = jnp.zeros((1, lanes), F32)
        for hh in range(PAIR):
            sink2 = jnp.where(head == hh, sink_ref[kh * GROUP + PAIR * p + hh] * LOG2E, sink2)
        m = jnp.maximum(m_next, sink2)
        if g + 1 < len(groups):
            m_next = logits(g + 1)
        pw = jnp.exp2(s_scr[0:span, g * lanes:(g + 1) * lanes] - m)
        pc = jnp.exp2(s_scr[span:, g * lanes:(g + 1) * lanes] - m)
        pv = (jnp.dot(vt_scr[:, qb * BLOCK:qb * BLOCK + span], pw.astype(BF16), preferred_element_type=F32)
              + jnp.dot(vt_scr[:, win:], pc.astype(BF16), preferred_element_type=F32))
        l = (jnp.sum(pw, axis=0, keepdims=True) + jnp.sum(pc, axis=0, keepdims=True) + jnp.exp2(sink2 - m))
        ot = pv * (1.0 / l)
        for hh in range(PAIR):
            col = (PAIR * p + hh) * HEAD_DIM
            o_ref[qb * BLOCK:(qb + 1) * BLOCK, col:col + HEAD_DIM] = ot[:, hh * BLOCK:(hh + 1) * BLOCK].T.astype(BF16)


def _window_bias():
    c = jnp.arange(3 * BLOCK, dtype=jnp.int32)[:, None]
    r = jnp.arange(PAIR * BLOCK, dtype=jnp.int32)[None, :] % BLOCK
    band = jnp.abs(r + BLOCK - c) <= WINDOW
    variants = [band & (c >= BLOCK), band, band & (c < 2 * BLOCK)]
    return jnp.where(jnp.stack(variants), 0.0, NEG).astype(F32)


def _cache_spec(cache, layer):
    return pl.BlockSpec((None, None, cache.shape[2], HEAD_DIM), lambda b, kh, qs: (b, layer, 0, kh))


def _attn_lat_win(y, sink, cache_k, cache_v, layer, *, n_ctx_rows, lat_batch, lat_len):
    qw = GROUP * HEAD_DIM
    nb = lat_len // BLOCK
    ns = lat_len // Q_STEP
    per = Q_STEP // BLOCK
    base = n_ctx_rows // BLOCK
    past = cache_k.shape[2]
    n_keys = Q_STEP + 2 * BLOCK + past

    def edge(col0, blk):
        return pl.BlockSpec((BLOCK, HEAD_DIM),
                            lambda b, kh, qs: (base + b * nb + jnp.clip(qs * per + blk, 0, nb - 1), col0 + kh))

    def centre(col0):
        return pl.BlockSpec((Q_STEP, HEAD_DIM), lambda b, kh, qs: (base // per + b * ns + qs, col0 + kh))

    kcol, vcol = N_HEADS, N_HEADS + N_KV_HEADS
    cache_spec = _cache_spec(cache_k, layer)
    return pl.pallas_call(
        _attn_lat_win_kernel,
        grid=(lat_batch, N_KV_HEADS, ns),
        in_specs=[pl.BlockSpec(memory_space=pltpu.SMEM),
                  pl.BlockSpec((Q_STEP, qw), lambda b, kh, qs: (base // per + b * ns + qs, kh)),
                  edge(kcol, -1), centre(kcol), edge(kcol, per),
                  edge(vcol, -1), centre(vcol), edge(vcol, per),
                  cache_spec, cache_spec,
                  pl.BlockSpec((3, 3 * BLOCK, PAIR * BLOCK), lambda b, kh, qs: (0, 0, 0))],
        out_specs=pl.BlockSpec((Q_STEP, qw), lambda b, kh, qs: (b * ns + qs, kh)),
        out_shape=jax.ShapeDtypeStruct((lat_batch * lat_len, N_HEADS * HEAD_DIM), BF16),
        scratch_shapes=[pltpu.VMEM((n_keys, HEAD_DIM), BF16), pltpu.VMEM((HEAD_DIM, n_keys), BF16),
                        pltpu.VMEM((3 * BLOCK + past, GROUP * Q_STEP), F32)],
        compiler_params=_params("parallel", "parallel", "arbitrary"),
        name="attn_latent_window",
    )(sink, y, y, y, y, y, y, y, cache_k, cache_v, _window_bias())


def _attn_lat_full_kernel(q_ref, k_ref, v_ref, ck_ref, cv_ref, o_ref, k_scr, vt_scr, s_scr, *, chunk):
    lat_len = k_ref.shape[0]
    n_keys = k_scr.shape[0]

    @pl.when(pl.program_id(2) == 0)
    def _():
        k_scr[0:lat_len, :] = k_ref[...].astype(BF16)
        k_scr[lat_len:, :] = ck_ref[...].astype(BF16)
        vt_scr[:, 0:lat_len] = v_ref[...].T.astype(BF16)
        vt_scr[:, lat_len:] = cv_ref[...].T.astype(BF16)

    chunks = tuple((st, chunk) for st in range(0, n_keys, chunk))
    _keys_by_queries_attention(q_ref, k_scr, vt_scr, s_scr, chunks, None, None, o_ref)


def _attn_lat_full(y, cache_k, cache_v, layer, *, n_ctx_rows, lat_batch, lat_len):
    qw = GROUP * HEAD_DIM
    ns = lat_len // Q_STEP
    base = n_ctx_rows // Q_STEP
    seq_base = n_ctx_rows // lat_len
    past = cache_k.shape[2]
    cache_spec = _cache_spec(cache_k, layer)
    n_keys = lat_len + past
    chunk = 512
    assert n_keys % chunk == 0
    return pl.pallas_call(
        functools.partial(_attn_lat_full_kernel, chunk=chunk),
        grid=(lat_batch, N_KV_HEADS, ns),
        in_specs=[pl.BlockSpec((Q_STEP, qw), lambda b, kh, qs: (base + b * ns + qs, kh)),
                  pl.BlockSpec((lat_len, HEAD_DIM), lambda b, kh, qs: (seq_base + b, N_HEADS + kh)),
                  pl.BlockSpec((lat_len, HEAD_DIM), lambda b, kh, qs: (seq_base + b, N_HEADS + N_KV_HEADS + kh)),
                  cache_spec, cache_spec],
        out_specs=pl.BlockSpec((Q_STEP, qw), lambda b, kh, qs: (b * ns + qs, kh)),
        out_shape=jax.ShapeDtypeStruct((lat_batch * lat_len, N_HEADS * HEAD_DIM), BF16),
        scratch_shapes=[pltpu.VMEM((n_keys, HEAD_DIM), BF16), pltpu.VMEM((HEAD_DIM, n_keys), BF16),
                        pltpu.VMEM((n_keys, GROUP * Q_STEP), F32)],
        compiler_params=_params("parallel", "parallel", "arbitrary"),
        name="attn_latent_full",
    )(y, y, y, cache_k, cache_v)


def _rope_tables(lat_len):
    half = HEAD_DIM // 2
    n_rows = lat_len // GRID_W
    row = jnp.repeat(jnp.arange(n_rows, dtype=F32), GRID_W)
    col = jnp.tile(jnp.arange(GRID_W, dtype=F32), n_rows)
    inv = ROPE_THETA ** (-jnp.arange(0, half, 2, dtype=F32) / half)
    ang_r, ang_c = row[:, None] * inv, col[:, None] * inv
    cos = jnp.concatenate([jnp.cos(ang_r)] * 2 + [jnp.cos(ang_c)] * 2, axis=-1)
    sin = jnp.concatenate([-jnp.sin(ang_r), jnp.sin(ang_r), -jnp.sin(ang_c), jnp.sin(ang_c)], axis=-1)
    return cos, sin


def _dft_mats(seq_len, tk):
    n = 2 * seq_len
    k = jnp.arange(seq_len, dtype=jnp.int32)
    step = 64
    a = jnp.arange(seq_len // step, dtype=jnp.int32) * step
    b = jnp.arange(step, dtype=jnp.int32)
    ang_a = ((k[:, None] * a[None, :]) % n).astype(F32) * (2.0 * math.pi / n)
    ang_b = ((k[:, None] * b[None, :]) % n).astype(F32) * (2.0 * math.pi / n)
    ca, sa = jnp.cos(ang_a)[:, :, None], jnp.sin(ang_a)[:, :, None]
    cb, sb = jnp.cos(ang_b)[:, None, :], jnp.sin(ang_b)[:, None, :]
    fc = (ca * cb - sa * sb).reshape(seq_len, seq_len)
    fs = -(sa * cb + ca * sb).reshape(seq_len, seq_len)
    sign = jnp.where(k % 2 == 0, 1.0, -1.0).astype(F32)
    fs = jnp.where(k[:, None] == 0, sign[None, :], fs)
    kfc = seq_len // tk
    ana = jnp.concatenate([fc.reshape(kfc, tk, seq_len), fs.reshape(kfc, tk, seq_len)], axis=1).astype(BF16)
    ana = lax.optimization_barrier(ana)
    return ana, ana.transpose(0, 2, 1)


def _hy_filter_kernel(feat_ref, tn_ref, w1_ref, b1_ref, w2_ref, b2_ref, fq_ref, w3f_ref, w3b_ref, dl_ref, ana_ref,
                      h_ref, pq_scr, inv_scr, nyq_scr):
    kf = pl.program_id(1)
    hp = lax.Precision.HIGHEST
    tk = h_ref.shape[1]

    @pl.when(kf == 0)
    def _():
        def taps(g):
            a = jnp.dot(feat_ref[g], w1_ref[...], precision=hp, preferred_element_type=F32) + b1_ref[...]
            a = jnp.sin(fq_ref[0:1, :] * a)
            a = jnp.dot(a, w2_ref[...], precision=hp, preferred_element_type=F32) + b2_ref[...]
            a = jnp.sin(fq_ref[1:2, :] * a)
            decay = jnp.exp(-tn_ref[g] * dl_ref[...])
            a = a.astype(BF16)
            return (jnp.dot(a, w3f_ref[...].astype(BF16), preferred_element_type=F32) * decay,
                    jnp.dot(a, w3b_ref[...].astype(BF16), preferred_element_type=F32) * decay)

        fwd0, bwd0 = taps(0)
        fwd1, bwd1 = taps(1)
        fwdm, bwdm = taps(2)
        rowi = lax.broadcasted_iota(jnp.int32, fwd0.shape, 0)
        first = rowi == 0
        bwd0 = jnp.where(first, 0.0, bwd0)
        absum = lambda v: jnp.sum(jnp.abs(v), axis=0, keepdims=True)
        inv_scr[...] = 1.0 / (absum(fwd0) + absum(fwd1) + absum(bwd0) + absum(bwd1) + EPS)
        parts = ((fwd0, bwd0),
                 (bwdm, jnp.where(first, 0.0, bwd1)),
                 (fwd1, jnp.where(first, 0.0, fwdm)))
        for n, (pos, neg) in enumerate(parts):
            p = pos + neg
            nyq_scr[n:n + 1, :] = jnp.sum(jnp.where((rowi & 1) == 0, p, -p), axis=0, keepdims=True)
            pq_scr[2 * n] = p.astype(BF16)
            pq_scr[2 * n + 1] = (pos - neg).astype(BF16)

    inv = inv_scr[...]
    for n in range(3):
        h_ref[2 * n] = jnp.dot(ana_ref[0:tk, :], pq_scr[2 * n], preferred_element_type=F32) * inv
        hi = jnp.dot(ana_ref[tk:, :], pq_scr[2 * n + 1], preferred_element_type=F32)
        rowk = lax.broadcasted_iota(jnp.int32, hi.shape, 0)
        hi = jnp.where((rowk == 0) & (kf == 0), nyq_scr[n:n + 1, :], hi)
        h_ref[2 * n + 1] = hi * inv


def _hy_filters(seq_len, f_w1, f_b1, f_w2, f_b2, f_w3, freq, ana_mats, *, tc):
    half = seq_len // 2
    tk = ana_mats.shape[1] // 2
    hid = f_w1.shape[1]
    nd = f_w3.shape[1] // 2
    d = nd // HY_ORDER
    t = jnp.arange(seq_len, dtype=F32)
    tnorm = t / max(seq_len - 1, 1)
    bands = (HY_EMB - 1) // 2
    fb = jnp.linspace(1e-4, bands - 1, bands, dtype=F32)
    w = 2.0 * math.pi * t / seq_len
    feats = jnp.concatenate([tnorm[:, None], jnp.cos(w[:, None] * fb), -jnp.sin(w[:, None] * fb)], axis=-1)
    emb = 64
    feats = jnp.pad(feats, ((0, 0), (0, emb - HY_EMB)))
    w1 = jnp.pad(f_w1, ((0, emb - HY_EMB), (0, 0)))
    mirror = half - jnp.arange(half)
    feats3 = jnp.stack([feats[:half], feats[half:], feats[mirror]])
    tn3 = jnp.stack([tnorm[:half], tnorm[half:], tnorm[mirror]]).reshape(3, half, 1)
    deltas = jnp.abs(jnp.linspace(HY_MIN_DECAY, HY_MAX_DECAY, d, dtype=F32))
    deltas = jnp.tile(deltas, HY_ORDER).reshape(1, nd)
    nct = nd // tc
    const = lambda shape: pl.BlockSpec(shape, lambda c, kf: (0,) * len(shape))
    return pl.pallas_call(
        _hy_filter_kernel,
        grid=(nct, half // tk),
        in_specs=[const((3, half, emb)), const((3, half, 1)),
                  const((emb, hid)), const((1, hid)), const((hid, hid)), const((1, hid)), const((2, hid)),
                  pl.BlockSpec((hid, tc), lambda c, kf: (0, c)),
                  pl.BlockSpec((hid, tc), lambda c, kf: (0, nct + c)),
                  pl.BlockSpec((1, tc), lambda c, kf: (0, c)),
                  pl.BlockSpec((None, 2 * tk, half), lambda c, kf: (kf, 0, 0))],
        out_specs=pl.BlockSpec((6, tk, tc), lambda c, kf: (0, kf, c)),
        out_shape=jax.ShapeDtypeStruct((6, half, nd), F32),
        scratch_shapes=[pltpu.VMEM((6, half, tc), BF16), pltpu.VMEM((1, tc), F32), pltpu.VMEM((8, tc), F32)],
        compiler_params=_params("parallel", "arbitrary"),
        name="hyena_filter",
    )(feats3, tn3, w1, f_b1.reshape(1, hid), f_w2, f_b2.reshape(1, hid), freq, f_w3, f_w3, deltas, ana_mats)


def _hy_inproj_kernel(xp_ref, x_ref, xn_ref, g_ref, sh_ref, sc_ref, w_ref, cw_ref, cb_ref, o_ref, h_scr, y_scr, *,
                      n_ctx_blocks, ctx_len, lat_len, tn):
    i = pl.program_id(0)
    tm = x_ref.shape[0]
    norm = lambda ref: _norm_mod(ref[...], g_ref[...], sh_ref[0], sc_ref[0]).astype(BF16)
    h_scr[0:HALO, :] = norm(xp_ref)
    h_scr[HALO:HALO + tm, :] = norm(x_ref)
    h_scr[HALO + tm:, :] = norm(xn_ref)

    row = lax.broadcasted_iota(jnp.int32, (tm, tn), 0)
    is_ctx = i < n_ctx_blocks
    pos = jnp.where(is_ctx, row % ctx_len, ((i - n_ctx_blocks) * tm + row) % lat_len)
    first = pos == 0
    last = pos == jnp.where(is_ctx, ctx_len, lat_len) - 1
    for jt in range(w_ref.shape[1] // tn):
        cols = slice(jt * tn, (jt + 1) * tn)
        y = y_scr.at[jt % 2]
        y[...] = jnp.dot(h_scr[...], w_ref[:, cols], preferred_element_type=F32)
        prev = jnp.where(first, 0.0, y[HALO - 1:HALO - 1 + tm, :])
        nxt = jnp.where(last, 0.0, y[HALO + 1:HALO + 1 + tm, :])
        o_ref[:, cols] = (prev * cw_ref[0:1, cols] + y[HALO:HALO + tm, :] * cw_ref[1:2, cols]
                          + nxt * cw_ref[2:3, cols] + cb_ref[:, cols]).astype(BF16)


def _hy_inproj(x, g, mod, w, conv_w, conv_b, *, n_ctx_rows, lat_len, ctx_len, tm):
    t, d = x.shape
    n = w.shape[1]
    per = tm // HALO
    row = functools.partial(_mod_row, tm=tm, n_ctx_rows=n_ctx_rows, lat_len=lat_len)
    return pl.pallas_call(
        functools.partial(_hy_inproj_kernel, n_ctx_blocks=n_ctx_rows // tm, ctx_len=ctx_len, lat_len=lat_len, tn=512),
        grid=(t // tm,),
        in_specs=[pl.BlockSpec((HALO, d), lambda i: (jnp.maximum(i * per - 1, 0), 0)),
                  pl.BlockSpec((tm, d), lambda i: (i, 0)),
                  pl.BlockSpec((HALO, d), lambda i: (jnp.minimum((i + 1) * per, t // HALO - 1), 0)),
                  pl.BlockSpec((1, d), lambda i: (0, 0)),
                  pl.BlockSpec((1, 1, d), lambda i: (row(i) * 6, 0, 0)),
                  pl.BlockSpec((1, 1, d), lambda i: (row(i) * 6 + 1, 0, 0)),
                  pl.BlockSpec((d, n), lambda i: (0, 0), pipeline_mode=pl.Buffered(1)),
                  pl.BlockSpec((3, n), lambda i: (0, 0)),
                  pl.BlockSpec((1, n), lambda i: (0, 0))],
        out_specs=pl.BlockSpec((tm, n), lambda i: (i, 0)),
        out_shape=jax.ShapeDtypeStruct((t, n), BF16),
        scratch_shapes=[pltpu.VMEM((tm + 2 * HALO, d), BF16), pltpu.VMEM((2, tm + 2 * HALO, 512), F32)],
        compiler_params=_params("parallel"),
        name="hyena_inproj",
    )(x, x, x, g.reshape(1, d), mod, mod, w, conv_w, conv_b.reshape(1, n))


def _hy_conv_kernel(a_ref, g_ref, skip_ref, ana_ref, syn_ref, h_ref, o_ref, acc_scr, *, seq_len):
    kf = pl.program_id(2)
    rows = a_ref.shape[0]
    tk = h_ref.shape[1]
    half = seq_len // 2
    chunk = 256

    @pl.when(kf == 0)
    def _():
        acc_scr[...] = jnp.zeros_like(acc_scr)

    u_ref = a_ref
    first =(lax.broadcasted_iota(jnp.int32, (tk, a_ref.shape[1]), 0) == 0) & (kf == 0)

    def respond(uf, n):
        ur, ui = uf[0:tk, :], uf[tk:, :]
        hr, hi = h_ref[2 * n], h_ref[2 * n + 1]
        yr = jnp.where(first, 0.5 * (ur * hr), ur * hr - ui * hi)
        yi = jnp.where(first, 0.5 * (ui * hi), ur * hi + ui * hr)
        return yr, yi

    step = min(half, 512)
    for seg in range(0, rows, seq_len):
        uf1 = jnp.dot(ana_ref[...], u_ref[seg:seg + half, :], preferred_element_type=F32)
        uf2 = jnp.dot(ana_ref[...], u_ref[seg + half:seg + seq_len, :], preferred_element_type=F32)
        for dst, (n1, n2) in ((seg, (0, 1)), (seg + half, (2, 0))):
            y1r, y1i = respond(uf1, n1)
            y2r, y2i = respond(uf2, n2)
            yf = jnp.concatenate([y1r + y2r, y1i + y2i], axis=0).astype(BF16)
            for n0 in range(0, half, step):
                acc_scr[dst + n0:dst + n0 + step, :] += jnp.dot(syn_ref[n0:n0 + step, :], yf,
                                                                preferred_element_type=F32)

    @pl.when(kf == pl.num_programs(2) - 1)
    def _():
        for r0 in range(0, rows, chunk):
            rs = slice(r0, r0 + chunk)
            a = a_ref[rs, :].astype(F32)
            g = g_ref[rs, :].astype(F32)
            o_ref[rs, :] = (g * (acc_scr[rs, :] * (1.0 / half) + a * skip_ref[...])).astype(BF16)


def _hy_conv(a, a_part, a_base, g, g_part, g_base, skip_row, mats, h, order, *, n_blocks, rows, seq_len, tc):
    d = skip_row.shape[1]
    nct = d // tc
    ana_mats, syn_mats = mats
    kfc = ana_mats.shape[0]
    tk = ana_mats.shape[1] // 2

    def amap(part, base):
        return lambda b, c, kf: (base + b, part * nct + c)

    half = seq_len // 2
    return pl.pallas_call(
        functools.partial(_hy_conv_kernel, seq_len=seq_len),
        grid=(n_blocks, nct, kfc),
        in_specs=[pl.BlockSpec((rows, tc), amap(a_part, a_base)),
                  pl.BlockSpec((rows, tc), amap(g_part, g_base)),
                  pl.BlockSpec((1, tc), lambda b, c, kf: (0, c)),
                  pl.BlockSpec((None, 2 * tk, half), lambda b, c, kf: (kf, 0, 0)),
                  pl.BlockSpec((None, half, 2 * tk), lambda b, c, kf: (kf, 0, 0)),
                  pl.BlockSpec((6, tk, tc), lambda b, c, kf: (0, kf, order * nct + c))],
        out_specs=pl.BlockSpec((rows, tc), lambda b, c, kf: (b, c)),
        out_shape=jax.ShapeDtypeStruct((n_blocks * rows, d), BF16),
        scratch_shapes=[pltpu.VMEM((rows, tc), F32)],
        compiler_params=_params("parallel", "parallel", "arbitrary"),
        name="hyena_conv",
    )(a, g, skip_row, ana_mats, syn_mats, h)


def _final_norm_kernel(x_ref, g_ref, o_ref):
    x = x_ref[...]
    o_ref[...] = x * lax.rsqrt(jnp.mean(x * x, axis=-1, keepdims=True) + EPS) * g_ref[...]


def _final_norm(x, g, *, row0, rows, tm):
    d = x.shape[1]
    base = row0 // tm
    return pl.pallas_call(
        _final_norm_kernel,
        grid=(rows // tm,),
        in_specs=[pl.BlockSpec((tm, d), lambda i: (base + i, 0)),
                  pl.BlockSpec((1, d), lambda i: (0, 0))],
        out_specs=pl.BlockSpec((tm, d), lambda i: (i, 0)),
        out_shape=jax.ShapeDtypeStruct((rows, d), F32),
        compiler_params=_params("parallel"),
        name="final_norm",
    )(x, g.reshape(1, d))


def kernel(x_prompt, x_sample, cache_win_k, cache_win_v, cache_ax_k, cache_ax_v, c, c_ctx, norm_mix_g, norm_ffn_g, mod_w, mod_b, win_wqkv, win_wo, win_sink, hy_w_in, hy_conv_w, hy_conv_b, hy_f_w1, hy_f_b1, hy_f_w2, hy_f_b2, hy_f_w3, hy_freq, hy_skip, hy_wo, ax_wqkv, ax_q_g, ax_k_g, ax_wo, ffn_w_gu, ffn_w_down, final_g):
    n_ctx_seqs, ctx_len, d = x_prompt.shape
    lat_batch, lat_len, _ = x_sample.shape
    depth = mod_w.shape[0]
    past = cache_win_k.shape[2]
    n_ctx_rows = n_ctx_seqs * ctx_len
    n_lat_rows = lat_batch * lat_len
    tm = 512
    assert n_ctx_rows % tm == 0 and lat_len % tm == 0 and n_ctx_rows % lat_len == 0
    assert lat_batch + 1 <= MOD_ROWS
    geom = dict(n_ctx_rows=n_ctx_rows, lat_len=lat_len, tm=tm)

    x, x_lat = x_prompt.reshape(n_ctx_rows, d), x_sample.reshape(n_lat_rows, d)
    cvec = jnp.concatenate([c_ctx[None, :], c, jnp.zeros((MOD_ROWS - 1 - lat_batch, d), F32)], axis=0)
    mod_all = _adaln_all(cvec, mod_w, mod_b)
    cos, sin = _rope_tables(lat_len)

    w_gu, w_down = ffn_w_gu.astype(BF16), ffn_w_down.astype(BF16)
    wqkv_all = {0: win_wqkv.astype(BF16), 2: ax_wqkv.astype(BF16)}
    wo_all = {0: win_wo.astype(BF16), 1: hy_wo.astype(BF16), 2: ax_wo.astype(BF16)}

    states = {0: None, 2: None}
    kv_w = N_KV_HEADS * HEAD_DIM
    for i in range(depth):
        mod = mod_all[i].reshape(MOD_ROWS * 6, 1, d)
        kind, j = i % 3, i // 3
        if kind == 1:
            assert x_lat is None
            u = _hy_inproj(x, norm_mix_g[i], mod, hy_w_in[j].astype(BF16), hy_conv_w[j], hy_conv_b[j],
                           ctx_len=ctx_len, **geom)
            rows = lat_len
            zs = []
            for (base, nblk, sl) in ((0, n_ctx_rows // rows, ctx_len), (n_ctx_rows // rows, lat_batch, lat_len)):
                mats = _dft_mats(sl // 2, min(512, sl // 2))
                resp = _hy_filters(sl, hy_f_w1[j], hy_f_b1[j], hy_f_w2[j], hy_f_b2[j], hy_f_w3[j], hy_freq[j],
                                   mats[0], tc=512)
                conv = functools.partial(_hy_conv, mats=mats, h=resp, n_blocks=nblk, rows=rows, seq_len=sl, tc=512)
                z1 = conv(u, 0, base, u, 1, base, skip_row=hy_skip[j, 0].reshape(1, d), order=0)
                zs.append(conv(z1, 0, 0, u, 2, base, skip_row=hy_skip[j, 1].reshape(1, d), order=1))
            x, h2 = _proj_residual(zs[0], zs[1], wo_all[kind], j, x, None, mod, norm_ffn_g[i], **geom)
        else:
            if kind == 0:
                sink, hg = win_sink[j], None
                ck, cv = cache_win_k, cache_win_v
            else:
                sink = jnp.zeros((N_HEADS,), F32)
                hg = jnp.stack([ax_q_g[j], ax_k_g[j]], axis=0)
                ck, cv = cache_ax_k, cache_ax_v
            ck = ck[:, j:j + 1].reshape(lat_batch, 1, past, kv_w)
            cv = cv[:, j:j + 1].reshape(lat_batch, 1, past, kv_w)
            y, k_c, v_c = _qkv_project(x, x_lat, norm_mix_g[i], mod, wqkv_all[kind], j, cos, sin, hg, states[kind],
                                       ctx_len=ctx_len, **geom)
            states[kind] = (k_c, v_c)
            o_c = _attn_ctx(y, sink, n_ctx_seqs=n_ctx_seqs, ctx_len=ctx_len, use_sink=kind == 0)
            lat_geom = dict(n_ctx_rows=n_ctx_rows, lat_batch=lat_batch, lat_len=lat_len)
            if kind == 0:
                o_l = _attn_lat_win(y, sink, ck, cv, 0, **lat_geom)
            else:
                o_l = _attn_lat_full(y, ck, cv, 0, **lat_geom)
            x, h2 = _proj_residual(o_c, o_l, wo_all[kind], j, x, x_lat, mod, norm_ffn_g[i], **geom)
            x_lat = None
        x = _ffn(x, h2, mod, w_gu, w_down, i, tf=512, **geom)

    y_prompt = _final_norm(x, final_g, row0=0, rows=n_ctx_rows, tm=tm).reshape(n_ctx_seqs, ctx_len, d)
    y_sample = _final_norm(x, final_g, row0=n_ctx_rows, rows=n_lat_rows, tm=tm).reshape(lat_batch, lat_len, d)
    return (y_prompt, y_sample) + states[0] + states[2]
```

```python
import functools
import math

import jax
import jax.numpy as jnp
from jax import lax
from jax.experimental import pallas as pl
from jax.experimental.pallas import tpu as pltpu

F32 = jnp.float32
BF16 = jnp.bfloat16

N_HEADS = 16
N_KV_HEADS = 4
GROUP = N_HEADS // N_KV_HEADS
HEAD_DIM = 128
WINDOW = 128
BLOCK = 128
GRID_W = 64
ROPE_THETA = 10000.0
HY_ORDER = 2
HY_EMB = 33
HY_TARGET = 1e-2
HY_FAST_DECAY = 0.3
HY_SLOW_DECAY = 1.5
HY_MIN_DECAY = math.log(HY_TARGET) / HY_SLOW_DECAY
HY_MAX_DECAY = math.log(HY_TARGET) / HY_FAST_DECAY
EPS = 1e-6
NEG = -1e30
SCALE = HEAD_DIM ** -0.5
LOG2E = math.log2(math.e)
PAIR = 2
Q_STEP = 4 * BLOCK
TOKEN_TILE = 512
COL_TILE = 512
HALO = 16
MOD_ROWS = 16
VMEM_LIMIT = 56 * 1024 * 1024

NT_DIMS = (((1,), (1,)), ((), ()))


def _params(*sem):
    return pltpu.CompilerParams(dimension_semantics=sem, vmem_limit_bytes=VMEM_LIMIT)


def _silu(x):
    return x * (1.0 / (1.0 + jnp.exp(-x)))


def _mod_row(i, tm, n_ctx_rows, lat_len):
    return jnp.maximum(i * tm - n_ctx_rows + lat_len, 0) // lat_len


def _mod_kernel(c_ref, w_ref, b_ref, o_ref):
    s = _silu(c_ref[...]).astype(BF16)
    o_ref[...] = jnp.dot(s, w_ref[0].astype(BF16), preferred_element_type=F32) + b_ref[0]


def _adaln_all(cvec, mod_w, mod_b):
    depth, d, n = mod_w.shape
    tn = 1024
    per = d // tn
    return pl.pallas_call(
        _mod_kernel,
        grid=(depth, n // tn),
        in_specs=[pl.BlockSpec((MOD_ROWS, d), lambda l, j: (0, 0)),
                  pl.BlockSpec((1, d, tn), lambda l, j: (l, 0, j)),
                  pl.BlockSpec((1, 1, tn), lambda l, j: (l, 0, j))],
        out_specs=pl.BlockSpec((None, MOD_ROWS, None, None, tn), lambda l, j: (l, 0, j // per, 0, j % per)),
        out_shape=jax.ShapeDtypeStruct((depth, MOD_ROWS, 6, 1, d), F32),
        compiler_params=_params("parallel", "parallel"),
        name="adaln_mod",
    )(cvec, mod_w, mod_b.reshape(depth, 1, n))


def _norm_mod(x, g, shift, scale):
    y = x * lax.rsqrt(jnp.mean(x * x, axis=-1, keepdims=True) + EPS) * g
    return y * (1.0 + scale) + shift


def _rope_rotate(a, cos, sin_signed):
    lane = lax.broadcasted_iota(jnp.int32, a.shape, 1)
    partner = jnp.where((lane & 32) == 0, pltpu.roll(a, 96, 1), pltpu.roll(a, 32, 1))
    return a * cos + partner * sin_signed


def _two_source_specs(x_ctx, x_lat, tm, n_ctx_blocks):
    d = x_ctx.shape[1]
    lat_base = n_ctx_blocks if x_lat is None else 0
    specs = [pl.BlockSpec((tm, d), lambda i: (jnp.minimum(i, n_ctx_blocks - 1), 0)),
             pl.BlockSpec((tm, d), lambda i: (lat_base + jnp.maximum(i - n_ctx_blocks, 0), 0))]
    return specs, [x_ctx, x_ctx if x_lat is None else x_lat]


def _qkv_kernel(*refs, qk_norm, tn, halves, n_ctx_blocks, state_slot):
    xc_ref, xl_ref, g_ref, sh_ref, sc_ref, w_ref, cos_ref, sin_ref, hg_ref = refs[:9]
    o_ref, ks_ref, vs_ref = refs[-3:]
    tm = xc_ref.shape[0]
    rh = tm // halves
    heads_per_tile = tn // HEAD_DIM
    kcol, vcol = N_HEADS * HEAD_DIM, (N_HEADS + N_KV_HEADS) * HEAD_DIM

    def body(x_ref, latent):
        for hf in range(halves):
            rs = slice(hf * rh, (hf + 1) * rh)
            h = _norm_mod(x_ref[rs, :], g_ref[...], sh_ref[0], sc_ref[0]).astype(BF16)
            for jt in range(w_ref.shape[1] // tn):
                acc = jnp.dot(h, w_ref[:, jt * tn:(jt + 1) * tn], preferred_element_type=F32)
                for hh in range(heads_per_tile):
                    head = jt * heads_per_tile + hh
                    a = acc[:, hh * HEAD_DIM:(hh + 1) * HEAD_DIM]
                    if head < N_HEADS + N_KV_HEADS:
                        if qk_norm:
                            gain = hg_ref[0:1, :] if head < N_HEADS else hg_ref[1:2, :]
                            a = a * lax.rsqrt(jnp.mean(a * a, axis=-1, keepdims=True) + EPS) * gain
                        if latent:
                            a = _rope_rotate(a, cos_ref[rs, :], sin_ref[rs, :])
                    o_ref[rs, head * HEAD_DIM:(head + 1) * HEAD_DIM] = a
        if not latent:
            seqs, ctx_len = ks_ref.shape[0], ks_ref.shape[-3]
            for ref, col in ((ks_ref, kcol), (vs_ref, vcol)):
                if state_slot is not None:
                    for other in range(ref.shape[1]):
                        if other != state_slot:
                            ref[:, other] = jnp.zeros(ref.shape[:1] + ref.shape[2:], F32)
                for s in range(seqs):
                    for kvh in range(N_KV_HEADS):
                        val = o_ref[s * ctx_len:(s + 1) * ctx_len, col + kvh * HEAD_DIM:col + (kvh + 1) * HEAD_DIM]
                        if state_slot is None:
                            ref[s, :, kvh, :] = val
                        else:
                            ref[s, state_slot, :, kvh, :] = val

    i = pl.program_id(0)
    pl.when(i < n_ctx_blocks)(lambda: body(xc_ref, False))
    pl.when(i >= n_ctx_blocks)(lambda: body(xl_ref, True))


def _qkv_project(x_ctx, x_lat, g, mod, w, layer, cos, sin, hg, state, *, n_ctx_rows, lat_len, ctx_len, tm):
    d = x_ctx.shape[1]
    seqs = tm // ctx_len
    n_layers = w.shape[0]
    state_shape = (n_ctx_rows // ctx_len, n_layers, ctx_len, N_KV_HEADS, HEAD_DIM)
    last_ctx = n_ctx_rows // tm - 1
    if state is None:
        state_spec = pl.BlockSpec((seqs, n_layers, ctx_len, N_KV_HEADS, HEAD_DIM),
                                  lambda i: (jnp.minimum(i, last_ctx), 0, 0, 0, 0))
        state_specs, state_args, aliases = [], [], {}
    else:
        state_spec = pl.BlockSpec((seqs, None, ctx_len, N_KV_HEADS, HEAD_DIM),
                                  lambda i: (jnp.minimum(i, last_ctx), layer, 0, 0, 0))
        state_specs, state_args = [pl.BlockSpec(memory_space=pl.ANY)] * 2, list(state)
        aliases = {9: 1, 10: 2}
    n = w.shape[2]
    ncb = n_ctx_rows // tm
    t = n_ctx_rows + (x_ctx.shape[0] - n_ctx_rows if x_lat is None else x_lat.shape[0])
    row = functools.partial(_mod_row, tm=tm, n_ctx_rows=n_ctx_rows, lat_len=lat_len)
    tab = lambda i: ((jnp.maximum(i - ncb, 0)) % (lat_len // tm), 0)
    qk_norm = hg is not None
    if hg is None:
        hg = jnp.ones((2, HEAD_DIM), F32)
    x_specs, x_args = _two_source_specs(x_ctx, x_lat, tm, ncb)
    return pl.pallas_call(
        functools.partial(_qkv_kernel, qk_norm=qk_norm, tn=COL_TILE, halves=2, n_ctx_blocks=ncb,
                          state_slot=layer if state is None else None),
        grid=(t // tm,),
        in_specs=x_specs + [
            pl.BlockSpec((1, d), lambda i: (0, 0)),
            pl.BlockSpec((1, 1, d), lambda i: (row(i) * 6, 0, 0)),
            pl.BlockSpec((1, 1, d), lambda i: (row(i) * 6 + 1, 0, 0)),
            pl.BlockSpec((None, d, n), lambda i: (layer, 0, 0), pipeline_mode=pl.Buffered(1)),
            pl.BlockSpec((tm, HEAD_DIM), tab), pl.BlockSpec((tm, HEAD_DIM), tab),
            pl.BlockSpec((2, HEAD_DIM), lambda i: (0, 0))] + state_specs,
        out_specs=[pl.BlockSpec((tm, n), lambda i: (i, 0)), state_spec, state_spec],
        out_shape=[jax.ShapeDtypeStruct((t, n), F32), jax.ShapeDtypeStruct(state_shape, F32),
                   jax.ShapeDtypeStruct(state_shape, F32)],
        input_output_aliases=aliases,
        compiler_params=_params("arbitrary"),
        name="qkv_project",
    )(*x_args, g.reshape(1, d), mod, mod, w, cos, sin, hg, *state_args)


def _proj_res_kernel(ac_ref, al_ref, xc_ref, xl_ref, w_ref, gate_ref, g2_ref, sh2_ref, sc2_ref, o_ref, h_ref, *,
                     n_ctx_blocks):
    i = pl.program_id(0)

    def emit(a_ref, x_ref):
        xn = x_ref[...] + gate_ref[0] * jnp.dot(a_ref[...], w_ref[...], preferred_element_type=F32)
        o_ref[...] = xn
        h_ref[...] = _norm_mod(xn, g2_ref[...], sh2_ref[0], sc2_ref[0]).astype(BF16)

    pl.when(i < n_ctx_blocks)(lambda: emit(ac_ref, xc_ref))
    pl.when(i >= n_ctx_blocks)(lambda: emit(al_ref, xl_ref))


def _proj_residual(a_ctx, a_lat, w, layer, x_ctx, x_lat, mod, g2, *, n_ctx_rows, lat_len, tm):
    d = x_ctx.shape[1]
    k = w.shape[1]
    ncb = n_ctx_rows // tm
    t = a_ctx.shape[0] + a_lat.shape[0]
    row = functools.partial(_mod_row, tm=tm, n_ctx_rows=n_ctx_rows, lat_len=lat_len)
    modspec = lambda which: pl.BlockSpec((1, 1, d), lambda i: (row(i) * 6 + which, 0, 0))
    x_specs, x_args = _two_source_specs(x_ctx, x_lat, tm, ncb)
    return pl.pallas_call(
        functools.partial(_proj_res_kernel, n_ctx_blocks=ncb),
        grid=(t // tm,),
        in_specs=[pl.BlockSpec((tm, k), lambda i: (jnp.minimum(i, ncb - 1), 0)),
                  pl.BlockSpec((tm, k), lambda i: (jnp.maximum(i - ncb, 0), 0))]
        + x_specs
        + [pl.BlockSpec((None, k, d), lambda i: (layer, 0, 0), pipeline_mode=pl.Buffered(1)),
           modspec(2),
           pl.BlockSpec((1, d), lambda i: (0, 0)),
           modspec(3), modspec(4)],
        out_specs=[pl.BlockSpec((tm, d), lambda i: (i, 0)), pl.BlockSpec((tm, d), lambda i: (i, 0))],
        out_shape=[jax.ShapeDtypeStruct((t, d), F32), jax.ShapeDtypeStruct((t, d), BF16)],
        compiler_params=_params("parallel"),
        name="proj_residual",
    )(a_ctx, a_lat, *x_args, w, mod, g2.reshape(1, d), mod, mod)


def _ffn_kernel(x_ref, h_ref, gate_ref, wg_ref, wu_ref, wd_ref, o_ref):
    f = pl.program_id(1)

    @pl.when(f == 0)
    def _():
        o_ref[...] = jnp.zeros_like(o_ref)

    h = h_ref[...]
    gv = jnp.dot(h, wg_ref[...], preferred_element_type=F32)
    uv = jnp.dot(h, wu_ref[...], preferred_element_type=F32)
    act = (_silu(gv) * uv).astype(BF16)
    o_ref[...] += jnp.dot(act, wd_ref[...], preferred_element_type=F32)

    @pl.when(f == pl.num_programs(1) - 1)
    def _():
        o_ref[...] = x_ref[...] + gate_ref[0] * o_ref[...]


def _ffn(x, h, mod, w_gu, w_down, layer, *, n_ctx_rows, lat_len, tm, tf):
    t, d = x.shape
    ff = w_down.shape[1]
    nf = ff // tf
    row = functools.partial(_mod_row, tm=tm, n_ctx_rows=n_ctx_rows, lat_len=lat_len)
    return pl.pallas_call(
        _ffn_kernel,
        grid=(t // tm, nf),
        in_specs=[pl.BlockSpec((tm, d), lambda i, f: (i, 0)),
                  pl.BlockSpec((tm, d), lambda i, f: (i, 0)),
                  pl.BlockSpec((1, 1, d), lambda i, f: (row(i) * 6 + 5, 0, 0)),
                  pl.BlockSpec((None, d, tf), lambda i, f: (layer, 0, f)),
                  pl.BlockSpec((None, d, tf), lambda i, f: (layer, 0, nf + f)),
                  pl.BlockSpec((None, tf, d), lambda i, f: (layer, f, 0))],
        out_specs=pl.BlockSpec((tm, d), lambda i, f: (i, 0)),
        out_shape=jax.ShapeDtypeStruct((t, d), F32),
        compiler_params=_params("parallel", "arbitrary"),
        name="ffn_swiglu",
    )(x, h, mod, w_gu, w_gu, w_down)


def _stack_heads(q):
    return jnp.concatenate([q[:, h * HEAD_DIM:(h + 1) * HEAD_DIM] for h in range(GROUP)], axis=0)


def _unstack_heads(o, rows):
    return jnp.concatenate([o[h * rows:(h + 1) * rows] for h in range(GROUP)], axis=1)


def _sink_column(sink_ref, kh, rows):
    head = lax.broadcasted_iota(jnp.int32, (GROUP * rows, 1), 0) // rows
    col = jnp.zeros((GROUP * rows, 1), F32)
    for h in range(GROUP):
        col = jnp.where(head == h, sink_ref[kh * GROUP + h], col)
    return col


def _softmax_av(s, v, sink_col):
    m = jnp.max(s, axis=-1, keepdims=True)
    if sink_col is not None:
        m = jnp.maximum(m, sink_col)
    p = jnp.exp(s - m)
    l = jnp.sum(p, axis=-1, keepdims=True)
    if sink_col is not None:
        l = l + jnp.exp(sink_col - m)
    return jnp.dot(p.astype(BF16), v, preferred_element_type=F32) / l


def _attn_ctx_kernel(sink_ref, q_ref, k_ref, v_ref, o_ref, *, use_sink):
    rows = q_ref.shape[0]
    qw = GROUP * HEAD_DIM
    for kh in range(N_KV_HEADS):
        qs = _stack_heads(q_ref[:, kh * qw:(kh + 1) * qw] * SCALE).astype(BF16)
        k = k_ref[:, kh * HEAD_DIM:(kh + 1) * HEAD_DIM].astype(BF16)
        v = v_ref[:, kh * HEAD_DIM:(kh + 1) * HEAD_DIM].astype(BF16)
        s = lax.dot_general(qs, k, NT_DIMS, preferred_element_type=F32)
        sink_col = _sink_column(sink_ref, kh, rows) if use_sink else None
        o_ref[:, kh * qw:(kh + 1) * qw] = _unstack_heads(_softmax_av(s, v, sink_col), rows).astype(BF16)


def _attn_ctx(y, sink, *, n_ctx_seqs, ctx_len, use_sink):
    t = n_ctx_seqs * ctx_len
    q_w, kv_w = N_HEADS * HEAD_DIM, N_KV_HEADS * HEAD_DIM
    return pl.pallas_call(
        functools.partial(_attn_ctx_kernel, use_sink=use_sink),
        grid=(n_ctx_seqs,),
        in_specs=[pl.BlockSpec(memory_space=pltpu.SMEM),
                  pl.BlockSpec((ctx_len, q_w), lambda b: (b, 0)),
                  pl.BlockSpec((ctx_len, kv_w), lambda b: (b, q_w // kv_w)),
                  pl.BlockSpec((ctx_len, kv_w), lambda b: (b, q_w // kv_w + 1))],
        out_specs=pl.BlockSpec((ctx_len, q_w), lambda b: (b, 0)),
        out_shape=jax.ShapeDtypeStruct((t, q_w), BF16),
        compiler_params=_params("parallel"),
        name="attn_context",
    )(sink, y, y, y)


def _keys_by_queries_attention(q_ref, k_scr, vt_scr, s_scr, chunks, bias_ref, sink_ref, o_ref):
    qn = q_ref.shape[0]
    kh = pl.program_id(1)

    def logits(h, ci):
        st, sz = chunks[ci]
        q = (q_ref[:, h * HEAD_DIM:(h + 1) * HEAD_DIM] * (SCALE * LOG2E)).astype(BF16)
        st_ = lax.dot_general(k_scr[st:st + sz, :], q, NT_DIMS, preferred_element_type=F32)
        if bias_ref is not None and ci == 0:
            st_ = st_ + bias_ref[...]
        s_scr[st:st + sz, h * qn:(h + 1) * qn] = st_
        return jnp.max(st_, axis=0, keepdims=True)

    def weigh(h, ci, m):
        st, sz = chunks[ci]
        p = jnp.exp2(s_scr[st:st + sz, h * qn:(h + 1) * qn] - m)
        pv = jnp.dot(vt_scr[:, st:st + sz], p.astype(BF16), preferred_element_type=F32)
        return jnp.sum(p, axis=0, keepdims=True), pv

    n = len(chunks)
    maxes = [logits(0, ci) for ci in range(n)]
    for h in range(GROUP):
        m = functools.reduce(jnp.maximum, maxes)
        if sink_ref is not None:
            sink2 = sink_ref[kh * GROUP + h] * LOG2E
            m = jnp.maximum(m, sink2)
        maxes, l, acc = [], None, None
        for ci in range(n):
            if h + 1 < GROUP:
                maxes.append(logits(h + 1, ci))
            lc, pv = weigh(h, ci, m)
            l, acc = (lc, pv) if l is None else (l + lc, acc + pv)
        if sink_ref is not None:
            l = l + jnp.exp2(sink2 - m)
        o_ref[:, h * HEAD_DIM:(h + 1) * HEAD_DIM] = (acc * (1.0 / l)).T.astype(BF16)


def _attn_lat_win_kernel(sink_ref, q_ref, kp_ref, kc_ref, kn_ref, vp_ref, vc_ref, vn_ref, ck_ref, cv_ref, bias_ref,
                         o_ref, k_scr, vt_scr, s_scr):
    past = ck_ref.shape[0]
    win = Q_STEP + 2 * BLOCK

    @pl.when(pl.program_id(2) == 0)
    def _():
        k_scr[win:, :] = ck_ref[...].astype(BF16)
        vt_scr[:, win:] = cv_ref[...].T.astype(BF16)

    row = 0
    for k_ref, v_ref in ((kp_ref, vp_ref), (kc_ref, vc_ref), (kn_ref, vn_ref)):
        n = k_ref.shape[0]
        k_scr[row:row + n, :] = k_ref[...].astype(BF16)
        vt_scr[:, row:row + n] = v_ref[...].T.astype(BF16)
        row += n

    kh, step, n_steps = pl.program_id(1), pl.program_id(2), pl.num_programs(2)
    span, lanes, nq = 3 * BLOCK, PAIR * BLOCK, Q_STEP // BLOCK
    groups = [(qb, p) for qb in range(nq) for p in range(GROUP // PAIR)]

    def logits(g):
        qb, p = groups[g]
        rows = slice(qb * BLOCK, (qb + 1) * BLOCK)
        q = jnp.concatenate([q_ref[rows, h * HEAD_DIM:(h + 1) * HEAD_DIM] for h in range(PAIR * p, PAIR * (p + 1))],
                            axis=0)
        q = (q * (SCALE * LOG2E)).astype(BF16)
        edge = 1
        if qb == 0:
            edge = jnp.where(step == 0, 0, 1)
        elif qb == nq - 1:
            edge = jnp.where(step == n_steps - 1, 2, 1)
        sw = lax.dot_general(k_scr[qb * BLOCK:qb * BLOCK + span, :], q, NT_DIMS,
                             preferred_element_type=F32) + bias_ref[edge]
        sc = lax.dot_general(k_scr[win:, :], q, NT_DIMS, preferred_element_type=F32)
        s_scr[0:span, g * lanes:(g + 1) * lanes] = sw
        s_scr[span:, g * lanes:(g + 1) * lanes] = sc
        return jnp.maximum(jnp.max(sw, axis=0, keepdims=True), jnp.max(sc, axis=0, keepdims=True))

    head = lax.broadcasted_iota(jnp.int32, (1, lanes), 1) // BLOCK
    m_next = logits(0)
    for g, (qb, p) in enumerate(groups):
        sink2 = jnp.zeros((1, lanes), F32)
        for hh in range(PAIR):
            sink2 = jnp.where(head == hh, sink_ref[kh * GROUP + PAIR * p + hh] * LOG2E, sink2)
        m = jnp.maximum(m_next, sink2)
        if g + 1 < len(groups):
            m_next = logits(g + 1)
        pw = jnp.exp2(s_scr[0:span, g * lanes:(g + 1) * lanes] - m)
        pc = jnp.exp2(s_scr[span:, g * lanes:(g + 1) * lanes] - m)
        pv = (jnp.dot(vt_scr[:, qb * BLOCK:qb * BLOCK + span], pw.astype(BF16), preferred_element_type=F32)
              + jnp.dot(vt_scr[:, win:], pc.astype(BF16), preferred_element_type=F32))
        l = (jnp.sum(pw, axis=0, keepdims=True) + jnp.sum(pc, axis=0, keepdims=True) + jnp.exp2(sink2 - m))
        ot = pv * (1.0 / l)
        for hh in range(PAIR):
            col = (PAIR * p + hh) * HEAD_DIM
            o_ref[qb * BLOCK:(qb + 1) * BLOCK, col:col + HEAD_DIM] = ot[:, hh * BLOCK:(hh + 1) * BLOCK].T.astype(BF16)


def _window_bias():
    c = jnp.arange(3 * BLOCK, dtype=jnp.int32)[:, None]
    r = jnp.arange(PAIR * BLOCK, dtype=jnp.int32)[None, :] % BLOCK
    band = jnp.abs(r + BLOCK - c) <= WINDOW
    variants = [band & (c >= BLOCK), band, band & (c < 2 * BLOCK)]
    return jnp.where(jnp.stack(variants), 0.0, NEG).astype(F32)


def _cache_spec(cache, layer):
    return pl.BlockSpec((None, None, cache.shape[2], HEAD_DIM), lambda b, kh, qs: (b, layer, 0, kh))


def _attn_lat_win(y, sink, cache_k, cache_v, layer, *, n_ctx_rows, lat_batch, lat_len):
    qw = GROUP * HEAD_DIM
    nb = lat_len // BLOCK
    ns = lat_len // Q_STEP
    per = Q_STEP // BLOCK
    base = n_ctx_rows // BLOCK
    past = cache_k.shape[2]
    n_keys = Q_STEP + 2 * BLOCK + past

    def edge(col0, blk):
        return pl.BlockSpec((BLOCK, HEAD_DIM),
                            lambda b, kh, qs: (base + b * nb + jnp.clip(qs * per + blk, 0, nb - 1), col0 + kh))

    def centre(col0):
        return pl.BlockSpec((Q_STEP, HEAD_DIM), lambda b, kh, qs: (base // per + b * ns + qs, col0 + kh))

    kcol, vcol = N_HEADS, N_HEADS + N_KV_HEADS
    cache_spec = _cache_spec(cache_k, layer)
    return pl.pallas_call(
        _attn_lat_win_kernel,
        grid=(lat_batch, N_KV_HEADS, ns),
        in_specs=[pl.BlockSpec(memory_space=pltpu.SMEM),
                  pl.BlockSpec((Q_STEP, qw), lambda b, kh, qs: (base // per + b * ns + qs, kh)),
                  edge(kcol, -1), centre(kcol), edge(kcol, per),
                  edge(vcol, -1), centre(vcol), edge(vcol, per),
                  cache_spec, cache_spec,
                  pl.BlockSpec((3, 3 * BLOCK, PAIR * BLOCK), lambda b, kh, qs: (0, 0, 0))],
        out_specs=pl.BlockSpec((Q_STEP, qw), lambda b, kh, qs: (b * ns + qs, kh)),
        out_shape=jax.ShapeDtypeStruct((lat_batch * lat_len, N_HEADS * HEAD_DIM), BF16),
        scratch_shapes=[pltpu.VMEM((n_keys, HEAD_DIM), BF16), pltpu.VMEM((HEAD_DIM, n_keys), BF16),
                        pltpu.VMEM((3 * BLOCK + past, GROUP * Q_STEP), F32)],
        compiler_params=_params("parallel", "parallel", "arbitrary"),
        name="attn_latent_window",
    )(sink, y, y, y, y, y, y, y, cache_k, cache_v, _window_bias())


def _attn_lat_full_kernel(q_ref, k_ref, v_ref, ck_ref, cv_ref, o_ref, k_scr, vt_scr, s_scr, *, chunk):
    lat_len = k_ref.shape[0]
    n_keys = k_scr.shape[0]

    @pl.when(pl.program_id(2) == 0)
    def _():
        k_scr[0:lat_len, :] = k_ref[...].astype(BF16)
        k_scr[lat_len:, :] = ck_ref[...].astype(BF16)
        vt_scr[:, 0:lat_len] = v_ref[...].T.astype(BF16)
        vt_scr[:, lat_len:] = cv_ref[...].T.astype(BF16)

    chunks = tuple((st, chunk) for st in range(0, n_keys, chunk))
    _keys_by_queries_attention(q_ref, k_scr, vt_scr, s_scr, chunks, None, None, o_ref)


def _attn_lat_full(y, cache_k, cache_v, layer, *, n_ctx_rows, lat_batch, lat_len):
    qw = GROUP * HEAD_DIM
    ns = lat_len // Q_STEP
    base = n_ctx_rows // Q_STEP
    seq_base = n_ctx_rows // lat_len
    past = cache_k.shape[2]
    cache_spec = _cache_spec(cache_k, layer)
    n_keys = lat_len + past
    chunk = 512
    assert n_keys % chunk == 0
    return pl.pallas_call(
        functools.partial(_attn_lat_full_kernel, chunk=chunk),
        grid=(lat_batch, N_KV_HEADS, ns),
        in_specs=[pl.BlockSpec((Q_STEP, qw), lambda b, kh, qs: (base + b * ns + qs, kh)),
                  pl.BlockSpec((lat_len, HEAD_DIM), lambda b, kh, qs: (seq_base + b, N_HEADS + kh)),
                  pl.BlockSpec((lat_len, HEAD_DIM), lambda b, kh, qs: (seq_base + b, N_HEADS + N_KV_HEADS + kh)),
                  cache_spec, cache_spec],
        out_specs=pl.BlockSpec((Q_STEP, qw), lambda b, kh, qs: (b * ns + qs, kh)),
        out_shape=jax.ShapeDtypeStruct((lat_batch * lat_len, N_HEADS * HEAD_DIM), BF16),
        scratch_shapes=[pltpu.VMEM((n_keys, HEAD_DIM), BF16), pltpu.VMEM((HEAD_DIM, n_keys), BF16),
                        pltpu.VMEM((n_keys, GROUP * Q_STEP), F32)],
        compiler_params=_params("parallel", "parallel", "arbitrary"),
        name="attn_latent_full",
    )(y, y, y, cache_k, cache_v)


def _rope_tables(lat_len):
    half = HEAD_DIM // 2
    n_rows = lat_len // GRID_W
    row = jnp.repeat(jnp.arange(n_rows, dtype=F32), GRID_W)
    col = jnp.tile(jnp.arange(GRID_W, dtype=F32), n_rows)
    inv = ROPE_THETA ** (-jnp.arange(0, half, 2, dtype=F32) / half)
    ang_r, ang_c = row[:, None] * inv, col[:, None] * inv
    cos = jnp.concatenate([jnp.cos(ang_r)] * 2 + [jnp.cos(ang_c)] * 2, axis=-1)
    sin = jnp.concatenate([-jnp.sin(ang_r), jnp.sin(ang_r), -jnp.sin(ang_c), jnp.sin(ang_c)], axis=-1)
    return cos, sin


def _dft_mats(seq_len, tk):
    n = 2 * seq_len
    k = jnp.arange(seq_len, dtype=jnp.int32)
    step = 64
    a = jnp.arange(seq_len // step, dtype=jnp.int32) * step
    b = jnp.arange(step, dtype=jnp.int32)
    ang_a = ((k[:, None] * a[None, :]) % n).astype(F32) * (2.0 * math.pi / n)
    ang_b = ((k[:, None] * b[None, :]) % n).astype(F32) * (2.0 * math.pi / n)
    ca, sa = jnp.cos(ang_a)[:, :, None], jnp.sin(ang_a)[:, :, None]
    cb, sb = jnp.cos(ang_b)[:, None, :], jnp.sin(ang_b)[:, None, :]
    fc = (ca * cb - sa * sb).reshape(seq_len, seq_len)
    fs = -(sa * cb + ca * sb).reshape(seq_len, seq_len)
    sign = jnp.where(k % 2 == 0, 1.0, -1.0).astype(F32)
    fs = jnp.where(k[:, None] == 0, sign[None, :], fs)
    kfc = seq_len // tk
    ana = jnp.concatenate([fc.reshape(kfc, tk, seq_len), fs.reshape(kfc, tk, seq_len)], axis=1).astype(BF16)
    ana = lax.optimization_barrier(ana)
    return ana, ana.transpose(0, 2, 1)


def _hy_filter_kernel(feat_ref, tn_ref, w1_ref, b1_ref, w2_ref, b2_ref, fq_ref, w3f_ref, w3b_ref, dl_ref, ana_ref,
                      h_ref, pq_scr, inv_scr, nyq_scr):
    kf = pl.program_id(1)
    hp = lax.Precision.HIGHEST
    tk = h_ref.shape[1]

    @pl.when(kf == 0)
    def _():
        def taps(g):
            a = jnp.dot(feat_ref[g], w1_ref[...], precision=hp, preferred_element_type=F32) + b1_ref[...]
            a = jnp.sin(fq_ref[0:1, :] * a)
            a = jnp.dot(a, w2_ref[...], precision=hp, preferred_element_type=F32) + b2_ref[...]
            a = jnp.sin(fq_ref[1:2, :] * a)
            decay = jnp.exp(-tn_ref[g] * dl_ref[...])
            a = a.astype(BF16)
            return (jnp.dot(a, w3f_ref[...].astype(BF16), preferred_element_type=F32) * decay,
                    jnp.dot(a, w3b_ref[...].astype(BF16), preferred_element_type=F32) * decay)

        fwd0, bwd0 = taps(0)
        fwd1, bwd1 = taps(1)
        fwdm, bwdm = taps(2)
        rowi = lax.broadcasted_iota(jnp.int32, fwd0.shape, 0)
        first = rowi == 0
        bwd0 = jnp.where(first, 0.0, bwd0)
        absum = lambda v: jnp.sum(jnp.abs(v), axis=0, keepdims=True)
        inv_scr[...] = 1.0 / (absum(fwd0) + absum(fwd1) + absum(bwd0) + absum(bwd1) + EPS)
        parts = ((fwd0, bwd0),
                 (bwdm, jnp.where(first, 0.0, bwd1)),
                 (fwd1, jnp.where(first, 0.0, fwdm)))
        for n, (pos, neg) in enumerate(parts):
            p = pos + neg
            nyq_scr[n:n + 1, :] = jnp.sum(jnp.where((rowi & 1) == 0, p, -p), axis=0, keepdims=True)
            pq_scr[2 * n] = p.astype(BF16)
            pq_scr[2 * n + 1] = (pos - neg).astype(BF16)

    inv = inv_scr[...]
    for n in range(3):
        h_ref[2 * n] = jnp.dot(ana_ref[0:tk, :], pq_scr[2 * n], preferred_element_type=F32) * inv
        hi = jnp.dot(ana_ref[tk:, :], pq_scr[2 * n + 1], preferred_element_type=F32)
        rowk = lax.broadcasted_iota(jnp.int32, hi.shape, 0)
        hi = jnp.where((rowk == 0) & (kf == 0), nyq_scr[n:n + 1, :], hi)
        h_ref[2 * n + 1] = hi * inv


def _hy_filters(seq_len, f_w1, f_b1, f_w2, f_b2, f_w3, freq, ana_mats, *, tc):
    half = seq_len // 2
    tk = ana_mats.shape[1] // 2
    hid = f_w1.shape[1]
    nd = f_w3.shape[1] // 2
    d = nd // HY_ORDER
    t = jnp.arange(seq_len, dtype=F32)
    tnorm = t / max(seq_len - 1, 1)
    bands = (HY_EMB - 1) // 2
    fb = jnp.linspace(1e-4, bands - 1, bands, dtype=F32)
    w = 2.0 * math.pi * t / seq_len
    feats = jnp.concatenate([tnorm[:, None], jnp.cos(w[:, None] * fb), -jnp.sin(w[:, None] * fb)], axis=-1)
    emb = 64
    feats = jnp.pad(feats, ((0, 0), (0, emb - HY_EMB)))
    w1 = jnp.pad(f_w1, ((0, emb - HY_EMB), (0, 0)))
    mirror = half - jnp.arange(half)
    feats3 = jnp.stack([feats[:half], feats[half:], feats[mirror]])
    tn3 = jnp.stack([tnorm[:half], tnorm[half:], tnorm[mirror]]).reshape(3, half, 1)
    deltas = jnp.abs(jnp.linspace(HY_MIN_DECAY, HY_MAX_DECAY, d, dtype=F32))
    deltas = jnp.tile(deltas, HY_ORDER).reshape(1, nd)
    nct = nd // tc
    const = lambda shape: pl.BlockSpec(shape, lambda c, kf: (0,) * len(shape))
    return pl.pallas_call(
        _hy_filter_kernel,
        grid=(nct, half // tk),
        in_specs=[const((3, half, emb)), const((3, half, 1)),
                  const((emb, hid)), const((1, hid)), const((hid, hid)), const((1, hid)), const((2, hid)),
                  pl.BlockSpec((hid, tc), lambda c, kf: (0, c)),
                  pl.BlockSpec((hid, tc), lambda c, kf: (0, nct + c)),
                  pl.BlockSpec((1, tc), lambda c, kf: (0, c)),
                  pl.BlockSpec((None, 2 * tk, half), lambda c, kf: (kf, 0, 0))],
        out_specs=pl.BlockSpec((6, tk, tc), lambda c, kf: (0, kf, c)),
        out_shape=jax.ShapeDtypeStruct((6, half, nd), F32),
        scratch_shapes=[pltpu.VMEM((6, half, tc), BF16), pltpu.VMEM((1, tc), F32), pltpu.VMEM((8, tc), F32)],
        compiler_params=_params("parallel", "arbitrary"),
        name="hyena_filter",
    )(feats3, tn3, w1, f_b1.reshape(1, hid), f_w2, f_b2.reshape(1, hid), freq, f_w3, f_w3, deltas, ana_mats)


def _hy_inproj_kernel(xp_ref, x_ref, xn_ref, g_ref, sh_ref, sc_ref, w_ref, cw_ref, cb_ref, o_ref, h_scr, y_scr, *,
                      n_ctx_blocks, ctx_len, lat_len, tn):
    i = pl.program_id(0)
    tm = x_ref.shape[0]
    norm = lambda ref: _norm_mod(ref[...], g_ref[...], sh_ref[0], sc_ref[0]).astype(BF16)
    h_scr[0:HALO, :] = norm(xp_ref)
    h_scr[HALO:HALO + tm, :] = norm(x_ref)
    h_scr[HALO + tm:, :] = norm(xn_ref)

    row = lax.broadcasted_iota(jnp.int32, (tm, tn), 0)
    is_ctx = i < n_ctx_blocks
    pos = jnp.where(is_ctx, row % ctx_len, ((i - n_ctx_blocks) * tm + row) % lat_len)
    first = pos == 0
    last = pos == jnp.where(is_ctx, ctx_len, lat_len) - 1
    for jt in range(w_ref.shape[1] // tn):
        cols = slice(jt * tn, (jt + 1) * tn)
        y = y_scr.at[jt % 2]
        y[...] = jnp.dot(h_scr[...], w_ref[:, cols], preferred_element_type=F32)
        prev = jnp.where(first, 0.0, y[HALO - 1:HALO - 1 + tm, :])
        nxt = jnp.where(last, 0.0, y[HALO + 1:HALO + 1 + tm, :])
        o_ref[:, cols] = (prev * cw_ref[0:1, cols] + y[HALO:HALO + tm, :] * cw_ref[1:2, cols]
                          + nxt * cw_ref[2:3, cols] + cb_ref[:, cols]).astype(BF16)


def _hy_inproj(x, g, mod, w, conv_w, conv_b, *, n_ctx_rows, lat_len, ctx_len, tm):
    t, d = x.shape
    n = w.shape[1]
    per = tm // HALO
    row = functools.partial(_mod_row, tm=tm, n_ctx_rows=n_ctx_rows, lat_len=lat_len)
    return pl.pallas_call(
        functools.partial(_hy_inproj_kernel, n_ctx_blocks=n_ctx_rows // tm, ctx_len=ctx_len, lat_len=lat_len,
                          tn=COL_TILE),
        grid=(t // tm,),
        in_specs=[pl.BlockSpec((HALO, d), lambda i: (jnp.maximum(i * per - 1, 0), 0)),
                  pl.BlockSpec((tm, d), lambda i: (i, 0)),
                  pl.BlockSpec((HALO, d), lambda i: (jnp.minimum((i + 1) * per, t // HALO - 1), 0)),
                  pl.BlockSpec((1, d), lambda i: (0, 0)),
                  pl.BlockSpec((1, 1, d), lambda i: (row(i) * 6, 0, 0)),
                  pl.BlockSpec((1, 1, d), lambda i: (row(i) * 6 + 1, 0, 0)),
                  pl.BlockSpec((d, n), lambda i: (0, 0), pipeline_mode=pl.Buffered(1)),
                  pl.BlockSpec((3, n), lambda i: (0, 0)),
                  pl.BlockSpec((1, n), lambda i: (0, 0))],
        out_specs=pl.BlockSpec((tm, n), lambda i: (i, 0)),
        out_shape=jax.ShapeDtypeStruct((t, n), BF16),
        scratch_shapes=[pltpu.VMEM((tm + 2 * HALO, d), BF16), pltpu.VMEM((2, tm + 2 * HALO, COL_TILE), F32)],
        compiler_params=_params("parallel"),
        name="hyena_inproj",
    )(x, x, x, g.reshape(1, d), mod, mod, w, conv_w, conv_b.reshape(1, n))


def _hy_conv_kernel(a_ref, g_ref, skip_ref, ana_ref, syn_ref, h_ref, o_ref, acc_scr, *, seq_len):
    kf = pl.program_id(2)
    rows = a_ref.shape[0]
    tk = h_ref.shape[1]
    half = seq_len // 2
    chunk = 256

    @pl.when(kf == 0)
    def _():
        acc_scr[...] = jnp.zeros_like(acc_scr)

    u_ref = a_ref
    first =(lax.broadcasted_iota(jnp.int32, (tk, a_ref.shape[1]), 0) == 0) & (kf == 0)

    def respond(uf, n):
        ur, ui = uf[0:tk, :], uf[tk:, :]
        hr, hi = h_ref[2 * n], h_ref[2 * n + 1]
        yr = jnp.where(first, 0.5 * (ur * hr), ur * hr - ui * hi)
        yi = jnp.where(first, 0.5 * (ui * hi), ur * hi + ui * hr)
        return yr, yi

    step = min(half, 512)
    for seg in range(0, rows, seq_len):
        uf1 = jnp.dot(ana_ref[...], u_ref[seg:seg + half, :], preferred_element_type=F32)
        uf2 = jnp.dot(ana_ref[...], u_ref[seg + half:seg + seq_len, :], preferred_element_type=F32)
        for dst, (n1, n2) in ((seg, (0, 1)), (seg + half, (2, 0))):
            y1r, y1i = respond(uf1, n1)
            y2r, y2i = respond(uf2, n2)
            yf = jnp.concatenate([y1r + y2r, y1i + y2i], axis=0).astype(BF16)
            for n0 in range(0, half, step):
                acc_scr[dst + n0:dst + n0 + step, :] += jnp.dot(syn_ref[n0:n0 + step, :], yf,
                                                                preferred_element_type=F32)

    @pl.when(kf == pl.num_programs(2) - 1)
    def _():
        for r0 in range(0, rows, chunk):
            rs = slice(r0, r0 + chunk)
            a = a_ref[rs, :].astype(F32)
            g = g_ref[rs, :].astype(F32)
            o_ref[rs, :] = (g * (acc_scr[rs, :] * (1.0 / half) + a * skip_ref[...])).astype(BF16)


def _hy_conv(a, a_part, a_base, g, g_part, g_base, skip_row, mats, h, order, *, n_blocks, rows, seq_len, tc):
    d = skip_row.shape[1]
    nct = d // tc
    ana_mats, syn_mats = mats
    kfc = ana_mats.shape[0]
    tk = ana_mats.shape[1] // 2

    def amap(part, base):
        return lambda b, c, kf: (base + b, part * nct + c)

    half = seq_len // 2
    return pl.pallas_call(
        functools.partial(_hy_conv_kernel, seq_len=seq_len),
        grid=(n_blocks, nct, kfc),
        in_specs=[pl.BlockSpec((rows, tc), amap(a_part, a_base)),
                  pl.BlockSpec((rows, tc), amap(g_part, g_base)),
                  pl.BlockSpec((1, tc), lambda b, c, kf: (0, c)),
                  pl.BlockSpec((None, 2 * tk, half), lambda b, c, kf: (kf, 0, 0)),
                  pl.BlockSpec((None, half, 2 * tk), lambda b, c, kf: (kf, 0, 0)),
                  pl.BlockSpec((6, tk, tc), lambda b, c, kf: (0, kf, order * nct + c))],
        out_specs=pl.BlockSpec((rows, tc), lambda b, c, kf: (b, c)),
        out_shape=jax.ShapeDtypeStruct((n_blocks * rows, d), BF16),
        scratch_shapes=[pltpu.VMEM((rows, tc), F32)],
        compiler_params=_params("parallel", "parallel", "arbitrary"),
        name="hyena_conv",
    )(a, g, skip_row, ana_mats, syn_mats, h)


def _final_norm_kernel(x_ref, g_ref, o_ref):
    x = x_ref[...]
    o_ref[...] = x * lax.rsqrt(jnp.mean(x * x, axis=-1, keepdims=True) + EPS) * g_ref[...]


def _final_norm(x, g, *, row0, rows, tm):
    d = x.shape[1]
    base = row0 // tm
    return pl.pallas_call(
        _final_norm_kernel,
        grid=(rows // tm,),
        in_specs=[pl.BlockSpec((tm, d), lambda i: (base + i, 0)),
                  pl.BlockSpec((1, d), lambda i: (0, 0))],
        out_specs=pl.BlockSpec((tm, d), lambda i: (i, 0)),
        out_shape=jax.ShapeDtypeStruct((rows, d), F32),
        compiler_params=_params("parallel"),
        name="final_norm",
    )(x, g.reshape(1, d))


def kernel(x_prompt, x_sample, cache_win_k, cache_win_v, cache_ax_k, cache_ax_v, c, c_ctx, norm_mix_g, norm_ffn_g, mod_w, mod_b, win_wqkv, win_wo, win_sink, hy_w_in, hy_conv_w, hy_conv_b, hy_f_w1, hy_f_b1, hy_f_w2, hy_f_b2, hy_f_w3, hy_freq, hy_skip, hy_wo, ax_wqkv, ax_q_g, ax_k_g, ax_wo, ffn_w_gu, ffn_w_down, final_g):
    n_ctx_seqs, ctx_len, d = x_prompt.shape
    lat_batch, lat_len, _ = x_sample.shape
    depth = mod_w.shape[0]
    past = cache_win_k.shape[2]
    n_ctx_rows = n_ctx_seqs * ctx_len
    n_lat_rows = lat_batch * lat_len
    tm = TOKEN_TILE
    assert n_ctx_rows % tm == 0 and lat_len % tm == 0 and n_ctx_rows % lat_len == 0
    assert lat_batch + 1 <= MOD_ROWS
    geom = dict(n_ctx_rows=n_ctx_rows, lat_len=lat_len, tm=tm)

    x, x_lat = x_prompt.reshape(n_ctx_rows, d), x_sample.reshape(n_lat_rows, d)
    cvec = jnp.concatenate([c_ctx[None, :], c, jnp.zeros((MOD_ROWS - 1 - lat_batch, d), F32)], axis=0)
    mod_all = _adaln_all(cvec, mod_w, mod_b)
    cos, sin = _rope_tables(lat_len)

    w_gu, w_down = ffn_w_gu.astype(BF16), ffn_w_down.astype(BF16)
    wqkv_all = {0: win_wqkv.astype(BF16), 2: ax_wqkv.astype(BF16)}
    wo_all = {0: win_wo.astype(BF16), 1: hy_wo.astype(BF16), 2: ax_wo.astype(BF16)}

    states = {0: None, 2: None}
    kv_w = N_KV_HEADS * HEAD_DIM
    for i in range(depth):
        mod = mod_all[i].reshape(MOD_ROWS * 6, 1, d)
        kind, j = i % 3, i // 3
        if kind == 1:
            assert x_lat is None
            u = _hy_inproj(x, norm_mix_g[i], mod, hy_w_in[j].astype(BF16), hy_conv_w[j], hy_conv_b[j],
                           ctx_len=ctx_len, **geom)
            rows = lat_len
            zs = []
            for (base, nblk, sl) in ((0, n_ctx_rows // rows, ctx_len), (n_ctx_rows // rows, lat_batch, lat_len)):
                mats = _dft_mats(sl // 2, min(COL_TILE, sl // 2))
                resp = _hy_filters(sl, hy_f_w1[j], hy_f_b1[j], hy_f_w2[j], hy_f_b2[j], hy_f_w3[j], hy_freq[j],
                                   mats[0], tc=COL_TILE)
                conv = functools.partial(_hy_conv, mats=mats, h=resp, n_blocks=nblk, rows=rows, seq_len=sl,
                                         tc=COL_TILE)
                z1 = conv(u, 0, base, u, 1, base, skip_row=hy_skip[j, 0].reshape(1, d), order=0)
                zs.append(conv(z1, 0, 0, u, 2, base, skip_row=hy_skip[j, 1].reshape(1, d), order=1))
            x, h2 = _proj_residual(zs[0], zs[1], wo_all[kind], j, x, None, mod, norm_ffn_g[i], **geom)
        else:
            if kind == 0:
                sink, hg = win_sink[j], None
                ck, cv = cache_win_k, cache_win_v
            else:
                sink = jnp.zeros((N_HEADS,), F32)
                hg = jnp.stack([ax_q_g[j], ax_k_g[j]], axis=0)
                ck, cv = cache_ax_k, cache_ax_v
            ck = ck[:, j:j + 1].reshape(lat_batch, 1, past, kv_w)
            cv = cv[:, j:j + 1].reshape(lat_batch, 1, past, kv_w)
            y, k_c, v_c = _qkv_project(x, x_lat, norm_mix_g[i], mod, wqkv_all[kind], j, cos, sin, hg, states[kind],
                                       ctx_len=ctx_len, **geom)
            states[kind] = (k_c, v_c)
            o_c = _attn_ctx(y, sink, n_ctx_seqs=n_ctx_seqs, ctx_len=ctx_len, use_sink=kind == 0)
            lat_geom = dict(n_ctx_rows=n_ctx_rows, lat_batch=lat_batch, lat_len=lat_len)
            if kind == 0:
                o_l = _attn_lat_win(y, sink, ck, cv, 0, **lat_geom)
            else:
                o_l = _attn_lat_full(y, ck, cv, 0, **lat_geom)
            x, h2 = _proj_residual(o_c, o_l, wo_all[kind], j, x, x_lat, mod, norm_ffn_g[i], **geom)
            x_lat = None
        x = _ffn(x, h2, mod, w_gu, w_down, i, tf=COL_TILE, **geom)

    y_prompt = _final_norm(x, final_g, row0=0, rows=n_ctx_rows, tm=tm).reshape(n_ctx_seqs, ctx_len, d)
    y_sample = _final_norm(x, final_g, row0=n_ctx_rows, rows=n_lat_rows, tm=tm).reshape(lat_batch, lat_len, d)
    return (y_prompt, y_sample) + states[0] + states[2]
```

```python
import functools
import math

import jax
import jax.numpy as jnp
from jax import lax
from jax.experimental import pallas as pl
from jax.experimental.pallas import tpu as pltpu

F32 = jnp.float32
BF16 = jnp.bfloat16

N_HEADS = 16
N_KV_HEADS = 4
GROUP = N_HEADS // N_KV_HEADS
HEAD_DIM = 128
WINDOW = 128
BLOCK = 128
GRID_W = 64
ROPE_THETA = 10000.0
HY_ORDER = 2
HY_EMB = 33
HY_TARGET = 1e-2
HY_FAST_DECAY = 0.3
HY_SLOW_DECAY = 1.5
HY_MIN_DECAY = math.log(HY_TARGET) / HY_SLOW_DECAY
HY_MAX_DECAY = math.log(HY_TARGET) / HY_FAST_DECAY
EPS = 1e-6
NEG = -1e30
SCALE = HEAD_DIM ** -0.5
LOG2E = math.log2(math.e)
PAIR = 2
Q_STEP = 4 * BLOCK
TOKEN_TILE = 512
COL_TILE = 512
HALO = 16
MOD_ROWS = 16
VMEM_LIMIT = 56 * 1024 * 1024

NT_DIMS = (((1,), (1,)), ((), ()))


def _params(*sem):
    return pltpu.CompilerParams(dimension_semantics=sem, vmem_limit_bytes=VMEM_LIMIT)


def _silu(x):
    return x * (1.0 / (1.0 + jnp.exp(-x)))


def _mod_row(i, tm, n_ctx_rows, lat_len):
    return jnp.maximum(i * tm - n_ctx_rows + lat_len, 0) // lat_len


def _mod_kernel(c_ref, w_ref, b_ref, o_ref):
    s = _silu(c_ref[...]).astype(BF16)
    o_ref[...] = jnp.dot(s, w_ref[0].astype(BF16), preferred_element_type=F32) + b_ref[0]


def _adaln_all(cvec, mod_w, mod_b):
    depth, d, n = mod_w.shape
    tn = 1024
    per = d // tn
    return pl.pallas_call(
        _mod_kernel,
        grid=(depth, n // tn),
        in_specs=[pl.BlockSpec((MOD_ROWS, d), lambda l, j: (0, 0)),
                  pl.BlockSpec((1, d, tn), lambda l, j: (l, 0, j)),
                  pl.BlockSpec((1, 1, tn), lambda l, j: (l, 0, j))],
        out_specs=pl.BlockSpec((None, MOD_ROWS, None, None, tn), lambda l, j: (l, 0, j // per, 0, j % per)),
        out_shape=jax.ShapeDtypeStruct((depth, MOD_ROWS, 6, 1, d), F32),
        compiler_params=_params("parallel", "parallel"),
        name="adaln_mod",
    )(cvec, mod_w, mod_b.reshape(depth, 1, n))


def _norm_mod(x, g, shift, scale):
    y = x * lax.rsqrt(jnp.mean(x * x, axis=-1, keepdims=True) + EPS) * g
    return y * (1.0 + scale) + shift


def _rope_rotate(a, cos, sin_signed):
    lane = lax.broadcasted_iota(jnp.int32, a.shape, 1)
    partner = jnp.where((lane & 32) == 0, pltpu.roll(a, 96, 1), pltpu.roll(a, 32, 1))
    return a * cos + partner * sin_signed


def _two_source_specs(x_ctx, x_lat, tm, n_ctx_blocks):
    d = x_ctx.shape[1]
    lat_base = n_ctx_blocks if x_lat is None else 0
    specs = [pl.BlockSpec((tm, d), lambda i: (jnp.minimum(i, n_ctx_blocks - 1), 0)),
             pl.BlockSpec((tm, d), lambda i: (lat_base + jnp.maximum(i - n_ctx_blocks, 0), 0))]
    return specs, [x_ctx, x_ctx if x_lat is None else x_lat]


def _qkv_kernel(*refs, qk_norm, tn, halves, n_ctx_blocks, state_slot):
    xc_ref, xl_ref, g_ref, sh_ref, sc_ref, w_ref, cos_ref, sin_ref, hg_ref = refs[:9]
    o_ref, ks_ref, vs_ref = refs[-3:]
    tm = xc_ref.shape[0]
    rh = tm // halves
    heads_per_tile = tn // HEAD_DIM
    kcol, vcol = N_HEADS * HEAD_DIM, (N_HEADS + N_KV_HEADS) * HEAD_DIM

    def body(x_ref, latent):
        for hf in range(halves):
            rs = slice(hf * rh, (hf + 1) * rh)
            h = _norm_mod(x_ref[rs, :], g_ref[...], sh_ref[0], sc_ref[0]).astype(BF16)
            for jt in range(w_ref.shape[1] // tn):
                acc = jnp.dot(h, w_ref[:, jt * tn:(jt + 1) * tn], preferred_element_type=F32)
                for hh in range(heads_per_tile):
                    head = jt * heads_per_tile + hh
                    a = acc[:, hh * HEAD_DIM:(hh + 1) * HEAD_DIM]
                    if head < N_HEADS + N_KV_HEADS:
                        if qk_norm:
                            gain = hg_ref[0:1, :] if head < N_HEADS else hg_ref[1:2, :]
                            a = a * lax.rsqrt(jnp.mean(a * a, axis=-1, keepdims=True) + EPS) * gain
                        if latent:
                            a = _rope_rotate(a, cos_ref[rs, :], sin_ref[rs, :])
                    o_ref[rs, head * HEAD_DIM:(head + 1) * HEAD_DIM] = a
        if not latent:
            seqs, ctx_len = ks_ref.shape[0], ks_ref.shape[-3]
            for ref, col in ((ks_ref, kcol), (vs_ref, vcol)):
                if state_slot is not None:
                    for other in range(ref.shape[1]):
                        if other != state_slot:
                            ref[:, other] = jnp.zeros(ref.shape[:1] + ref.shape[2:], F32)
                for s in range(seqs):
                    for kvh in range(N_KV_HEADS):
                        val = o_ref[s * ctx_len:(s + 1) * ctx_len, col + kvh * HEAD_DIM:col + (kvh + 1) * HEAD_DIM]
                        if state_slot is None:
                            ref[s, :, kvh, :] = val
                        else:
                            ref[s, state_slot, :, kvh, :] = val

    i = pl.program_id(0)
    pl.when(i < n_ctx_blocks)(lambda: body(xc_ref, False))
    pl.when(i >= n_ctx_blocks)(lambda: body(xl_ref, True))


def _qkv_project(x_ctx, x_lat, g, mod, w, layer, cos, sin, hg, state, *, n_ctx_rows, lat_len, ctx_len, tm):
    d = x_ctx.shape[1]
    seqs = tm // ctx_len
    n_layers = w.shape[0]
    state_shape = (n_ctx_rows // ctx_len, n_layers, ctx_len, N_KV_HEADS, HEAD_DIM)
    last_ctx = n_ctx_rows // tm - 1
    if state is None:
        state_spec = pl.BlockSpec((seqs, n_layers, ctx_len, N_KV_HEADS, HEAD_DIM),
                                  lambda i: (jnp.minimum(i, last_ctx), 0, 0, 0, 0))
        state_specs, state_args, aliases = [], [], {}
    else:
        state_spec = pl.BlockSpec((seqs, None, ctx_len, N_KV_HEADS, HEAD_DIM),
                                  lambda i: (jnp.minimum(i, last_ctx), layer, 0, 0, 0))
        state_specs, state_args = [pl.BlockSpec(memory_space=pl.ANY)] * 2, list(state)
        aliases = {9: 1, 10: 2}
    n = w.shape[2]
    ncb = n_ctx_rows // tm
    t = n_ctx_rows + (x_ctx.shape[0] - n_ctx_rows if x_lat is None else x_lat.shape[0])
    row = functools.partial(_mod_row, tm=tm, n_ctx_rows=n_ctx_rows, lat_len=lat_len)
    tab = lambda i: ((jnp.maximum(i - ncb, 0)) % (lat_len // tm), 0)
    qk_norm = hg is not None
    if hg is None:
        hg = jnp.ones((2, HEAD_DIM), F32)
    x_specs, x_args = _two_source_specs(x_ctx, x_lat, tm, ncb)
    return pl.pallas_call(
        functools.partial(_qkv_kernel, qk_norm=qk_norm, tn=COL_TILE, halves=2, n_ctx_blocks=ncb,
                          state_slot=layer if state is None else None),
        grid=(t // tm,),
        in_specs=x_specs + [
            pl.BlockSpec((1, d), lambda i: (0, 0)),
            pl.BlockSpec((1, 1, d), lambda i: (row(i) * 6, 0, 0)),
            pl.BlockSpec((1, 1, d), lambda i: (row(i) * 6 + 1, 0, 0)),
            pl.BlockSpec((None, d, n), lambda i: (layer, 0, 0), pipeline_mode=pl.Buffered(1)),
            pl.BlockSpec((tm, HEAD_DIM), tab), pl.BlockSpec((tm, HEAD_DIM), tab),
            pl.BlockSpec((2, HEAD_DIM), lambda i: (0, 0))] + state_specs,
        out_specs=[pl.BlockSpec((tm, n), lambda i: (i, 0)), state_spec, state_spec],
        out_shape=[jax.ShapeDtypeStruct((t, n), F32), jax.ShapeDtypeStruct(state_shape, F32),
                   jax.ShapeDtypeStruct(state_shape, F32)],
        input_output_aliases=aliases,
        compiler_params=_params("arbitrary"),
        name="qkv_project",
    )(*x_args, g.reshape(1, d), mod, mod, w, cos, sin, hg, *state_args)


def _proj_res_kernel(ac_ref, al_ref, xc_ref, xl_ref, w_ref, gate_ref, g2_ref, sh2_ref, sc2_ref, o_ref, h_ref, *,
                     n_ctx_blocks):
    i = pl.program_id(0)

    def emit(a_ref, x_ref):
        xn = x_ref[...] + gate_ref[0] * jnp.dot(a_ref[...], w_ref[...], preferred_element_type=F32)
        o_ref[...] = xn
        h_ref[...] = _norm_mod(xn, g2_ref[...], sh2_ref[0], sc2_ref[0]).astype(BF16)

    pl.when(i < n_ctx_blocks)(lambda: emit(ac_ref, xc_ref))
    pl.when(i >= n_ctx_blocks)(lambda: emit(al_ref, xl_ref))


def _proj_residual(a_ctx, a_lat, w, layer, x_ctx, x_lat, mod, g2, *, n_ctx_rows, lat_len, tm):
    d = x_ctx.shape[1]
    k = w.shape[1]
    ncb = n_ctx_rows // tm
    t = a_ctx.shape[0] + a_lat.shape[0]
    row = functools.partial(_mod_row, tm=tm, n_ctx_rows=n_ctx_rows, lat_len=lat_len)
    modspec = lambda which: pl.BlockSpec((1, 1, d), lambda i: (row(i) * 6 + which, 0, 0))
    x_specs, x_args = _two_source_specs(x_ctx, x_lat, tm, ncb)
    return pl.pallas_call(
        functools.partial(_proj_res_kernel, n_ctx_blocks=ncb),
        grid=(t // tm,),
        in_specs=[pl.BlockSpec((tm, k), lambda i: (jnp.minimum(i, ncb - 1), 0)),
                  pl.BlockSpec((tm, k), lambda i: (jnp.maximum(i - ncb, 0), 0))]
        + x_specs
        + [pl.BlockSpec((None, k, d), lambda i: (layer, 0, 0), pipeline_mode=pl.Buffered(1)),
           modspec(2),
           pl.BlockSpec((1, d), lambda i: (0, 0)),
           modspec(3), modspec(4)],
        out_specs=[pl.BlockSpec((tm, d), lambda i: (i, 0)), pl.BlockSpec((tm, d), lambda i: (i, 0))],
        out_shape=[jax.ShapeDtypeStruct((t, d), F32), jax.ShapeDtypeStruct((t, d), BF16)],
        compiler_params=_params("parallel"),
        name="proj_residual",
    )(a_ctx, a_lat, *x_args, w, mod, g2.reshape(1, d), mod, mod)


def _ffn_kernel(x_ref, h_ref, gate_ref, wg_ref, wu_ref, wd_ref, o_ref):
    f = pl.program_id(1)

    @pl.when(f == 0)
    def _():
        o_ref[...] = jnp.zeros_like(o_ref)

    h = h_ref[...]
    gv = jnp.dot(h, wg_ref[...], preferred_element_type=F32)
    uv = jnp.dot(h, wu_ref[...], preferred_element_type=F32)
    act = (_silu(gv) * uv).astype(BF16)
    o_ref[...] += jnp.dot(act, wd_ref[...], preferred_element_type=F32)

    @pl.when(f == pl.num_programs(1) - 1)
    def _():
        o_ref[...] = x_ref[...] + gate_ref[0] * o_ref[...]


def _ffn(x, h, mod, w_gu, w_down, layer, *, n_ctx_rows, lat_len, tm, tf):
    t, d = x.shape
    ff = w_down.shape[1]
    nf = ff // tf
    row = functools.partial(_mod_row, tm=tm, n_ctx_rows=n_ctx_rows, lat_len=lat_len)
    return pl.pallas_call(
        _ffn_kernel,
        grid=(t // tm, nf),
        in_specs=[pl.BlockSpec((tm, d), lambda i, f: (i, 0)),
                  pl.BlockSpec((tm, d), lambda i, f: (i, 0)),
                  pl.BlockSpec((1, 1, d), lambda i, f: (row(i) * 6 + 5, 0, 0)),
                  pl.BlockSpec((None, d, tf), lambda i, f: (layer, 0, f)),
                  pl.BlockSpec((None, d, tf), lambda i, f: (layer, 0, nf + f)),
                  pl.BlockSpec((None, tf, d), lambda i, f: (layer, f, 0))],
        out_specs=pl.BlockSpec((tm, d), lambda i, f: (i, 0)),
        out_shape=jax.ShapeDtypeStruct((t, d), F32),
        compiler_params=_params("parallel", "arbitrary"),
        name="ffn_swiglu",
    )(x, h, mod, w_gu, w_gu, w_down)


def _stack_heads(q):
    return jnp.concatenate([q[:, h * HEAD_DIM:(h + 1) * HEAD_DIM] for h in range(GROUP)], axis=0)


def _unstack_heads(o, rows):
    return jnp.concatenate([o[h * rows:(h + 1) * rows] for h in range(GROUP)], axis=1)


def _sink_column(sink_ref, kh, rows):
    head = lax.broadcasted_iota(jnp.int32, (GROUP * rows, 1), 0) // rows
    col = jnp.zeros((GROUP * rows, 1), F32)
    for h in range(GROUP):
        col = jnp.where(head == h, sink_ref[kh * GROUP + h], col)
    return col


def _softmax_av(s, v, sink_col):
    m = jnp.max(s, axis=-1, keepdims=True)
    if sink_col is not None:
        m = jnp.maximum(m, sink_col)
    p = jnp.exp(s - m)
    l = jnp.sum(p, axis=-1, keepdims=True)
    if sink_col is not None:
        l = l + jnp.exp(sink_col - m)
    return jnp.dot(p.astype(BF16), v, preferred_element_type=F32) / l


def _attn_ctx_kernel(sink_ref, q_ref, k_ref, v_ref, o_ref, *, use_sink):
    rows = q_ref.shape[0]
    qw = GROUP * HEAD_DIM
    for kh in range(N_KV_HEADS):
        qs = _stack_heads(q_ref[:, kh * qw:(kh + 1) * qw] * SCALE).astype(BF16)
        k = k_ref[:, kh * HEAD_DIM:(kh + 1) * HEAD_DIM].astype(BF16)
        v = v_ref[:, kh * HEAD_DIM:(kh + 1) * HEAD_DIM].astype(BF16)
        s = lax.dot_general(qs, k, NT_DIMS, preferred_element_type=F32)
        sink_col = _sink_column(sink_ref, kh, rows) if use_sink else None
        o_ref[:, kh * qw:(kh + 1) * qw] = _unstack_heads(_softmax_av(s, v, sink_col), rows).astype(BF16)


def _attn_ctx(y, sink, *, n_ctx_seqs, ctx_len, use_sink):
    t = n_ctx_seqs * ctx_len
    q_w, kv_w = N_HEADS * HEAD_DIM, N_KV_HEADS * HEAD_DIM
    return pl.pallas_call(
        functools.partial(_attn_ctx_kernel, use_sink=use_sink),
        grid=(n_ctx_seqs,),
        in_specs=[pl.BlockSpec(memory_space=pltpu.SMEM),
                  pl.BlockSpec((ctx_len, q_w), lambda b: (b, 0)),
                  pl.BlockSpec((ctx_len, kv_w), lambda b: (b, q_w // kv_w)),
                  pl.BlockSpec((ctx_len, kv_w), lambda b: (b, q_w // kv_w + 1))],
        out_specs=pl.BlockSpec((ctx_len, q_w), lambda b: (b, 0)),
        out_shape=jax.ShapeDtypeStruct((t, q_w), BF16),
        compiler_params=_params("parallel"),
        name="attn_context",
    )(sink, y, y, y)


def _keys_by_queries_attention(q_ref, k_scr, vt_scr, s_scr, chunks, bias_ref, sink_ref, o_ref):
    qn = q_ref.shape[0]
    kh = pl.program_id(1)

    def logits(h, ci):
        st, sz = chunks[ci]
        q = (q_ref[:, h * HEAD_DIM:(h + 1) * HEAD_DIM] * (SCALE * LOG2E)).astype(BF16)
        st_ = lax.dot_general(k_scr[st:st + sz, :], q, NT_DIMS, preferred_element_type=F32)
        if bias_ref is not None and ci == 0:
            st_ = st_ + bias_ref[...]
        s_scr[st:st + sz, h * qn:(h + 1) * qn] = st_
        return jnp.max(st_, axis=0, keepdims=True)

    def weigh(h, ci, m):
        st, sz = chunks[ci]
        p = jnp.exp2(s_scr[st:st + sz, h * qn:(h + 1) * qn] - m)
        pv = jnp.dot(vt_scr[:, st:st + sz], p.astype(BF16), preferred_element_type=F32)
        return jnp.sum(p, axis=0, keepdims=True), pv

    n = len(chunks)
    maxes = [logits(0, ci) for ci in range(n)]
    for h in range(GROUP):
        m = functools.reduce(jnp.maximum, maxes)
        if sink_ref is not None:
            sink2 = sink_ref[kh * GROUP + h] * LOG2E
            m = jnp.maximum(m, sink2)
        maxes, l, acc = [], None, None
        for ci in range(n):
            if h + 1 < GROUP:
                maxes.append(logits(h + 1, ci))
            lc, pv = weigh(h, ci, m)
            l, acc = (lc, pv) if l is None else (l + lc, acc + pv)
        if sink_ref is not None:
            l = l + jnp.exp2(sink2 - m)
        o_ref[:, h * HEAD_DIM:(h + 1) * HEAD_DIM] = (acc * (1.0 / l)).T.astype(BF16)


def _attn_lat_win_kernel(sink_ref, q_ref, kp_ref, kc_ref, kn_ref, vp_ref, vc_ref, vn_ref, ck_ref, cv_ref, bias_ref,
                         o_ref, k_scr, vt_scr, s_scr):
    past = ck_ref.shape[0]
    win = Q_STEP + 2 * BLOCK

    @pl.when(pl.program_id(2) == 0)
    def _():
        k_scr[win:, :] = ck_ref[...].astype(BF16)
        vt_scr[:, win:] = cv_ref[...].T.astype(BF16)

    row = 0
    for k_ref, v_ref in ((kp_ref, vp_ref), (kc_ref, vc_ref), (kn_ref, vn_ref)):
        n = k_ref.shape[0]
        k_scr[row:row + n, :] = k_ref[...].astype(BF16)
        vt_scr[:, row:row + n] = v_ref[...].T.astype(BF16)
        row += n

    kh, step, n_steps = pl.program_id(1), pl.program_id(2), pl.num_programs(2)
    span, lanes, nq = 3 * BLOCK, PAIR * BLOCK, Q_STEP // BLOCK
    groups = [(qb, p) for qb in range(nq) for p in range(GROUP // PAIR)]

    def logits(g):
        qb, p = groups[g]
        rows = slice(qb * BLOCK, (qb + 1) * BLOCK)
        q = jnp.concatenate([q_ref[rows, h * HEAD_DIM:(h + 1) * HEAD_DIM] for h in range(PAIR * p, PAIR * (p + 1))],
                            axis=0)
        q = (q * (SCALE * LOG2E)).astype(BF16)
        edge = 1
        if qb == 0:
            edge = jnp.where(step == 0, 0, 1)
        elif qb == nq - 1:
            edge = jnp.where(step == n_steps - 1, 2, 1)
        sw = lax.dot_general(k_scr[qb * BLOCK:qb * BLOCK + span, :], q, NT_DIMS,
                             preferred_element_type=F32) + bias_ref[edge]
        sc = lax.dot_general(k_scr[win:, :], q, NT_DIMS, preferred_element_type=F32)
        s_scr[0:span, g * lanes:(g + 1) * lanes] = sw
        s_scr[span:, g * lanes:(g + 1) * lanes] = sc
        return jnp.maximum(jnp.max(sw, axis=0, keepdims=True), jnp.max(sc, axis=0, keepdims=True))

    head = lax.broadcasted_iota(jnp.int32, (1, lanes), 1) // BLOCK
    m_next = logits(0)
    for g, (qb, p) in enumerate(groups):
        sink2 = jnp.zeros((1, lanes), F32)
        for hh in range(PAIR):
            sink2 = jnp.where(head == hh, sink_ref[kh * GROUP + PAIR * p + hh] * LOG2E, sink2)
        m = jnp.maximum(m_next, sink2)
        if g + 1 < len(groups):
            m_next = logits(g + 1)
        pw = jnp.exp2(s_scr[0:span, g * lanes:(g + 1) * lanes] - m)
        pc = jnp.exp2(s_scr[span:, g * lanes:(g + 1) * lanes] - m)
        pv = (jnp.dot(vt_scr[:, qb * BLOCK:qb * BLOCK + span], pw.astype(BF16), preferred_element_type=F32)
              + jnp.dot(vt_scr[:, win:], pc.astype(BF16), preferred_element_type=F32))
        l = (jnp.sum(pw, axis=0, keepdims=True) + jnp.sum(pc, axis=0, keepdims=True) + jnp.exp2(sink2 - m))
        ot = pv * (1.0 / l)
        for hh in range(PAIR):
            col = (PAIR * p + hh) * HEAD_DIM
            o_ref[qb * BLOCK:(qb + 1) * BLOCK, col:col + HEAD_DIM] = ot[:, hh * BLOCK:(hh + 1) * BLOCK].T.astype(BF16)


def _window_bias():
    c = jnp.arange(3 * BLOCK, dtype=jnp.int32)[:, None]
    r = jnp.arange(PAIR * BLOCK, dtype=jnp.int32)[None, :] % BLOCK
    band = jnp.abs(r + BLOCK - c) <= WINDOW
    variants = [band & (c >= BLOCK), band, band & (c < 2 * BLOCK)]
    return jnp.where(jnp.stack(variants), 0.0, NEG).astype(F32)


def _cache_spec(cache, layer):
    return pl.BlockSpec((None, None, cache.shape[2], HEAD_DIM), lambda b, kh, qs: (b, layer, 0, kh))


def _attn_lat_win(y, sink, cache_k, cache_v, layer, *, n_ctx_rows, lat_batch, lat_len):
    qw = GROUP * HEAD_DIM
    nb = lat_len // BLOCK
    ns = lat_len // Q_STEP
    per = Q_STEP // BLOCK
    base = n_ctx_rows // BLOCK
    past = cache_k.shape[2]
    n_keys = Q_STEP + 2 * BLOCK + past

    def edge(col0, blk):
        return pl.BlockSpec((BLOCK, HEAD_DIM),
                            lambda b, kh, qs: (base + b * nb + jnp.clip(qs * per + blk, 0, nb - 1), col0 + kh))

    def centre(col0):
        return pl.BlockSpec((Q_STEP, HEAD_DIM), lambda b, kh, qs: (base // per + b * ns + qs, col0 + kh))

    kcol, vcol = N_HEADS, N_HEADS + N_KV_HEADS
    cache_spec = _cache_spec(cache_k, layer)
    return pl.pallas_call(
        _attn_lat_win_kernel,
        grid=(lat_batch, N_KV_HEADS, ns),
        in_specs=[pl.BlockSpec(memory_space=pltpu.SMEM),
                  pl.BlockSpec((Q_STEP, qw), lambda b, kh, qs: (base // per + b * ns + qs, kh)),
                  edge(kcol, -1), centre(kcol), edge(kcol, per),
                  edge(vcol, -1), centre(vcol), edge(vcol, per),
                  cache_spec, cache_spec,
                  pl.BlockSpec((3, 3 * BLOCK, PAIR * BLOCK), lambda b, kh, qs: (0, 0, 0))],
        out_specs=pl.BlockSpec((Q_STEP, qw), lambda b, kh, qs: (b * ns + qs, kh)),
        out_shape=jax.ShapeDtypeStruct((lat_batch * lat_len, N_HEADS * HEAD_DIM), BF16),
        scratch_shapes=[pltpu.VMEM((n_keys, HEAD_DIM), BF16), pltpu.VMEM((HEAD_DIM, n_keys), BF16),
                        pltpu.VMEM((3 * BLOCK + past, GROUP * Q_STEP), F32)],
        compiler_params=_params("parallel", "parallel", "arbitrary"),
        name="attn_latent_window",
    )(sink, y, y, y, y, y, y, y, cache_k, cache_v, _window_bias())


def _attn_lat_full_kernel(q_ref, k_ref, v_ref, ck_ref, cv_ref, o_ref, k_scr, vt_scr, s_scr, *, chunk):
    lat_len = k_ref.shape[0]
    n_keys = k_scr.shape[0]

    @pl.when(pl.program_id(2) == 0)
    def _():
        k_scr[0:lat_len, :] = k_ref[...].astype(BF16)
        k_scr[lat_len:, :] = ck_ref[...].astype(BF16)
        vt_scr[:, 0:lat_len] = v_ref[...].T.astype(BF16)
        vt_scr[:, lat_len:] = cv_ref[...].T.astype(BF16)

    chunks = tuple((st, chunk) for st in range(0, n_keys, chunk))
    _keys_by_queries_attention(q_ref, k_scr, vt_scr, s_scr, chunks, None, None, o_ref)


def _attn_lat_full(y, cache_k, cache_v, layer, *, n_ctx_rows, lat_batch, lat_len):
    qw = GROUP * HEAD_DIM
    ns = lat_len // Q_STEP
    base = n_ctx_rows // Q_STEP
    seq_base = n_ctx_rows // lat_len
    past = cache_k.shape[2]
    cache_spec = _cache_spec(cache_k, layer)
    n_keys = lat_len + past
    chunk = 512
    assert n_keys % chunk == 0
    return pl.pallas_call(
        functools.partial(_attn_lat_full_kernel, chunk=chunk),
        grid=(lat_batch, N_KV_HEADS, ns),
        in_specs=[pl.BlockSpec((Q_STEP, qw), lambda b, kh, qs: (base + b * ns + qs, kh)),
                  pl.BlockSpec((lat_len, HEAD_DIM), lambda b, kh, qs: (seq_base + b, N_HEADS + kh)),
                  pl.BlockSpec((lat_len, HEAD_DIM), lambda b, kh, qs: (seq_base + b, N_HEADS + N_KV_HEADS + kh)),
                  cache_spec, cache_spec],
        out_specs=pl.BlockSpec((Q_STEP, qw), lambda b, kh, qs: (b * ns + qs, kh)),
        out_shape=jax.ShapeDtypeStruct((lat_batch * lat_len, N_HEADS * HEAD_DIM), BF16),
        scratch_shapes=[pltpu.VMEM((n_keys, HEAD_DIM), BF16), pltpu.VMEM((HEAD_DIM, n_keys), BF16),
                        pltpu.VMEM((n_keys, GROUP * Q_STEP), F32)],
        compiler_params=_params("parallel", "parallel", "arbitrary"),
        name="attn_latent_full",
    )(y, y, y, cache_k, cache_v)


def _rope_tables(lat_len):
    half = HEAD_DIM // 2
    n_rows = lat_len // GRID_W
    row = jnp.repeat(jnp.arange(n_rows, dtype=F32), GRID_W)
    col = jnp.tile(jnp.arange(GRID_W, dtype=F32), n_rows)
    inv = ROPE_THETA ** (-jnp.arange(0, half, 2, dtype=F32) / half)
    ang_r, ang_c = row[:, None] * inv, col[:, None] * inv
    cos = jnp.concatenate([jnp.cos(ang_r)] * 2 + [jnp.cos(ang_c)] * 2, axis=-1)
    sin = jnp.concatenate([-jnp.sin(ang_r), jnp.sin(ang_r), -jnp.sin(ang_c), jnp.sin(ang_c)], axis=-1)
    return cos, sin


def _dft_mats(seq_len):
    n = 2 * seq_len
    k = jnp.arange(seq_len, dtype=jnp.int32)
    step = 64
    a = jnp.arange(seq_len // step, dtype=jnp.int32) * step
    b = jnp.arange(step, dtype=jnp.int32)
    ang_a = ((k[:, None] * a[None, :]) % n).astype(F32) * (2.0 * math.pi / n)
    ang_b = ((k[:, None] * b[None, :]) % n).astype(F32) * (2.0 * math.pi / n)
    ca, sa = jnp.cos(ang_a)[:, :, None], jnp.sin(ang_a)[:, :, None]
    cb, sb = jnp.cos(ang_b)[:, None, :], jnp.sin(ang_b)[:, None, :]
    fc = (ca * cb - sa * sb).reshape(seq_len, seq_len)
    fs = -(sa * cb + ca * sb).reshape(seq_len, seq_len)
    sign = jnp.where(k % 2 == 0, 1.0, -1.0).astype(F32)
    fs = jnp.where(k[:, None] == 0, sign[None, :], fs)
    ana = jnp.concatenate([fc, fs], axis=0).astype(BF16)
    ana = lax.optimization_barrier(ana)
    return ana, ana.T


def _hy_filter_kernel(feat_ref, tn_ref, w1_ref, b1_ref, w2_ref, b2_ref, fq_ref, w3f_ref, w3b_ref, dl_ref,
                      anar_ref, anai_ref, h_ref, pq_scr, inv_scr, nyq_scr):
    kf = pl.program_id(1)
    hp = lax.Precision.HIGHEST
    tk = h_ref.shape[1]

    @pl.when(kf == 0)
    def _():
        def taps(g):
            a = jnp.dot(feat_ref[g], w1_ref[...], precision=hp, preferred_element_type=F32) + b1_ref[...]
            a = jnp.sin(fq_ref[0:1, :] * a)
            a = jnp.dot(a, w2_ref[...], precision=hp, preferred_element_type=F32) + b2_ref[...]
            a = jnp.sin(fq_ref[1:2, :] * a)
            decay = jnp.exp(-tn_ref[g] * dl_ref[...])
            a = a.astype(BF16)
            return (jnp.dot(a, w3f_ref[...].astype(BF16), preferred_element_type=F32) * decay,
                    jnp.dot(a, w3b_ref[...].astype(BF16), preferred_element_type=F32) * decay)

        fwd0, bwd0 = taps(0)
        fwd1, bwd1 = taps(1)
        fwdm, bwdm = taps(2)
        rowi = lax.broadcasted_iota(jnp.int32, fwd0.shape, 0)
        first = rowi == 0
        bwd0 = jnp.where(first, 0.0, bwd0)
        absum = lambda v: jnp.sum(jnp.abs(v), axis=0, keepdims=True)
        inv_scr[...] = 1.0 / (absum(fwd0) + absum(fwd1) + absum(bwd0) + absum(bwd1) + EPS)
        parts = ((fwd0, bwd0),
                 (bwdm, jnp.where(first, 0.0, bwd1)),
                 (fwd1, jnp.where(first, 0.0, fwdm)))
        for n, (pos, neg) in enumerate(parts):
            p = pos + neg
            nyq_scr[n:n + 1, :] = jnp.sum(jnp.where((rowi & 1) == 0, p, -p), axis=0, keepdims=True)
            pq_scr[2 * n] = p.astype(BF16)
            pq_scr[2 * n + 1] = (pos - neg).astype(BF16)

    inv = inv_scr[...]
    for n in range(3):
        h_ref[2 * n] = jnp.dot(anar_ref[...], pq_scr[2 * n], preferred_element_type=F32) * inv
        hi = jnp.dot(anai_ref[...], pq_scr[2 * n + 1], preferred_element_type=F32)
        rowk = lax.broadcasted_iota(jnp.int32, hi.shape, 0)
        hi = jnp.where((rowk == 0) & (kf == 0), nyq_scr[n:n + 1, :], hi)
        h_ref[2 * n + 1] = hi * inv


def _hy_filters(seq_len, f_w1, f_b1, f_w2, f_b2, f_w3, freq, ana, *, tc, tk):
    half = seq_len // 2
    hid = f_w1.shape[1]
    nd = f_w3.shape[1] // 2
    d = nd // HY_ORDER
    t = jnp.arange(seq_len, dtype=F32)
    tnorm = t / max(seq_len - 1, 1)
    bands = (HY_EMB - 1) // 2
    fb = jnp.linspace(1e-4, bands - 1, bands, dtype=F32)
    w = 2.0 * math.pi * t / seq_len
    feats = jnp.concatenate([tnorm[:, None], jnp.cos(w[:, None] * fb), -jnp.sin(w[:, None] * fb)], axis=-1)
    emb = 64
    feats = jnp.pad(feats, ((0, 0), (0, emb - HY_EMB)))
    w1 = jnp.pad(f_w1, ((0, emb - HY_EMB), (0, 0)))
    mirror = half - jnp.arange(half)
    feats3 = jnp.stack([feats[:half], feats[half:], feats[mirror]])
    tn3 = jnp.stack([tnorm[:half], tnorm[half:], tnorm[mirror]]).reshape(3, half, 1)
    deltas = jnp.abs(jnp.linspace(HY_MIN_DECAY, HY_MAX_DECAY, d, dtype=F32))
    deltas = jnp.tile(deltas, HY_ORDER).reshape(1, nd)
    nct = nd // tc
    const = lambda shape: pl.BlockSpec(shape, lambda c, kf: (0,) * len(shape))
    return pl.pallas_call(
        _hy_filter_kernel,
        grid=(nct, half // tk),
        in_specs=[const((3, half, emb)), const((3, half, 1)),
                  const((emb, hid)), const((1, hid)), const((hid, hid)), const((1, hid)), const((2, hid)),
                  pl.BlockSpec((hid, tc), lambda c, kf: (0, c)),
                  pl.BlockSpec((hid, tc), lambda c, kf: (0, nct + c)),
                  pl.BlockSpec((1, tc), lambda c, kf: (0, c)),
                  pl.BlockSpec((tk, half), lambda c, kf: (kf, 0)),
                  pl.BlockSpec((tk, half), lambda c, kf: (half // tk + kf, 0))],
        out_specs=pl.BlockSpec((6, tk, tc), lambda c, kf: (0, kf, c)),
        out_shape=jax.ShapeDtypeStruct((6, half, nd), F32),
        scratch_shapes=[pltpu.VMEM((6, half, tc), BF16), pltpu.VMEM((1, tc), F32), pltpu.VMEM((8, tc), F32)],
        compiler_params=_params("parallel", "arbitrary"),
        name="hyena_filter",
    )(feats3, tn3, w1, f_b1.reshape(1, hid), f_w2, f_b2.reshape(1, hid), freq, f_w3, f_w3, deltas, ana, ana)


def _hy_inproj_kernel(xp_ref, x_ref, xn_ref, g_ref, sh_ref, sc_ref, w_ref, cw_ref, cb_ref, o_ref, h_scr, y_scr, *,
                      n_ctx_blocks, ctx_len, lat_len, tn):
    i = pl.program_id(0)
    tm = x_ref.shape[0]
    norm = lambda ref: _norm_mod(ref[...], g_ref[...], sh_ref[0], sc_ref[0]).astype(BF16)
    h_scr[0:HALO, :] = norm(xp_ref)
    h_scr[HALO:HALO + tm, :] = norm(x_ref)
    h_scr[HALO + tm:, :] = norm(xn_ref)

    row = lax.broadcasted_iota(jnp.int32, (tm, tn), 0)
    is_ctx = i < n_ctx_blocks
    pos = jnp.where(is_ctx, row % ctx_len, ((i - n_ctx_blocks) * tm + row) % lat_len)
    first = pos == 0
    last = pos == jnp.where(is_ctx, ctx_len, lat_len) - 1
    for jt in range(w_ref.shape[1] // tn):
        cols = slice(jt * tn, (jt + 1) * tn)
        y = y_scr.at[jt % 2]
        y[...] = jnp.dot(h_scr[...], w_ref[:, cols], preferred_element_type=F32)
        prev = jnp.where(first, 0.0, y[HALO - 1:HALO - 1 + tm, :])
        nxt = jnp.where(last, 0.0, y[HALO + 1:HALO + 1 + tm, :])
        o_ref[:, cols] = (prev * cw_ref[0:1, cols] + y[HALO:HALO + tm, :] * cw_ref[1:2, cols]
                          + nxt * cw_ref[2:3, cols] + cb_ref[:, cols]).astype(BF16)


def _hy_inproj(x, g, mod, w, conv_w, conv_b, *, n_ctx_rows, lat_len, ctx_len, tm):
    t, d = x.shape
    n = w.shape[1]
    per = tm // HALO
    row = functools.partial(_mod_row, tm=tm, n_ctx_rows=n_ctx_rows, lat_len=lat_len)
    return pl.pallas_call(
        functools.partial(_hy_inproj_kernel, n_ctx_blocks=n_ctx_rows // tm, ctx_len=ctx_len, lat_len=lat_len,
                          tn=COL_TILE),
        grid=(t // tm,),
        in_specs=[pl.BlockSpec((HALO, d), lambda i: (jnp.maximum(i * per - 1, 0), 0)),
                  pl.BlockSpec((tm, d), lambda i: (i, 0)),
                  pl.BlockSpec((HALO, d), lambda i: (jnp.minimum((i + 1) * per, t // HALO - 1), 0)),
                  pl.BlockSpec((1, d), lambda i: (0, 0)),
                  pl.BlockSpec((1, 1, d), lambda i: (row(i) * 6, 0, 0)),
                  pl.BlockSpec((1, 1, d), lambda i: (row(i) * 6 + 1, 0, 0)),
                  pl.BlockSpec((d, n), lambda i: (0, 0), pipeline_mode=pl.Buffered(1)),
                  pl.BlockSpec((3, n), lambda i: (0, 0)),
                  pl.BlockSpec((1, n), lambda i: (0, 0))],
        out_specs=pl.BlockSpec((tm, n), lambda i: (i, 0)),
        out_shape=jax.ShapeDtypeStruct((t, n), BF16),
        scratch_shapes=[pltpu.VMEM((tm + 2 * HALO, d), BF16), pltpu.VMEM((2, tm + 2 * HALO, COL_TILE), F32)],
        compiler_params=_params("parallel"),
        name="hyena_inproj",
    )(x, x, x, g.reshape(1, d), mod, mod, w, conv_w, conv_b.reshape(1, n))


def _hy_conv_kernel(a_ref, g_ref, skip_ref, ana_ref, syn_ref, h_ref, o_ref, *, seq_len):
    rows = a_ref.shape[0]
    half = seq_len // 2
    first = lax.broadcasted_iota(jnp.int32, (half, a_ref.shape[1]), 0) == 0

    def respond(uf, n):
        ur, ui = uf[0:half, :], uf[half:, :]
        hr, hi = h_ref[2 * n], h_ref[2 * n + 1]
        yr = jnp.where(first, 0.5 * (ur * hr), ur * hr - ui * hi)
        yi = jnp.where(first, 0.5 * (ui * hi), ur * hi + ui * hr)
        return yr, yi

    step = min(half, 512)
    for seg in range(0, rows, seq_len):
        uf1 = jnp.dot(ana_ref[...], a_ref[seg:seg + half, :], preferred_element_type=F32)
        uf2 = jnp.dot(ana_ref[...], a_ref[seg + half:seg + seq_len, :], preferred_element_type=F32)
        for dst, (n1, n2) in ((seg, (0, 1)), (seg + half, (2, 0))):
            y1r, y1i = respond(uf1, n1)
            y2r, y2i = respond(uf2, n2)
            yf = jnp.concatenate([y1r + y2r, y1i + y2i], axis=0).astype(BF16)
            for n0 in range(0, half, step):
                rs = slice(dst + n0, dst + n0 + step)
                y = jnp.dot(syn_ref[n0:n0 + step, :], yf, preferred_element_type=F32)
                a = a_ref[rs, :].astype(F32)
                g = g_ref[rs, :].astype(F32)
                o_ref[rs, :] = (g * (y * (1.0 / half) + a * skip_ref[...])).astype(BF16)


def _hy_conv(a, a_part, a_base, g, g_part, g_base, skip_row, mats, h, order, *, n_blocks, rows, seq_len, tc):
    d = skip_row.shape[1]
    nct = d // tc
    ana, syn = mats
    half = seq_len // 2

    def amap(part, base):
        return lambda c, b: (base + b, part * nct + c)

    once = dict(pipeline_mode=pl.Buffered(1))
    return pl.pallas_call(
        functools.partial(_hy_conv_kernel, seq_len=seq_len),
        grid=(nct, n_blocks),
        in_specs=[pl.BlockSpec((rows, tc), amap(a_part, a_base)),
                  pl.BlockSpec((rows, tc), amap(g_part, g_base)),
                  pl.BlockSpec((1, tc), lambda c, b: (0, c)),
                  pl.BlockSpec((2 * half, half), lambda c, b: (0, 0), **once),
                  pl.BlockSpec((half, 2 * half), lambda c, b: (0, 0), **once),
                  pl.BlockSpec((6, half, tc), lambda c, b: (0, 0, order * nct + c), **once)],
        out_specs=pl.BlockSpec((rows, tc), lambda c, b: (b, c)),
        out_shape=jax.ShapeDtypeStruct((n_blocks * rows, d), BF16),
        compiler_params=_params("parallel", "parallel"),
        name="hyena_conv",
    )(a, g, skip_row, ana, syn, h)


def _final_norm_kernel(x_ref, g_ref, o_ref):
    x = x_ref[...]
    o_ref[...] = x * lax.rsqrt(jnp.mean(x * x, axis=-1, keepdims=True) + EPS) * g_ref[...]


def _final_norm(x, g, *, row0, rows, tm):
    d = x.shape[1]
    base = row0 // tm
    return pl.pallas_call(
        _final_norm_kernel,
        grid=(rows // tm,),
        in_specs=[pl.BlockSpec((tm, d), lambda i: (base + i, 0)),
                  pl.BlockSpec((1, d), lambda i: (0, 0))],
        out_specs=pl.BlockSpec((tm, d), lambda i: (i, 0)),
        out_shape=jax.ShapeDtypeStruct((rows, d), F32),
        compiler_params=_params("parallel"),
        name="final_norm",
    )(x, g.reshape(1, d))


def kernel(x_prompt, x_sample, cache_win_k, cache_win_v, cache_ax_k, cache_ax_v, c, c_ctx, norm_mix_g, norm_ffn_g, mod_w, mod_b, win_wqkv, win_wo, win_sink, hy_w_in, hy_conv_w, hy_conv_b, hy_f_w1, hy_f_b1, hy_f_w2, hy_f_b2, hy_f_w3, hy_freq, hy_skip, hy_wo, ax_wqkv, ax_q_g, ax_k_g, ax_wo, ffn_w_gu, ffn_w_down, final_g):
    n_ctx_seqs, ctx_len, d = x_prompt.shape
    lat_batch, lat_len, _ = x_sample.shape
    depth = mod_w.shape[0]
    past = cache_win_k.shape[2]
    n_ctx_rows = n_ctx_seqs * ctx_len
    n_lat_rows = lat_batch * lat_len
    tm = TOKEN_TILE
    assert n_ctx_rows % tm == 0 and lat_len % tm == 0 and n_ctx_rows % lat_len == 0
    assert lat_batch + 1 <= MOD_ROWS
    geom = dict(n_ctx_rows=n_ctx_rows, lat_len=lat_len, tm=tm)

    x, x_lat = x_prompt.reshape(n_ctx_rows, d), x_sample.reshape(n_lat_rows, d)
    cvec = jnp.concatenate([c_ctx[None, :], c, jnp.zeros((MOD_ROWS - 1 - lat_batch, d), F32)], axis=0)
    mod_all = _adaln_all(cvec, mod_w, mod_b)
    cos, sin = _rope_tables(lat_len)

    w_gu, w_down = ffn_w_gu.astype(BF16), ffn_w_down.astype(BF16)
    wqkv_all = {0: win_wqkv.astype(BF16), 2: ax_wqkv.astype(BF16)}
    wo_all = {0: win_wo.astype(BF16), 1: hy_wo.astype(BF16), 2: ax_wo.astype(BF16)}

    states = {0: None, 2: None}
    kv_w = N_KV_HEADS * HEAD_DIM
    for i in range(depth):
        mod = mod_all[i].reshape(MOD_ROWS * 6, 1, d)
        kind, j = i % 3, i // 3
        if kind == 1:
            assert x_lat is None
            u = _hy_inproj(x, norm_mix_g[i], mod, hy_w_in[j].astype(BF16), hy_conv_w[j], hy_conv_b[j],
                           ctx_len=ctx_len, **geom)
            rows = lat_len
            zs = []
            for (base, nblk, sl) in ((0, n_ctx_rows // rows, ctx_len), (n_ctx_rows // rows, lat_batch, lat_len)):
                mats = _dft_mats(sl // 2)
                resp = _hy_filters(sl, hy_f_w1[j], hy_f_b1[j], hy_f_w2[j], hy_f_b2[j], hy_f_w3[j], hy_freq[j],
                                   mats[0], tc=COL_TILE, tk=min(COL_TILE, sl // 2))
                conv = functools.partial(_hy_conv, mats=mats, h=resp, n_blocks=nblk, rows=rows, seq_len=sl,
                                         tc=COL_TILE)
                z1 = conv(u, 0, base, u, 1, base, skip_row=hy_skip[j, 0].reshape(1, d), order=0)
                zs.append(conv(z1, 0, 0, u, 2, base, skip_row=hy_skip[j, 1].reshape(1, d), order=1))
            x, h2 = _proj_residual(zs[0], zs[1], wo_all[kind], j, x, None, mod, norm_ffn_g[i], **geom)
        else:
            if kind == 0:
                sink, hg = win_sink[j], None
                ck, cv = cache_win_k, cache_win_v
            else:
                sink = jnp.zeros((N_HEADS,), F32)
                hg = jnp.stack([ax_q_g[j], ax_k_g[j]], axis=0)
                ck, cv = cache_ax_k, cache_ax_v
            ck = ck[:, j:j + 1].reshape(lat_batch, 1, past, kv_w)
            cv = cv[:, j:j + 1].reshape(lat_batch, 1, past, kv_w)
            y, k_c, v_c = _qkv_project(x, x_lat, norm_mix_g[i], mod, wqkv_all[kind], j, cos, sin, hg, states[kind],
                                       ctx_len=ctx_len, **geom)
            states[kind] = (k_c, v_c)
            o_c = _attn_ctx(y, sink, n_ctx_seqs=n_ctx_seqs, ctx_len=ctx_len, use_sink=kind == 0)
            lat_geom = dict(n_ctx_rows=n_ctx_rows, lat_batch=lat_batch, lat_len=lat_len)
            if kind == 0:
                o_l = _attn_lat_win(y, sink, ck, cv, 0, **lat_geom)
            else:
                o_l = _attn_lat_full(y, ck, cv, 0, **lat_geom)
            x, h2 = _proj_residual(o_c, o_l, wo_all[kind], j, x, x_lat, mod, norm_ffn_g[i], **geom)
            x_lat = None
        x = _ffn(x, h2, mod, w_gu, w_down, i, tf=COL_TILE, **geom)

    y_prompt = _final_norm(x, final_g, row0=0, rows=n_ctx_rows, tm=tm).reshape(n_ctx_seqs, ctx_len, d)
    y_sample = _final_norm(x, final_g, row0=n_ctx_rows, rows=n_lat_rows, tm=tm).reshape(lat_batch, lat_len, d)
    return (y_prompt, y_sample) + states[0] + states[2]
```

```python
import functools
import math

import jax
import jax.numpy as jnp
from jax import lax
from jax.experimental import pallas as pl
from jax.experimental.pallas import tpu as pltpu

F32 = jnp.float32
BF16 = jnp.bfloat16

N_HEADS = 16
N_KV_HEADS = 4
GROUP = N_HEADS // N_KV_HEADS
HEAD_DIM = 128
WINDOW = 128
BLOCK = 128
GRID_W = 64
ROPE_THETA = 10000.0
HY_ORDER = 2
HY_EMB = 33
HY_TARGET = 1e-2
HY_FAST_DECAY = 0.3
HY_SLOW_DECAY = 1.5
HY_MIN_DECAY = math.log(HY_TARGET) / HY_SLOW_DECAY
HY_MAX_DECAY = math.log(HY_TARGET) / HY_FAST_DECAY
EPS = 1e-6
NEG = -1e30
SCALE = HEAD_DIM ** -0.5
LOG2E = math.log2(math.e)
PAIR = 2
Q_STEP = 4 * BLOCK
TOKEN_TILE = 512
COL_TILE = 512
HALO = 16
MOD_ROWS = 16
VMEM_LIMIT = 56 * 1024 * 1024

NT_DIMS = (((1,), (1,)), ((), ()))


def _params(*sem):
    return pltpu.CompilerParams(dimension_semantics=sem, vmem_limit_bytes=VMEM_LIMIT)


def _silu(x):
    return x * (1.0 / (1.0 + jnp.exp(-x)))


def _mod_row(i, tm, n_ctx_rows, lat_len):
    return jnp.maximum(i * tm - n_ctx_rows + lat_len, 0) // lat_len


def _mod_kernel(c_ref, w_ref, b_ref, o_ref):
    s = _silu(c_ref[...]).astype(BF16)
    o_ref[...] = jnp.dot(s, w_ref[0].astype(BF16), preferred_element_type=F32) + b_ref[0]


def _adaln_all(cvec, mod_w, mod_b):
    depth, d, n = mod_w.shape
    tn = 1024
    per = d // tn
    return pl.pallas_call(
        _mod_kernel,
        grid=(depth, n // tn),
        in_specs=[pl.BlockSpec((MOD_ROWS, d), lambda l, j: (0, 0)),
                  pl.BlockSpec((1, d, tn), lambda l, j: (l, 0, j)),
                  pl.BlockSpec((1, 1, tn), lambda l, j: (l, 0, j))],
        out_specs=pl.BlockSpec((None, MOD_ROWS, None, None, tn), lambda l, j: (l, 0, j // per, 0, j % per)),
        out_shape=jax.ShapeDtypeStruct((depth, MOD_ROWS, 6, 1, d), F32),
        compiler_params=_params("parallel", "parallel"),
        name="adaln_mod",
    )(cvec, mod_w, mod_b.reshape(depth, 1, n))


def _norm_mod(x, g, shift, scale):
    y = x * lax.rsqrt(jnp.mean(x * x, axis=-1, keepdims=True) + EPS) * g
    return y * (1.0 + scale) + shift


def _rope_rotate(a, cos, sin_signed):
    lane = lax.broadcasted_iota(jnp.int32, a.shape, 1)
    partner = jnp.where((lane & 32) == 0, pltpu.roll(a, 96, 1), pltpu.roll(a, 32, 1))
    return a * cos + partner * sin_signed


def _two_source_specs(x_ctx, x_lat, tm, n_ctx_blocks):
    d = x_ctx.shape[1]
    lat_base = n_ctx_blocks if x_lat is None else 0
    specs = [pl.BlockSpec((tm, d), lambda i: (jnp.minimum(i, n_ctx_blocks - 1), 0)),
             pl.BlockSpec((tm, d), lambda i: (lat_base + jnp.maximum(i - n_ctx_blocks, 0), 0))]
    return specs, [x_ctx, x_ctx if x_lat is None else x_lat]


def _qkv_kernel(*refs, qk_norm, tn, halves, n_ctx_blocks, state_slot):
    xc_ref, xl_ref, g_ref, sh_ref, sc_ref, w_ref, cos_ref, sin_ref, hg_ref = refs[:9]
    o_ref, ks_ref, vs_ref = refs[-3:]
    tm = xc_ref.shape[0]
    rh = tm // halves
    heads_per_tile = tn // HEAD_DIM
    kcol, vcol = N_HEADS * HEAD_DIM, (N_HEADS + N_KV_HEADS) * HEAD_DIM

    def body(x_ref, latent):
        for hf in range(halves):
            rs = slice(hf * rh, (hf + 1) * rh)
            h = _norm_mod(x_ref[rs, :], g_ref[...], sh_ref[0], sc_ref[0]).astype(BF16)
            for jt in range(w_ref.shape[1] // tn):
                acc = jnp.dot(h, w_ref[:, jt * tn:(jt + 1) * tn], preferred_element_type=F32)
                for hh in range(heads_per_tile):
                    head = jt * heads_per_tile + hh
                    a = acc[:, hh * HEAD_DIM:(hh + 1) * HEAD_DIM]
                    if head < N_HEADS + N_KV_HEADS:
                        if qk_norm:
                            gain = hg_ref[0:1, :] if head < N_HEADS else hg_ref[1:2, :]
                            a = a * lax.rsqrt(jnp.mean(a * a, axis=-1, keepdims=True) + EPS) * gain
                        if latent:
                            a = _rope_rotate(a, cos_ref[rs, :], sin_ref[rs, :])
                    o_ref[rs, head * HEAD_DIM:(head + 1) * HEAD_DIM] = a
        if not latent:
            seqs, ctx_len = ks_ref.shape[0], ks_ref.shape[-3]
            for ref, col in ((ks_ref, kcol), (vs_ref, vcol)):
                if state_slot is not None:
                    for other in range(ref.shape[1]):
                        if other != state_slot:
                            ref[:, other] = jnp.zeros(ref.shape[:1] + ref.shape[2:], F32)
                for s in range(seqs):
                    for kvh in range(N_KV_HEADS):
                        val = o_ref[s * ctx_len:(s + 1) * ctx_len, col + kvh * HEAD_DIM:col + (kvh + 1) * HEAD_DIM]
                        if state_slot is None:
                            ref[s, :, kvh, :] = val
                        else:
                            ref[s, state_slot, :, kvh, :] = val

    i = pl.program_id(0)
    pl.when(i < n_ctx_blocks)(lambda: body(xc_ref, False))
    pl.when(i >= n_ctx_blocks)(lambda: body(xl_ref, True))


def _qkv_project(x_ctx, x_lat, g, mod, w, layer, cos, sin, hg, state, *, n_ctx_rows, lat_len, ctx_len, tm):
    d = x_ctx.shape[1]
    seqs = tm // ctx_len
    n_layers = w.shape[0]
    state_shape = (n_ctx_rows // ctx_len, n_layers, ctx_len, N_KV_HEADS, HEAD_DIM)
    last_ctx = n_ctx_rows // tm - 1
    if state is None:
        state_spec = pl.BlockSpec((seqs, n_layers, ctx_len, N_KV_HEADS, HEAD_DIM),
                                  lambda i: (jnp.minimum(i, last_ctx), 0, 0, 0, 0))
        state_specs, state_args, aliases = [], [], {}
    else:
        state_spec = pl.BlockSpec((seqs, None, ctx_len, N_KV_HEADS, HEAD_DIM),
                                  lambda i: (jnp.minimum(i, last_ctx), layer, 0, 0, 0))
        state_specs, state_args = [pl.BlockSpec(memory_space=pl.ANY)] * 2, list(state)
        aliases = {9: 1, 10: 2}
    n = w.shape[2]
    ncb = n_ctx_rows // tm
    t = n_ctx_rows + (x_ctx.shape[0] - n_ctx_rows if x_lat is None else x_lat.shape[0])
    row = functools.partial(_mod_row, tm=tm, n_ctx_rows=n_ctx_rows, lat_len=lat_len)
    tab = lambda i: ((jnp.maximum(i - ncb, 0)) % (lat_len // tm), 0)
    qk_norm = hg is not None
    if hg is None:
        hg = jnp.ones((2, HEAD_DIM), F32)
    x_specs, x_args = _two_source_specs(x_ctx, x_lat, tm, ncb)
    return pl.pallas_call(
        functools.partial(_qkv_kernel, qk_norm=qk_norm, tn=COL_TILE, halves=2, n_ctx_blocks=ncb,
                          state_slot=layer if state is None else None),
        grid=(t // tm,),
        in_specs=x_specs + [
            pl.BlockSpec((1, d), lambda i: (0, 0)),
            pl.BlockSpec((1, 1, d), lambda i: (row(i) * 6, 0, 0)),
            pl.BlockSpec((1, 1, d), lambda i: (row(i) * 6 + 1, 0, 0)),
            pl.BlockSpec((None, d, n), lambda i: (layer, 0, 0), pipeline_mode=pl.Buffered(1)),
            pl.BlockSpec((tm, HEAD_DIM), tab), pl.BlockSpec((tm, HEAD_DIM), tab),
            pl.BlockSpec((2, HEAD_DIM), lambda i: (0, 0))] + state_specs,
        out_specs=[pl.BlockSpec((tm, n), lambda i: (i, 0)), state_spec, state_spec],
        out_shape=[jax.ShapeDtypeStruct((t, n), F32), jax.ShapeDtypeStruct(state_shape, F32),
                   jax.ShapeDtypeStruct(state_shape, F32)],
        input_output_aliases=aliases,
        compiler_params=_params("arbitrary"),
        name="qkv_project",
    )(*x_args, g.reshape(1, d), mod, mod, w, cos, sin, hg, *state_args)


def _proj_res_kernel(ac_ref, al_ref, xc_ref, xl_ref, w_ref, gate_ref, g2_ref, sh2_ref, sc2_ref, o_ref, h_ref, *,
                     n_ctx_blocks):
    i = pl.program_id(0)

    def emit(a_ref, x_ref):
        xn = x_ref[...] + gate_ref[0] * jnp.dot(a_ref[...], w_ref[...], preferred_element_type=F32)
        o_ref[...] = xn
        h_ref[...] = _norm_mod(xn, g2_ref[...], sh2_ref[0], sc2_ref[0]).astype(BF16)

    pl.when(i < n_ctx_blocks)(lambda: emit(ac_ref, xc_ref))
    pl.when(i >= n_ctx_blocks)(lambda: emit(al_ref, xl_ref))


def _proj_residual(a_ctx, a_lat, w, layer, x_ctx, x_lat, mod, g2, *, n_ctx_rows, lat_len, tm):
    d = x_ctx.shape[1]
    k = w.shape[1]
    ncb = n_ctx_rows // tm
    t = a_ctx.shape[0] + a_lat.shape[0]
    row = functools.partial(_mod_row, tm=tm, n_ctx_rows=n_ctx_rows, lat_len=lat_len)
    modspec = lambda which: pl.BlockSpec((1, 1, d), lambda i: (row(i) * 6 + which, 0, 0))
    x_specs, x_args = _two_source_specs(x_ctx, x_lat, tm, ncb)
    return pl.pallas_call(
        functools.partial(_proj_res_kernel, n_ctx_blocks=ncb),
        grid=(t // tm,),
        in_specs=[pl.BlockSpec((tm, k), lambda i: (jnp.minimum(i, ncb - 1), 0)),
                  pl.BlockSpec((tm, k), lambda i: (jnp.maximum(i - ncb, 0), 0))]
        + x_specs
        + [pl.BlockSpec((None, k, d), lambda i: (layer, 0, 0), pipeline_mode=pl.Buffered(1)),
           modspec(2),
           pl.BlockSpec((1, d), lambda i: (0, 0)),
           modspec(3), modspec(4)],
        out_specs=[pl.BlockSpec((tm, d), lambda i: (i, 0)), pl.BlockSpec((tm, d), lambda i: (i, 0))],
        out_shape=[jax.ShapeDtypeStruct((t, d), F32), jax.ShapeDtypeStruct((t, d), BF16)],
        compiler_params=_params("parallel"),
        name="proj_residual",
    )(a_ctx, a_lat, *x_args, w, mod, g2.reshape(1, d), mod, mod)


def _ffn_kernel(x_ref, h_ref, gate_ref, wgu_ref, wd_ref, o_ref):
    f = pl.program_id(1)
    tf = wd_ref.shape[0]

    @pl.when(f == 0)
    def _():
        o_ref[...] = jnp.zeros_like(o_ref)

    gu = jnp.dot(h_ref[...], wgu_ref[...], preferred_element_type=F32)
    act = (_silu(gu[:, 0:tf]) * gu[:, tf:]).astype(BF16)
    o_ref[...] += jnp.dot(act, wd_ref[...], preferred_element_type=F32)

    @pl.when(f == pl.num_programs(1) - 1)
    def _():
        o_ref[...] = x_ref[...] + gate_ref[0] * o_ref[...]


def _ffn(x, h, mod, w_gu, w_down, layer, *, n_ctx_rows, lat_len, tm, tf):
    t, d = x.shape
    ff = w_down.shape[1]
    nf = ff // tf
    row = functools.partial(_mod_row, tm=tm, n_ctx_rows=n_ctx_rows, lat_len=lat_len)
    return pl.pallas_call(
        _ffn_kernel,
        grid=(t // tm, nf),
        in_specs=[pl.BlockSpec((tm, d), lambda i, f: (i, 0)),
                  pl.BlockSpec((tm, d), lambda i, f: (i, 0)),
                  pl.BlockSpec((1, 1, d), lambda i, f: (row(i) * 6 + 5, 0, 0)),
                  pl.BlockSpec((None, d, 2 * tf), lambda i, f: (layer, 0, f)),
                  pl.BlockSpec((None, tf, d), lambda i, f: (layer, f, 0))],
        out_specs=pl.BlockSpec((tm, d), lambda i, f: (i, 0)),
        out_shape=jax.ShapeDtypeStruct((t, d), F32),
        compiler_params=_params("parallel", "arbitrary"),
        name="ffn_swiglu",
    )(x, h, mod, w_gu, w_down)


def _stack_heads(q):
    return jnp.concatenate([q[:, h * HEAD_DIM:(h + 1) * HEAD_DIM] for h in range(GROUP)], axis=0)


def _unstack_heads(o, rows):
    return jnp.concatenate([o[h * rows:(h + 1) * rows] for h in range(GROUP)], axis=1)


def _sink_column(sink_ref, kh, rows):
    head = lax.broadcasted_iota(jnp.int32, (GROUP * rows, 1), 0) // rows
    col = jnp.zeros((GROUP * rows, 1), F32)
    for h in range(GROUP):
        col = jnp.where(head == h, sink_ref[kh * GROUP + h], col)
    return col


def _softmax_av(s, v, sink_col):
    m = jnp.max(s, axis=-1, keepdims=True)
    if sink_col is not None:
        m = jnp.maximum(m, sink_col)
    p = jnp.exp(s - m)
    l = jnp.sum(p, axis=-1, keepdims=True)
    if sink_col is not None:
        l = l + jnp.exp(sink_col - m)
    return jnp.dot(p.astype(BF16), v, preferred_element_type=F32) / l


def _attn_ctx_kernel(sink_ref, q_ref, k_ref, v_ref, o_ref, *, use_sink):
    rows = q_ref.shape[0]
    qw = GROUP * HEAD_DIM
    for kh in range(N_KV_HEADS):
        qs = _stack_heads(q_ref[:, kh * qw:(kh + 1) * qw] * SCALE).astype(BF16)
        k = k_ref[:, kh * HEAD_DIM:(kh + 1) * HEAD_DIM].astype(BF16)
        v = v_ref[:, kh * HEAD_DIM:(kh + 1) * HEAD_DIM].astype(BF16)
        s = lax.dot_general(qs, k, NT_DIMS, preferred_element_type=F32)
        sink_col = _sink_column(sink_ref, kh, rows) if use_sink else None
        o_ref[:, kh * qw:(kh + 1) * qw] = _unstack_heads(_softmax_av(s, v, sink_col), rows).astype(BF16)


def _attn_ctx(y, sink, *, n_ctx_seqs, ctx_len, use_sink):
    t = n_ctx_seqs * ctx_len
    q_w, kv_w = N_HEADS * HEAD_DIM, N_KV_HEADS * HEAD_DIM
    return pl.pallas_call(
        functools.partial(_attn_ctx_kernel, use_sink=use_sink),
        grid=(n_ctx_seqs,),
        in_specs=[pl.BlockSpec(memory_space=pltpu.SMEM),
                  pl.BlockSpec((ctx_len, q_w), lambda b: (b, 0)),
                  pl.BlockSpec((ctx_len, kv_w), lambda b: (b, q_w // kv_w)),
                  pl.BlockSpec((ctx_len, kv_w), lambda b: (b, q_w // kv_w + 1))],
        out_specs=pl.BlockSpec((ctx_len, q_w), lambda b: (b, 0)),
        out_shape=jax.ShapeDtypeStruct((t, q_w), BF16),
        compiler_params=_params("parallel"),
        name="attn_context",
    )(sink, y, y, y)


def _keys_by_queries_attention(q_ref, k_scr, vt_scr, s_scr, chunks, bias_ref, sink_ref, o_ref):
    qn = q_ref.shape[0]
    kh = pl.program_id(1)

    def logits(h, ci):
        st, sz = chunks[ci]
        q = (q_ref[:, h * HEAD_DIM:(h + 1) * HEAD_DIM] * (SCALE * LOG2E)).astype(BF16)
        st_ = lax.dot_general(k_scr[st:st + sz, :], q, NT_DIMS, preferred_element_type=F32)
        if bias_ref is not None and ci == 0:
            st_ = st_ + bias_ref[...]
        s_scr[st:st + sz, h * qn:(h + 1) * qn] = st_
        return jnp.max(st_, axis=0, keepdims=True)

    def weigh(h, ci, m):
        st, sz = chunks[ci]
        p = jnp.exp2(s_scr[st:st + sz, h * qn:(h + 1) * qn] - m)
        pv = jnp.dot(vt_scr[:, st:st + sz], p.astype(BF16), preferred_element_type=F32)
        return jnp.sum(p, axis=0, keepdims=True), pv

    n = len(chunks)
    maxes = [logits(0, ci) for ci in range(n)]
    for h in range(GROUP):
        m = functools.reduce(jnp.maximum, maxes)
        if sink_ref is not None:
            sink2 = sink_ref[kh * GROUP + h] * LOG2E
            m = jnp.maximum(m, sink2)
        maxes, l, acc = [], None, None
        for ci in range(n):
            if h + 1 < GROUP:
                maxes.append(logits(h + 1, ci))
            lc, pv = weigh(h, ci, m)
            l, acc = (lc, pv) if l is None else (l + lc, acc + pv)
        if sink_ref is not None:
            l = l + jnp.exp2(sink2 - m)
        o_ref[:, h * HEAD_DIM:(h + 1) * HEAD_DIM] = (acc * (1.0 / l)).T.astype(BF16)


def _attn_lat_win_kernel(sink_ref, q_ref, kp_ref, kc_ref, kn_ref, vp_ref, vc_ref, vn_ref, ck_ref, cv_ref, bias_ref,
                         o_ref, k_scr, vt_scr, s_scr):
    past = ck_ref.shape[0]
    win = Q_STEP + 2 * BLOCK

    @pl.when(pl.program_id(2) == 0)
    def _():
        k_scr[win:, :] = ck_ref[...].astype(BF16)
        vt_scr[:, win:] = cv_ref[...].T.astype(BF16)

    row = 0
    for k_ref, v_ref in ((kp_ref, vp_ref), (kc_ref, vc_ref), (kn_ref, vn_ref)):
        n = k_ref.shape[0]
        k_scr[row:row + n, :] = k_ref[...].astype(BF16)
        vt_scr[:, row:row + n] = v_ref[...].T.astype(BF16)
        row += n

    kh, step, n_steps = pl.program_id(1), pl.program_id(2), pl.num_programs(2)
    span, lanes, nq = 3 * BLOCK, PAIR * BLOCK, Q_STEP // BLOCK
    groups = [(qb, p) for qb in range(nq) for p in range(GROUP // PAIR)]

    def logits(g):
        qb, p = groups[g]
        rows = slice(qb * BLOCK, (qb + 1) * BLOCK)
        q = jnp.concatenate([q_ref[rows, h * HEAD_DIM:(h + 1) * HEAD_DIM] for h in range(PAIR * p, PAIR * (p + 1))],
                            axis=0)
        q = (q * (SCALE * LOG2E)).astype(BF16)
        edge = 1
        if qb == 0:
            edge = jnp.where(step == 0, 0, 1)
        elif qb == nq - 1:
            edge = jnp.where(step == n_steps - 1, 2, 1)
        sw = lax.dot_general(k_scr[qb * BLOCK:qb * BLOCK + span, :], q, NT_DIMS,
                             preferred_element_type=F32) + bias_ref[edge]
        sc = lax.dot_general(k_scr[win:, :], q, NT_DIMS, preferred_element_type=F32)
        s_scr[0:span, g * lanes:(g + 1) * lanes] = sw
        s_scr[span:, g * lanes:(g + 1) * lanes] = sc
        return jnp.maximum(jnp.max(sw, axis=0, keepdims=True), jnp.max(sc, axis=0, keepdims=True))

    head = lax.broadcasted_iota(jnp.int32, (1, lanes), 1) // BLOCK
    m_next = logits(0)
    for g, (qb, p) in enumerate(groups):
        sink2 = jnp.zeros((1, lanes), F32)
        for hh in range(PAIR):
            sink2 = jnp.where(head == hh, sink_ref[kh * GROUP + PAIR * p + hh] * LOG2E, sink2)
        m = jnp.maximum(m_next, sink2)
        if g + 1 < len(groups):
            m_next = logits(g + 1)
        pw = jnp.exp2(s_scr[0:span, g * lanes:(g + 1) * lanes] - m)
        pc = jnp.exp2(s_scr[span:, g * lanes:(g + 1) * lanes] - m)
        pv = (jnp.dot(vt_scr[:, qb * BLOCK:qb * BLOCK + span], pw.astype(BF16), preferred_element_type=F32)
              + jnp.dot(vt_scr[:, win:], pc.astype(BF16), preferred_element_type=F32))
        l = (jnp.sum(pw, axis=0, keepdims=True) + jnp.sum(pc, axis=0, keepdims=True) + jnp.exp2(sink2 - m))
        ot = pv * (1.0 / l)
        for hh in range(PAIR):
            col = (PAIR * p + hh) * HEAD_DIM
            o_ref[qb * BLOCK:(qb + 1) * BLOCK, col:col + HEAD_DIM] = ot[:, hh * BLOCK:(hh + 1) * BLOCK].T.astype(BF16)


def _window_bias():
    c = jnp.arange(3 * BLOCK, dtype=jnp.int32)[:, None]
    r = jnp.arange(PAIR * BLOCK, dtype=jnp.int32)[None, :] % BLOCK
    band = jnp.abs(r + BLOCK - c) <= WINDOW
    variants = [band & (c >= BLOCK), band, band & (c < 2 * BLOCK)]
    return jnp.where(jnp.stack(variants), 0.0, NEG).astype(F32)


def _cache_spec(cache, layer):
    return pl.BlockSpec((None, None, cache.shape[2], HEAD_DIM), lambda b, kh, qs: (b, layer, 0, kh))


def _attn_lat_win(y, sink, cache_k, cache_v, layer, *, n_ctx_rows, lat_batch, lat_len):
    qw = GROUP * HEAD_DIM
    nb = lat_len // BLOCK
    ns = lat_len // Q_STEP
    per = Q_STEP // BLOCK
    base = n_ctx_rows // BLOCK
    past = cache_k.shape[2]
    n_keys = Q_STEP + 2 * BLOCK + past

    def edge(col0, blk):
        return pl.BlockSpec((BLOCK, HEAD_DIM),
                            lambda b, kh, qs: (base + b * nb + jnp.clip(qs * per + blk, 0, nb - 1), col0 + kh))

    def centre(col0):
        return pl.BlockSpec((Q_STEP, HEAD_DIM), lambda b, kh, qs: (base // per + b * ns + qs, col0 + kh))

    kcol, vcol = N_HEADS, N_HEADS + N_KV_HEADS
    cache_spec = _cache_spec(cache_k, layer)
    return pl.pallas_call(
        _attn_lat_win_kernel,
        grid=(lat_batch, N_KV_HEADS, ns),
        in_specs=[pl.BlockSpec(memory_space=pltpu.SMEM),
                  pl.BlockSpec((Q_STEP, qw), lambda b, kh, qs: (base // per + b * ns + qs, kh)),
                  edge(kcol, -1), centre(kcol), edge(kcol, per),
                  edge(vcol, -1), centre(vcol), edge(vcol, per),
                  cache_spec, cache_spec,
                  pl.BlockSpec((3, 3 * BLOCK, PAIR * BLOCK), lambda b, kh, qs: (0, 0, 0))],
        out_specs=pl.BlockSpec((Q_STEP, qw), lambda b, kh, qs: (b * ns + qs, kh)),
        out_shape=jax.ShapeDtypeStruct((lat_batch * lat_len, N_HEADS * HEAD_DIM), BF16),
        scratch_shapes=[pltpu.VMEM((n_keys, HEAD_DIM), BF16), pltpu.VMEM((HEAD_DIM, n_keys), BF16),
                        pltpu.VMEM((3 * BLOCK + past, GROUP * Q_STEP), F32)],
        compiler_params=_params("parallel", "parallel", "arbitrary"),
        name="attn_latent_window",
    )(sink, y, y, y, y, y, y, y, cache_k, cache_v, _window_bias())


def _attn_lat_full_kernel(q_ref, k_ref, v_ref, ck_ref, cv_ref, o_ref, k_scr, vt_scr, s_scr, *, chunk):
    lat_len = k_ref.shape[0]
    n_keys = k_scr.shape[0]

    @pl.when(pl.program_id(2) == 0)
    def _():
        k_scr[0:lat_len, :] = k_ref[...].astype(BF16)
        k_scr[lat_len:, :] = ck_ref[...].astype(BF16)
        vt_scr[:, 0:lat_len] = v_ref[...].T.astype(BF16)
        vt_scr[:, lat_len:] = cv_ref[...].T.astype(BF16)

    chunks = tuple((st, chunk) for st in range(0, n_keys, chunk))
    _keys_by_queries_attention(q_ref, k_scr, vt_scr, s_scr, chunks, None, None, o_ref)


def _attn_lat_full(y, cache_k, cache_v, layer, *, n_ctx_rows, lat_batch, lat_len):
    qw = GROUP * HEAD_DIM
    ns = lat_len // Q_STEP
    base = n_ctx_rows // Q_STEP
    seq_base = n_ctx_rows // lat_len
    past = cache_k.shape[2]
    cache_spec = _cache_spec(cache_k, layer)
    n_keys = lat_len + past
    chunk = 512
    assert n_keys % chunk == 0
    return pl.pallas_call(
        functools.partial(_attn_lat_full_kernel, chunk=chunk),
        grid=(lat_batch, N_KV_HEADS, ns),
        in_specs=[pl.BlockSpec((Q_STEP, qw), lambda b, kh, qs: (base + b * ns + qs, kh)),
                  pl.BlockSpec((lat_len, HEAD_DIM), lambda b, kh, qs: (seq_base + b, N_HEADS + kh)),
                  pl.BlockSpec((lat_len, HEAD_DIM), lambda b, kh, qs: (seq_base + b, N_HEADS + N_KV_HEADS + kh)),
                  cache_spec, cache_spec],
        out_specs=pl.BlockSpec((Q_STEP, qw), lambda b, kh, qs: (b * ns + qs, kh)),
        out_shape=jax.ShapeDtypeStruct((lat_batch * lat_len, N_HEADS * HEAD_DIM), BF16),
        scratch_shapes=[pltpu.VMEM((n_keys, HEAD_DIM), BF16), pltpu.VMEM((HEAD_DIM, n_keys), BF16),
                        pltpu.VMEM((n_keys, GROUP * Q_STEP), F32)],
        compiler_params=_params("parallel", "parallel", "arbitrary"),
        name="attn_latent_full",
    )(y, y, y, cache_k, cache_v)


def _rope_tables(lat_len):
    half = HEAD_DIM // 2
    n_rows = lat_len // GRID_W
    row = jnp.repeat(jnp.arange(n_rows, dtype=F32), GRID_W)
    col = jnp.tile(jnp.arange(GRID_W, dtype=F32), n_rows)
    inv = ROPE_THETA ** (-jnp.arange(0, half, 2, dtype=F32) / half)
    ang_r, ang_c = row[:, None] * inv, col[:, None] * inv
    cos = jnp.concatenate([jnp.cos(ang_r)] * 2 + [jnp.cos(ang_c)] * 2, axis=-1)
    sin = jnp.concatenate([-jnp.sin(ang_r), jnp.sin(ang_r), -jnp.sin(ang_c), jnp.sin(ang_c)], axis=-1)
    return cos, sin


def _dft_mats(seq_len):
    n = 2 * seq_len
    k = jnp.arange(seq_len, dtype=jnp.int32)
    step = 64
    a = jnp.arange(seq_len // step, dtype=jnp.int32) * step
    b = jnp.arange(step, dtype=jnp.int32)
    ang_a = ((k[:, None] * a[None, :]) % n).astype(F32) * (2.0 * math.pi / n)
    ang_b = ((k[:, None] * b[None, :]) % n).astype(F32) * (2.0 * math.pi / n)
    ca, sa = jnp.cos(ang_a)[:, :, None], jnp.sin(ang_a)[:, :, None]
    cb, sb = jnp.cos(ang_b)[:, None, :], jnp.sin(ang_b)[:, None, :]
    fc = (ca * cb - sa * sb).reshape(seq_len, seq_len)
    fs = -(sa * cb + ca * sb).reshape(seq_len, seq_len)
    sign = jnp.where(k % 2 == 0, 1.0, -1.0).astype(F32)
    fs = jnp.where(k[:, None] == 0, sign[None, :], fs)
    ana = jnp.concatenate([fc, fs], axis=0).astype(BF16)
    ana = lax.optimization_barrier(ana)
    return ana, ana.T


def _hy_filter_kernel(feat_ref, tn_ref, w1_ref, b1_ref, w2_ref, b2_ref, fq_ref, w3f_ref, w3b_ref, dl_ref,
                      anar_ref, anai_ref, h_ref, pq_scr, inv_scr, nyq_scr):
    kf = pl.program_id(1)
    hp = lax.Precision.HIGHEST
    tk = h_ref.shape[1]

    @pl.when(kf == 0)
    def _():
        def taps(g):
            a = jnp.dot(feat_ref[g], w1_ref[...], precision=hp, preferred_element_type=F32) + b1_ref[...]
            a = jnp.sin(fq_ref[0:1, :] * a)
            a = jnp.dot(a, w2_ref[...], precision=hp, preferred_element_type=F32) + b2_ref[...]
            a = jnp.sin(fq_ref[1:2, :] * a)
            decay = jnp.exp(-tn_ref[g] * dl_ref[...])
            a = a.astype(BF16)
            return (jnp.dot(a, w3f_ref[...].astype(BF16), preferred_element_type=F32) * decay,
                    jnp.dot(a, w3b_ref[...].astype(BF16), preferred_element_type=F32) * decay)

        fwd0, bwd0 = taps(0)
        fwd1, bwd1 = taps(1)
        fwdm, bwdm = taps(2)
        rowi = lax.broadcasted_iota(jnp.int32, fwd0.shape, 0)
        first = rowi == 0
        bwd0 = jnp.where(first, 0.0, bwd0)
        absum = lambda v: jnp.sum(jnp.abs(v), axis=0, keepdims=True)
        inv_scr[...] = 1.0 / (absum(fwd0) + absum(fwd1) + absum(bwd0) + absum(bwd1) + EPS)
        parts = ((fwd0, bwd0),
                 (bwdm, jnp.where(first, 0.0, bwd1)),
                 (fwd1, jnp.where(first, 0.0, fwdm)))
        for n, (pos, neg) in enumerate(parts):
            p = pos + neg
            nyq_scr[n:n + 1, :] = jnp.sum(jnp.where((rowi & 1) == 0, p, -p), axis=0, keepdims=True)
            pq_scr[2 * n] = p.astype(BF16)
            pq_scr[2 * n + 1] = (pos - neg).astype(BF16)

    inv = inv_scr[...]
    for n in range(3):
        h_ref[2 * n] = jnp.dot(anar_ref[...], pq_scr[2 * n], preferred_element_type=F32) * inv
        hi = jnp.dot(anai_ref[...], pq_scr[2 * n + 1], preferred_element_type=F32)
        rowk = lax.broadcasted_iota(jnp.int32, hi.shape, 0)
        hi = jnp.where((rowk == 0) & (kf == 0), nyq_scr[n:n + 1, :], hi)
        h_ref[2 * n + 1] = hi * inv


def _hy_filters(seq_len, f_w1, f_b1, f_w2, f_b2, f_w3, freq, ana, *, tc, tk):
    half = seq_len // 2
    hid = f_w1.shape[1]
    nd = f_w3.shape[1] // 2
    d = nd // HY_ORDER
    t = jnp.arange(seq_len, dtype=F32)
    tnorm = t / max(seq_len - 1, 1)
    bands = (HY_EMB - 1) // 2
    fb = jnp.linspace(1e-4, bands - 1, bands, dtype=F32)
    w = 2.0 * math.pi * t / seq_len
    feats = jnp.concatenate([tnorm[:, None], jnp.cos(w[:, None] * fb), -jnp.sin(w[:, None] * fb)], axis=-1)
    emb = 64
    feats = jnp.pad(feats, ((0, 0), (0, emb - HY_EMB)))
    w1 = jnp.pad(f_w1, ((0, emb - HY_EMB), (0, 0)))
    mirror = half - jnp.arange(half)
    feats3 = jnp.stack([feats[:half], feats[half:], feats[mirror]])
    tn3 = jnp.stack([tnorm[:half], tnorm[half:], tnorm[mirror]]).reshape(3, half, 1)
    deltas = jnp.abs(jnp.linspace(HY_MIN_DECAY, HY_MAX_DECAY, d, dtype=F32))
    deltas = jnp.tile(deltas, HY_ORDER).reshape(1, nd)
    nct = nd // tc
    const = lambda shape: pl.BlockSpec(shape, lambda c, kf: (0,) * len(shape))
    return pl.pallas_call(
        _hy_filter_kernel,
        grid=(nct, half // tk),
        in_specs=[const((3, half, emb)), const((3, half, 1)),
                  const((emb, hid)), const((1, hid)), const((hid, hid)), const((1, hid)), const((2, hid)),
                  pl.BlockSpec((hid, tc), lambda c, kf: (0, c)),
                  pl.BlockSpec((hid, tc), lambda c, kf: (0, nct + c)),
                  pl.BlockSpec((1, tc), lambda c, kf: (0, c)),
                  pl.BlockSpec((tk, half), lambda c, kf: (kf, 0)),
                  pl.BlockSpec((tk, half), lambda c, kf: (half // tk + kf, 0))],
        out_specs=pl.BlockSpec((6, tk, tc), lambda c, kf: (0, kf, c)),
        out_shape=jax.ShapeDtypeStruct((6, half, nd), F32),
        scratch_shapes=[pltpu.VMEM((6, half, tc), BF16), pltpu.VMEM((1, tc), F32), pltpu.VMEM((8, tc), F32)],
        compiler_params=_params("parallel", "arbitrary"),
        name="hyena_filter",
    )(feats3, tn3, w1, f_b1.reshape(1, hid), f_w2, f_b2.reshape(1, hid), freq, f_w3, f_w3, deltas, ana, ana)


def _hy_inproj_kernel(xp_ref, x_ref, xn_ref, g_ref, sh_ref, sc_ref, w_ref, cw_ref, cb_ref, o_ref, h_scr, y_scr, *,
                      n_ctx_blocks, ctx_len, lat_len, tn):
    i = pl.program_id(0)
    tm = x_ref.shape[0]
    norm = lambda ref: _norm_mod(ref[...], g_ref[...], sh_ref[0], sc_ref[0]).astype(BF16)
    h_scr[0:HALO, :] = norm(xp_ref)
    h_scr[HALO:HALO + tm, :] = norm(x_ref)
    h_scr[HALO + tm:, :] = norm(xn_ref)

    row = lax.broadcasted_iota(jnp.int32, (tm, tn), 0)
    is_ctx = i < n_ctx_blocks
    pos = jnp.where(is_ctx, row % ctx_len, ((i - n_ctx_blocks) * tm + row) % lat_len)
    first = pos == 0
    last = pos == jnp.where(is_ctx, ctx_len, lat_len) - 1
    for jt in range(w_ref.shape[1] // tn):
        cols = slice(jt * tn, (jt + 1) * tn)
        y = y_scr.at[jt % 2]
        y[...] = jnp.dot(h_scr[...], w_ref[:, cols], preferred_element_type=F32)
        prev = jnp.where(first, 0.0, y[HALO - 1:HALO - 1 + tm, :])
        nxt = jnp.where(last, 0.0, y[HALO + 1:HALO + 1 + tm, :])
        o_ref[:, cols] = (prev * cw_ref[0:1, cols] + y[HALO:HALO + tm, :] * cw_ref[1:2, cols]
                          + nxt * cw_ref[2:3, cols] + cb_ref[:, cols]).astype(BF16)


def _hy_inproj(x, g, mod, w, conv_w, conv_b, *, n_ctx_rows, lat_len, ctx_len, tm):
    t, d = x.shape
    n = w.shape[1]
    per = tm // HALO
    row = functools.partial(_mod_row, tm=tm, n_ctx_rows=n_ctx_rows, lat_len=lat_len)
    return pl.pallas_call(
        functools.partial(_hy_inproj_kernel, n_ctx_blocks=n_ctx_rows // tm, ctx_len=ctx_len, lat_len=lat_len,
                          tn=COL_TILE),
        grid=(t // tm,),
        in_specs=[pl.BlockSpec((HALO, d), lambda i: (jnp.maximum(i * per - 1, 0), 0)),
                  pl.BlockSpec((tm, d), lambda i: (i, 0)),
                  pl.BlockSpec((HALO, d), lambda i: (jnp.minimum((i + 1) * per, t // HALO - 1), 0)),
                  pl.BlockSpec((1, d), lambda i: (0, 0)),
                  pl.BlockSpec((1, 1, d), lambda i: (row(i) * 6, 0, 0)),
                  pl.BlockSpec((1, 1, d), lambda i: (row(i) * 6 + 1, 0, 0)),
                  pl.BlockSpec((d, n), lambda i: (0, 0), pipeline_mode=pl.Buffered(1)),
                  pl.BlockSpec((3, n), lambda i: (0, 0)),
                  pl.BlockSpec((1, n), lambda i: (0, 0))],
        out_specs=pl.BlockSpec((tm, n), lambda i: (i, 0)),
        out_shape=jax.ShapeDtypeStruct((t, n), BF16),
        scratch_shapes=[pltpu.VMEM((tm + 2 * HALO, d), BF16), pltpu.VMEM((2, tm + 2 * HALO, COL_TILE), F32)],
        compiler_params=_params("parallel"),
        name="hyena_inproj",
    )(x, x, x, g.reshape(1, d), mod, mod, w, conv_w, conv_b.reshape(1, n))


def _hy_conv_kernel(a_ref, g_ref, skip_ref, ana_ref, syn_ref, h_ref, o_ref, *, seq_len):
    rows = a_ref.shape[0]
    half = seq_len // 2
    first = lax.broadcasted_iota(jnp.int32, (half, a_ref.shape[1]), 0) == 0

    def respond(uf, n):
        ur, ui = uf[0:half, :], uf[half:, :]
        hr, hi = h_ref[2 * n], h_ref[2 * n + 1]
        yr = jnp.where(first, 0.5 * (ur * hr), ur * hr - ui * hi)
        yi = jnp.where(first, 0.5 * (ui * hi), ur * hi + ui * hr)
        return yr, yi

    step = min(half, 512)
    for seg in range(0, rows, seq_len):
        uf1 = jnp.dot(ana_ref[...], a_ref[seg:seg + half, :], preferred_element_type=F32)
        uf2 = jnp.dot(ana_ref[...], a_ref[seg + half:seg + seq_len, :], preferred_element_type=F32)
        for dst, (n1, n2) in ((seg, (0, 1)), (seg + half, (2, 0))):
            y1r, y1i = respond(uf1, n1)
            y2r, y2i = respond(uf2, n2)
            yf = jnp.concatenate([y1r + y2r, y1i + y2i], axis=0).astype(BF16)
            for n0 in range(0, half, step):
                rs = slice(dst + n0, dst + n0 + step)
                y = jnp.dot(syn_ref[n0:n0 + step, :], yf, preferred_element_type=F32)
                a = a_ref[rs, :].astype(F32)
                g = g_ref[rs, :].astype(F32)
                o_ref[rs, :] = (g * (y * (1.0 / half) + a * skip_ref[...])).astype(BF16)


def _hy_conv(a, a_part, a_base, g, g_part, g_base, skip_row, mats, h, order, *, n_blocks, rows, seq_len, tc):
    d = skip_row.shape[1]
    nct = d // tc
    ana, syn = mats
    half = seq_len // 2

    def amap(part, base):
        return lambda c, b: (base + b, part * nct + c)

    once = dict(pipeline_mode=pl.Buffered(1))
    return pl.pallas_call(
        functools.partial(_hy_conv_kernel, seq_len=seq_len),
        grid=(nct, n_blocks),
        in_specs=[pl.BlockSpec((rows, tc), amap(a_part, a_base)),
                  pl.BlockSpec((rows, tc), amap(g_part, g_base)),
                  pl.BlockSpec((1, tc), lambda c, b: (0, c)),
                  pl.BlockSpec((2 * half, half), lambda c, b: (0, 0), **once),
                  pl.BlockSpec((half, 2 * half), lambda c, b: (0, 0), **once),
                  pl.BlockSpec((6, half, tc), lambda c, b: (0, 0, order * nct + c), **once)],
        out_specs=pl.BlockSpec((rows, tc), lambda c, b: (b, c)),
        out_shape=jax.ShapeDtypeStruct((n_blocks * rows, d), BF16),
        compiler_params=_params("parallel", "parallel"),
        name="hyena_conv",
    )(a, g, skip_row, ana, syn, h)


def _final_norm_kernel(x_ref, g_ref, o_ref):
    x = x_ref[...]
    o_ref[...] = x * lax.rsqrt(jnp.mean(x * x, axis=-1, keepdims=True) + EPS) * g_ref[...]


def _final_norm(x, g, *, row0, rows, tm):
    d = x.shape[1]
    base = row0 // tm
    return pl.pallas_call(
        _final_norm_kernel,
        grid=(rows // tm,),
        in_specs=[pl.BlockSpec((tm, d), lambda i: (base + i, 0)),
                  pl.BlockSpec((1, d), lambda i: (0, 0))],
        out_specs=pl.BlockSpec((tm, d), lambda i: (i, 0)),
        out_shape=jax.ShapeDtypeStruct((rows, d), F32),
        compiler_params=_params("parallel"),
        name="final_norm",
    )(x, g.reshape(1, d))


def kernel(x_prompt, x_sample, cache_win_k, cache_win_v, cache_ax_k, cache_ax_v, c, c_ctx, norm_mix_g, norm_ffn_g, mod_w, mod_b, win_wqkv, win_wo, win_sink, hy_w_in, hy_conv_w, hy_conv_b, hy_f_w1, hy_f_b1, hy_f_w2, hy_f_b2, hy_f_w3, hy_freq, hy_skip, hy_wo, ax_wqkv, ax_q_g, ax_k_g, ax_wo, ffn_w_gu, ffn_w_down, final_g):
    n_ctx_seqs, ctx_len, d = x_prompt.shape
    lat_batch, lat_len, _ = x_sample.shape
    depth = mod_w.shape[0]
    past = cache_win_k.shape[2]
    n_ctx_rows = n_ctx_seqs * ctx_len
    n_lat_rows = lat_batch * lat_len
    tm = TOKEN_TILE
    assert n_ctx_rows % tm == 0 and lat_len % tm == 0 and n_ctx_rows % lat_len == 0
    assert lat_batch + 1 <= MOD_ROWS
    geom = dict(n_ctx_rows=n_ctx_rows, lat_len=lat_len, tm=tm)

    x, x_lat = x_prompt.reshape(n_ctx_rows, d), x_sample.reshape(n_lat_rows, d)
    cvec = jnp.concatenate([c_ctx[None, :], c, jnp.zeros((MOD_ROWS - 1 - lat_batch, d), F32)], axis=0)
    mod_all = _adaln_all(cvec, mod_w, mod_b)
    cos, sin = _rope_tables(lat_len)

    d_ff = ffn_w_down.shape[1]
    w_gu = ffn_w_gu.astype(BF16).reshape(depth, d, 2, d_ff // COL_TILE, COL_TILE).transpose(0, 1, 3, 2, 4)
    w_gu = w_gu.reshape(depth, d, 2 * d_ff)
    w_down = ffn_w_down.astype(BF16)
    wqkv_all = {0: win_wqkv.astype(BF16), 2: ax_wqkv.astype(BF16)}
    wo_all = {0: win_wo.astype(BF16), 1: hy_wo.astype(BF16), 2: ax_wo.astype(BF16)}

    states = {0: None, 2: None}
    kv_w = N_KV_HEADS * HEAD_DIM
    for i in range(depth):
        mod = mod_all[i].reshape(MOD_ROWS * 6, 1, d)
        kind, j = i % 3, i // 3
        if kind == 1:
            assert x_lat is None
            u = _hy_inproj(x, norm_mix_g[i], mod, hy_w_in[j].astype(BF16), hy_conv_w[j], hy_conv_b[j],
                           ctx_len=ctx_len, **geom)
            rows = lat_len
            zs = []
            for (base, nblk, sl) in ((0, n_ctx_rows // rows, ctx_len), (n_ctx_rows // rows, lat_batch, lat_len)):
                mats = _dft_mats(sl // 2)
                resp = _hy_filters(sl, hy_f_w1[j], hy_f_b1[j], hy_f_w2[j], hy_f_b2[j], hy_f_w3[j], hy_freq[j],
                                   mats[0], tc=COL_TILE, tk=min(COL_TILE, sl // 2))
                conv = functools.partial(_hy_conv, mats=mats, h=resp, n_blocks=nblk, rows=rows, seq_len=sl,
                                         tc=COL_TILE)
                z1 = conv(u, 0, base, u, 1, base, skip_row=hy_skip[j, 0].reshape(1, d), order=0)
                zs.append(conv(z1, 0, 0, u, 2, base, skip_row=hy_skip[j, 1].reshape(1, d), order=1))
            x, h2 = _proj_residual(zs[0], zs[1], wo_all[kind], j, x, None, mod, norm_ffn_g[i], **geom)
        else:
            if kind == 0:
                sink, hg = win_sink[j], None
                ck, cv = cache_win_k, cache_win_v
            else:
                sink = jnp.zeros((N_HEADS,), F32)
                hg = jnp.stack([ax_q_g[j], ax_k_g[j]], axis=0)
                ck, cv = cache_ax_k, cache_ax_v
            ck = ck[:, j:j + 1].reshape(lat_batch, 1, past, kv_w)
            cv = cv[:, j:j + 1].reshape(lat_batch, 1, past, kv_w)
            y, k_c, v_c = _qkv_project(x, x_lat, norm_mix_g[i], mod, wqkv_all[kind], j, cos, sin, hg, states[kind],
                                       ctx_len=ctx_len, **geom)
            states[kind] = (k_c, v_c)
            o_c = _attn_ctx(y, sink, n_ctx_seqs=n_ctx_seqs, ctx_len=ctx_len, use_sink=kind == 0)
            lat_geom = dict(n_ctx_rows=n_ctx_rows, lat_batch=lat_batch, lat_len=lat_len)
            if kind == 0:
                o_l = _attn_lat_win(y, sink, ck, cv, 0, **lat_geom)
            else:
                o_l = _attn_lat_full(y, ck, cv, 0, **lat_geom)
            x, h2 = _proj_residual(o_c, o_l, wo_all[kind], j, x, x_lat, mod, norm_ffn_g[i], **geom)
            x_lat = None
        x = _ffn(x, h2, mod, w_gu, w_down, i, tf=COL_TILE, **geom)

    y_prompt = _final_norm(x, final_g, row0=0, rows=n_ctx_rows, tm=tm).reshape(n_ctx_seqs, ctx_len, d)
    y_sample = _final_norm(x, final_g, row0=n_ctx_rows, rows=n_lat_rows, tm=tm).reshape(lat_batch, lat_len, d)
    return (y_prompt, y_sample) + states[0] + states[2]
```

```python
import functools
import math

import jax
import jax.numpy as jnp
from jax import lax
from jax.experimental import pallas as pl
from jax.experimental.pallas import tpu as pltpu

F32 = jnp.float32
BF16 = jnp.bfloat16

N_HEADS = 16
N_KV_HEADS = 4
GROUP = N_HEADS // N_KV_HEADS
HEAD_DIM = 128
WINDOW = 128
BLOCK = 128
GRID_W = 64
ROPE_THETA = 10000.0
HY_ORDER = 2
HY_EMB = 33
HY_TARGET = 1e-2
HY_FAST_DECAY = 0.3
HY_SLOW_DECAY = 1.5
HY_MIN_DECAY = math.log(HY_TARGET) / HY_SLOW_DECAY
HY_MAX_DECAY = math.log(HY_TARGET) / HY_FAST_DECAY
EPS = 1e-6
NEG = -1e30
SCALE = HEAD_DIM ** -0.5
LOG2E = math.log2(math.e)
PAIR = 2
Q_STEP = 4 * BLOCK
TOKEN_TILE = 512
COL_TILE = 512
RING = 3
HALO = 16
MOD_ROWS = 16
VMEM_LIMIT = 56 * 1024 * 1024

NT_DIMS = (((1,), (1,)), ((), ()))


def _params(*sem):
    return pltpu.CompilerParams(dimension_semantics=sem, vmem_limit_bytes=VMEM_LIMIT)


def _silu(x):
    return x * (1.0 / (1.0 + jnp.exp(-x)))


def _mod_row(i, tm, n_ctx_rows, lat_len):
    return jnp.maximum(i * tm - n_ctx_rows + lat_len, 0) // lat_len


def _mod_kernel(c_ref, w_ref, b_ref, o_ref):
    s = _silu(c_ref[...]).astype(BF16)
    o_ref[...] = jnp.dot(s, w_ref[0].astype(BF16), preferred_element_type=F32) + b_ref[0]


def _adaln_all(cvec, mod_w, mod_b):
    depth, d, n = mod_w.shape
    tn = 1024
    per = d // tn
    return pl.pallas_call(
        _mod_kernel,
        grid=(depth, n // tn),
        in_specs=[pl.BlockSpec((MOD_ROWS, d), lambda l, j: (0, 0)),
                  pl.BlockSpec((1, d, tn), lambda l, j: (l, 0, j)),
                  pl.BlockSpec((1, 1, tn), lambda l, j: (l, 0, j))],
        out_specs=pl.BlockSpec((None, MOD_ROWS, None, None, tn), lambda l, j: (l, 0, j // per, 0, j % per)),
        out_shape=jax.ShapeDtypeStruct((depth, MOD_ROWS, 6, 1, d), F32),
        compiler_params=_params("parallel", "parallel"),
        name="adaln_mod",
    )(cvec, mod_w, mod_b.reshape(depth, 1, n))


def _norm_mod(x, g, shift, scale):
    y = x * lax.rsqrt(jnp.mean(x * x, axis=-1, keepdims=True) + EPS) * g
    return y * (1.0 + scale) + shift


def _rope_rotate(a, cos, sin_signed):
    lane = lax.broadcasted_iota(jnp.int32, a.shape, 1)
    partner = jnp.where((lane & 32) == 0, pltpu.roll(a, 96, 1), pltpu.roll(a, 32, 1))
    return a * cos + partner * sin_signed


def _two_source_specs(x_ctx, x_lat, tm, n_ctx_blocks):
    d = x_ctx.shape[1]
    lat_base = n_ctx_blocks if x_lat is None else 0
    specs = [pl.BlockSpec((tm, d), lambda i: (jnp.minimum(i, n_ctx_blocks - 1), 0)),
             pl.BlockSpec((tm, d), lambda i: (lat_base + jnp.maximum(i - n_ctx_blocks, 0), 0))]
    return specs, [x_ctx, x_ctx if x_lat is None else x_lat]


def _qkv_kernel(*refs, qk_norm, tn, halves, n_ctx_blocks, state_slot):
    xc_ref, xl_ref, g_ref, sh_ref, sc_ref, w_ref, cos_ref, sin_ref, hg_ref = refs[:9]
    o_ref, ks_ref, vs_ref = refs[-3:]
    tm = xc_ref.shape[0]
    rh = tm // halves
    heads_per_tile = tn // HEAD_DIM
    kcol, vcol = N_HEADS * HEAD_DIM, (N_HEADS + N_KV_HEADS) * HEAD_DIM

    def body(x_ref, latent):
        for hf in range(halves):
            rs = slice(hf * rh, (hf + 1) * rh)
            h = _norm_mod(x_ref[rs, :], g_ref[...], sh_ref[0], sc_ref[0]).astype(BF16)
            for jt in range(w_ref.shape[1] // tn):
                acc = jnp.dot(h, w_ref[:, jt * tn:(jt + 1) * tn], preferred_element_type=F32)
                for hh in range(heads_per_tile):
                    head = jt * heads_per_tile + hh
                    a = acc[:, hh * HEAD_DIM:(hh + 1) * HEAD_DIM]
                    if head < N_HEADS + N_KV_HEADS:
                        if qk_norm:
                            gain = hg_ref[0:1, :] if head < N_HEADS else hg_ref[1:2, :]
                            a = a * lax.rsqrt(jnp.mean(a * a, axis=-1, keepdims=True) + EPS) * gain
                        if latent:
                            a = _rope_rotate(a, cos_ref[rs, :], sin_ref[rs, :])
                    o_ref[rs, head * HEAD_DIM:(head + 1) * HEAD_DIM] = a
        if not latent:
            seqs, ctx_len = ks_ref.shape[0], ks_ref.shape[-3]
            for ref, col in ((ks_ref, kcol), (vs_ref, vcol)):
                if state_slot is not None:
                    for other in range(ref.shape[1]):
                        if other != state_slot:
                            ref[:, other] = jnp.zeros(ref.shape[:1] + ref.shape[2:], F32)
                for s in range(seqs):
                    for kvh in range(N_KV_HEADS):
                        val = o_ref[s * ctx_len:(s + 1) * ctx_len, col + kvh * HEAD_DIM:col + (kvh + 1) * HEAD_DIM]
                        if state_slot is None:
                            ref[s, :, kvh, :] = val
                        else:
                            ref[s, state_slot, :, kvh, :] = val

    i = pl.program_id(0)
    pl.when(i < n_ctx_blocks)(lambda: body(xc_ref, False))
    pl.when(i >= n_ctx_blocks)(lambda: body(xl_ref, True))


def _qkv_project(x_ctx, x_lat, g, mod, w, layer, cos, sin, hg, state, *, n_ctx_rows, lat_len, ctx_len, tm):
    d = x_ctx.shape[1]
    seqs = tm // ctx_len
    n_layers = w.shape[0]
    state_shape = (n_ctx_rows // ctx_len, n_layers, ctx_len, N_KV_HEADS, HEAD_DIM)
    last_ctx = n_ctx_rows // tm - 1
    if state is None:
        state_spec = pl.BlockSpec((seqs, n_layers, ctx_len, N_KV_HEADS, HEAD_DIM),
                                  lambda i: (jnp.minimum(i, last_ctx), 0, 0, 0, 0))
        state_specs, state_args, aliases = [], [], {}
    else:
        state_spec = pl.BlockSpec((seqs, None, ctx_len, N_KV_HEADS, HEAD_DIM),
                                  lambda i: (jnp.minimum(i, last_ctx), layer, 0, 0, 0))
        state_specs, state_args = [pl.BlockSpec(memory_space=pl.ANY)] * 2, list(state)
        aliases = {9: 1, 10: 2}
    n = w.shape[2]
    ncb = n_ctx_rows // tm
    t = n_ctx_rows + (x_ctx.shape[0] - n_ctx_rows if x_lat is None else x_lat.shape[0])
    row = functools.partial(_mod_row, tm=tm, n_ctx_rows=n_ctx_rows, lat_len=lat_len)
    tab = lambda i: ((jnp.maximum(i - ncb, 0)) % (lat_len // tm), 0)
    qk_norm = hg is not None
    if hg is None:
        hg = jnp.ones((2, HEAD_DIM), F32)
    x_specs, x_args = _two_source_specs(x_ctx, x_lat, tm, ncb)
    return pl.pallas_call(
        functools.partial(_qkv_kernel, qk_norm=qk_norm, tn=COL_TILE, halves=2, n_ctx_blocks=ncb,
                          state_slot=layer if state is None else None),
        grid=(t // tm,),
        in_specs=x_specs + [
            pl.BlockSpec((1, d), lambda i: (0, 0)),
            pl.BlockSpec((1, 1, d), lambda i: (row(i) * 6, 0, 0)),
            pl.BlockSpec((1, 1, d), lambda i: (row(i) * 6 + 1, 0, 0)),
            pl.BlockSpec((None, d, n), lambda i: (layer, 0, 0), pipeline_mode=pl.Buffered(1)),
            pl.BlockSpec((tm, HEAD_DIM), tab), pl.BlockSpec((tm, HEAD_DIM), tab),
            pl.BlockSpec((2, HEAD_DIM), lambda i: (0, 0))] + state_specs,
        out_specs=[pl.BlockSpec((tm, n), lambda i: (i, 0)), state_spec, state_spec],
        out_shape=[jax.ShapeDtypeStruct((t, n), F32), jax.ShapeDtypeStruct(state_shape, F32),
                   jax.ShapeDtypeStruct(state_shape, F32)],
        input_output_aliases=aliases,
        compiler_params=_params("arbitrary"),
        name="qkv_project",
    )(*x_args, g.reshape(1, d), mod, mod, w, cos, sin, hg, *state_args)


def _proj_res_kernel(ac_ref, al_ref, xc_ref, xl_ref, w_ref, gate_ref, g2_ref, sh2_ref, sc2_ref, o_ref, h_ref, *,
                     n_ctx_blocks):
    i = pl.program_id(0)

    def emit(a_ref, x_ref):
        xn = x_ref[...] + gate_ref[0] * jnp.dot(a_ref[...], w_ref[...], preferred_element_type=F32)
        o_ref[...] = xn
        h_ref[...] = _norm_mod(xn, g2_ref[...], sh2_ref[0], sc2_ref[0]).astype(BF16)

    pl.when(i < n_ctx_blocks)(lambda: emit(ac_ref, xc_ref))
    pl.when(i >= n_ctx_blocks)(lambda: emit(al_ref, xl_ref))


def _proj_residual(a_ctx, a_lat, w, layer, x_ctx, x_lat, mod, g2, *, n_ctx_rows, lat_len, tm):
    d = x_ctx.shape[1]
    k = w.shape[1]
    ncb = n_ctx_rows // tm
    t = a_ctx.shape[0] + a_lat.shape[0]
    row = functools.partial(_mod_row, tm=tm, n_ctx_rows=n_ctx_rows, lat_len=lat_len)
    modspec = lambda which: pl.BlockSpec((1, 1, d), lambda i: (row(i) * 6 + which, 0, 0))
    x_specs, x_args = _two_source_specs(x_ctx, x_lat, tm, ncb)
    return pl.pallas_call(
        functools.partial(_proj_res_kernel, n_ctx_blocks=ncb),
        grid=(t // tm,),
        in_specs=[pl.BlockSpec((tm, k), lambda i: (jnp.minimum(i, ncb - 1), 0)),
                  pl.BlockSpec((tm, k), lambda i: (jnp.maximum(i - ncb, 0), 0))]
        + x_specs
        + [pl.BlockSpec((None, k, d), lambda i: (layer, 0, 0), pipeline_mode=pl.Buffered(1)),
           modspec(2),
           pl.BlockSpec((1, d), lambda i: (0, 0)),
           modspec(3), modspec(4)],
        out_specs=[pl.BlockSpec((tm, d), lambda i: (i, 0)), pl.BlockSpec((tm, d), lambda i: (i, 0))],
        out_shape=[jax.ShapeDtypeStruct((t, d), F32), jax.ShapeDtypeStruct((t, d), BF16)],
        compiler_params=_params("parallel"),
        name="proj_residual",
    )(a_ctx, a_lat, *x_args, w, mod, g2.reshape(1, d), mod, mod)


def _ffn_kernel(x_ref, h_ref, gate_ref, wgu_hbm, wd_hbm, o_ref, wg_buf, wu_buf, wd_buf, sem, *, layer, nf, tf):
    i, f = pl.program_id(0), pl.program_id(1)
    s = i * nf + f
    total = pl.num_programs(0) * nf

    def copies(step):
        slot, ft = step % RING, step % nf
        return (pltpu.make_async_copy(wgu_hbm.at[layer, :, pl.ds(pl.multiple_of(ft * tf, tf), tf)],
                                      wg_buf.at[slot], sem.at[0, slot]),
                pltpu.make_async_copy(wgu_hbm.at[layer, :, pl.ds(pl.multiple_of((nf + ft) * tf, tf), tf)],
                                      wu_buf.at[slot], sem.at[1, slot]),
                pltpu.make_async_copy(wd_hbm.at[layer, pl.ds(pl.multiple_of(ft * tf, tf), tf), :],
                                      wd_buf.at[slot], sem.at[2, slot]))

    @pl.when(s == 0)
    def _():
        for step in range(RING - 1):
            for c in copies(step):
                c.start()

    @pl.when(s + RING - 1 < total)
    def _():
        for c in copies(s + RING - 1):
            c.start()

    @pl.when(f == 0)
    def _():
        o_ref[...] = jnp.zeros_like(o_ref)

    for c in copies(s):
        c.wait()
    slot = s % RING
    h = h_ref[...]
    gv = jnp.dot(h, wg_buf[slot], preferred_element_type=F32)
    uv = jnp.dot(h, wu_buf[slot], preferred_element_type=F32)
    act = (_silu(gv) * uv).astype(BF16)
    o_ref[...] += jnp.dot(act, wd_buf[slot], preferred_element_type=F32)

    @pl.when(f == pl.num_programs(1) - 1)
    def _():
        o_ref[...] = x_ref[...] + gate_ref[0] * o_ref[...]


def _ffn(x, h, mod, w_gu, w_down, layer, *, n_ctx_rows, lat_len, tm, tf):
    t, d = x.shape
    ff = w_down.shape[1]
    nf = ff // tf
    row = functools.partial(_mod_row, tm=tm, n_ctx_rows=n_ctx_rows, lat_len=lat_len)
    return pl.pallas_call(
        functools.partial(_ffn_kernel, layer=layer, nf=nf, tf=tf),
        grid=(t // tm, nf),
        in_specs=[pl.BlockSpec((tm, d), lambda i, f: (i, 0)),
                  pl.BlockSpec((tm, d), lambda i, f: (i, 0)),
                  pl.BlockSpec((1, 1, d), lambda i, f: (row(i) * 6 + 5, 0, 0)),
                  pl.BlockSpec(memory_space=pl.ANY),
                  pl.BlockSpec(memory_space=pl.ANY)],
        out_specs=pl.BlockSpec((tm, d), lambda i, f: (i, 0)),
        out_shape=jax.ShapeDtypeStruct((t, d), F32),
        scratch_shapes=[pltpu.VMEM((RING, d, tf), BF16), pltpu.VMEM((RING, d, tf), BF16),
                        pltpu.VMEM((RING, tf, d), BF16), pltpu.SemaphoreType.DMA((3, RING))],
        compiler_params=_params("arbitrary", "arbitrary"),
        name="ffn_swiglu",
    )(x, h, mod, w_gu, w_down)


def _stack_heads(q):
    return jnp.concatenate([q[:, h * HEAD_DIM:(h + 1) * HEAD_DIM] for h in range(GROUP)], axis=0)


def _unstack_heads(o, rows):
    return jnp.concatenate([o[h * rows:(h + 1) * rows] for h in range(GROUP)], axis=1)


def _sink_column(sink_ref, kh, rows):
    head = lax.broadcasted_iota(jnp.int32, (GROUP * rows, 1), 0) // rows
    col = jnp.zeros((GROUP * rows, 1), F32)
    for h in range(GROUP):
        col = jnp.where(head == h, sink_ref[kh * GROUP + h], col)
    return col


def _softmax_av(s, v, sink_col):
    m = jnp.max(s, axis=-1, keepdims=True)
    if sink_col is not None:
        m = jnp.maximum(m, sink_col)
    p = jnp.exp(s - m)
    l = jnp.sum(p, axis=-1, keepdims=True)
    if sink_col is not None:
        l = l + jnp.exp(sink_col - m)
    return jnp.dot(p.astype(BF16), v, preferred_element_type=F32) / l


def _attn_ctx_kernel(sink_ref, q_ref, k_ref, v_ref, o_ref, *, use_sink):
    rows = q_ref.shape[0]
    qw = GROUP * HEAD_DIM
    for kh in range(N_KV_HEADS):
        qs = _stack_heads(q_ref[:, kh * qw:(kh + 1) * qw] * SCALE).astype(BF16)
        k = k_ref[:, kh * HEAD_DIM:(kh + 1) * HEAD_DIM].astype(BF16)
        v = v_ref[:, kh * HEAD_DIM:(kh + 1) * HEAD_DIM].astype(BF16)
        s = lax.dot_general(qs, k, NT_DIMS, preferred_element_type=F32)
        sink_col = _sink_column(sink_ref, kh, rows) if use_sink else None
        o_ref[:, kh * qw:(kh + 1) * qw] = _unstack_heads(_softmax_av(s, v, sink_col), rows).astype(BF16)


def _attn_ctx(y, sink, *, n_ctx_seqs, ctx_len, use_sink):
    t = n_ctx_seqs * ctx_len
    q_w, kv_w = N_HEADS * HEAD_DIM, N_KV_HEADS * HEAD_DIM
    return pl.pallas_call(
        functools.partial(_attn_ctx_kernel, use_sink=use_sink),
        grid=(n_ctx_seqs,),
        in_specs=[pl.BlockSpec(memory_space=pltpu.SMEM),
                  pl.BlockSpec((ctx_len, q_w), lambda b: (b, 0)),
                  pl.BlockSpec((ctx_len, kv_w), lambda b: (b, q_w // kv_w)),
                  pl.BlockSpec((ctx_len, kv_w), lambda b: (b, q_w // kv_w + 1))],
        out_specs=pl.BlockSpec((ctx_len, q_w), lambda b: (b, 0)),
        out_shape=jax.ShapeDtypeStruct((t, q_w), BF16),
        compiler_params=_params("parallel"),
        name="attn_context",
    )(sink, y, y, y)


def _keys_by_queries_attention(q_ref, k_scr, vt_scr, s_scr, chunks, bias_ref, sink_ref, o_ref):
    qn = q_ref.shape[0]
    kh = pl.program_id(1)

    def logits(h, ci):
        st, sz = chunks[ci]
        q = (q_ref[:, h * HEAD_DIM:(h + 1) * HEAD_DIM] * (SCALE * LOG2E)).astype(BF16)
        st_ = lax.dot_general(k_scr[st:st + sz, :], q, NT_DIMS, preferred_element_type=F32)
        if bias_ref is not None and ci == 0:
            st_ = st_ + bias_ref[...]
        s_scr[st:st + sz, h * qn:(h + 1) * qn] = st_
        return jnp.max(st_, axis=0, keepdims=True)

    def weigh(h, ci, m):
        st, sz = chunks[ci]
        p = jnp.exp2(s_scr[st:st + sz, h * qn:(h + 1) * qn] - m)
        pv = jnp.dot(vt_scr[:, st:st + sz], p.astype(BF16), preferred_element_type=F32)
        return jnp.sum(p, axis=0, keepdims=True), pv

    n = len(chunks)
    maxes = [logits(0, ci) for ci in range(n)]
    for h in range(GROUP):
        m = functools.reduce(jnp.maximum, maxes)
        if sink_ref is not None:
            sink2 = sink_ref[kh * GROUP + h] * LOG2E
            m = jnp.maximum(m, sink2)
        maxes, l, acc = [], None, None
        for ci in range(n):
            if h + 1 < GROUP:
                maxes.append(logits(h + 1, ci))
            lc, pv = weigh(h, ci, m)
            l, acc = (lc, pv) if l is None else (l + lc, acc + pv)
        if sink_ref is not None:
            l = l + jnp.exp2(sink2 - m)
        o_ref[:, h * HEAD_DIM:(h + 1) * HEAD_DIM] = (acc * (1.0 / l)).T.astype(BF16)


def _attn_lat_win_kernel(sink_ref, q_ref, kp_ref, kc_ref, kn_ref, vp_ref, vc_ref, vn_ref, ck_ref, cv_ref, bias_ref,
                         o_ref, k_scr, vt_scr, s_scr):
    past = ck_ref.shape[0]
    win = Q_STEP + 2 * BLOCK

    @pl.when(pl.program_id(2) == 0)
    def _():
        k_scr[win:, :] = ck_ref[...].astype(BF16)
        vt_scr[:, win:] = cv_ref[...].T.astype(BF16)

    row = 0
    for k_ref, v_ref in ((kp_ref, vp_ref), (kc_ref, vc_ref), (kn_ref, vn_ref)):
        n = k_ref.shape[0]
        k_scr[row:row + n, :] = k_ref[...].astype(BF16)
        vt_scr[:, row:row + n] = v_ref[...].T.astype(BF16)
        row += n

    kh, step, n_steps = pl.program_id(1), pl.program_id(2), pl.num_programs(2)
    span, lanes, nq = 3 * BLOCK, PAIR * BLOCK, Q_STEP // BLOCK
    groups = [(qb, p) for qb in range(nq) for p in range(GROUP // PAIR)]

    def logits(g):
        qb, p = groups[g]
        rows = slice(qb * BLOCK, (qb + 1) * BLOCK)
        q = jnp.concatenate([q_ref[rows, h * HEAD_DIM:(h + 1) * HEAD_DIM] for h in range(PAIR * p, PAIR * (p + 1))],
                            axis=0)
        q = (q * (SCALE * LOG2E)).astype(BF16)
        edge = 1
        if qb == 0:
            edge = jnp.where(step == 0, 0, 1)
        elif qb == nq - 1:
            edge = jnp.where(step == n_steps - 1, 2, 1)
        sw = lax.dot_general(k_scr[qb * BLOCK:qb * BLOCK + span, :], q, NT_DIMS,
                             preferred_element_type=F32) + bias_ref[edge]
        sc = lax.dot_general(k_scr[win:, :], q, NT_DIMS, preferred_element_type=F32)
        s_scr[0:span, g * lanes:(g + 1) * lanes] = sw
        s_scr[span:, g * lanes:(g + 1) * lanes] = sc
        return jnp.maximum(jnp.max(sw, axis=0, keepdims=True), jnp.max(sc, axis=0, keepdims=True))

    head = lax.broadcasted_iota(jnp.int32, (1, lanes), 1) // BLOCK
    m_next = logits(0)
    for g, (qb, p) in enumerate(groups):
        sink2 = jnp.zeros((1, lanes), F32)
        for hh in range(PAIR):
            sink2 = jnp.where(head == hh, sink_ref[kh * GROUP + PAIR * p + hh] * LOG2E, sink2)
        m = jnp.maximum(m_next, sink2)
        if g + 1 < len(groups):
            m_next = logits(g + 1)
        pw = jnp.exp2(s_scr[0:span, g * lanes:(g + 1) * lanes] - m)
        pc = jnp.exp2(s_scr[span:, g * lanes:(g + 1) * lanes] - m)
        pv = (jnp.dot(vt_scr[:, qb * BLOCK:qb * BLOCK + span], pw.astype(BF16), preferred_element_type=F32)
              + jnp.dot(vt_scr[:, win:], pc.astype(BF16), preferred_element_type=F32))
        l = (jnp.sum(pw, axis=0, keepdims=True) + jnp.sum(pc, axis=0, keepdims=True) + jnp.exp2(sink2 - m))
        ot = pv * (1.0 / l)
        for hh in range(PAIR):
            col = (PAIR * p + hh) * HEAD_DIM
            o_ref[qb * BLOCK:(qb + 1) * BLOCK, col:col + HEAD_DIM] = ot[:, hh * BLOCK:(hh + 1) * BLOCK].T.astype(BF16)


def _window_bias():
    c = jnp.arange(3 * BLOCK, dtype=jnp.int32)[:, None]
    r = jnp.arange(PAIR * BLOCK, dtype=jnp.int32)[None, :] % BLOCK
    band = jnp.abs(r + BLOCK - c) <= WINDOW
    variants = [band & (c >= BLOCK), band, band & (c < 2 * BLOCK)]
    return jnp.where(jnp.stack(variants), 0.0, NEG).astype(F32)


def _cache_spec(cache, layer):
    return pl.BlockSpec((None, None, cache.shape[2], HEAD_DIM), lambda b, kh, qs: (b, layer, 0, kh))


def _attn_lat_win(y, sink, cache_k, cache_v, layer, *, n_ctx_rows, lat_batch, lat_len):
    qw = GROUP * HEAD_DIM
    nb = lat_len // BLOCK
    ns = lat_len // Q_STEP
    per = Q_STEP // BLOCK
    base = n_ctx_rows // BLOCK
    past = cache_k.shape[2]
    n_keys = Q_STEP + 2 * BLOCK + past

    def edge(col0, blk):
        return pl.BlockSpec((BLOCK, HEAD_DIM),
                            lambda b, kh, qs: (base + b * nb + jnp.clip(qs * per + blk, 0, nb - 1), col0 + kh))

    def centre(col0):
        return pl.BlockSpec((Q_STEP, HEAD_DIM), lambda b, kh, qs: (base // per + b * ns + qs, col0 + kh))

    kcol, vcol = N_HEADS, N_HEADS + N_KV_HEADS
    cache_spec = _cache_spec(cache_k, layer)
    return pl.pallas_call(
        _attn_lat_win_kernel,
        grid=(lat_batch, N_KV_HEADS, ns),
        in_specs=[pl.BlockSpec(memory_space=pltpu.SMEM),
                  pl.BlockSpec((Q_STEP, qw), lambda b, kh, qs: (base // per + b * ns + qs, kh)),
                  edge(kcol, -1), centre(kcol), edge(kcol, per),
                  edge(vcol, -1), centre(vcol), edge(vcol, per),
                  cache_spec, cache_spec,
                  pl.BlockSpec((3, 3 * BLOCK, PAIR * BLOCK), lambda b, kh, qs: (0, 0, 0))],
        out_specs=pl.BlockSpec((Q_STEP, qw), lambda b, kh, qs: (b * ns + qs, kh)),
        out_shape=jax.ShapeDtypeStruct((lat_batch * lat_len, N_HEADS * HEAD_DIM), BF16),
        scratch_shapes=[pltpu.VMEM((n_keys, HEAD_DIM), BF16), pltpu.VMEM((HEAD_DIM, n_keys), BF16),
                        pltpu.VMEM((3 * BLOCK + past, GROUP * Q_STEP), F32)],
        compiler_params=_params("parallel", "parallel", "arbitrary"),
        name="attn_latent_window",
    )(sink, y, y, y, y, y, y, y, cache_k, cache_v, _window_bias())


def _attn_lat_full_kernel(q_ref, k_ref, v_ref, ck_ref, cv_ref, o_ref, k_scr, vt_scr, s_scr, *, chunk):
    lat_len = k_ref.shape[0]
    n_keys = k_scr.shape[0]

    @pl.when(pl.program_id(2) == 0)
    def _():
        k_scr[0:lat_len, :] = k_ref[...].astype(BF16)
        k_scr[lat_len:, :] = ck_ref[...].astype(BF16)
        vt_scr[:, 0:lat_len] = v_ref[...].T.astype(BF16)
        vt_scr[:, lat_len:] = cv_ref[...].T.astype(BF16)

    chunks = tuple((st, chunk) for st in range(0, n_keys, chunk))
    _keys_by_queries_attention(q_ref, k_scr, vt_scr, s_scr, chunks, None, None, o_ref)


def _attn_lat_full(y, cache_k, cache_v, layer, *, n_ctx_rows, lat_batch, lat_len):
    qw = GROUP * HEAD_DIM
    ns = lat_len // Q_STEP
    base = n_ctx_rows // Q_STEP
    seq_base = n_ctx_rows // lat_len
    past = cache_k.shape[2]
    cache_spec = _cache_spec(cache_k, layer)
    n_keys = lat_len + past
    chunk = 512
    assert n_keys % chunk == 0
    return pl.pallas_call(
        functools.partial(_attn_lat_full_kernel, chunk=chunk),
        grid=(lat_batch, N_KV_HEADS, ns),
        in_specs=[pl.BlockSpec((Q_STEP, qw), lambda b, kh, qs: (base + b * ns + qs, kh)),
                  pl.BlockSpec((lat_len, HEAD_DIM), lambda b, kh, qs: (seq_base + b, N_HEADS + kh)),
                  pl.BlockSpec((lat_len, HEAD_DIM), lambda b, kh, qs: (seq_base + b, N_HEADS + N_KV_HEADS + kh)),
                  cache_spec, cache_spec],
        out_specs=pl.BlockSpec((Q_STEP, qw), lambda b, kh, qs: (b * ns + qs, kh)),
        out_shape=jax.ShapeDtypeStruct((lat_batch * lat_len, N_HEADS * HEAD_DIM), BF16),
        scratch_shapes=[pltpu.VMEM((n_keys, HEAD_DIM), BF16), pltpu.VMEM((HEAD_DIM, n_keys), BF16),
                        pltpu.VMEM((n_keys, GROUP * Q_STEP), F32)],
        compiler_params=_params("parallel", "parallel", "arbitrary"),
        name="attn_latent_full",
    )(y, y, y, cache_k, cache_v)


def _rope_tables(lat_len):
    half = HEAD_DIM // 2
    n_rows = lat_len // GRID_W
    row = jnp.repeat(jnp.arange(n_rows, dtype=F32), GRID_W)
    col = jnp.tile(jnp.arange(GRID_W, dtype=F32), n_rows)
    inv = ROPE_THETA ** (-jnp.arange(0, half, 2, dtype=F32) / half)
    ang_r, ang_c = row[:, None] * inv, col[:, None] * inv
    cos = jnp.concatenate([jnp.cos(ang_r)] * 2 + [jnp.cos(ang_c)] * 2, axis=-1)
    sin = jnp.concatenate([-jnp.sin(ang_r), jnp.sin(ang_r), -jnp.sin(ang_c), jnp.sin(ang_c)], axis=-1)
    return cos, sin


def _dft_mats(seq_len):
    n = 2 * seq_len
    k = jnp.arange(seq_len, dtype=jnp.int32)
    step = 64
    a = jnp.arange(seq_len // step, dtype=jnp.int32) * step
    b = jnp.arange(step, dtype=jnp.int32)
    ang_a = ((k[:, None] * a[None, :]) % n).astype(F32) * (2.0 * math.pi / n)
    ang_b = ((k[:, None] * b[None, :]) % n).astype(F32) * (2.0 * math.pi / n)
    ca, sa = jnp.cos(ang_a)[:, :, None], jnp.sin(ang_a)[:, :, None]
    cb, sb = jnp.cos(ang_b)[:, None, :], jnp.sin(ang_b)[:, None, :]
    fc = (ca * cb - sa * sb).reshape(seq_len, seq_len)
    fs = -(sa * cb + ca * sb).reshape(seq_len, seq_len)
    sign = jnp.where(k % 2 == 0, 1.0, -1.0).astype(F32)
    fs = jnp.where(k[:, None] == 0, sign[None, :], fs)
    ana = jnp.concatenate([fc, fs], axis=0).astype(BF16)
    ana = lax.optimization_barrier(ana)
    return ana, ana.T


def _hy_filter_kernel(feat_ref, tn_ref, w1_ref, b1_ref, w2_ref, b2_ref, fq_ref, w3f_ref, w3b_ref, dl_ref,
                      anar_ref, anai_ref, h_ref, pq_scr, inv_scr, nyq_scr):
    kf = pl.program_id(1)
    hp = lax.Precision.HIGHEST
    tk = h_ref.shape[1]

    @pl.when(kf == 0)
    def _():
        def taps(g):
            a = jnp.dot(feat_ref[g], w1_ref[...], precision=hp, preferred_element_type=F32) + b1_ref[...]
            a = jnp.sin(fq_ref[0:1, :] * a)
            a = jnp.dot(a, w2_ref[...], precision=hp, preferred_element_type=F32) + b2_ref[...]
            a = jnp.sin(fq_ref[1:2, :] * a)
            decay = jnp.exp(-tn_ref[g] * dl_ref[...])
            a = a.astype(BF16)
            return (jnp.dot(a, w3f_ref[...].astype(BF16), preferred_element_type=F32) * decay,
                    jnp.dot(a, w3b_ref[...].astype(BF16), preferred_element_type=F32) * decay)

        fwd0, bwd0 = taps(0)
        fwd1, bwd1 = taps(1)
        fwdm, bwdm = taps(2)
        rowi = lax.broadcasted_iota(jnp.int32, fwd0.shape, 0)
        first = rowi == 0
        bwd0 = jnp.where(first, 0.0, bwd0)
        absum = lambda v: jnp.sum(jnp.abs(v), axis=0, keepdims=True)
        inv_scr[...] = 1.0 / (absum(fwd0) + absum(fwd1) + absum(bwd0) + absum(bwd1) + EPS)
        parts = ((fwd0, bwd0),
                 (bwdm, jnp.where(first, 0.0, bwd1)),
                 (fwd1, jnp.where(first, 0.0, fwdm)))
        for n, (pos, neg) in enumerate(parts):
            p = pos + neg
            nyq_scr[n:n + 1, :] = jnp.sum(jnp.where((rowi & 1) == 0, p, -p), axis=0, keepdims=True)
            pq_scr[2 * n] = p.astype(BF16)
            pq_scr[2 * n + 1] = (pos - neg).astype(BF16)

    inv = inv_scr[...]
    for n in range(3):
        h_ref[2 * n] = jnp.dot(anar_ref[...], pq_scr[2 * n], preferred_element_type=F32) * inv
        hi = jnp.dot(anai_ref[...], pq_scr[2 * n + 1], preferred_element_type=F32)
        rowk = lax.broadcasted_iota(jnp.int32, hi.shape, 0)
        hi = jnp.where((rowk == 0) & (kf == 0), nyq_scr[n:n + 1, :], hi)
        h_ref[2 * n + 1] = hi * inv


def _hy_filters(seq_len, f_w1, f_b1, f_w2, f_b2, f_w3, freq, ana, *, tc, tk):
    half = seq_len // 2
    hid = f_w1.shape[1]
    nd = f_w3.shape[1] // 2
    d = nd // HY_ORDER
    t = jnp.arange(seq_len, dtype=F32)
    tnorm = t / max(seq_len - 1, 1)
    bands = (HY_EMB - 1) // 2
    fb = jnp.linspace(1e-4, bands - 1, bands, dtype=F32)
    w = 2.0 * math.pi * t / seq_len
    feats = jnp.concatenate([tnorm[:, None], jnp.cos(w[:, None] * fb), -jnp.sin(w[:, None] * fb)], axis=-1)
    emb = 64
    feats = jnp.pad(feats, ((0, 0), (0, emb - HY_EMB)))
    w1 = jnp.pad(f_w1, ((0, emb - HY_EMB), (0, 0)))
    mirror = half - jnp.arange(half)
    feats3 = jnp.stack([feats[:half], feats[half:], feats[mirror]])
    tn3 = jnp.stack([tnorm[:half], tnorm[half:], tnorm[mirror]]).reshape(3, half, 1)
    deltas = jnp.abs(jnp.linspace(HY_MIN_DECAY, HY_MAX_DECAY, d, dtype=F32))
    deltas = jnp.tile(deltas, HY_ORDER).reshape(1, nd)
    nct = nd // tc
    const = lambda shape: pl.BlockSpec(shape, lambda c, kf: (0,) * len(shape))
    return pl.pallas_call(
        _hy_filter_kernel,
        grid=(nct, half // tk),
        in_specs=[const((3, half, emb)), const((3, half, 1)),
                  const((emb, hid)), const((1, hid)), const((hid, hid)), const((1, hid)), const((2, hid)),
                  pl.BlockSpec((hid, tc), lambda c, kf: (0, c)),
                  pl.BlockSpec((hid, tc), lambda c, kf: (0, nct + c)),
                  pl.BlockSpec((1, tc), lambda c, kf: (0, c)),
                  pl.BlockSpec((tk, half), lambda c, kf: (kf, 0)),
                  pl.BlockSpec((tk, half), lambda c, kf: (half // tk + kf, 0))],
        out_specs=pl.BlockSpec((6, tk, tc), lambda c, kf: (0, kf, c)),
        out_shape=jax.ShapeDtypeStruct((6, half, nd), F32),
        scratch_shapes=[pltpu.VMEM((6, half, tc), BF16), pltpu.VMEM((1, tc), F32), pltpu.VMEM((8, tc), F32)],
        compiler_params=_params("parallel", "arbitrary"),
        name="hyena_filter",
    )(feats3, tn3, w1, f_b1.reshape(1, hid), f_w2, f_b2.reshape(1, hid), freq, f_w3, f_w3, deltas, ana, ana)


def _hy_inproj_kernel(xp_ref, x_ref, xn_ref, g_ref, sh_ref, sc_ref, w_ref, cw_ref, cb_ref, o_ref, h_scr, y_scr, *,
                      n_ctx_blocks, ctx_len, lat_len, tn):
    i = pl.program_id(0)
    tm = x_ref.shape[0]
    norm = lambda ref: _norm_mod(ref[...], g_ref[...], sh_ref[0], sc_ref[0]).astype(BF16)
    h_scr[0:HALO, :] = norm(xp_ref)
    h_scr[HALO:HALO + tm, :] = norm(x_ref)
    h_scr[HALO + tm:, :] = norm(xn_ref)

    row = lax.broadcasted_iota(jnp.int32, (tm, tn), 0)
    is_ctx = i < n_ctx_blocks
    pos = jnp.where(is_ctx, row % ctx_len, ((i - n_ctx_blocks) * tm + row) % lat_len)
    first = pos == 0
    last = pos == jnp.where(is_ctx, ctx_len, lat_len) - 1
    for jt in range(w_ref.shape[1] // tn):
        cols = slice(jt * tn, (jt + 1) * tn)
        y = y_scr.at[jt % 2]
        y[...] = jnp.dot(h_scr[...], w_ref[:, cols], preferred_element_type=F32)
        prev = jnp.where(first, 0.0, y[HALO - 1:HALO - 1 + tm, :])
        nxt = jnp.where(last, 0.0, y[HALO + 1:HALO + 1 + tm, :])
        o_ref[:, cols] = (prev * cw_ref[0:1, cols] + y[HALO:HALO + tm, :] * cw_ref[1:2, cols]
                          + nxt * cw_ref[2:3, cols] + cb_ref[:, cols]).astype(BF16)


def _hy_inproj(x, g, mod, w, conv_w, conv_b, *, n_ctx_rows, lat_len, ctx_len, tm):
    t, d = x.shape
    n = w.shape[1]
    per = tm // HALO
    row = functools.partial(_mod_row, tm=tm, n_ctx_rows=n_ctx_rows, lat_len=lat_len)
    return pl.pallas_call(
        functools.partial(_hy_inproj_kernel, n_ctx_blocks=n_ctx_rows // tm, ctx_len=ctx_len, lat_len=lat_len,
                          tn=COL_TILE),
        grid=(t // tm,),
        in_specs=[pl.BlockSpec((HALO, d), lambda i: (jnp.maximum(i * per - 1, 0), 0)),
                  pl.BlockSpec((tm, d), lambda i: (i, 0)),
                  pl.BlockSpec((HALO, d), lambda i: (jnp.minimum((i + 1) * per, t // HALO - 1), 0)),
                  pl.BlockSpec((1, d), lambda i: (0, 0)),
                  pl.BlockSpec((1, 1, d), lambda i: (row(i) * 6, 0, 0)),
                  pl.BlockSpec((1, 1, d), lambda i: (row(i) * 6 + 1, 0, 0)),
                  pl.BlockSpec((d, n), lambda i: (0, 0), pipeline_mode=pl.Buffered(1)),
                  pl.BlockSpec((3, n), lambda i: (0, 0)),
                  pl.BlockSpec((1, n), lambda i: (0, 0))],
        out_specs=pl.BlockSpec((tm, n), lambda i: (i, 0)),
        out_shape=jax.ShapeDtypeStruct((t, n), BF16),
        scratch_shapes=[pltpu.VMEM((tm + 2 * HALO, d), BF16), pltpu.VMEM((2, tm + 2 * HALO, COL_TILE), F32)],
        compiler_params=_params("parallel"),
        name="hyena_inproj",
    )(x, x, x, g.reshape(1, d), mod, mod, w, conv_w, conv_b.reshape(1, n))


def _hy_conv_kernel(a_ref, g_ref, skip_ref, ana_ref, syn_ref, h_ref, o_ref, *, seq_len):
    rows = a_ref.shape[0]
    half = seq_len // 2
    first = lax.broadcasted_iota(jnp.int32, (half, a_ref.shape[1]), 0) == 0

    def respond(uf, n):
        ur, ui = uf[0:half, :], uf[half:, :]
        hr, hi = h_ref[2 * n], h_ref[2 * n + 1]
        yr = jnp.where(first, 0.5 * (ur * hr), ur * hr - ui * hi)
        yi = jnp.where(first, 0.5 * (ui * hi), ur * hi + ui * hr)
        return yr, yi

    step = min(half, 512)
    for seg in range(0, rows, seq_len):
        uf1 = jnp.dot(ana_ref[...], a_ref[seg:seg + half, :], preferred_element_type=F32)
        uf2 = jnp.dot(ana_ref[...], a_ref[seg + half:seg + seq_len, :], preferred_element_type=F32)
        for dst, (n1, n2) in ((seg, (0, 1)), (seg + half, (2, 0))):
            y1r, y1i = respond(uf1, n1)
            y2r, y2i = respond(uf2, n2)
            yf = jnp.concatenate([y1r + y2r, y1i + y2i], axis=0).astype(BF16)
            for n0 in range(0, half, step):
                rs = slice(dst + n0, dst + n0 + step)
                y = jnp.dot(syn_ref[n0:n0 + step, :], yf, preferred_element_type=F32)
                a = a_ref[rs, :].astype(F32)
                g = g_ref[rs, :].astype(F32)
                o_ref[rs, :] = (g * (y * (1.0 / half) + a * skip_ref[...])).astype(BF16)


def _hy_conv(a, a_part, a_base, g, g_part, g_base, skip_row, mats, h, order, *, n_blocks, rows, seq_len, tc):
    d = skip_row.shape[1]
    nct = d // tc
    ana, syn = mats
    half = seq_len // 2

    def amap(part, base):
        return lambda c, b: (base + b, part * nct + c)

    once = dict(pipeline_mode=pl.Buffered(1))
    return pl.pallas_call(
        functools.partial(_hy_conv_kernel, seq_len=seq_len),
        grid=(nct, n_blocks),
        in_specs=[pl.BlockSpec((rows, tc), amap(a_part, a_base)),
                  pl.BlockSpec((rows, tc), amap(g_part, g_base)),
                  pl.BlockSpec((1, tc), lambda c, b: (0, c)),
                  pl.BlockSpec((2 * half, half), lambda c, b: (0, 0), **once),
                  pl.BlockSpec((half, 2 * half), lambda c, b: (0, 0), **once),
                  pl.BlockSpec((6, half, tc), lambda c, b: (0, 0, order * nct + c), **once)],
        out_specs=pl.BlockSpec((rows, tc), lambda c, b: (b, c)),
        out_shape=jax.ShapeDtypeStruct((n_blocks * rows, d), BF16),
        compiler_params=_params("parallel", "parallel"),
        name="hyena_conv",
    )(a, g, skip_row, ana, syn, h)


def _final_norm_kernel(x_ref, g_ref, o_ref):
    x = x_ref[...]
    o_ref[...] = x * lax.rsqrt(jnp.mean(x * x, axis=-1, keepdims=True) + EPS) * g_ref[...]


def _final_norm(x, g, *, row0, rows, tm):
    d = x.shape[1]
    base = row0 // tm
    return pl.pallas_call(
        _final_norm_kernel,
        grid=(rows // tm,),
        in_specs=[pl.BlockSpec((tm, d), lambda i: (base + i, 0)),
                  pl.BlockSpec((1, d), lambda i: (0, 0))],
        out_specs=pl.BlockSpec((tm, d), lambda i: (i, 0)),
        out_shape=jax.ShapeDtypeStruct((rows, d), F32),
        compiler_params=_params("parallel"),
        name="final_norm",
    )(x, g.reshape(1, d))


def kernel(x_prompt, x_sample, cache_win_k, cache_win_v, cache_ax_k, cache_ax_v, c, c_ctx, norm_mix_g, norm_ffn_g, mod_w, mod_b, win_wqkv, win_wo, win_sink, hy_w_in, hy_conv_w, hy_conv_b, hy_f_w1, hy_f_b1, hy_f_w2, hy_f_b2, hy_f_w3, hy_freq, hy_skip, hy_wo, ax_wqkv, ax_q_g, ax_k_g, ax_wo, ffn_w_gu, ffn_w_down, final_g):
    n_ctx_seqs, ctx_len, d = x_prompt.shape
    lat_batch, lat_len, _ = x_sample.shape
    depth = mod_w.shape[0]
    past = cache_win_k.shape[2]
    n_ctx_rows = n_ctx_seqs * ctx_len
    n_lat_rows = lat_batch * lat_len
    tm = TOKEN_TILE
    assert n_ctx_rows % tm == 0 and lat_len % tm == 0 and n_ctx_rows % lat_len == 0
    assert lat_batch + 1 <= MOD_ROWS
    geom = dict(n_ctx_rows=n_ctx_rows, lat_len=lat_len, tm=tm)

    x, x_lat = x_prompt.reshape(n_ctx_rows, d), x_sample.reshape(n_lat_rows, d)
    cvec = jnp.concatenate([c_ctx[None, :], c, jnp.zeros((MOD_ROWS - 1 - lat_batch, d), F32)], axis=0)
    mod_all = _adaln_all(cvec, mod_w, mod_b)
    cos, sin = _rope_tables(lat_len)

    w_gu, w_down = ffn_w_gu.astype(BF16), ffn_w_down.astype(BF16)
    wqkv_all = {0: win_wqkv.astype(BF16), 2: ax_wqkv.astype(BF16)}
    wo_all = {0: win_wo.astype(BF16), 1: hy_wo.astype(BF16), 2: ax_wo.astype(BF16)}

    states = {0: None, 2: None}
    kv_w = N_KV_HEADS * HEAD_DIM
    for i in range(depth):
        mod = mod_all[i].reshape(MOD_ROWS * 6, 1, d)
        kind, j = i % 3, i // 3
        if kind == 1:
            assert x_lat is None
            u = _hy_inproj(x, norm_mix_g[i], mod, hy_w_in[j].astype(BF16), hy_conv_w[j], hy_conv_b[j],
                           ctx_len=ctx_len, **geom)
            rows = lat_len
            zs = []
            for (base, nblk, sl) in ((0, n_ctx_rows // rows, ctx_len), (n_ctx_rows // rows, lat_batch, lat_len)):
                mats = _dft_mats(sl // 2)
                resp = _hy_filters(sl, hy_f_w1[j], hy_f_b1[j], hy_f_w2[j], hy_f_b2[j], hy_f_w3[j], hy_freq[j],
                                   mats[0], tc=COL_TILE, tk=min(COL_TILE, sl // 2))
                conv = functools.partial(_hy_conv, mats=mats, h=resp, n_blocks=nblk, rows=rows, seq_len=sl,
                                         tc=COL_TILE)
                z1 = conv(u, 0, base, u, 1, base, skip_row=hy_skip[j, 0].reshape(1, d), order=0)
                zs.append(conv(z1, 0, 0, u, 2, base, skip_row=hy_skip[j, 1].reshape(1, d), order=1))
            x, h2 = _proj_residual(zs[0], zs[1], wo_all[kind], j, x, None, mod, norm_ffn_g[i], **geom)
        else:
            if kind == 0:
                sink, hg = win_sink[j], None
                ck, cv = cache_win_k, cache_win_v
            else:
                sink = jnp.zeros((N_HEADS,), F32)
                hg = jnp.stack([ax_q_g[j], ax_k_g[j]], axis=0)
                ck, cv = cache_ax_k, cache_ax_v
            ck = ck[:, j:j + 1].reshape(lat_batch, 1, past, kv_w)
            cv = cv[:, j:j + 1].reshape(lat_batch, 1, past, kv_w)
            y, k_c, v_c = _qkv_project(x, x_lat, norm_mix_g[i], mod, wqkv_all[kind], j, cos, sin, hg, states[kind],
                                       ctx_len=ctx_len, **geom)
            states[kind] = (k_c, v_c)
            o_c = _attn_ctx(y, sink, n_ctx_seqs=n_ctx_seqs, ctx_len=ctx_len, use_sink=kind == 0)
            lat_geom = dict(n_ctx_rows=n_ctx_rows, lat_batch=lat_batch, lat_len=lat_len)
            if kind == 0:
                o_l = _attn_lat_win(y, sink, ck, cv, 0, **lat_geom)
            else:
                o_l = _attn_lat_full(y, ck, cv, 0, **lat_geom)
            x, h2 = _proj_residual(o_c, o_l, wo_all[kind], j, x, x_lat, mod, norm_ffn_g[i], **geom)
            x_lat = None
        x = _ffn(x, h2, mod, w_gu, w_down, i, tf=COL_TILE, **geom)

    y_prompt = _final_norm(x, final_g, row0=0, rows=n_ctx_rows, tm=tm).reshape(n_ctx_seqs, ctx_len, d)
    y_sample = _final_norm(x, final_g, row0=n_ctx_rows, rows=n_lat_rows, tm=tm).reshape(lat_batch, lat_len, d)
    return (y_prompt, y_sample) + states[0] + states[2]
```
